```python
import jax, jax.numpy as jnp
from jax import lax
import numpy as np


D_MODEL = 2048
BATCH = 4
SEQ = 2048
DEPTH = 1
DEC_BATCH = 32
DEC_SEQ = 32
PAST_LEN = 1024

CHUNK = 64
D_SB = D_MODEL // 2
SB_HEAD_DIM = 128
SB_HEADS = D_SB // SB_HEAD_DIM
D_CONV = D_MODEL // 2
CONV_W = 3
D_MIX = D_SB + D_CONV
D_IN = 4 * D_SB + 4 * D_CONV
Q_BLOCK = 128
EPS = 1e-6

kernel_name = 'stick_break_shortconv_hybrid_step'


def _rmsnorm(x, g):
    x32 = x.astype(jnp.float32)
    r = lax.rsqrt(jnp.mean(x32 * x32, axis=-1, keepdims=True) + EPS)
    return (x32 * r).astype(x.dtype) * g


def _project(h, w_in):
    proj = jnp.einsum('btd,de->bte', h, w_in)
    cuts = [D_SB, 2 * D_SB, 3 * D_SB, 4 * D_SB,
            4 * D_SB + D_CONV, 4 * D_SB + 2 * D_CONV, 4 * D_SB + 3 * D_CONV]
    q, k, v, g_sb, b, c, u, g_cv = jnp.split(proj, cuts, axis=-1)
    bsz, t = h.shape[0], h.shape[1]
    heads = lambda a: a.reshape(bsz, t, SB_HEADS, SB_HEAD_DIM)
    return heads(q), heads(k), heads(v), g_sb, b, c, u, g_cv


def _sb_attend(q, k, v, q_pos, k_pos):
    scale = SB_HEAD_DIM ** -0.5
    z = jnp.einsum('bqhd,bkhd->bhqk', q, k, preferred_element_type=jnp.float32) * scale
    mask = k_pos[None, :] < q_pos[:, None]
    log_beta = jax.nn.log_sigmoid(z)
    log_1m = jnp.where(mask, jax.nn.log_sigmoid(-z), 0.0)
    suffix = lax.cumsum(log_1m, axis=3, reverse=True) - log_1m
    w = jnp.where(mask, jnp.exp(log_beta + suffix), 0.0)
    return jnp.einsum('bhqk,bkhd->bqhd', w.astype(v.dtype), v)


def _sb_prompt(q, k, v):
    bsz, t = q.shape[0], q.shape[1]
    nb = t // Q_BLOCK
    qb = q.reshape(bsz, nb, Q_BLOCK, SB_HEADS, SB_HEAD_DIM).transpose(1, 0, 2, 3, 4)
    q_pos = jnp.arange(nb)[:, None] * Q_BLOCK + jnp.arange(Q_BLOCK)[None, :]
    k_pos = jnp.arange(t)
    ob = lax.map(lambda a: _sb_attend(a[0], k, v, a[1], k_pos), (qb, q_pos))
    return ob.transpose(1, 0, 2, 3, 4).reshape(bsz, t, SB_HEADS, SB_HEAD_DIM)


def _short_conv(b, cu, left, conv_w):
    t = cu.shape[1]
    full = jnp.concatenate([left.astype(cu.dtype), cu], axis=1)
    conv = conv_w[0] * full[:, 0:t]
    for i in range(1, CONV_W):
        conv = conv + conv_w[i] * full[:, i:i + t]
    return b * conv, full[:, t:]


def _merge(o_sb, g_sb, o_cv, g_cv, w_out):
    bsz, t = o_sb.shape[0], o_sb.shape[1]
    mix = jnp.concatenate([o_sb.reshape(bsz, t, D_SB) * jax.nn.silu(g_sb),
                           o_cv * jax.nn.silu(g_cv)], axis=-1)
    return jnp.einsum('bte,ed->btd', mix, w_out)


def _prompt_layer(x, norm_g, w_in, conv_w, w_out):
    h = _rmsnorm(x, norm_g)
    q, k, v, g_sb, b, c, u, g_cv = _project(h, w_in)
    o_sb = _sb_prompt(q, k, v)
    left = jnp.zeros((x.shape[0], CONV_W - 1, D_CONV), x.dtype)
    o_cv, conv_tail = _short_conv(b, c * u, left, conv_w)
    return x + _merge(o_sb, g_sb, o_cv, g_cv, w_out), k, v, conv_tail


def _sample_layer(x, k_past, v_past, conv_past, norm_g, w_in, conv_w, w_out):
    h = _rmsnorm(x, norm_g)
    q, k, v, g_sb, b, c, u, g_cv = _project(h, w_in)
    past, t = k_past.shape[1], x.shape[1]
    k_all = jnp.concatenate([k_past.astype(k.dtype), k], axis=1)
    v_all = jnp.concatenate([v_past.astype(v.dtype), v], axis=1)
    q_pos = past + jnp.arange(t)
    k_pos = jnp.arange(past + t)
    o_sb = _sb_attend(q, k_all, v_all, q_pos, k_pos)
    o_cv, conv_tail = _short_conv(b, c * u, conv_past, conv_w)
    return x + _merge(o_sb, g_sb, o_cv, g_cv, w_out), k, v, conv_tail


def setup_inputs(seed: int = 0) -> dict:
    key = jax.random.key(seed)
    ks = jax.random.split(key, 10)
    f32 = jnp.float32
    return {
        'x_prompt': jax.random.normal(ks[0], (BATCH, SEQ, D_MODEL), f32),
        'x_sample': jax.random.normal(ks[1], (DEC_BATCH, DEC_SEQ, D_MODEL), f32),
        'cache_k': jax.random.normal(ks[2], (DEPTH, DEC_BATCH, PAST_LEN, SB_HEADS, SB_HEAD_DIM), f32),
        'cache_v': jax.random.normal(ks[3], (DEPTH, DEC_BATCH, PAST_LEN, SB_HEADS, SB_HEAD_DIM), f32),
        'state_conv': jax.random.normal(ks[4], (DEPTH, DEC_BATCH, CONV_W - 1, D_CONV), f32),
        'norm_g': 1.0 + 0.01 * jax.random.normal(ks[5], (DEPTH, D_MODEL), f32),
        'w_in': jax.random.normal(ks[6], (DEPTH, D_MODEL, D_IN), f32) * D_MODEL ** -0.5,
        'conv_w': jax.random.normal(ks[7], (DEPTH, CONV_W, D_CONV), f32) * CONV_W ** -0.5,
        'w_out': jax.random.normal(ks[8], (DEPTH, D_MIX, D_MODEL), f32) * D_MIX ** -0.5,
        'final_g': 1.0 + 0.01 * jax.random.normal(ks[9], (D_MODEL,), f32),
    }


def reference(x_prompt, x_sample, cache_k, cache_v, state_conv, norm_g, w_in, conv_w, w_out, final_g):
    xp, xs = x_prompt, x_sample
    kp_l, vp_l, cp_l, ks_l, vs_l, cs_l = [], [], [], [], [], []
    for layer in range(DEPTH):
        xp, kp, vp, cp = _prompt_layer(xp, norm_g[layer], w_in[layer], conv_w[layer], w_out[layer])
        xs, kn, vn, cn = _sample_layer(xs, cache_k[layer], cache_v[layer], state_conv[layer],
                                       norm_g[layer], w_in[layer], conv_w[layer], w_out[layer])
        kp_l.append(kp); vp_l.append(vp); cp_l.append(cp)
        ks_l.append(kn); vs_l.append(vn); cs_l.append(cn)
    y_prompt = _rmsnorm(xp, final_g)
    y_sample = _rmsnorm(xs, final_g)
    return (y_prompt, y_sample, jnp.stack(kp_l), jnp.stack(vp_l), jnp.stack(cp_l),
            jnp.stack(ks_l), jnp.stack(vs_l), jnp.stack(cs_l))
```

```python
import functools

import jax
import jax.numpy as jnp
from jax import lax
from jax.experimental import pallas as pl
from jax.experimental.pallas import tpu as pltpu

F32 = jnp.float32
BF16 = jnp.bfloat16

HEAD_DIM = 128
NUM_GROUPS = 8
NUM_SEGMENTS = 8
CONV_W = 3
EPS = 1e-6
SCALE = HEAD_DIM ** -0.5

V7X_VMEM_BYTES = 64 * 1024 * 1024
VMEM_LIMIT_BYTES = 56 * 1024 * 1024


def _compiler_params(semantics):
    return pltpu.CompilerParams(dimension_semantics=semantics,
                                vmem_limit_bytes=VMEM_LIMIT_BYTES)


def _rmsnorm_rows(x, g):
    r = lax.rsqrt(jnp.mean(x * x, axis=-1, keepdims=True) + EPS)
    return (x * r) * g


def _proj_kernel(x_ref, g_ref, w_ref, cw_ref, left_ref,
                 q_ref, k_ref, v_ref, kb_ref, vb_ref, sg_ref, mcv_ref, tail_ref,
                 hn_ref, carry_ref, *, seq_rows, tiles_per_seq):
    i = pl.program_id(0)
    h = pl.program_id(1)
    tm = x_ref.shape[0]

    @pl.when(h == 0)
    def _():
        hn_ref[...] = _rmsnorm_rows(x_ref[...], g_ref[...]).astype(BF16)

    acc = jnp.dot(hn_ref[...], w_ref[...], preferred_element_type=F32)
    seg = lambda s: acc[:, s * HEAD_DIM:(s + 1) * HEAD_DIM]
    q, k, v, g_sb, b, c, u, g_cv = [seg(s) for s in range(NUM_SEGMENTS)]

    q_ref[...] = (q * (-SCALE)).astype(BF16)
    k_ref[...] = k
    v_ref[...] = v
    kb_ref[...] = k.astype(BF16)
    vb_ref[...] = v.astype(BF16)
    sg_ref[...] = (g_sb * (1.0 / (1.0 + jnp.exp(-g_sb)))).astype(BF16)

    cu = c * u
    if tiles_per_seq > 1:
        @pl.when(i % tiles_per_seq == 0)
        def _():
            carry_ref[h, 0:2, :] = left_ref[0]
        left = carry_ref[h][None, 0:2, :]
        carry_ref[h, 0:2, :] = cu[tm - 2:tm, :]
    else:
        left = left_ref[...]
    rows = seq_rows if tiles_per_seq == 1 else tm
    nseq = tm // rows
    expand = lambda a: jnp.broadcast_to(a, (nseq, rows, HEAD_DIM)).reshape(tm, HEAD_DIM)
    l0 = expand(left[:, 0:1, :])
    l1 = expand(left[:, 1:2, :])
    rs = lax.broadcasted_iota(jnp.int32, (tm, HEAD_DIM), 0) & (rows - 1)
    r1 = jnp.where(rs == 0, l1, pltpu.roll(cu, 1, axis=0))
    r2 = jnp.where(rs == 0, l0, jnp.where(rs == 1, l1, pltpu.roll(cu, 2, axis=0)))
    cw = cw_ref[...]
    conv = cw[0:1, :] * r2 + cw[1:2, :] * r1 + cw[2:3, :] * cu
    mcv_ref[...] = (b * conv * (g_cv * (1.0 / (1.0 + jnp.exp(-g_cv))))).astype(BF16)
    tail_ref[...] = cu.reshape(nseq, rows, HEAD_DIM)[:, rows - 2:rows, :]


def _project(x2d, norm_g, w_r, conv_w, left, *, seq_rows, tm):
    m, d = x2d.shape
    nseq_total = left.shape[0]
    if tm >= seq_rows:
        tiles_per_seq, ns = 1, tm // seq_rows
        left_idx = lambda i, h: (i, 0, h)
    else:
        tiles_per_seq, ns = seq_rows // tm, 1
        left_idx = lambda i, h: (i // tiles_per_seq, 0, h)
    tile = lambda dt: jax.ShapeDtypeStruct((m, NUM_GROUPS * HEAD_DIM), dt)
    col_block = pl.BlockSpec((tm, HEAD_DIM), lambda i, h: (i, h))
    kern = functools.partial(_proj_kernel, seq_rows=seq_rows, tiles_per_seq=tiles_per_seq)
    return pl.pallas_call(
        kern,
        grid=(m // tm, NUM_GROUPS),
        in_specs=[
            pl.BlockSpec((tm, d), lambda i, h: (i, 0)),
            pl.BlockSpec((1, d), lambda i, h: (0, 0)),
            pl.BlockSpec((None, d, NUM_SEGMENTS * HEAD_DIM), lambda i, h: (h, 0, 0)),
            pl.BlockSpec((CONV_W, HEAD_DIM), lambda i, h: (0, h)),
            pl.BlockSpec((ns, 2, HEAD_DIM), left_idx),
        ],
        out_specs=[col_block] * 7 + [pl.BlockSpec((ns, 2, HEAD_DIM), left_idx)],
        out_shape=[tile(BF16), tile(F32), tile(F32), tile(BF16), tile(BF16), tile(BF16), tile(BF16),
                   jax.ShapeDtypeStruct((nseq_total, 2, NUM_GROUPS * HEAD_DIM), F32)],
        scratch_shapes=[pltpu.VMEM((tm, d), BF16),
                        pltpu.VMEM((NUM_GROUPS, 8, HEAD_DIM), F32)],
        compiler_params=_compiler_params(("arbitrary", "arbitrary")),
        name="proj",
    )(x2d, norm_g.reshape(1, d), w_r, conv_w, left)


def _sb_block(qn, kblk, vblk, tri, carry, mask):
    zn = lax.dot_general(qn, kblk, (((1,), (1,)), ((), ())), preferred_element_type=F32)
    sp = jnp.log(1.0 + jnp.exp(-jnp.abs(zn)))
    log_1m = jnp.minimum(zn, 0.0) - sp
    log_beta = log_1m - zn
    if mask is not None:
        log_1m = jnp.where(mask, log_1m, 0.0)
    hi = log_1m.astype(BF16)
    lo = (log_1m - hi.astype(F32)).astype(BF16)
    suffix = (jnp.dot(hi, tri, preferred_element_type=F32)
              + jnp.dot(lo, tri, preferred_element_type=F32))
    w = jnp.exp(log_beta + suffix + carry)
    if mask is not None:
        w = jnp.where(mask, w, 0.0)
    pv = jnp.dot(w.astype(BF16), vblk, preferred_element_type=F32)
    return pv, carry + suffix[:, 0:1] + log_1m[:, 0:1]


def _attn_prompt_kernel(q_ref, k_ref, v_ref, sg_ref, tri_ref, o_ref):
    qi = pl.program_id(2)
    tq = q_ref.shape[0]
    tk = tri_ref.shape[0]
    qn = q_ref[...]
    tri = tri_ref[...]

    def kv(j):
        start = pl.multiple_of(j * tk, tk)
        return k_ref[pl.ds(start, tk), :], v_ref[pl.ds(start, tk), :]

    rows = lax.broadcasted_iota(jnp.int32, (tq, tk), 0)
    cols = lax.broadcasted_iota(jnp.int32, (tq, tk), 1)
    kblk, vblk = kv(qi)
    acc, carry = _sb_block(qn, kblk, vblk, tri, jnp.zeros((tq, 1), F32), cols < rows)

    def body(jj, state):
        acc, carry = state
        kblk, vblk = kv(qi - 1 - jj)
        pv, carry = _sb_block(qn, kblk, vblk, tri, carry, None)
        return acc + pv, carry

    acc, _ = lax.fori_loop(0, qi, body, (acc, carry))
    o_ref[...] = (acc * sg_ref[...].astype(F32)).astype(BF16)


def _attend_prompt(q, kb, vb, sg, tri, *, batch, seq, tq):
    nq = seq // tq
    assert tq == tri.shape[0], "query block and key block share the diagonal mask"
    qblock = pl.BlockSpec((tq, HEAD_DIM), lambda b, h, i: (b * nq + i, h))
    kvblock = pl.BlockSpec((seq, HEAD_DIM), lambda b, h, i: (b, h))
    return pl.pallas_call(
        _attn_prompt_kernel,
        grid=(batch, NUM_GROUPS, nq),
        in_specs=[qblock, kvblock, kvblock, qblock,
                  pl.BlockSpec(tri.shape, lambda b, h, i: (0, 0))],
        out_specs=qblock,
        out_shape=jax.ShapeDtypeStruct(q.shape, BF16),
        compiler_params=_compiler_params(("arbitrary", "arbitrary", "arbitrary")),
        name="attn_prompt",
    )(q, kb, vb, sg, tri)


def _attn_sample_kernel(q_ref, kn_ref, vn_ref, kc_ref, vc_ref, sg_ref, tri_ref, o_ref,
                        kpad_ref, vpad_ref):
    t = q_ref.shape[0]
    past = kc_ref.shape[0]
    tk = tri_ref.shape[0]
    pad = kpad_ref.shape[0]
    tri = tri_ref[...]
    rows = lax.broadcasted_iota(jnp.int32, (t, pad), 0)
    cols = lax.broadcasted_iota(jnp.int32, (t, pad), 1)
    kpad_ref[...] = jnp.zeros(kpad_ref.shape, BF16)
    vpad_ref[...] = jnp.zeros(vpad_ref.shape, BF16)
    for h in range(NUM_GROUPS):
        hs = slice(h * HEAD_DIM, (h + 1) * HEAD_DIM)
        qn = q_ref[:, hs]
        kpad_ref[0:t, :] = kn_ref[:, hs].astype(BF16)
        vpad_ref[0:t, :] = vn_ref[:, hs].astype(BF16)
        acc, carry = _sb_block(qn, kpad_ref[...], vpad_ref[...], tri[0:pad, 0:pad],
                               jnp.zeros((t, 1), F32), cols < rows)
        for j in reversed(range(past // tk)):
            kblk = kc_ref[j * tk:(j + 1) * tk, hs].astype(BF16)
            vblk = vc_ref[j * tk:(j + 1) * tk, hs].astype(BF16)
            pv, carry = _sb_block(qn, kblk, vblk, tri, carry, None)
            acc = acc + pv
        o_ref[:, hs] = (acc * sg_ref[:, hs].astype(F32)).astype(BF16)


def _attend_sample(q, k_new, v_new, cache_k, cache_v, sg, tri, *, batch, t):
    past, width = cache_k.shape[1], cache_k.shape[2]
    rowblock = pl.BlockSpec((t, width), lambda b: (b, 0))
    cache = pl.BlockSpec((None, past, width), lambda b: (b, 0, 0))
    return pl.pallas_call(
        _attn_sample_kernel,
        grid=(batch,),
        in_specs=[rowblock, rowblock, rowblock, cache, cache, rowblock,
                  pl.BlockSpec(tri.shape, lambda b: (0, 0))],
        out_specs=rowblock,
        out_shape=jax.ShapeDtypeStruct(q.shape, BF16),
        scratch_shapes=[pltpu.VMEM((HEAD_DIM, HEAD_DIM), BF16),
                        pltpu.VMEM((HEAD_DIM, HEAD_DIM), BF16)],
        compiler_params=_compiler_params(("arbitrary",)),
        name="attn_sample",
    )(q, k_new, v_new, cache_k, cache_v, sg, tri)


def _out_kernel(x_ref, msb_ref, mcv_ref, wsb_ref, wcv_ref, g_ref, y_ref):
    y = (x_ref[...]
         + jnp.dot(msb_ref[...], wsb_ref[...], preferred_element_type=F32)
         + jnp.dot(mcv_ref[...], wcv_ref[...], preferred_element_type=F32))
    y_ref[...] = _rmsnorm_rows(y, g_ref[...])


def _merge_out(x2d, mix_sb, mix_cv, w_sb, w_cv, final_g, *, tm):
    m, d = x2d.shape
    half = mix_sb.shape[1]
    rows = lambda width: pl.BlockSpec((tm, width), lambda i: (i, 0))
    whole = lambda shape: pl.BlockSpec(shape, lambda i: (0, 0))
    return pl.pallas_call(
        _out_kernel,
        grid=(m // tm,),
        in_specs=[rows(d), rows(half), rows(half), whole(w_sb.shape), whole(w_cv.shape), whole((1, d))],
        out_specs=rows(d),
        out_shape=jax.ShapeDtypeStruct((m, d), F32),
        compiler_params=_compiler_params(("arbitrary",)),
        name="merge_out",
    )(x2d, mix_sb, mix_cv, w_sb, w_cv, final_g.reshape(1, d))


def _strict_lower_tri(n):
    j = lax.broadcasted_iota(jnp.int32, (n, n), 0)
    s = lax.broadcasted_iota(jnp.int32, (n, n), 1)
    return (j > s).astype(BF16)


def kernel(x_prompt, x_sample, cache_k, cache_v, state_conv, norm_g, w_in, conv_w, w_out, final_g):
    depth = w_in.shape[0]
    assert depth == 1, "single-layer step"
    bsz, seq, d = x_prompt.shape
    dbsz, dseq, _ = x_sample.shape
    past = cache_k.shape[2]
    width = NUM_GROUPS * HEAD_DIM

    w_r = (w_in[0].reshape(d, NUM_SEGMENTS, NUM_GROUPS, HEAD_DIM)
           .transpose(2, 0, 1, 3).reshape(NUM_GROUPS, d, NUM_SEGMENTS * HEAD_DIM).astype(BF16))
    w_sb = w_out[0, :width].astype(BF16)
    w_cv = w_out[0, width:].astype(BF16)
    tri = _strict_lower_tri(256)

    xp = x_prompt.reshape(bsz * seq, d)
    zeros_left = jnp.zeros((bsz, CONV_W - 1, width), F32)
    qp, kp, vp, kbp, vbp, sgp, mcvp, tailp = _project(
        xp, norm_g[0], w_r, conv_w[0], zeros_left, seq_rows=seq, tm=512)
    msbp = _attend_prompt(qp, kbp, vbp, sgp, tri, batch=bsz, seq=seq, tq=256)
    yp = _merge_out(xp, msbp, mcvp, w_sb, w_cv, final_g, tm=512)

    xs = x_sample.reshape(dbsz * dseq, d)
    qs, ks, vs, _, _, sgs, mcvs, tails = _project(
        xs, norm_g[0], w_r, conv_w[0], state_conv[0], seq_rows=dseq, tm=512)
    msbs = _attend_sample(qs, ks, vs, cache_k[0].reshape(dbsz, past, width),
                          cache_v[0].reshape(dbsz, past, width), sgs, tri, batch=dbsz, t=dseq)
    ys = _merge_out(xs, msbs, mcvs, w_sb, w_cv, final_g, tm=512)

    heads = lambda a, b_, t_: a.reshape(1, b_, t_, NUM_GROUPS, HEAD_DIM)
    return (yp.reshape(bsz, seq, d), ys.reshape(dbsz, dseq, d),
            heads(kp, bsz, seq), heads(vp, bsz, seq), tailp[None],
            heads(ks, dbsz, dseq), heads(vs, dbsz, dseq), tails[None])
```

```python
import functools
import math

import jax
import jax.numpy as jnp
from jax import lax
from jax.experimental import pallas as pl
from jax.experimental.pallas import tpu as pltpu

F32 = jnp.float32
BF16 = jnp.bfloat16

HEAD_DIM = 128
NUM_GROUPS = 8
NUM_SEGMENTS = 8
CONV_W = 3
EPS = 1e-6
LOG2E = math.log2(math.e)
Q_PRESCALE = -(HEAD_DIM ** -0.5) * LOG2E
KEY_BLOCK = 256
SIGN_BIT = 0x80000000

VMEM_LIMIT_BYTES = 56 * 1024 * 1024


def _compiler_params(semantics):
    return pltpu.CompilerParams(dimension_semantics=semantics,
                                vmem_limit_bytes=VMEM_LIMIT_BYTES)


def _rmsnorm_rows(x, g):
    r = lax.rsqrt(jnp.mean(x * x, axis=-1, keepdims=True) + EPS)
    return (x * r) * g


def _silu(x):
    return x * (1.0 / (1.0 + jnp.exp(-x)))


def _proj_kernel(x_ref, g_ref, w_ref, cw_ref, left_ref,
                 q_ref, k_ref, v_ref, kb_ref, vb_ref, sg_ref, mcv_ref, tail_ref,
                 hn_ref, carry_ref, *, seq_rows, tiles_per_seq):
    i = pl.program_id(0)
    h = pl.program_id(1)
    tm = x_ref.shape[0]

    @pl.when(h == 0)
    def _():
        hn_ref[...] = _rmsnorm_rows(x_ref[...], g_ref[...]).astype(BF16)

    acc = jnp.dot(hn_ref[...], w_ref[...], preferred_element_type=F32)
    seg = lambda s: acc[:, s * HEAD_DIM:(s + 1) * HEAD_DIM]
    q, k, v, g_sb, b, c, u, g_cv = [seg(s) for s in range(NUM_SEGMENTS)]

    q_ref[...] = (q * Q_PRESCALE).astype(BF16)
    k_ref[pl.ds(h, tm, stride=NUM_GROUPS), :] = k
    v_ref[pl.ds(h, tm, stride=NUM_GROUPS), :] = v
    kb_ref[...] = k.astype(BF16)
    vb_ref[...] = v.astype(BF16)
    sg_ref[...] = _silu(g_sb).astype(BF16)

    cu = c * u
    if tiles_per_seq > 1:
        @pl.when(i % tiles_per_seq == 0)
        def _():
            carry_ref[h, 0:2, :] = left_ref[0]
        left = carry_ref[h][None, 0:2, :]
        carry_ref[h, 0:2, :] = cu[tm - 2:tm, :]
    else:
        left = left_ref[...]
    rows = seq_rows if tiles_per_seq == 1 else tm
    nseq = tm // rows
    expand = lambda a: jnp.broadcast_to(a, (nseq, rows, HEAD_DIM)).reshape(tm, HEAD_DIM)
    l0 = expand(left[:, 0:1, :])
    l1 = expand(left[:, 1:2, :])
    rs = lax.broadcasted_iota(jnp.int32, (tm, HEAD_DIM), 0) & (rows - 1)
    r1 = jnp.where(rs == 0, l1, pltpu.roll(cu, 1, axis=0))
    r2 = jnp.where(rs == 0, l0, jnp.where(rs == 1, l1, pltpu.roll(cu, 2, axis=0)))
    cw = cw_ref[...]
    conv = cw[0:1, :] * r2 + cw[1:2, :] * r1 + cw[2:3, :] * cu
    mcv_ref[...] = (b * conv * _silu(g_cv)).astype(BF16)
    tail_ref[...] = cu.reshape(nseq, rows, HEAD_DIM)[:, rows - 2:rows, :]


def _project(x2d, norm_g, w_r, conv_w, left, *, seq_rows, tm):
    m, d = x2d.shape
    width = NUM_GROUPS * HEAD_DIM
    if tm >= seq_rows:
        tiles_per_seq, ns = 1, tm // seq_rows
        left_idx = lambda i, h: (i, 0, h)
    else:
        tiles_per_seq, ns = seq_rows // tm, 1
        left_idx = lambda i, h: (i // tiles_per_seq, 0, h)
    n_tails = (m // tm) * ns
    tile = lambda dt: jax.ShapeDtypeStruct((m, width), dt)
    native = jax.ShapeDtypeStruct((m * NUM_GROUPS, HEAD_DIM), F32)
    col_block = pl.BlockSpec((tm, HEAD_DIM), lambda i, h: (i, h))
    native_block = pl.BlockSpec((tm * NUM_GROUPS, HEAD_DIM), lambda i, h: (i, 0))
    kern = functools.partial(_proj_kernel, seq_rows=seq_rows, tiles_per_seq=tiles_per_seq)
    q, k, v, kb, vb, sg, mcv, tails = pl.pallas_call(
        kern,
        grid=(m // tm, NUM_GROUPS),
        in_specs=[
            pl.BlockSpec((tm, d), lambda i, h: (i, 0)),
            pl.BlockSpec((1, d), lambda i, h: (0, 0)),
            pl.BlockSpec((None, d, NUM_SEGMENTS * HEAD_DIM), lambda i, h: (h, 0, 0)),
            pl.BlockSpec((CONV_W, HEAD_DIM), lambda i, h: (0, h)),
            pl.BlockSpec((ns, 2, HEAD_DIM), left_idx),
        ],
        out_specs=[col_block, native_block, native_block, col_block, col_block, col_block,
                   col_block, pl.BlockSpec((ns, 2, HEAD_DIM), lambda i, h: (i, 0, h))],
        out_shape=[tile(BF16), native, native, tile(BF16), tile(BF16), tile(BF16), tile(BF16),
                   jax.ShapeDtypeStruct((n_tails, 2, width), F32)],
        scratch_shapes=[pltpu.VMEM((tm, d), BF16),
                        pltpu.VMEM((NUM_GROUPS, 8, HEAD_DIM), F32)],
        compiler_params=_compiler_params(("arbitrary", "arbitrary")),
        name="proj",
    )(x2d, norm_g.reshape(1, d), w_r, conv_w, left)
    tails = tails.reshape(-1, tiles_per_seq, 2, width)[:, tiles_per_seq - 1]
    return q, k, v, kb, vb, sg, mcv, tails


def _log2_one_minus_beta(zn, mask):
    softplus2 = jnp.log(1.0 + jnp.exp2(-jnp.abs(zn))) * LOG2E
    log_1m = jnp.minimum(zn, 0.0) - softplus2
    return log_1m if mask is None else jnp.where(mask, log_1m, 0.0)


def _split_bf16(x, axis):
    hi = x.astype(BF16)
    return jnp.concatenate([hi, (x - hi.astype(F32)).astype(BF16)], axis=axis)


def _block_weights(suffix, zn, mask):
    w = jnp.exp2(suffix - zn)
    return (w if mask is None else jnp.where(mask, w, 0.0)).astype(BF16)


def _attn_prompt_kernel(q_ref, k_ref, v_ref, sg_ref, tri_ref, o_ref, acc_ref):
    qi = pl.program_id(2)
    tq = q_ref.shape[0]
    tk = tri_ref.shape[0]
    heads = q_ref.shape[1] // HEAD_DIM
    tri2 = jnp.concatenate([tri_ref[...]] * 2, axis=0)
    hs = [slice(a * HEAD_DIM, (a + 1) * HEAD_DIM) for a in range(heads)]
    nt_dims = (((1,), (1,)), ((), ()))

    def step(j, carries, mask):
        start = pl.multiple_of(j * tk, tk)
        zns = [lax.dot_general(q_ref[:, s], k_ref[pl.ds(start, tk), s], nt_dims,
                               preferred_element_type=F32) for s in hs]
        pieces = [_split_bf16(_log2_one_minus_beta(zn, mask), axis=1) for zn in zns]
        suffixes = [jnp.dot(p, tri2, preferred_element_type=F32) for p in pieces]
        new = []
        for a, s in enumerate(hs):
            pv = jnp.dot(_block_weights(suffixes[a], zns[a], mask), v_ref[pl.ds(start, tk), s],
                         preferred_element_type=F32)
            if mask is None:
                acc_ref[:, s] += jnp.exp2(carries[a]) * pv
            else:
                acc_ref[:, s] = pv
            new.append(carries[a] + suffixes[a][:, 0:1])
        return tuple(new)

    rows = lax.broadcasted_iota(jnp.int32, (tq, tk), 0)
    cols = lax.broadcasted_iota(jnp.int32, (tq, tk), 1)
    carries = step(qi, (jnp.zeros((tq, 1), F32),) * heads, cols < rows)
    lax.fori_loop(0, qi, lambda jj, cs: step(qi - 1 - jj, cs, None), carries)
    o_ref[...] = (acc_ref[...] * sg_ref[...].astype(F32)).astype(BF16)


def _attend_prompt(q, kb, vb, sg, tri, *, batch, seq, tq, heads_per_step):
    nq = seq // tq
    assert tq == tri.shape[0], "query block and key block share the diagonal mask"
    cols = heads_per_step * HEAD_DIM
    qblock = pl.BlockSpec((tq, cols), lambda b, g, i: (b * nq + i, g))
    kvblock = pl.BlockSpec((seq, cols), lambda b, g, i: (b, g))
    return pl.pallas_call(
        _attn_prompt_kernel,
        grid=(batch, NUM_GROUPS // heads_per_step, nq),
        in_specs=[qblock, kvblock, kvblock, qblock,
                  pl.BlockSpec(tri.shape, lambda b, g, i: (0, 0))],
        out_specs=qblock,
        out_shape=jax.ShapeDtypeStruct(q.shape, BF16),
        scratch_shapes=[pltpu.VMEM((tq, cols), F32)],
        compiler_params=_compiler_params(("arbitrary", "arbitrary", "arbitrary")),
        name="attn_prompt",
    )(q, kb, vb, sg, tri)


def _attn_sample_kernel(q_ref, kn_ref, vn_ref, kc_ref, vc_ref, sg_ref, trit_ref, o_ref, qnt_ref):
    t = q_ref.shape[0]
    past = kc_ref.shape[0] // NUM_GROUPS
    tk = trit_ref.shape[0]
    lanes = NUM_GROUPS * t
    pairs = NUM_GROUPS // 2
    pad = HEAD_DIM
    nt_dims = (((1,), (1,)), ((), ()))

    qnt_ref[...] = jnp.zeros(qnt_ref.shape, BF16)
    for h in range(NUM_GROUPS):
        p, half = divmod(h, 2)
        qnt_ref[p, h * t:(h + 1) * t, half * HEAD_DIM:(half + 1) * HEAD_DIM] = (
            q_ref[:, h * HEAD_DIM:(h + 1) * HEAD_DIM])

    def neg_logits(load_k):
        zn = None
        for p in range(pairs):
            lhs = jnp.concatenate([load_k(2 * p), load_k(2 * p + 1)], axis=1).astype(BF16)
            d = lax.dot_general(lhs, qnt_ref[p], nt_dims, preferred_element_type=F32)
            zn = d if zn is None else zn + d
        return zn

    def attend(zn, load_v, tri_t, carry, mask, accs):
        pieces = _split_bf16(_log2_one_minus_beta(zn, mask), axis=0)
        suffix = jnp.dot(jnp.concatenate([tri_t, tri_t], axis=1), pieces,
                         preferred_element_type=F32)
        w = jnp.exp2(suffix - zn + carry)
        if mask is not None:
            w = jnp.where(mask, w, 0.0)
        w_t = w.T
        accs = [acc + jnp.dot(w_t[h * t:(h + 1) * t, :].astype(BF16), load_v(h),
                              preferred_element_type=F32)
                for h, acc in enumerate(accs)]
        return accs, carry + suffix[0:1, :]

    key_idx = lax.broadcasted_iota(jnp.int32, (pad, lanes), 0)
    query_idx = lax.broadcasted_iota(jnp.int32, (pad, lanes), 1) & (t - 1)
    zero_rows = lambda width: jnp.zeros((pad - t, width), F32)
    zn = neg_logits(lambda h: kn_ref[pl.ds(h, t, stride=NUM_GROUPS), :])
    zn = jnp.concatenate([zn, zero_rows(lanes)], axis=0)
    new_v = lambda h: jnp.concatenate(
        [vn_ref[pl.ds(h, t, stride=NUM_GROUPS), :], zero_rows(HEAD_DIM)], axis=0).astype(BF16)
    accs, carry = attend(zn, new_v, trit_ref[0:pad, 0:pad], jnp.zeros((1, lanes), F32),
                         key_idx < query_idx, [jnp.zeros((t, HEAD_DIM), F32)] * NUM_GROUPS)

    tri_t = trit_ref[...]
    for j in reversed(range(past // tk)):
        rows = lambda ref, h: ref[pl.ds(j * tk * NUM_GROUPS + h, tk, stride=NUM_GROUPS), :]
        zn = neg_logits(lambda h: rows(kc_ref, h))
        accs, carry = attend(zn, lambda h: rows(vc_ref, h).astype(BF16), tri_t, carry, None, accs)

    for h, acc in enumerate(accs):
        hs = slice(h * HEAD_DIM, (h + 1) * HEAD_DIM)
        o_ref[:, hs] = (acc * sg_ref[:, hs].astype(F32)).astype(BF16)


def _attend_sample(q, k_new, v_new, cache_k, cache_v, sg, tri_t, *, batch, t):
    rows = cache_k.shape[1]
    width = q.shape[1]
    assert t & (t - 1) == 0 and NUM_GROUPS * t == tri_t.shape[0]
    rowblock = pl.BlockSpec((t, width), lambda b: (b, 0))
    newblock = pl.BlockSpec((t * NUM_GROUPS, HEAD_DIM), lambda b: (b, 0))
    cache = pl.BlockSpec((None, rows, HEAD_DIM), lambda b: (b, 0, 0))
    return pl.pallas_call(
        _attn_sample_kernel,
        grid=(batch,),
        in_specs=[rowblock, newblock, newblock, cache, cache, rowblock,
                  pl.BlockSpec(tri_t.shape, lambda b: (0, 0))],
        out_specs=rowblock,
        out_shape=jax.ShapeDtypeStruct(q.shape, BF16),
        scratch_shapes=[pltpu.VMEM((NUM_GROUPS // 2, NUM_GROUPS * t, 2 * HEAD_DIM), BF16)],
        compiler_params=_compiler_params(("arbitrary",)),
        name="attn_sample",
    )(q, k_new, v_new, cache_k, cache_v, sg, tri_t)


def _out_kernel(x_ref, msb_ref, mcv_ref, wsb_ref, wcv_ref, g_ref, y_ref):
    y = (x_ref[...]
         + jnp.dot(msb_ref[...], wsb_ref[...], preferred_element_type=F32)
         + jnp.dot(mcv_ref[...], wcv_ref[...], preferred_element_type=F32))
    y_ref[...] = _rmsnorm_rows(y, g_ref[...])


def _merge_out(x2d, mix_sb, mix_cv, w_sb, w_cv, final_g, *, tm):
    m, d = x2d.shape
    half = mix_sb.shape[1]
    rows = lambda width: pl.BlockSpec((tm, width), lambda i: (i, 0))
    whole = lambda shape: pl.BlockSpec(shape, lambda i: (0, 0))
    return pl.pallas_call(
        _out_kernel,
        grid=(m // tm,),
        in_specs=[rows(d), rows(half), rows(half), whole(w_sb.shape), whole(w_cv.shape), whole((1, d))],
        out_specs=rows(d),
        out_shape=jax.ShapeDtypeStruct((m, d), F32),
        compiler_params=_compiler_params(("arbitrary",)),
        name="merge_out",
    )(x2d, mix_sb, mix_cv, w_sb, w_cv, final_g.reshape(1, d))


def _lower_tri(n):
    j = lax.broadcasted_iota(jnp.int32, (n, n), 0)
    s = lax.broadcasted_iota(jnp.int32, (n, n), 1)
    return (j >= s).astype(BF16)


def kernel(x_prompt, x_sample, cache_k, cache_v, state_conv, norm_g, w_in, conv_w, w_out, final_g):
    depth = w_in.shape[0]
    assert depth == 1, "single-layer step"
    bsz, seq, d = x_prompt.shape
    dbsz, dseq, _ = x_sample.shape
    past = cache_k.shape[2]
    width = NUM_GROUPS * HEAD_DIM

    w_r = (w_in[0].reshape(d, NUM_SEGMENTS, NUM_GROUPS, HEAD_DIM)
           .transpose(2, 0, 1, 3).reshape(NUM_GROUPS, d, NUM_SEGMENTS * HEAD_DIM).astype(BF16))
    w_sb = w_out[0, :width].astype(BF16)
    w_cv = w_out[0, width:].astype(BF16)
    tri = _lower_tri(KEY_BLOCK)

    xp = x_prompt.reshape(bsz * seq, d)
    zeros_left = jnp.zeros((bsz, CONV_W - 1, width), F32)
    qp, kp, vp, kbp, vbp, sgp, mcvp, tailp = _project(
        xp, norm_g[0], w_r, conv_w[0], zeros_left, seq_rows=seq, tm=512)
    msbp = _attend_prompt(qp, kbp, vbp, sgp, tri, batch=bsz, seq=seq, tq=KEY_BLOCK,
                          heads_per_step=4)
    yp = _merge_out(xp, msbp, mcvp, w_sb, w_cv, final_g, tm=512)

    xs = x_sample.reshape(dbsz * dseq, d)
    qs, ks, vs, _, _, sgs, mcvs, tails = _project(
        xs, norm_g[0], w_r, conv_w[0], state_conv[0], seq_rows=dseq, tm=512)
    msbs = _attend_sample(qs, ks, vs,
                          cache_k[0].reshape(dbsz, past * NUM_GROUPS, HEAD_DIM),
                          cache_v[0].reshape(dbsz, past * NUM_GROUPS, HEAD_DIM),
                          sgs, tri.T, batch=dbsz, t=dseq)
    ys = _merge_out(xs, msbs, mcvs, w_sb, w_cv, final_g, tm=512)

    heads = lambda a, b_, t_: a.reshape(1, b_, t_, NUM_GROUPS, HEAD_DIM)
    return (yp.reshape(bsz, seq, d), ys.reshape(dbsz, dseq, d),
            heads(kp, bsz, seq), heads(vp, bsz, seq), tailp[None],
            heads(ks, dbsz, dseq), heads(vs, dbsz, dseq), tails[None])
```

```python
import functools
import math

import jax
import jax.numpy as jnp
from jax import lax
from jax.experimental import pallas as pl
from jax.experimental.pallas import tpu as pltpu

F32 = jnp.float32
BF16 = jnp.bfloat16

HEAD_DIM = 128
NUM_GROUPS = 8
NUM_SEGMENTS = 8
CONV_W = 3
EPS = 1e-6
LOG2E = math.log2(math.e)
Q_PRESCALE = -(HEAD_DIM ** -0.5) * LOG2E
KEY_BLOCK = 256
PROJ_ROWS = 1024
OUT_ROWS = 512
PROMPT_HEADS_PER_STEP = 4

VMEM_LIMIT_BYTES = 56 * 1024 * 1024


def _compiler_params(semantics):
    return pltpu.CompilerParams(dimension_semantics=semantics,
                                vmem_limit_bytes=VMEM_LIMIT_BYTES)


def _rmsnorm_rows(x, g):
    r = lax.rsqrt(jnp.mean(x * x, axis=-1, keepdims=True) + EPS)
    return (x * r) * g


def _silu(x):
    return x * (1.0 / (1.0 + jnp.exp(-x)))


def _proj_kernel(x_ref, g_ref, wq_ref, wk_ref, wv_ref, wgs_ref, wb_ref, wc_ref, wu_ref, wgc_ref,
                 cw_ref, left_ref,
                 q_ref, k_ref, v_ref, kb_ref, vb_ref, sg_ref, mcv_ref, tail_ref,
                 hn_ref, carry_ref, *, seq_rows, tiles_per_seq):
    i = pl.program_id(0)
    h = pl.program_id(1)
    tm = x_ref.shape[0]

    @pl.when(h == 0)
    def _():
        hn_ref[...] = _rmsnorm_rows(x_ref[...], g_ref[...]).astype(BF16)

    def project(*w_refs):
        w = jnp.concatenate([r[...] for r in w_refs], axis=1)
        acc = jnp.dot(hn_ref[...], w, preferred_element_type=F32)
        return [acc[:, s * HEAD_DIM:(s + 1) * HEAD_DIM] for s in range(len(w_refs))]

    b, c, u, g_cv = project(wb_ref, wc_ref, wu_ref, wgc_ref)
    q, k, v, g_sb = project(wq_ref, wk_ref, wv_ref, wgs_ref)

    q_ref[...] = (q * Q_PRESCALE).astype(BF16)
    k_ref[pl.ds(h, tm, stride=NUM_GROUPS), :] = k
    v_ref[pl.ds(h, tm, stride=NUM_GROUPS), :] = v
    kb_ref[...] = k.astype(BF16)
    vb_ref[...] = v.astype(BF16)
    sg_ref[...] = _silu(g_sb).astype(BF16)

    cu = c * u
    if tiles_per_seq > 1:
        @pl.when(i % tiles_per_seq == 0)
        def _():
            carry_ref[h, 0:2, :] = left_ref[0]
        left = carry_ref[h][None, 0:2, :]
        carry_ref[h, 0:2, :] = cu[tm - 2:tm, :]
    else:
        left = left_ref[...]
    rows = seq_rows if tiles_per_seq == 1 else tm
    nseq = tm // rows
    expand = lambda a: jnp.broadcast_to(a, (nseq, rows, HEAD_DIM)).reshape(tm, HEAD_DIM)
    l0 = expand(left[:, 0:1, :])
    l1 = expand(left[:, 1:2, :])
    rs = lax.broadcasted_iota(jnp.int32, (tm, HEAD_DIM), 0) & (rows - 1)
    r1 = jnp.where(rs == 0, l1, pltpu.roll(cu, 1, axis=0))
    r2 = jnp.where(rs == 0, l0, jnp.where(rs == 1, l1, pltpu.roll(cu, 2, axis=0)))
    cw = cw_ref[...]
    conv = cw[0:1, :] * r2 + cw[1:2, :] * r1 + cw[2:3, :] * cu
    mcv_ref[...] = (b * conv * _silu(g_cv)).astype(BF16)
    tail_ref[...] = cu.reshape(nseq, rows, HEAD_DIM)[:, rows - 2:rows, :]


def _project(x2d, norm_g, w_bf, conv_w, left, *, seq_rows, tm):
    m, d = x2d.shape
    segment = lambda s: pl.BlockSpec((d, HEAD_DIM), lambda i, h: (0, s * NUM_GROUPS + h))
    width = NUM_GROUPS * HEAD_DIM
    if tm >= seq_rows:
        tiles_per_seq, ns = 1, tm // seq_rows
        left_idx = lambda i, h: (i, 0, h)
    else:
        tiles_per_seq, ns = seq_rows // tm, 1
        left_idx = lambda i, h: (i // tiles_per_seq, 0, h)
    n_tails = (m // tm) * ns
    tile = lambda dt: jax.ShapeDtypeStruct((m, width), dt)
    native = jax.ShapeDtypeStruct((m * NUM_GROUPS, HEAD_DIM), F32)
    col_block = pl.BlockSpec((tm, HEAD_DIM), lambda i, h: (i, h))
    native_block = pl.BlockSpec((tm * NUM_GROUPS, HEAD_DIM), lambda i, h: (i, 0))
    kern = functools.partial(_proj_kernel, seq_rows=seq_rows, tiles_per_seq=tiles_per_seq)
    q, k, v, kb, vb, sg, mcv, tails = pl.pallas_call(
        kern,
        grid=(m // tm, NUM_GROUPS),
        in_specs=[
            pl.BlockSpec((tm, d), lambda i, h: (i, 0)),
            pl.BlockSpec((1, d), lambda i, h: (0, 0)),
            *[segment(s) for s in range(NUM_SEGMENTS)],
            pl.BlockSpec((CONV_W, HEAD_DIM), lambda i, h: (0, h)),
            pl.BlockSpec((ns, 2, HEAD_DIM), left_idx),
        ],
        out_specs=[col_block, native_block, native_block, col_block, col_block, col_block,
                   col_block, pl.BlockSpec((ns, 2, HEAD_DIM), lambda i, h: (i, 0, h))],
        out_shape=[tile(BF16), native, native, tile(BF16), tile(BF16), tile(BF16), tile(BF16),
                   jax.ShapeDtypeStruct((n_tails, 2, width), F32)],
        scratch_shapes=[pltpu.VMEM((tm, d), BF16),
                        pltpu.VMEM((NUM_GROUPS, 8, HEAD_DIM), F32)],
        compiler_params=_compiler_params(("arbitrary", "arbitrary")),
        name="proj",
    )(x2d, norm_g.reshape(1, d), *([w_bf] * NUM_SEGMENTS), conv_w, left)
    tails = tails.reshape(-1, tiles_per_seq, 2, width)[:, tiles_per_seq - 1]
    return q, k, v, kb, vb, sg, mcv, tails


def _log2_one_minus_beta(zn, mask):
    softplus2 = jnp.log(1.0 + jnp.exp2(-jnp.abs(zn))) * LOG2E
    log_1m = jnp.minimum(zn, 0.0) - softplus2
    return log_1m if mask is None else jnp.where(mask, log_1m, 0.0)


def _split_bf16(x, axis):
    hi = x.astype(BF16)
    return jnp.concatenate([hi, (x - hi.astype(F32)).astype(BF16)], axis=axis)


def _block_weights(suffix, zn, mask):
    w = jnp.exp2(suffix - zn)
    return (w if mask is None else jnp.where(mask, w, 0.0)).astype(BF16)


def _attn_prompt_kernel(q_ref, k_ref, v_ref, sg_ref, tri_ref, o_ref, acc_ref):
    qi = pl.program_id(2)
    tq = q_ref.shape[0]
    tk = tri_ref.shape[0]
    heads = q_ref.shape[1] // HEAD_DIM
    tri2 = jnp.concatenate([tri_ref[...]] * 2, axis=0)
    hs = [slice(a * HEAD_DIM, (a + 1) * HEAD_DIM) for a in range(heads)]
    nt_dims = (((1,), (1,)), ((), ()))

    def step(j, carries, mask):
        start = pl.multiple_of(j * tk, tk)
        zns = [lax.dot_general(q_ref[:, s], k_ref[pl.ds(start, tk), s], nt_dims,
                               preferred_element_type=F32) for s in hs]
        pieces = [_split_bf16(_log2_one_minus_beta(zn, mask), axis=1) for zn in zns]
        suffixes = [jnp.dot(p, tri2, preferred_element_type=F32) for p in pieces]
        new = []
        for a, s in enumerate(hs):
            pv = jnp.dot(_block_weights(suffixes[a], zns[a], mask), v_ref[pl.ds(start, tk), s],
                         preferred_element_type=F32)
            if mask is None:
                acc_ref[:, s] += jnp.exp2(carries[a]) * pv
            else:
                acc_ref[:, s] = pv
            new.append(carries[a] + suffixes[a][:, 0:1])
        return tuple(new)

    rows = lax.broadcasted_iota(jnp.int32, (tq, tk), 0)
    cols = lax.broadcasted_iota(jnp.int32, (tq, tk), 1)
    carries = step(qi, (jnp.zeros((tq, 1), F32),) * heads, cols < rows)
    lax.fori_loop(0, qi, lambda jj, cs: step(qi - 1 - jj, cs, None), carries)
    o_ref[...] = (acc_ref[...] * sg_ref[...].astype(F32)).astype(BF16)


def _attend_prompt(q, kb, vb, sg, tri, *, batch, seq, tq, heads_per_step):
    nq = seq // tq
    assert tq == tri.shape[0], "query block and key block share the diagonal mask"
    cols = heads_per_step * HEAD_DIM
    qblock = pl.BlockSpec((tq, cols), lambda b, g, i: (b * nq + i, g))
    kvblock = pl.BlockSpec((seq, cols), lambda b, g, i: (b, g))
    return pl.pallas_call(
        _attn_prompt_kernel,
        grid=(batch, NUM_GROUPS // heads_per_step, nq),
        in_specs=[qblock, kvblock, kvblock, qblock,
                  pl.BlockSpec(tri.shape, lambda b, g, i: (0, 0))],
        out_specs=qblock,
        out_shape=jax.ShapeDtypeStruct(q.shape, BF16),
        scratch_shapes=[pltpu.VMEM((tq, cols), F32)],
        compiler_params=_compiler_params(("arbitrary", "arbitrary", "arbitrary")),
        name="attn_prompt",
    )(q, kb, vb, sg, tri)


def _attn_sample_kernel(q_ref, kn_ref, vn_ref, kc_ref, vc_ref, sg_ref, trit_ref, o_ref, qnt_ref):
    t = q_ref.shape[0]
    past = kc_ref.shape[0] // NUM_GROUPS
    tk = trit_ref.shape[0]
    lanes = NUM_GROUPS * t
    pairs = NUM_GROUPS // 2
    pad = HEAD_DIM
    nt_dims = (((1,), (1,)), ((), ()))

    qnt_ref[...] = jnp.zeros(qnt_ref.shape, BF16)
    for h in range(NUM_GROUPS):
        p, half = divmod(h, 2)
        qnt_ref[p, h * t:(h + 1) * t, half * HEAD_DIM:(half + 1) * HEAD_DIM] = (
            q_ref[:, h * HEAD_DIM:(h + 1) * HEAD_DIM])

    def neg_logits(load_k):
        zn = None
        for p in range(pairs):
            lhs = jnp.concatenate([load_k(2 * p), load_k(2 * p + 1)], axis=1).astype(BF16)
            d = lax.dot_general(lhs, qnt_ref[p], nt_dims, preferred_element_type=F32)
            zn = d if zn is None else zn + d
        return zn

    def attend(zn, load_v, tri_t, carry, mask, accs):
        pieces = _split_bf16(_log2_one_minus_beta(zn, mask), axis=0)
        suffix = jnp.dot(jnp.concatenate([tri_t, tri_t], axis=1), pieces,
                         preferred_element_type=F32)
        w = jnp.exp2(suffix - zn + carry)
        if mask is not None:
            w = jnp.where(mask, w, 0.0)
        w_t = w.T
        accs = [acc + jnp.dot(w_t[h * t:(h + 1) * t, :].astype(BF16), load_v(h),
                              preferred_element_type=F32)
                for h, acc in enumerate(accs)]
        return accs, carry + suffix[0:1, :]

    key_idx = lax.broadcasted_iota(jnp.int32, (pad, lanes), 0)
    query_idx = lax.broadcasted_iota(jnp.int32, (pad, lanes), 1) & (t - 1)
    zero_rows = lambda width: jnp.zeros((pad - t, width), F32)
    zn = neg_logits(lambda h: kn_ref[pl.ds(h, t, stride=NUM_GROUPS), :])
    zn = jnp.concatenate([zn, zero_rows(lanes)], axis=0)
    new_v = lambda h: jnp.concatenate(
        [vn_ref[pl.ds(h, t, stride=NUM_GROUPS), :], zero_rows(HEAD_DIM)], axis=0).astype(BF16)
    accs, carry = attend(zn, new_v, trit_ref[0:pad, 0:pad], jnp.zeros((1, lanes), F32),
                         key_idx < query_idx, [jnp.zeros((t, HEAD_DIM), F32)] * NUM_GROUPS)

    tri_t = trit_ref[...]
    for j in reversed(range(past // tk)):
        rows = lambda ref, h: ref[pl.ds(j * tk * NUM_GROUPS + h, tk, stride=NUM_GROUPS), :]
        zn = neg_logits(lambda h: rows(kc_ref, h))
        accs, carry = attend(zn, lambda h: rows(vc_ref, h).astype(BF16), tri_t, carry, None, accs)

    for h, acc in enumerate(accs):
        hs = slice(h * HEAD_DIM, (h + 1) * HEAD_DIM)
        o_ref[:, hs] = (acc * sg_ref[:, hs].astype(F32)).astype(BF16)


def _attend_sample(q, k_new, v_new, cache_k, cache_v, sg, tri_t, *, batch, t):
    rows = cache_k.shape[1]
    width = q.shape[1]
    assert t & (t - 1) == 0 and NUM_GROUPS * t == tri_t.shape[0]
    rowblock = pl.BlockSpec((t, width), lambda b: (b, 0))
    newblock = pl.BlockSpec((t * NUM_GROUPS, HEAD_DIM), lambda b: (b, 0))
    cache = pl.BlockSpec((None, rows, HEAD_DIM), lambda b: (b, 0, 0))
    return pl.pallas_call(
        _attn_sample_kernel,
        grid=(batch,),
        in_specs=[rowblock, newblock, newblock, cache, cache, rowblock,
                  pl.BlockSpec(tri_t.shape, lambda b: (0, 0))],
        out_specs=rowblock,
        out_shape=jax.ShapeDtypeStruct(q.shape, BF16),
        scratch_shapes=[pltpu.VMEM((NUM_GROUPS // 2, NUM_GROUPS * t, 2 * HEAD_DIM), BF16)],
        compiler_params=_compiler_params(("arbitrary",)),
        name="attn_sample",
    )(q, k_new, v_new, cache_k, cache_v, sg, tri_t)


def _out_kernel(x_ref, msb_ref, mcv_ref, wsb_ref, wcv_ref, g_ref, y_ref):
    y = (x_ref[...]
         + jnp.dot(msb_ref[...], wsb_ref[...], preferred_element_type=F32)
         + jnp.dot(mcv_ref[...], wcv_ref[...], preferred_element_type=F32))
    y_ref[...] = _rmsnorm_rows(y, g_ref[...])


def _merge_out(x2d, mix_sb, mix_cv, w_bf, final_g, *, tm):
    m, d = x2d.shape
    half = mix_sb.shape[1]
    rows = lambda width: pl.BlockSpec((tm, width), lambda i: (i, 0))
    w_rows = lambda r: pl.BlockSpec((half, d), lambda i: (r, 0))
    return pl.pallas_call(
        _out_kernel,
        grid=(m // tm,),
        in_specs=[rows(d), rows(half), rows(half), w_rows(0), w_rows(1),
                  pl.BlockSpec((1, d), lambda i: (0, 0))],
        out_specs=rows(d),
        out_shape=jax.ShapeDtypeStruct((m, d), F32),
        compiler_params=_compiler_params(("arbitrary",)),
        name="merge_out",
    )(x2d, mix_sb, mix_cv, w_bf, w_bf, final_g.reshape(1, d))


def _lower_tri(n):
    j = lax.broadcasted_iota(jnp.int32, (n, n), 0)
    s = lax.broadcasted_iota(jnp.int32, (n, n), 1)
    return (j >= s).astype(BF16)


def kernel(x_prompt, x_sample, cache_k, cache_v, state_conv, norm_g, w_in, conv_w, w_out, final_g):
    depth = w_in.shape[0]
    assert depth == 1, "single-layer step"
    bsz, seq, d = x_prompt.shape
    dbsz, dseq, _ = x_sample.shape
    past = cache_k.shape[2]
    width = NUM_GROUPS * HEAD_DIM

    w_in_bf = w_in[0].astype(BF16)
    w_out_bf = w_out[0].astype(BF16)
    tri = _lower_tri(KEY_BLOCK)

    xp = x_prompt.reshape(bsz * seq, d)
    zeros_left = jnp.zeros((bsz, CONV_W - 1, width), F32)
    qp, kp, vp, kbp, vbp, sgp, mcvp, tailp = _project(
        xp, norm_g[0], w_in_bf, conv_w[0], zeros_left, seq_rows=seq, tm=PROJ_ROWS)
    msbp = _attend_prompt(qp, kbp, vbp, sgp, tri, batch=bsz, seq=seq, tq=KEY_BLOCK,
                          heads_per_step=PROMPT_HEADS_PER_STEP)
    yp = _merge_out(xp, msbp, mcvp, w_out_bf, final_g, tm=OUT_ROWS)

    xs = x_sample.reshape(dbsz * dseq, d)
    qs, ks, vs, _, _, sgs, mcvs, tails = _project(
        xs, norm_g[0], w_in_bf, conv_w[0], state_conv[0], seq_rows=dseq, tm=PROJ_ROWS)
    msbs = _attend_sample(qs, ks, vs,
                          cache_k[0].reshape(dbsz, past * NUM_GROUPS, HEAD_DIM),
                          cache_v[0].reshape(dbsz, past * NUM_GROUPS, HEAD_DIM),
                          sgs, tri.T, batch=dbsz, t=dseq)
    ys = _merge_out(xs, msbs, mcvs, w_out_bf, final_g, tm=OUT_ROWS)

    heads = lambda a, b_, t_: a.reshape(1, b_, t_, NUM_GROUPS, HEAD_DIM)
    return (yp.reshape(bsz, seq, d), ys.reshape(dbsz, dseq, d),
            heads(kp, bsz, seq), heads(vp, bsz, seq), tailp[None],
            heads(ks, dbsz, dseq), heads(vs, dbsz, dseq), tails[None])
```

```python
import functools
import math

import jax
import jax.numpy as jnp
from jax import lax
from jax.experimental import pallas as pl
from jax.experimental.pallas import tpu as pltpu

F32 = jnp.float32
BF16 = jnp.bfloat16

HEAD_DIM = 128
NUM_GROUPS = 8
NUM_SEGMENTS = 8
CONV_W = 3
EPS = 1e-6
LOG2E = math.log2(math.e)
Q_PRESCALE = -(HEAD_DIM ** -0.5) * LOG2E
MXU_DEPTH = 256
KEY_BLOCK = MXU_DEPTH
PROJ_ROWS = 1024
OUT_ROWS = 512
PROMPT_HEADS_PER_STEP = 8

VMEM_LIMIT_BYTES = 56 * 1024 * 1024


def _compiler_params(semantics):
    return pltpu.CompilerParams(dimension_semantics=semantics,
                                vmem_limit_bytes=VMEM_LIMIT_BYTES)


def _rmsnorm_rows(x, g):
    r = lax.rsqrt(jnp.mean(x * x, axis=-1, keepdims=True) + EPS)
    return (x * r) * g


def _silu(x):
    return x * (1.0 / (1.0 + jnp.exp(-x)))


def _proj_kernel(x_ref, g_ref, wq_ref, wk_ref, wv_ref, wgs_ref, wb_ref, wc_ref, wu_ref, wgc_ref,
                 cw_ref, left_ref,
                 q_ref, k_ref, v_ref, kb_ref, vb_ref, sg_ref, mcv_ref, tail_ref,
                 hn_ref, carry_ref, *, seq_rows, tiles_per_seq):
    i = pl.program_id(0)
    h = pl.program_id(1)
    tm = x_ref.shape[0]

    @pl.when(h == 0)
    def _():
        hn_ref[...] = _rmsnorm_rows(x_ref[...], g_ref[...]).astype(BF16)

    def project(*w_refs):
        w = jnp.concatenate([r[...] for r in w_refs], axis=1)
        acc = jnp.dot(hn_ref[...], w, preferred_element_type=F32)
        return [acc[:, s * HEAD_DIM:(s + 1) * HEAD_DIM] for s in range(len(w_refs))]

    b, c, u, g_cv = project(wb_ref, wc_ref, wu_ref, wgc_ref)
    q, k, v, g_sb = project(wq_ref, wk_ref, wv_ref, wgs_ref)

    q_ref[...] = (q * Q_PRESCALE).astype(BF16)
    k_ref[pl.ds(h, tm, stride=NUM_GROUPS), :] = k
    v_ref[pl.ds(h, tm, stride=NUM_GROUPS), :] = v
    kb_ref[...] = k.astype(BF16)
    vb_ref[...] = v.astype(BF16)
    sg_ref[...] = _silu(g_sb).astype(BF16)

    cu = c * u
    if tiles_per_seq > 1:
        @pl.when(i % tiles_per_seq == 0)
        def _():
            carry_ref[h, 0:2, :] = left_ref[0]
        left = carry_ref[h][None, 0:2, :]
        carry_ref[h, 0:2, :] = cu[tm - 2:tm, :]
    else:
        left = left_ref[...]
    rows = seq_rows if tiles_per_seq == 1 else tm
    nseq = tm // rows
    expand = lambda a: jnp.broadcast_to(a, (nseq, rows, HEAD_DIM)).reshape(tm, HEAD_DIM)
    l0 = expand(left[:, 0:1, :])
    l1 = expand(left[:, 1:2, :])
    rs = lax.broadcasted_iota(jnp.int32, (tm, HEAD_DIM), 0) & (rows - 1)
    r1 = jnp.where(rs == 0, l1, pltpu.roll(cu, 1, axis=0))
    r2 = jnp.where(rs == 0, l0, jnp.where(rs == 1, l1, pltpu.roll(cu, 2, axis=0)))
    cw = cw_ref[...]
    conv = cw[0:1, :] * r2 + cw[1:2, :] * r1 + cw[2:3, :] * cu
    mcv_ref[...] = (b * conv * _silu(g_cv)).astype(BF16)
    tail_ref[...] = cu.reshape(nseq, rows, HEAD_DIM)[:, rows - 2:rows, :]


def _project(x2d, norm_g, w_bf, conv_w, left, *, seq_rows, tm):
    m, d = x2d.shape
    segment = lambda s: pl.BlockSpec((d, HEAD_DIM), lambda i, h: (0, s * NUM_GROUPS + h))
    width = NUM_GROUPS * HEAD_DIM
    if tm >= seq_rows:
        tiles_per_seq, ns = 1, tm // seq_rows
        left_idx = lambda i, h: (i, 0, h)
    else:
        tiles_per_seq, ns = seq_rows // tm, 1
        left_idx = lambda i, h: (i // tiles_per_seq, 0, h)
    n_tails = (m // tm) * ns
    tile = lambda dt: jax.ShapeDtypeStruct((m, width), dt)
    native = jax.ShapeDtypeStruct((m * NUM_GROUPS, HEAD_DIM), F32)
    col_block = pl.BlockSpec((tm, HEAD_DIM), lambda i, h: (i, h))
    native_block = pl.BlockSpec((tm * NUM_GROUPS, HEAD_DIM), lambda i, h: (i, 0))
    kern = functools.partial(_proj_kernel, seq_rows=seq_rows, tiles_per_seq=tiles_per_seq)
    q, k, v, kb, vb, sg, mcv, tails = pl.pallas_call(
        kern,
        grid=(m // tm, NUM_GROUPS),
        in_specs=[
            pl.BlockSpec((tm, d), lambda i, h: (i, 0)),
            pl.BlockSpec((1, d), lambda i, h: (0, 0)),
            *[segment(s) for s in range(NUM_SEGMENTS)],
            pl.BlockSpec((CONV_W, HEAD_DIM), lambda i, h: (0, h)),
            pl.BlockSpec((ns, 2, HEAD_DIM), left_idx),
        ],
        out_specs=[col_block, native_block, native_block, col_block, col_block, col_block,
                   col_block, pl.BlockSpec((ns, 2, HEAD_DIM), lambda i, h: (i, 0, h))],
        out_shape=[tile(BF16), native, native, tile(BF16), tile(BF16), tile(BF16), tile(BF16),
                   jax.ShapeDtypeStruct((n_tails, 2, width), F32)],
        scratch_shapes=[pltpu.VMEM((tm, d), BF16),
                        pltpu.VMEM((NUM_GROUPS, 8, HEAD_DIM), F32)],
        compiler_params=_compiler_params(("arbitrary", "arbitrary")),
        name="proj",
    )(x2d, norm_g.reshape(1, d), *([w_bf] * NUM_SEGMENTS), conv_w, left)
    tails = tails.reshape(-1, tiles_per_seq, 2, width)[:, tiles_per_seq - 1]
    return q, k, v, kb, vb, sg, mcv, tails


def _log2_one_minus_beta(zn, mask):
    softplus2 = jnp.log(1.0 + jnp.exp2(-jnp.abs(zn))) * LOG2E
    log_1m = jnp.minimum(zn, 0.0) - softplus2
    return log_1m if mask is None else jnp.where(mask, log_1m, 0.0)


def _split_bf16(x, axis):
    hi = x.astype(BF16)
    return jnp.concatenate([hi, (x - hi.astype(F32)).astype(BF16)], axis=axis)


def _block_weights(suffix, zn, mask):
    w = jnp.exp2(suffix - zn)
    return (w if mask is None else jnp.where(mask, w, 0.0)).astype(BF16)


def _attn_prompt_kernel(q_ref, k_ref, v_ref, sg_ref, tri_ref, o_ref, acc_ref):
    qi = pl.program_id(2)
    tq = q_ref.shape[0]
    tk = tri_ref.shape[0]
    heads = q_ref.shape[1] // HEAD_DIM
    tri2 = jnp.concatenate([tri_ref[...]] * 2, axis=0)
    hs = [slice(a * HEAD_DIM, (a + 1) * HEAD_DIM) for a in range(heads)]
    nt_dims = (((1,), (1,)), ((), ()))

    def rows_of(j):
        return pl.ds(j * tk if isinstance(j, int) else pl.multiple_of(j * tk, tk), tk)

    def logits(j):
        return tuple(lax.dot_general(q_ref[:, s], k_ref[rows_of(j), s], nt_dims,
                                     preferred_element_type=F32) for s in hs)

    def weights(zns, mask):
        pieces = [_split_bf16(_log2_one_minus_beta(zn, mask), axis=1) for zn in zns]
        suffixes = [jnp.dot(p, tri2, preferred_element_type=F32) for p in pieces]
        ws = tuple(_block_weights(sfx, zn, mask) for sfx, zn in zip(suffixes, zns))
        return ws, tuple(sfx[:, 0:1] for sfx in suffixes)

    def accumulate(ws, j, scales):
        for a, s in enumerate(hs):
            acc_ref[:, s] += scales[a] * jnp.dot(ws[a], v_ref[rows_of(j), s],
                                                 preferred_element_type=F32)

    def step(j, carries, mask):
        ws, totals = weights(logits(j), mask)
        accumulate(ws, j, [jnp.exp2(c) for c in carries])
        return tuple(c + tot for c, tot in zip(carries, totals))

    acc_ref[...] = jnp.zeros(acc_ref.shape, F32)
    rows = lax.broadcasted_iota(jnp.int32, (tq, tk), 0)
    cols = lax.broadcasted_iota(jnp.int32, (tq, tk), 1)
    carries = step(qi, (jnp.zeros((tq, 1), F32),) * heads, cols < rows)
    lax.fori_loop(0, qi, lambda t, cs: step(qi - 1 - t, cs, None), carries)
    o_ref[...] = (acc_ref[...] * sg_ref[...].astype(F32)).astype(BF16)


def _attend_prompt(q, kb, vb, sg, tri, *, batch, seq, tq, heads_per_step):
    nq = seq // tq
    assert tq == tri.shape[0], "query block and key block share the diagonal mask"
    cols = heads_per_step * HEAD_DIM
    qblock = pl.BlockSpec((tq, cols), lambda b, g, i: (b * nq + i, g))
    kvblock = pl.BlockSpec((seq, cols), lambda b, g, i: (b, g))
    return pl.pallas_call(
        _attn_prompt_kernel,
        grid=(batch, NUM_GROUPS // heads_per_step, nq),
        in_specs=[qblock, kvblock, kvblock, qblock,
                  pl.BlockSpec(tri.shape, lambda b, g, i: (0, 0))],
        out_specs=qblock,
        out_shape=jax.ShapeDtypeStruct(q.shape, BF16),
        scratch_shapes=[pltpu.VMEM((tq, cols), F32)],
        compiler_params=_compiler_params(("arbitrary", "arbitrary", "arbitrary")),
        name="attn_prompt",
    )(q, kb, vb, sg, tri)


def _attn_sample_kernel(q_ref, kn_ref, vn_ref, kc_ref, vc_ref, sg_ref, trit_ref, o_ref, qnt_ref):
    t = q_ref.shape[0]
    past = kc_ref.shape[0] // NUM_GROUPS
    tk = trit_ref.shape[0]
    lanes = NUM_GROUPS * t
    pairs = NUM_GROUPS // 2
    pad = HEAD_DIM
    nt_dims = (((1,), (1,)), ((), ()))

    qnt_ref[...] = jnp.zeros(qnt_ref.shape, BF16)
    for h in range(NUM_GROUPS):
        p, half = divmod(h, 2)
        qnt_ref[p, h * t:(h + 1) * t, half * HEAD_DIM:(half + 1) * HEAD_DIM] = (
            q_ref[:, h * HEAD_DIM:(h + 1) * HEAD_DIM])

    def neg_logits(load_k):
        zn = None
        for p in range(pairs):
            lhs = jnp.concatenate([load_k(2 * p), load_k(2 * p + 1)], axis=1).astype(BF16)
            d = lax.dot_general(lhs, qnt_ref[p], nt_dims, preferred_element_type=F32)
            zn = d if zn is None else zn + d
        return zn

    key_idx = lax.broadcasted_iota(jnp.int32, (pad, lanes), 0)
    query_idx = lax.broadcasted_iota(jnp.int32, (pad, lanes), 1) & (t - 1)
    zero_rows = lambda width: jnp.zeros((pad - t, width), F32)
    cache_rows = lambda ref, j, h: ref[pl.ds(j * tk * NUM_GROUPS + h, tk, stride=NUM_GROUPS), :]
    cache_blocks = list(reversed(range(past // tk)))

    zns = [jnp.concatenate([neg_logits(lambda h: kn_ref[pl.ds(h, t, stride=NUM_GROUPS), :]),
                            zero_rows(lanes)], axis=0)]
    zns += [neg_logits(lambda h, j=j: cache_rows(kc_ref, j, h)) for j in cache_blocks]
    masks = [key_idx < query_idx] + [None] * len(cache_blocks)
    load_vs = [lambda h: jnp.concatenate([vn_ref[pl.ds(h, t, stride=NUM_GROUPS), :],
                                          zero_rows(HEAD_DIM)], axis=0)]
    load_vs += [lambda h, j=j: cache_rows(vc_ref, j, h) for j in cache_blocks]

    tri_t = trit_ref[...]
    tris = [jnp.concatenate([trit_ref[0:pad, 0:pad]] * 2, axis=1)]
    tris += [jnp.concatenate([tri_t, tri_t], axis=1)] * len(cache_blocks)
    pieces = [_split_bf16(_log2_one_minus_beta(zn, m), axis=0) for zn, m in zip(zns, masks)]
    suffixes = [jnp.dot(tr, p, preferred_element_type=F32) for tr, p in zip(tris, pieces)]

    accs = [jnp.zeros((t, HEAD_DIM), F32)] * NUM_GROUPS
    carry = jnp.zeros((1, lanes), F32)
    for zn, m, sfx, load_v in zip(zns, masks, suffixes, load_vs):
        w = jnp.exp2(sfx - zn + carry)
        if m is not None:
            w = jnp.where(m, w, 0.0)
        w_t = w.T
        accs = [acc + jnp.dot(w_t[h * t:(h + 1) * t, :].astype(BF16), load_v(h).astype(BF16),
                              preferred_element_type=F32)
                for h, acc in enumerate(accs)]
        carry = carry + sfx[0:1, :]

    for h, acc in enumerate(accs):
        hs = slice(h * HEAD_DIM, (h + 1) * HEAD_DIM)
        o_ref[:, hs] = (acc * sg_ref[:, hs].astype(F32)).astype(BF16)


def _attend_sample(q, k_new, v_new, cache_k, cache_v, sg, tri_t, *, batch, t):
    rows = cache_k.shape[1]
    width = q.shape[1]
    assert t & (t - 1) == 0 and NUM_GROUPS * t == tri_t.shape[0]
    rowblock = pl.BlockSpec((t, width), lambda b: (b, 0))
    newblock = pl.BlockSpec((t * NUM_GROUPS, HEAD_DIM), lambda b: (b, 0))
    cache = pl.BlockSpec((None, rows, HEAD_DIM), lambda b: (b, 0, 0))
    return pl.pallas_call(
        _attn_sample_kernel,
        grid=(batch,),
        in_specs=[rowblock, newblock, newblock, cache, cache, rowblock,
                  pl.BlockSpec(tri_t.shape, lambda b: (0, 0))],
        out_specs=rowblock,
        out_shape=jax.ShapeDtypeStruct(q.shape, BF16),
        scratch_shapes=[pltpu.VMEM((NUM_GROUPS // 2, NUM_GROUPS * t, 2 * HEAD_DIM), BF16)],
        compiler_params=_compiler_params(("arbitrary",)),
        name="attn_sample",
    )(q, k_new, v_new, cache_k, cache_v, sg, tri_t)


def _out_kernel(x_ref, msb_ref, mcv_ref, wsb_ref, wcv_ref, g_ref, y_ref):
    y = (x_ref[...]
         + jnp.dot(msb_ref[...], wsb_ref[...], preferred_element_type=F32)
         + jnp.dot(mcv_ref[...], wcv_ref[...], preferred_element_type=F32))
    y_ref[...] = _rmsnorm_rows(y, g_ref[...])


def _merge_out(x2d, mix_sb, mix_cv, w_bf, final_g, *, tm):
    m, d = x2d.shape
    half = mix_sb.shape[1]
    rows = lambda width: pl.BlockSpec((tm, width), lambda i: (i, 0))
    w_rows = lambda r: pl.BlockSpec((half, d), lambda i: (r, 0))
    return pl.pallas_call(
        _out_kernel,
        grid=(m // tm,),
        in_specs=[rows(d), rows(half), rows(half), w_rows(0), w_rows(1),
                  pl.BlockSpec((1, d), lambda i: (0, 0))],
        out_specs=rows(d),
        out_shape=jax.ShapeDtypeStruct((m, d), F32),
        compiler_params=_compiler_params(("arbitrary",)),
        name="merge_out",
    )(x2d, mix_sb, mix_cv, w_bf, w_bf, final_g.reshape(1, d))


def _lower_tri(n):
    j = lax.broadcasted_iota(jnp.int32, (n, n), 0)
    s = lax.broadcasted_iota(jnp.int32, (n, n), 1)
    return (j >= s).astype(BF16)


def kernel(x_prompt, x_sample, cache_k, cache_v, state_conv, norm_g, w_in, conv_w, w_out, final_g):
    depth = w_in.shape[0]
    assert depth == 1, "single-layer step"
    bsz, seq, d = x_prompt.shape
    dbsz, dseq, _ = x_sample.shape
    past = cache_k.shape[2]
    width = NUM_GROUPS * HEAD_DIM

    w_in_bf = w_in[0].astype(BF16)
    w_out_bf = w_out[0].astype(BF16)
    tri = _lower_tri(KEY_BLOCK)

    xp = x_prompt.reshape(bsz * seq, d)
    zeros_left = jnp.zeros((bsz, CONV_W - 1, width), F32)
    qp, kp, vp, kbp, vbp, sgp, mcvp, tailp = _project(
        xp, norm_g[0], w_in_bf, conv_w[0], zeros_left, seq_rows=seq, tm=PROJ_ROWS)
    msbp = _attend_prompt(qp, kbp, vbp, sgp, tri, batch=bsz, seq=seq, tq=KEY_BLOCK,
                          heads_per_step=PROMPT_HEADS_PER_STEP)
    yp = _merge_out(xp, msbp, mcvp, w_out_bf, final_g, tm=OUT_ROWS)

    xs = x_sample.reshape(dbsz * dseq, d)
    qs, ks, vs, _, _, sgs, mcvs, tails = _project(
        xs, norm_g[0], w_in_bf, conv_w[0], state_conv[0], seq_rows=dseq, tm=PROJ_ROWS)
    msbs = _attend_sample(qs, ks, vs,
                          cache_k[0].reshape(dbsz, past * NUM_GROUPS, HEAD_DIM),
                          cache_v[0].reshape(dbsz, past * NUM_GROUPS, HEAD_DIM),
                          sgs, tri.T, batch=dbsz, t=dseq)
    ys = _merge_out(xs, msbs, mcvs, w_out_bf, final_g, tm=OUT_ROWS)

    heads = lambda a, b_, t_: a.reshape(1, b_, t_, NUM_GROUPS, HEAD_DIM)
    return (yp.reshape(bsz, seq, d), ys.reshape(dbsz, dseq, d),
            heads(kp, bsz, seq), heads(vp, bsz, seq), tailp[None],
            heads(ks, dbsz, dseq), heads(vs, dbsz, dseq), tails[None])
```

```python
import functools
import math

import jax
import jax.numpy as jnp
from jax import lax
from jax.experimental import pallas as pl
from jax.experimental.pallas import tpu as pltpu

F32 = jnp.float32
BF16 = jnp.bfloat16

HEAD_DIM = 128
NUM_GROUPS = 8
NUM_SEGMENTS = 8
CONV_W = 3
EPS = 1e-6
LOG2E = math.log2(math.e)
Q_PRESCALE = -(HEAD_DIM ** -0.5) * LOG2E
MXU_DEPTH = 256
KEY_BLOCK = MXU_DEPTH
PROJ_ROWS = 1024
OUT_ROWS = 512
PROMPT_HEADS_PER_STEP = 8
EXP2_UNDERFLOW = -160.0

VMEM_LIMIT_BYTES = 56 * 1024 * 1024


def _compiler_params(semantics):
    return pltpu.CompilerParams(dimension_semantics=semantics,
                                vmem_limit_bytes=VMEM_LIMIT_BYTES)


def _rmsnorm_rows(x, g):
    r = lax.rsqrt(jnp.mean(x * x, axis=-1, keepdims=True) + EPS)
    return (x * r) * g


def _silu(x):
    return x * (1.0 / (1.0 + jnp.exp(-x)))


def _proj_kernel(x_ref, g_ref, wq_ref, wk_ref, wv_ref, wgs_ref, wb_ref, wc_ref, wu_ref, wgc_ref,
                 cw_ref, left_ref,
                 q_ref, k_ref, v_ref, kb_ref, vb_ref, sg_ref, mcv_ref, tail_ref,
                 hn_ref, carry_ref, *, seq_rows, tiles_per_seq):
    i = pl.program_id(0)
    h = pl.program_id(1)
    tm = x_ref.shape[0]

    @pl.when(h == 0)
    def _():
        hn_ref[...] = _rmsnorm_rows(x_ref[...], g_ref[...]).astype(BF16)

    def project(*w_refs):
        w = jnp.concatenate([r[...] for r in w_refs], axis=1)
        acc = jnp.dot(hn_ref[...], w, preferred_element_type=F32)
        return [acc[:, s * HEAD_DIM:(s + 1) * HEAD_DIM] for s in range(len(w_refs))]

    b, c, u, g_cv = project(wb_ref, wc_ref, wu_ref, wgc_ref)
    q, k, v, g_sb = project(wq_ref, wk_ref, wv_ref, wgs_ref)

    q_ref[...] = (q * Q_PRESCALE).astype(BF16)
    k_ref[pl.ds(h, tm, stride=NUM_GROUPS), :] = k
    v_ref[pl.ds(h, tm, stride=NUM_GROUPS), :] = v
    kb_ref[...] = k.astype(BF16)
    vb_ref[...] = v.astype(BF16)
    sg_ref[...] = _silu(g_sb).astype(BF16)

    cu = c * u
    if tiles_per_seq > 1:
        @pl.when(i % tiles_per_seq == 0)
        def _():
            carry_ref[h, 0:2, :] = left_ref[0]
        left = carry_ref[h][None, 0:2, :]
        carry_ref[h, 0:2, :] = cu[tm - 2:tm, :]
    else:
        left = left_ref[...]
    rows = seq_rows if tiles_per_seq == 1 else tm
    nseq = tm // rows
    expand = lambda a: jnp.broadcast_to(a, (nseq, rows, HEAD_DIM)).reshape(tm, HEAD_DIM)
    l0 = expand(left[:, 0:1, :])
    l1 = expand(left[:, 1:2, :])
    rs = lax.broadcasted_iota(jnp.int32, (tm, HEAD_DIM), 0) & (rows - 1)
    r1 = jnp.where(rs == 0, l1, pltpu.roll(cu, 1, axis=0))
    r2 = jnp.where(rs == 0, l0, jnp.where(rs == 1, l1, pltpu.roll(cu, 2, axis=0)))
    cw = cw_ref[...]
    conv = cw[0:1, :] * r2 + cw[1:2, :] * r1 + cw[2:3, :] * cu
    mcv_ref[...] = (b * conv * _silu(g_cv)).astype(BF16)
    tail_ref[...] = cu.reshape(nseq, rows, HEAD_DIM)[:, rows - 2:rows, :]


def _project(x2d, norm_g, w_bf, conv_w, left, *, seq_rows, tm):
    m, d = x2d.shape
    segment = lambda s: pl.BlockSpec((d, HEAD_DIM), lambda i, h: (0, s * NUM_GROUPS + h))
    width = NUM_GROUPS * HEAD_DIM
    if tm >= seq_rows:
        tiles_per_seq, ns = 1, tm // seq_rows
        left_idx = lambda i, h: (i, 0, h)
    else:
        tiles_per_seq, ns = seq_rows // tm, 1
        left_idx = lambda i, h: (i // tiles_per_seq, 0, h)
    n_tails = (m // tm) * ns
    tile = lambda dt: jax.ShapeDtypeStruct((m, width), dt)
    native = jax.ShapeDtypeStruct((m * NUM_GROUPS, HEAD_DIM), F32)
    col_block = pl.BlockSpec((tm, HEAD_DIM), lambda i, h: (i, h))
    native_block = pl.BlockSpec((tm * NUM_GROUPS, HEAD_DIM), lambda i, h: (i, 0))
    kern = functools.partial(_proj_kernel, seq_rows=seq_rows, tiles_per_seq=tiles_per_seq)
    q, k, v, kb, vb, sg, mcv, tails = pl.pallas_call(
        kern,
        grid=(m // tm, NUM_GROUPS),
        in_specs=[
            pl.BlockSpec((tm, d), lambda i, h: (i, 0)),
            pl.BlockSpec((1, d), lambda i, h: (0, 0)),
            *[segment(s) for s in range(NUM_SEGMENTS)],
            pl.BlockSpec((CONV_W, HEAD_DIM), lambda i, h: (0, h)),
            pl.BlockSpec((ns, 2, HEAD_DIM), left_idx),
        ],
        out_specs=[col_block, native_block, native_block, col_block, col_block, col_block,
                   col_block, pl.BlockSpec((ns, 2, HEAD_DIM), lambda i, h: (i, 0, h))],
        out_shape=[tile(BF16), native, native, tile(BF16), tile(BF16), tile(BF16), tile(BF16),
                   jax.ShapeDtypeStruct((n_tails, 2, width), F32)],
        scratch_shapes=[pltpu.VMEM((tm, d), BF16),
                        pltpu.VMEM((NUM_GROUPS, 8, HEAD_DIM), F32)],
        compiler_params=_compiler_params(("arbitrary", "arbitrary")),
        name="proj",
    )(x2d, norm_g.reshape(1, d), *([w_bf] * NUM_SEGMENTS), conv_w, left)
    tails = tails.reshape(-1, tiles_per_seq, 2, width)[:, tiles_per_seq - 1]
    return q, k, v, kb, vb, sg, mcv, tails


def _log2_one_minus_beta(zn, mask):
    softplus2 = jnp.log(1.0 + jnp.exp2(-jnp.abs(zn))) * LOG2E
    log_1m = jnp.minimum(zn, 0.0) - softplus2
    return log_1m if mask is None else jnp.where(mask, log_1m, 0.0)


def _split_bf16(x, axis):
    hi = x.astype(BF16)
    return jnp.concatenate([hi, (x - hi.astype(F32)).astype(BF16)], axis=axis)


def _block_weights(suffix, zn, mask):
    w = jnp.exp2(suffix - zn)
    return (w if mask is None else jnp.where(mask, w, 0.0)).astype(BF16)


def _attn_prompt_kernel(q_ref, k_ref, v_ref, sg_ref, tri_ref, o_ref, acc_ref):
    qi = pl.program_id(2)
    tq = q_ref.shape[0]
    tk = tri_ref.shape[0]
    heads = q_ref.shape[1] // HEAD_DIM
    tri2 = jnp.concatenate([tri_ref[...]] * 2, axis=0)
    hs = [slice(a * HEAD_DIM, (a + 1) * HEAD_DIM) for a in range(heads)]
    nt_dims = (((1,), (1,)), ((), ()))

    def rows_of(j):
        return pl.ds(j * tk if isinstance(j, int) else pl.multiple_of(j * tk, tk), tk)

    def logits(j):
        return tuple(lax.dot_general(q_ref[:, s], k_ref[rows_of(j), s], nt_dims,
                                     preferred_element_type=F32) for s in hs)

    def weights(zns, mask):
        pieces = [_split_bf16(_log2_one_minus_beta(zn, mask), axis=1) for zn in zns]
        suffixes = [jnp.dot(p, tri2, preferred_element_type=F32) for p in pieces]
        ws = tuple(_block_weights(sfx, zn, mask) for sfx, zn in zip(suffixes, zns))
        return ws, tuple(sfx[:, 0:1] for sfx in suffixes)

    def accumulate(ws, j, scales):
        for a, s in enumerate(hs):
            acc_ref[:, s] += scales[a] * jnp.dot(ws[a], v_ref[rows_of(j), s],
                                                 preferred_element_type=F32)

    def step(j, carries, mask):
        ws, totals = weights(logits(j), mask)
        accumulate(ws, j, [jnp.exp2(c) for c in carries])
        return tuple(c + tot for c, tot in zip(carries, totals))

    acc_ref[...] = jnp.zeros(acc_ref.shape, F32)
    rows = lax.broadcasted_iota(jnp.int32, (tq, tk), 0)
    cols = lax.broadcasted_iota(jnp.int32, (tq, tk), 1)
    carries = step(qi, (jnp.zeros((tq, 1), F32),) * heads, cols < rows)

    def any_live(carries):
        return jnp.max(functools.reduce(jnp.maximum, carries)) > EXP2_UNDERFLOW

    def body(state):
        t, _, carries = state
        carries = step(qi - 1 - t, carries, None)
        return t + 1, any_live(carries), carries

    lax.while_loop(lambda state: jnp.logical_and(state[0] < qi, state[1]), body,
                   (jnp.int32(0), any_live(carries), carries))
    o_ref[...] = (acc_ref[...] * sg_ref[...].astype(F32)).astype(BF16)


def _attend_prompt(q, kb, vb, sg, tri, *, batch, seq, tq, heads_per_step):
    nq = seq // tq
    assert tq == tri.shape[0], "query block and key block share the diagonal mask"
    cols = heads_per_step * HEAD_DIM
    qblock = pl.BlockSpec((tq, cols), lambda b, g, i: (b * nq + i, g))
    kvblock = pl.BlockSpec((seq, cols), lambda b, g, i: (b, g))
    return pl.pallas_call(
        _attn_prompt_kernel,
        grid=(batch, NUM_GROUPS // heads_per_step, nq),
        in_specs=[qblock, kvblock, kvblock, qblock,
                  pl.BlockSpec(tri.shape, lambda b, g, i: (0, 0))],
        out_specs=qblock,
        out_shape=jax.ShapeDtypeStruct(q.shape, BF16),
        scratch_shapes=[pltpu.VMEM((tq, cols), F32)],
        compiler_params=_compiler_params(("arbitrary", "arbitrary", "arbitrary")),
        name="attn_prompt",
    )(q, kb, vb, sg, tri)


def _attn_sample_kernel(q_ref, kn_ref, vn_ref, kc_ref, vc_ref, sg_ref, trit_ref, o_ref, qnt_ref):
    t = q_ref.shape[0]
    past = kc_ref.shape[0] // NUM_GROUPS
    tk = trit_ref.shape[0]
    lanes = NUM_GROUPS * t
    pairs = NUM_GROUPS // 2
    pad = HEAD_DIM
    nt_dims = (((1,), (1,)), ((), ()))

    qnt_ref[...] = jnp.zeros(qnt_ref.shape, BF16)
    for h in range(NUM_GROUPS):
        p, half = divmod(h, 2)
        qnt_ref[p, h * t:(h + 1) * t, half * HEAD_DIM:(half + 1) * HEAD_DIM] = (
            q_ref[:, h * HEAD_DIM:(h + 1) * HEAD_DIM])

    def neg_logits(load_k):
        zn = None
        for p in range(pairs):
            lhs = jnp.concatenate([load_k(2 * p), load_k(2 * p + 1)], axis=1).astype(BF16)
            d = lax.dot_general(lhs, qnt_ref[p], nt_dims, preferred_element_type=F32)
            zn = d if zn is None else zn + d
        return zn

    key_idx = lax.broadcasted_iota(jnp.int32, (pad, lanes), 0)
    query_idx = lax.broadcasted_iota(jnp.int32, (pad, lanes), 1) & (t - 1)
    zero_rows = lambda width: jnp.zeros((pad - t, width), F32)
    cache_rows = lambda ref, j, h: ref[pl.ds(j * tk * NUM_GROUPS + h, tk, stride=NUM_GROUPS), :]
    cache_blocks = list(reversed(range(past // tk)))

    zns = [jnp.concatenate([neg_logits(lambda h: kn_ref[pl.ds(h, t, stride=NUM_GROUPS), :]),
                            zero_rows(lanes)], axis=0)]
    zns += [neg_logits(lambda h, j=j: cache_rows(kc_ref, j, h)) for j in cache_blocks]
    masks = [key_idx < query_idx] + [None] * len(cache_blocks)
    load_vs = [lambda h: jnp.concatenate([vn_ref[pl.ds(h, t, stride=NUM_GROUPS), :],
                                          zero_rows(HEAD_DIM)], axis=0)]
    load_vs += [lambda h, j=j: cache_rows(vc_ref, j, h) for j in cache_blocks]

    tri_t = trit_ref[...]
    tris = [jnp.concatenate([trit_ref[0:pad, 0:pad]] * 2, axis=1)]
    tris += [jnp.concatenate([tri_t, tri_t], axis=1)] * len(cache_blocks)
    pieces = [_split_bf16(_log2_one_minus_beta(zn, m), axis=0) for zn, m in zip(zns, masks)]
    suffixes = [jnp.dot(tr, p, preferred_element_type=F32) for tr, p in zip(tris, pieces)]

    accs = [jnp.zeros((t, HEAD_DIM), F32)] * NUM_GROUPS
    carry = jnp.zeros((1, lanes), F32)
    for zn, m, sfx, load_v in zip(zns, masks, suffixes, load_vs):
        w = jnp.exp2(sfx - zn + carry)
        if m is not None:
            w = jnp.where(m, w, 0.0)
        w_t = w.T
        accs = [acc + jnp.dot(w_t[h * t:(h + 1) * t, :].astype(BF16), load_v(h).astype(BF16),
                              preferred_element_type=F32)
                for h, acc in enumerate(accs)]
        carry = carry + sfx[0:1, :]

    for h, acc in enumerate(accs):
        hs = slice(h * HEAD_DIM, (h + 1) * HEAD_DIM)
        o_ref[:, hs] = (acc * sg_ref[:, hs].astype(F32)).astype(BF16)


def _attend_sample(q, k_new, v_new, cache_k, cache_v, sg, tri_t, *, batch, t):
    rows = cache_k.shape[1]
    width = q.shape[1]
    assert t & (t - 1) == 0 and NUM_GROUPS * t == tri_t.shape[0]
    rowblock = pl.BlockSpec((t, width), lambda b: (b, 0))
    newblock = pl.BlockSpec((t * NUM_GROUPS, HEAD_DIM), lambda b: (b, 0))
    cache = pl.BlockSpec((None, rows, HEAD_DIM), lambda b: (b, 0, 0))
    return pl.pallas_call(
        _attn_sample_kernel,
        grid=(batch,),
        in_specs=[rowblock, newblock, newblock, cache, cache, rowblock,
                  pl.BlockSpec(tri_t.shape, lambda b: (0, 0))],
        out_specs=rowblock,
        out_shape=jax.ShapeDtypeStruct(q.shape, BF16),
        scratch_shapes=[pltpu.VMEM((NUM_GROUPS // 2, NUM_GROUPS * t, 2 * HEAD_DIM), BF16)],
        compiler_params=_compiler_params(("arbitrary",)),
        name="attn_sample",
    )(q, k_new, v_new, cache_k, cache_v, sg, tri_t)


def _out_kernel(x_ref, msb_ref, mcv_ref, wsb_ref, wcv_ref, g_ref, y_ref):
    y = (x_ref[...]
         + jnp.dot(msb_ref[...], wsb_ref[...], preferred_element_type=F32)
         + jnp.dot(mcv_ref[...], wcv_ref[...], preferred_element_type=F32))
    y_ref[...] = _rmsnorm_rows(y, g_ref[...])


def _merge_out(x2d, mix_sb, mix_cv, w_bf, final_g, *, tm):
    m, d = x2d.shape
    half = mix_sb.shape[1]
    rows = lambda width: pl.BlockSpec((tm, width), lambda i: (i, 0))
    w_rows = lambda r: pl.BlockSpec((half, d), lambda i: (r, 0))
    return pl.pallas_call(
        _out_kernel,
        grid=(m // tm,),
        in_specs=[rows(d), rows(half), rows(half), w_rows(0), w_rows(1),
                  pl.BlockSpec((1, d), lambda i: (0, 0))],
        out_specs=rows(d),
        out_shape=jax.ShapeDtypeStruct((m, d), F32),
        compiler_params=_compiler_params(("arbitrary",)),
        name="merge_out",
    )(x2d, mix_sb, mix_cv, w_bf, w_bf, final_g.reshape(1, d))


def _lower_tri(n):
    j = lax.broadcasted_iota(jnp.int32, (n, n), 0)
    s = lax.broadcasted_iota(jnp.int32, (n, n), 1)
    return (j >= s).astype(BF16)


def kernel(x_prompt, x_sample, cache_k, cache_v, state_conv, norm_g, w_in, conv_w, w_out, final_g):
    depth = w_in.shape[0]
    assert depth == 1, "single-layer step"
    bsz, seq, d = x_prompt.shape
    dbsz, dseq, _ = x_sample.shape
    past = cache_k.shape[2]
    width = NUM_GROUPS * HEAD_DIM

    w_in_bf = w_in[0].astype(BF16)
    w_out_bf = w_out[0].astype(BF16)
    tri = _lower_tri(KEY_BLOCK)

    xp = x_prompt.reshape(bsz * seq, d)
    zeros_left = jnp.zeros((bsz, CONV_W - 1, width), F32)
    qp, kp, vp, kbp, vbp, sgp, mcvp, tailp = _project(
        xp, norm_g[0], w_in_bf, conv_w[0], zeros_left, seq_rows=seq, tm=PROJ_ROWS)
    msbp = _attend_prompt(qp, kbp, vbp, sgp, tri, batch=bsz, seq=seq, tq=KEY_BLOCK,
                          heads_per_step=PROMPT_HEADS_PER_STEP)
    yp = _merge_out(xp, msbp, mcvp, w_out_bf, final_g, tm=OUT_ROWS)

    xs = x_sample.reshape(dbsz * dseq, d)
    qs, ks, vs, _, _, sgs, mcvs, tails = _project(
        xs, norm_g[0], w_in_bf, conv_w[0], state_conv[0], seq_rows=dseq, tm=PROJ_ROWS)
    msbs = _attend_sample(qs, ks, vs,
                          cache_k[0].reshape(dbsz, past * NUM_GROUPS, HEAD_DIM),
                          cache_v[0].reshape(dbsz, past * NUM_GROUPS, HEAD_DIM),
                          sgs, tri.T, batch=dbsz, t=dseq)
    ys = _merge_out(xs, msbs, mcvs, w_out_bf, final_g, tm=OUT_ROWS)

    heads = lambda a, b_, t_: a.reshape(1, b_, t_, NUM_GROUPS, HEAD_DIM)
    return (yp.reshape(bsz, seq, d), ys.reshape(dbsz, dseq, d),
            heads(kp, bsz, seq), heads(vp, bsz, seq), tailp[None],
            heads(ks, dbsz, dseq), heads(vs, dbsz, dseq), tails[None])
```

```python
import functools
import math

import jax
import jax.numpy as jnp
from jax import lax
from jax.experimental import pallas as pl
from jax.experimental.pallas import tpu as pltpu

F32 = jnp.float32
BF16 = jnp.bfloat16

HEAD_DIM = 128
NUM_GROUPS = 8
NUM_SEGMENTS = 8
CONV_W = 3
EPS = 1e-6
LOG2E = math.log2(math.e)
Q_PRESCALE = -(HEAD_DIM ** -0.5) * LOG2E
MXU_DEPTH = 256
KEY_BLOCK = MXU_DEPTH
PROJ_ROWS = 1024
OUT_ROWS = 512
PROMPT_HEADS_PER_STEP = 8
EXP2_UNDERFLOW = -160.0

VMEM_LIMIT_BYTES = 56 * 1024 * 1024


def _compiler_params(semantics):
    return pltpu.CompilerParams(dimension_semantics=semantics,
                                vmem_limit_bytes=VMEM_LIMIT_BYTES)


def _rmsnorm_rows(x, g):
    r = lax.rsqrt(jnp.mean(x * x, axis=-1, keepdims=True) + EPS)
    return (x * r) * g


def _silu(x):
    return x * (1.0 / (1.0 + jnp.exp(-x)))


def _proj_kernel(x_ref, g_ref, wq_ref, wk_ref, wv_ref, wgs_ref, wb_ref, wc_ref, wu_ref, wgc_ref,
                 cw_ref, left_ref,
                 q_ref, k_ref, v_ref, kb_ref, vb_ref, sg_ref, mcv_ref, tail_ref,
                 hn_ref, carry_ref, *, seq_rows, tiles_per_seq):
    i = pl.program_id(0)
    h = pl.program_id(1)
    tm = x_ref.shape[0]

    @pl.when(h == 0)
    def _():
        hn_ref[...] = _rmsnorm_rows(x_ref[...], g_ref[...]).astype(BF16)

    def project(*w_refs):
        w = jnp.concatenate([r[...] for r in w_refs], axis=1)
        acc = jnp.dot(hn_ref[...], w, preferred_element_type=F32)
        return [acc[:, s * HEAD_DIM:(s + 1) * HEAD_DIM] for s in range(len(w_refs))]

    b, c, u, g_cv = project(wb_ref, wc_ref, wu_ref, wgc_ref)
    q, k, v, g_sb = project(wq_ref, wk_ref, wv_ref, wgs_ref)

    q_ref[...] = (q * Q_PRESCALE).astype(BF16)
    k_ref[pl.ds(h, tm, stride=NUM_GROUPS), :] = k
    v_ref[pl.ds(h, tm, stride=NUM_GROUPS), :] = v
    kb_ref[...] = k.astype(BF16)
    vb_ref[...] = v.astype(BF16)
    sg_ref[...] = _silu(g_sb).astype(BF16)

    cu = c * u
    if tiles_per_seq > 1:
        @pl.when(i % tiles_per_seq == 0)
        def _():
            carry_ref[h, 0:2, :] = left_ref[0]
        left = carry_ref[h][None, 0:2, :]
        carry_ref[h, 0:2, :] = cu[tm - 2:tm, :]
    else:
        left = left_ref[...]
    rows = seq_rows if tiles_per_seq == 1 else tm
    nseq = tm // rows
    expand = lambda a: jnp.broadcast_to(a, (nseq, rows, HEAD_DIM)).reshape(tm, HEAD_DIM)
    l0 = expand(left[:, 0:1, :])
    l1 = expand(left[:, 1:2, :])
    rs = lax.broadcasted_iota(jnp.int32, (tm, HEAD_DIM), 0) & (rows - 1)
    r1 = jnp.where(rs == 0, l1, pltpu.roll(cu, 1, axis=0))
    r2 = jnp.where(rs == 0, l0, jnp.where(rs == 1, l1, pltpu.roll(cu, 2, axis=0)))
    cw = cw_ref[...]
    conv = cw[0:1, :] * r2 + cw[1:2, :] * r1 + cw[2:3, :] * cu
    mcv_ref[...] = (b * conv * _silu(g_cv)).astype(BF16)
    tail_ref[...] = cu.reshape(nseq, rows, HEAD_DIM)[:, rows - 2:rows, :]


def _project(x2d, norm_g, w_bf, conv_w, left, *, seq_rows, tm):
    m, d = x2d.shape
    segment = lambda s: pl.BlockSpec((d, HEAD_DIM), lambda i, h: (0, s * NUM_GROUPS + h))
    width = NUM_GROUPS * HEAD_DIM
    if tm >= seq_rows:
        tiles_per_seq, ns = 1, tm // seq_rows
        left_idx = lambda i, h: (i, 0, h)
    else:
        tiles_per_seq, ns = seq_rows // tm, 1
        left_idx = lambda i, h: (i // tiles_per_seq, 0, h)
    n_tails = (m // tm) * ns
    tile = lambda dt: jax.ShapeDtypeStruct((m, width), dt)
    native = jax.ShapeDtypeStruct((m * NUM_GROUPS, HEAD_DIM), F32)
    col_block = pl.BlockSpec((tm, HEAD_DIM), lambda i, h: (i, h))
    native_block = pl.BlockSpec((tm * NUM_GROUPS, HEAD_DIM), lambda i, h: (i, 0))
    kern = functools.partial(_proj_kernel, seq_rows=seq_rows, tiles_per_seq=tiles_per_seq)
    q, k, v, kb, vb, sg, mcv, tails = pl.pallas_call(
        kern,
        grid=(m // tm, NUM_GROUPS),
        in_specs=[
            pl.BlockSpec((tm, d), lambda i, h: (i, 0)),
            pl.BlockSpec((1, d), lambda i, h: (0, 0)),
            *[segment(s) for s in range(NUM_SEGMENTS)],
            pl.BlockSpec((CONV_W, HEAD_DIM), lambda i, h: (0, h)),
            pl.BlockSpec((ns, 2, HEAD_DIM), left_idx),
        ],
        out_specs=[col_block, native_block, native_block, col_block, col_block, col_block,
                   col_block, pl.BlockSpec((ns, 2, HEAD_DIM), lambda i, h: (i, 0, h))],
        out_shape=[tile(BF16), native, native, tile(BF16), tile(BF16), tile(BF16), tile(BF16),
                   jax.ShapeDtypeStruct((n_tails, 2, width), F32)],
        scratch_shapes=[pltpu.VMEM((tm, d), BF16),
                        pltpu.VMEM((NUM_GROUPS, 8, HEAD_DIM), F32)],
        compiler_params=_compiler_params(("arbitrary", "arbitrary")),
        name="proj",
    )(x2d, norm_g.reshape(1, d), *([w_bf] * NUM_SEGMENTS), conv_w, left)
    tails = tails.reshape(-1, tiles_per_seq, 2, width)[:, tiles_per_seq - 1]
    return q, k, v, kb, vb, sg, mcv, tails


def _log2_one_minus_beta(zn, mask):
    softplus2 = jnp.log(1.0 + jnp.exp2(-jnp.abs(zn))) * LOG2E
    log_1m = jnp.minimum(zn, 0.0) - softplus2
    return log_1m if mask is None else jnp.where(mask, log_1m, 0.0)


def _split_bf16(x, axis):
    hi = x.astype(BF16)
    return jnp.concatenate([hi, (x - hi.astype(F32)).astype(BF16)], axis=axis)


def _block_weights(suffix, zn, mask):
    w = jnp.exp2(suffix - zn)
    return (w if mask is None else jnp.where(mask, w, 0.0)).astype(BF16)


def _attn_prompt_kernel(q_ref, k_ref, v_ref, sg_ref, tri_ref, o_ref, acc_ref):
    qi = pl.program_id(2)
    tq = q_ref.shape[0]
    tk = tri_ref.shape[0]
    heads = q_ref.shape[1] // HEAD_DIM
    tri2 = jnp.concatenate([tri_ref[...]] * 2, axis=0)
    hs = [slice(a * HEAD_DIM, (a + 1) * HEAD_DIM) for a in range(heads)]
    nt_dims = (((1,), (1,)), ((), ()))

    def rows_of(j):
        return pl.ds(j * tk if isinstance(j, int) else pl.multiple_of(j * tk, tk), tk)

    def logits(j):
        return tuple(lax.dot_general(q_ref[:, s], k_ref[rows_of(j), s], nt_dims,
                                     preferred_element_type=F32) for s in hs)

    def weights(zns, mask):
        pieces = [_split_bf16(_log2_one_minus_beta(zn, mask), axis=1) for zn in zns]
        suffixes = [jnp.dot(p, tri2, preferred_element_type=F32) for p in pieces]
        ws = tuple(_block_weights(sfx, zn, mask) for sfx, zn in zip(suffixes, zns))
        return ws, tuple(sfx[:, 0:1] for sfx in suffixes)

    def accumulate(ws, j, scales):
        for a, s in enumerate(hs):
            acc_ref[:, s] += scales[a] * jnp.dot(ws[a], v_ref[rows_of(j), s],
                                                 preferred_element_type=F32)

    def step(j, carries, mask):
        ws, totals = weights(logits(j), mask)
        accumulate(ws, j, [jnp.exp2(c) for c in carries])
        return tuple(c + tot for c, tot in zip(carries, totals))

    acc_ref[...] = jnp.zeros(acc_ref.shape, F32)
    rows = lax.broadcasted_iota(jnp.int32, (tq, tk), 0)
    cols = lax.broadcasted_iota(jnp.int32, (tq, tk), 1)
    carries = step(qi, (jnp.zeros((tq, 1), F32),) * heads, cols < rows)

    def any_live(carries):
        return jnp.max(functools.reduce(jnp.maximum, carries)) > EXP2_UNDERFLOW

    def body(state):
        t, _, carries = state
        carries = step(qi - 1 - t, carries, None)
        return t + 1, any_live(carries), carries

    lax.while_loop(lambda state: jnp.logical_and(state[0] < qi, state[1]), body,
                   (jnp.int32(0), any_live(carries), carries))
    o_ref[...] = (acc_ref[...] * sg_ref[...].astype(F32)).astype(BF16)


def _attend_prompt(q, kb, vb, sg, tri, *, batch, seq, tq, heads_per_step):
    nq = seq // tq
    assert tq == tri.shape[0], "query block and key block share the diagonal mask"
    cols = heads_per_step * HEAD_DIM
    qblock = pl.BlockSpec((tq, cols), lambda b, g, i: (b * nq + i, g))
    kvblock = pl.BlockSpec((seq, cols), lambda b, g, i: (b, g))
    return pl.pallas_call(
        _attn_prompt_kernel,
        grid=(batch, NUM_GROUPS // heads_per_step, nq),
        in_specs=[qblock, kvblock, kvblock, qblock,
                  pl.BlockSpec(tri.shape, lambda b, g, i: (0, 0))],
        out_specs=qblock,
        out_shape=jax.ShapeDtypeStruct(q.shape, BF16),
        scratch_shapes=[pltpu.VMEM((tq, cols), F32)],
        compiler_params=_compiler_params(("arbitrary", "arbitrary", "arbitrary")),
        name="attn_prompt",
    )(q, kb, vb, sg, tri)


def _attn_sample_kernel(q_ref, kn_ref, vn_ref, kc_hbm, vc_hbm, sg_ref, trit_ref, o_ref,
                        qnt_ref, knear_ref, vnear_ref, kfar_ref, vfar_ref, near_sems, far_sems):
    b = pl.program_id(0)
    t = q_ref.shape[0]
    tk = trit_ref.shape[0]
    block_rows = tk * NUM_GROUPS
    n_cache = kc_hbm.shape[1] // block_rows
    lanes = NUM_GROUPS * t
    pairs = NUM_GROUPS // 2
    pad = HEAD_DIM
    nt_dims = (((1,), (1,)), ((), ()))
    slot = b % 2

    def cache_copies(stream, j, k_dst, v_dst, k_sem, v_sem):
        rows = pl.ds(j * block_rows, block_rows)
        return (pltpu.make_async_copy(kc_hbm.at[stream, rows, :], k_dst, k_sem),
                pltpu.make_async_copy(vc_hbm.at[stream, rows, :], v_dst, v_sem))

    def near_copies(stream, s):
        return cache_copies(stream, n_cache - 1, knear_ref.at[s], vnear_ref.at[s],
                            near_sems.at[0, s], near_sems.at[1, s])

    @pl.when(b == 0)
    def _():
        for c in near_copies(0, 0):
            c.start()

    @pl.when(b + 1 < pl.num_programs(0))
    def _():
        for c in near_copies(b + 1, 1 - slot):
            c.start()

    qnt_ref[...] = jnp.zeros(qnt_ref.shape, BF16)
    for h in range(NUM_GROUPS):
        p, half = divmod(h, 2)
        qnt_ref[p, h * t:(h + 1) * t, half * HEAD_DIM:(half + 1) * HEAD_DIM] = (
            q_ref[:, h * HEAD_DIM:(h + 1) * HEAD_DIM])

    def neg_logits(load_k):
        zn = None
        for p in range(pairs):
            lhs = jnp.concatenate([load_k(2 * p), load_k(2 * p + 1)], axis=1).astype(BF16)
            d = lax.dot_general(lhs, qnt_ref[p], nt_dims, preferred_element_type=F32)
            zn = d if zn is None else zn + d
        return zn

    def walk(blocks, carry, accs):
        pieces = [_split_bf16(_log2_one_minus_beta(zn, m), axis=0) for zn, _, _, m in blocks]
        suffixes = [jnp.dot(jnp.concatenate([tr, tr], axis=1), p, preferred_element_type=F32)
                    for (_, _, tr, _), p in zip(blocks, pieces)]
        for (zn, load_v, _, m), sfx in zip(blocks, suffixes):
            w = jnp.exp2(sfx - zn + carry)
            if m is not None:
                w = jnp.where(m, w, 0.0)
            w_t = w.T
            accs = [acc + jnp.dot(w_t[h * t:(h + 1) * t, :].astype(BF16),
                                  load_v(h).astype(BF16), preferred_element_type=F32)
                    for h, acc in enumerate(accs)]
            carry = carry + sfx[0:1, :]
        return carry, accs

    head_rows = lambda ref, n, h: ref[pl.ds(h, n, stride=NUM_GROUPS), :]

    key_idx = lax.broadcasted_iota(jnp.int32, (pad, lanes), 0)
    query_idx = lax.broadcasted_iota(jnp.int32, (pad, lanes), 1) & (t - 1)
    zero_rows = lambda width: jnp.zeros((pad - t, width), F32)
    new_block = (jnp.concatenate([neg_logits(lambda h: head_rows(kn_ref, t, h)),
                                  zero_rows(lanes)], axis=0),
                 lambda h: jnp.concatenate([head_rows(vn_ref, t, h), zero_rows(HEAD_DIM)], axis=0),
                 trit_ref[0:pad, 0:pad], key_idx < query_idx)
    for c in near_copies(b, slot):
        c.wait()
    near_block = (neg_logits(lambda h: head_rows(knear_ref.at[slot], tk, h)),
                  lambda h: head_rows(vnear_ref.at[slot], tk, h), trit_ref[...], None)
    carry, accs = walk([new_block, near_block], jnp.zeros((1, lanes), F32),
                       [jnp.zeros((t, HEAD_DIM), F32)] * NUM_GROUPS)

    any_live = lambda carry: jnp.max(carry) > EXP2_UNDERFLOW

    def older(state):
        j, _, carry, accs = state
        copies = cache_copies(b, j, kfar_ref, vfar_ref, far_sems.at[0], far_sems.at[1])
        for c in copies:
            c.start()
        for c in copies:
            c.wait()
        far_block = (neg_logits(lambda h: head_rows(kfar_ref, tk, h)),
                     lambda h: head_rows(vfar_ref, tk, h), trit_ref[...], None)
        carry, accs = walk([far_block], carry, list(accs))
        return j - 1, any_live(carry), carry, tuple(accs)

    _, _, _, accs = lax.while_loop(lambda s: jnp.logical_and(s[0] >= 0, s[1]), older,
                                   (jnp.int32(n_cache - 2), any_live(carry), carry, tuple(accs)))

    for h, acc in enumerate(accs):
        hs = slice(h * HEAD_DIM, (h + 1) * HEAD_DIM)
        o_ref[:, hs] = (acc * sg_ref[:, hs].astype(F32)).astype(BF16)


def _attend_sample(q, k_new, v_new, cache_k, cache_v, sg, tri_t, *, batch, t):
    width = q.shape[1]
    tk = tri_t.shape[0]
    block_rows = tk * NUM_GROUPS
    assert t & (t - 1) == 0 and NUM_GROUPS * t == tk and cache_k.shape[1] % block_rows == 0
    rowblock = pl.BlockSpec((t, width), lambda b: (b, 0))
    newblock = pl.BlockSpec((t * NUM_GROUPS, HEAD_DIM), lambda b: (b, 0))
    in_hbm = pl.BlockSpec(memory_space=pl.ANY)
    key_block = lambda *lead: pltpu.VMEM((*lead, block_rows, HEAD_DIM), F32)
    return pl.pallas_call(
        _attn_sample_kernel,
        grid=(batch,),
        in_specs=[rowblock, newblock, newblock, in_hbm, in_hbm, rowblock,
                  pl.BlockSpec(tri_t.shape, lambda b: (0, 0))],
        out_specs=rowblock,
        out_shape=jax.ShapeDtypeStruct(q.shape, BF16),
        scratch_shapes=[pltpu.VMEM((NUM_GROUPS // 2, NUM_GROUPS * t, 2 * HEAD_DIM), BF16),
                        key_block(2), key_block(2), key_block(), key_block(),
                        pltpu.SemaphoreType.DMA((2, 2)), pltpu.SemaphoreType.DMA((2,))],
        compiler_params=_compiler_params(("arbitrary",)),
        name="attn_sample",
    )(q, k_new, v_new, cache_k, cache_v, sg, tri_t)


def _out_kernel(x_ref, msb_ref, mcv_ref, wsb_ref, wcv_ref, g_ref, y_ref):
    y = (x_ref[...]
         + jnp.dot(msb_ref[...], wsb_ref[...], preferred_element_type=F32)
         + jnp.dot(mcv_ref[...], wcv_ref[...], preferred_element_type=F32))
    y_ref[...] = _rmsnorm_rows(y, g_ref[...])


def _merge_out(x2d, mix_sb, mix_cv, w_bf, final_g, *, tm):
    m, d = x2d.shape
    half = mix_sb.shape[1]
    rows = lambda width: pl.BlockSpec((tm, width), lambda i: (i, 0))
    w_rows = lambda r: pl.BlockSpec((half, d), lambda i: (r, 0))
    return pl.pallas_call(
        _out_kernel,
        grid=(m // tm,),
        in_specs=[rows(d), rows(half), rows(half), w_rows(0), w_rows(1),
                  pl.BlockSpec((1, d), lambda i: (0, 0))],
        out_specs=rows(d),
        out_shape=jax.ShapeDtypeStruct((m, d), F32),
        compiler_params=_compiler_params(("arbitrary",)),
        name="merge_out",
    )(x2d, mix_sb, mix_cv, w_bf, w_bf, final_g.reshape(1, d))


def _lower_tri(n):
    j = lax.broadcasted_iota(jnp.int32, (n, n), 0)
    s = lax.broadcasted_iota(jnp.int32, (n, n), 1)
    return (j >= s).astype(BF16)


def kernel(x_prompt, x_sample, cache_k, cache_v, state_conv, norm_g, w_in, conv_w, w_out, final_g):
    depth = w_in.shape[0]
    assert depth == 1, "single-layer step"
    bsz, seq, d = x_prompt.shape
    dbsz, dseq, _ = x_sample.shape
    past = cache_k.shape[2]
    width = NUM_GROUPS * HEAD_DIM

    w_in_bf = w_in[0].astype(BF16)
    w_out_bf = w_out[0].astype(BF16)
    tri = _lower_tri(KEY_BLOCK)

    xp = x_prompt.reshape(bsz * seq, d)
    zeros_left = jnp.zeros((bsz, CONV_W - 1, width), F32)
    qp, kp, vp, kbp, vbp, sgp, mcvp, tailp = _project(
        xp, norm_g[0], w_in_bf, conv_w[0], zeros_left, seq_rows=seq, tm=PROJ_ROWS)
    msbp = _attend_prompt(qp, kbp, vbp, sgp, tri, batch=bsz, seq=seq, tq=KEY_BLOCK,
                          heads_per_step=PROMPT_HEADS_PER_STEP)
    yp = _merge_out(xp, msbp, mcvp, w_out_bf, final_g, tm=OUT_ROWS)

    xs = x_sample.reshape(dbsz * dseq, d)
    qs, ks, vs, _, _, sgs, mcvs, tails = _project(
        xs, norm_g[0], w_in_bf, conv_w[0], state_conv[0], seq_rows=dseq, tm=PROJ_ROWS)
    msbs = _attend_sample(qs, ks, vs,
                          cache_k[0].reshape(dbsz, past * NUM_GROUPS, HEAD_DIM),
                          cache_v[0].reshape(dbsz, past * NUM_GROUPS, HEAD_DIM),
                          sgs, tri.T, batch=dbsz, t=dseq)
    ys = _merge_out(xs, msbs, mcvs, w_out_bf, final_g, tm=OUT_ROWS)

    heads = lambda a, b_, t_: a.reshape(1, b_, t_, NUM_GROUPS, HEAD_DIM)
    return (yp.reshape(bsz, seq, d), ys.reshape(dbsz, dseq, d),
            heads(kp, bsz, seq), heads(vp, bsz, seq), tailp[None],
            heads(ks, dbsz, dseq), heads(vs, dbsz, dseq), tails[None])
```

```python
import functools
import math

import jax
import jax.numpy as jnp
from jax import lax
from jax.experimental import pallas as pl
from jax.experimental.pallas import tpu as pltpu

F32 = jnp.float32
BF16 = jnp.bfloat16

HEAD_DIM = 128
NUM_GROUPS = 8
NUM_SEGMENTS = 8
CONV_W = 3
EPS = 1e-6
LOG2E = math.log2(math.e)
Q_PRESCALE = -(HEAD_DIM ** -0.5) * LOG2E
MXU_DEPTH = 256
KEY_BLOCK = MXU_DEPTH
PROJ_ROWS = 1024
OUT_ROWS = 512
PROMPT_HEADS_PER_STEP = 8
EXP2_UNDERFLOW = -160.0

VMEM_LIMIT_BYTES = 56 * 1024 * 1024


def _compiler_params(semantics):
    return pltpu.CompilerParams(dimension_semantics=semantics,
                                vmem_limit_bytes=VMEM_LIMIT_BYTES)


def _rmsnorm_rows(x, g):
    r = lax.rsqrt(jnp.mean(x * x, axis=-1, keepdims=True) + EPS)
    return (x * r) * g


def _silu(x):
    return x * (1.0 / (1.0 + jnp.exp(-x)))


def _proj_kernel(x_ref, g_ref, wq_ref, wk_ref, wv_ref, wgs_ref, wb_ref, wc_ref, wu_ref, wgc_ref,
                 cw_ref, left_ref,
                 q_ref, k_ref, v_ref, kb_ref, vb_ref, sg_ref, mcv_ref, tail_ref,
                 hn_ref, carry_ref, *, seq_rows, tiles_per_seq):
    i = pl.program_id(0)
    h = pl.program_id(1)
    tm = x_ref.shape[0]

    @pl.when(h == 0)
    def _():
        hn_ref[...] = _rmsnorm_rows(x_ref[...], g_ref[...]).astype(BF16)

    @pl.when(jnp.logical_and(i == 0, h == 0))
    def _():
        carry_ref[...] = jnp.zeros(carry_ref.shape, F32)

    def project(*w_refs):
        w = jnp.concatenate([r[...] for r in w_refs], axis=1)
        acc = jnp.dot(hn_ref[...], w, preferred_element_type=F32)
        return [acc[:, s * HEAD_DIM:(s + 1) * HEAD_DIM] for s in range(len(w_refs))]

    c, u = project(wc_ref, wu_ref)
    b, g_cv = project(wb_ref, wgc_ref)
    q, k = project(wq_ref, wk_ref)
    v, g_sb = project(wv_ref, wgs_ref)

    q_ref[...] = (q * Q_PRESCALE).astype(BF16)
    k_ref[pl.ds(h, tm, stride=NUM_GROUPS), :] = k
    v_ref[pl.ds(h, tm, stride=NUM_GROUPS), :] = v
    kb_ref[...] = k.astype(BF16)
    vb_ref[...] = v.astype(BF16)
    sg_ref[...] = _silu(g_sb).astype(BF16)

    cu = c * u
    if tiles_per_seq > 1:
        left = jnp.where(i % tiles_per_seq == 0, left_ref[...], carry_ref[h][None, 0:2, :])
        carry_ref[h, 0:2, :] = cu[tm - 2:tm, :]
    else:
        left = left_ref[...]
    rows = seq_rows if tiles_per_seq == 1 else tm
    nseq = tm // rows
    expand = lambda a: jnp.broadcast_to(a, (nseq, rows, HEAD_DIM)).reshape(tm, HEAD_DIM)
    l0 = expand(left[:, 0:1, :])
    l1 = expand(left[:, 1:2, :])
    rs = lax.broadcasted_iota(jnp.int32, (tm, HEAD_DIM), 0) & (rows - 1)
    r1 = jnp.where(rs == 0, l1, pltpu.roll(cu, 1, axis=0))
    r2 = jnp.where(rs == 0, l0, jnp.where(rs == 1, l1, pltpu.roll(cu, 2, axis=0)))
    cw = cw_ref[...]
    conv = cw[0:1, :] * r2 + cw[1:2, :] * r1 + cw[2:3, :] * cu
    mcv_ref[...] = (b * conv * _silu(g_cv)).astype(BF16)
    tail_ref[...] = cu.reshape(nseq, rows, HEAD_DIM)[:, rows - 2:rows, :]


def _project(x2d, norm_g, w_bf, conv_w, left, *, seq_rows, tm):
    m, d = x2d.shape
    segment = lambda s: pl.BlockSpec((d, HEAD_DIM), lambda i, h: (0, s * NUM_GROUPS + h))
    width = NUM_GROUPS * HEAD_DIM
    if tm >= seq_rows:
        tiles_per_seq, ns = 1, tm // seq_rows
        left_idx = lambda i, h: (i, 0, h)
    else:
        tiles_per_seq, ns = seq_rows // tm, 1
        left_idx = lambda i, h: (i // tiles_per_seq, 0, h)
    n_tails = (m // tm) * ns
    tile = lambda dt: jax.ShapeDtypeStruct((m, width), dt)
    native = jax.ShapeDtypeStruct((m * NUM_GROUPS, HEAD_DIM), F32)
    col_block = pl.BlockSpec((tm, HEAD_DIM), lambda i, h: (i, h))
    native_block = pl.BlockSpec((tm * NUM_GROUPS, HEAD_DIM), lambda i, h: (i, 0))
    kern = functools.partial(_proj_kernel, seq_rows=seq_rows, tiles_per_seq=tiles_per_seq)
    q, k, v, kb, vb, sg, mcv, tails = pl.pallas_call(
        kern,
        grid=(m // tm, NUM_GROUPS),
        in_specs=[
            pl.BlockSpec((tm, d), lambda i, h: (i, 0)),
            pl.BlockSpec((1, d), lambda i, h: (0, 0)),
            *[segment(s) for s in range(NUM_SEGMENTS)],
            pl.BlockSpec((CONV_W, HEAD_DIM), lambda i, h: (0, h)),
            pl.BlockSpec((ns, 2, HEAD_DIM), left_idx),
        ],
        out_specs=[col_block, native_block, native_block, col_block, col_block, col_block,
                   col_block, pl.BlockSpec((ns, 2, HEAD_DIM), lambda i, h: (i, 0, h))],
        out_shape=[tile(BF16), native, native, tile(BF16), tile(BF16), tile(BF16), tile(BF16),
                   jax.ShapeDtypeStruct((n_tails, 2, width), F32)],
        scratch_shapes=[pltpu.VMEM((tm, d), BF16),
                        pltpu.VMEM((NUM_GROUPS, 8, HEAD_DIM), F32)],
        compiler_params=_compiler_params(("arbitrary", "arbitrary")),
        name="proj",
    )(x2d, norm_g.reshape(1, d), *([w_bf] * NUM_SEGMENTS), conv_w, left)
    tails = tails.reshape(-1, tiles_per_seq, 2, width)[:, tiles_per_seq - 1]
    return q, k, v, kb, vb, sg, mcv, tails


def _log2_one_minus_beta(zn, mask):
    softplus2 = jnp.log(1.0 + jnp.exp2(-jnp.abs(zn))) * LOG2E
    log_1m = jnp.minimum(zn, 0.0) - softplus2
    return log_1m if mask is None else jnp.where(mask, log_1m, 0.0)


def _split_bf16(x, axis):
    hi = x.astype(BF16)
    return jnp.concatenate([hi, (x - hi.astype(F32)).astype(BF16)], axis=axis)


def _block_weights(suffix, zn, mask):
    w = jnp.exp2(suffix - zn)
    return (w if mask is None else jnp.where(mask, w, 0.0)).astype(BF16)


def _attn_prompt_kernel(q_ref, k_ref, v_ref, sg_ref, tri_ref, o_ref, acc_ref):
    qi = pl.program_id(2)
    tq = q_ref.shape[0]
    tk = tri_ref.shape[0]
    heads = q_ref.shape[1] // HEAD_DIM
    tri2 = jnp.concatenate([tri_ref[...]] * 2, axis=0)
    hs = [slice(a * HEAD_DIM, (a + 1) * HEAD_DIM) for a in range(heads)]
    nt_dims = (((1,), (1,)), ((), ()))

    def rows_of(j):
        return pl.ds(j * tk if isinstance(j, int) else pl.multiple_of(j * tk, tk), tk)

    def logits(j):
        return tuple(lax.dot_general(q_ref[:, s], k_ref[rows_of(j), s], nt_dims,
                                     preferred_element_type=F32) for s in hs)

    def weights(zns, mask):
        pieces = [_split_bf16(_log2_one_minus_beta(zn, mask), axis=1) for zn in zns]
        suffixes = [jnp.dot(p, tri2, preferred_element_type=F32) for p in pieces]
        ws = tuple(_block_weights(sfx, zn, mask) for sfx, zn in zip(suffixes, zns))
        return ws, tuple(sfx[:, 0:1] for sfx in suffixes)

    def accumulate(ws, j, scales):
        for a, s in enumerate(hs):
            acc_ref[:, s] += scales[a] * jnp.dot(ws[a], v_ref[rows_of(j), s],
                                                 preferred_element_type=F32)

    def step(j, carries, mask):
        ws, totals = weights(logits(j), mask)
        accumulate(ws, j, [jnp.exp2(c) for c in carries])
        return tuple(c + tot for c, tot in zip(carries, totals))

    acc_ref[...] = jnp.zeros(acc_ref.shape, F32)
    rows = lax.broadcasted_iota(jnp.int32, (tq, tk), 0)
    cols = lax.broadcasted_iota(jnp.int32, (tq, tk), 1)
    carries = step(qi, (jnp.zeros((tq, 1), F32),) * heads, cols < rows)

    def any_live(carries):
        return jnp.max(functools.reduce(jnp.maximum, carries)) > EXP2_UNDERFLOW

    def body(state):
        t, _, carries = state
        carries = step(qi - 1 - t, carries, None)
        return t + 1, any_live(carries), carries

    lax.while_loop(lambda state: jnp.logical_and(state[0] < qi, state[1]), body,
                   (jnp.int32(0), any_live(carries), carries))
    o_ref[...] = (acc_ref[...] * sg_ref[...].astype(F32)).astype(BF16)


def _attend_prompt(q, kb, vb, sg, tri, *, batch, seq, tq, heads_per_step):
    nq = seq // tq
    assert tq == tri.shape[0], "query block and key block share the diagonal mask"
    cols = heads_per_step * HEAD_DIM
    qblock = pl.BlockSpec((tq, cols), lambda b, g, i: (b * nq + i, g))
    kvblock = pl.BlockSpec((seq, cols), lambda b, g, i: (b, g))
    return pl.pallas_call(
        _attn_prompt_kernel,
        grid=(batch, NUM_GROUPS // heads_per_step, nq),
        in_specs=[qblock, kvblock, kvblock, qblock,
                  pl.BlockSpec(tri.shape, lambda b, g, i: (0, 0))],
        out_specs=qblock,
        out_shape=jax.ShapeDtypeStruct(q.shape, BF16),
        scratch_shapes=[pltpu.VMEM((tq, cols), F32)],
        compiler_params=_compiler_params(("arbitrary", "arbitrary", "arbitrary")),
        name="attn_prompt",
    )(q, kb, vb, sg, tri)


def _attn_sample_kernel(q_ref, kn_ref, vn_ref, kc_hbm, vc_hbm, sg_ref, trit_ref, o_ref,
                        qnt_ref, knear_ref, vnear_ref, kfar_ref, vfar_ref, near_sems, far_sems):
    b = pl.program_id(0)
    t = q_ref.shape[0]
    tk = trit_ref.shape[0]
    block_rows = tk * NUM_GROUPS
    n_cache = kc_hbm.shape[1] // block_rows
    lanes = NUM_GROUPS * t
    pairs = NUM_GROUPS // 2
    pad = HEAD_DIM
    nt_dims = (((1,), (1,)), ((), ()))
    slot = b % 2

    def cache_copies(stream, j, k_dst, v_dst, k_sem, v_sem):
        rows = pl.ds(j * block_rows, block_rows)
        return (pltpu.make_async_copy(kc_hbm.at[stream, rows, :], k_dst, k_sem),
                pltpu.make_async_copy(vc_hbm.at[stream, rows, :], v_dst, v_sem))

    def near_copies(stream, s):
        return cache_copies(stream, n_cache - 1, knear_ref.at[s], vnear_ref.at[s],
                            near_sems.at[0, s], near_sems.at[1, s])

    @pl.when(b == 0)
    def _():
        for c in near_copies(0, 0):
            c.start()

    @pl.when(b + 1 < pl.num_programs(0))
    def _():
        for c in near_copies(b + 1, 1 - slot):
            c.start()

    qnt_ref[...] = jnp.zeros(qnt_ref.shape, BF16)
    for h in range(NUM_GROUPS):
        p, half = divmod(h, 2)
        qnt_ref[p, h * t:(h + 1) * t, half * HEAD_DIM:(half + 1) * HEAD_DIM] = (
            q_ref[:, h * HEAD_DIM:(h + 1) * HEAD_DIM])

    def neg_logits(load_k):
        zn = None
        for p in range(pairs):
            lhs = jnp.concatenate([load_k(2 * p), load_k(2 * p + 1)], axis=1).astype(BF16)
            d = lax.dot_general(lhs, qnt_ref[p], nt_dims, preferred_element_type=F32)
            zn = d if zn is None else zn + d
        return zn

    def walk(blocks, carry, accs):
        pieces = [_split_bf16(_log2_one_minus_beta(zn, m), axis=0) for zn, _, _, m in blocks]
        suffixes = [jnp.dot(jnp.concatenate([tr, tr], axis=1), p, preferred_element_type=F32)
                    for (_, _, tr, _), p in zip(blocks, pieces)]
        for (zn, load_v, _, m), sfx in zip(blocks, suffixes):
            w = jnp.exp2(sfx - zn + carry)
            if m is not None:
                w = jnp.where(m, w, 0.0)
            w_t = w.T
            accs = [acc + jnp.dot(w_t[h * t:(h + 1) * t, :].astype(BF16),
                                  load_v(h).astype(BF16), preferred_element_type=F32)
                    for h, acc in enumerate(accs)]
            carry = carry + sfx[0:1, :]
        return carry, accs

    head_rows = lambda ref, n, h: ref[pl.ds(h, n, stride=NUM_GROUPS), :]

    key_idx = lax.broadcasted_iota(jnp.int32, (pad, lanes), 0)
    query_idx = lax.broadcasted_iota(jnp.int32, (pad, lanes), 1) & (t - 1)
    zero_rows = lambda width: jnp.zeros((pad - t, width), F32)
    new_block = (jnp.concatenate([neg_logits(lambda h: head_rows(kn_ref, t, h)),
                                  zero_rows(lanes)], axis=0),
                 lambda h: jnp.concatenate([head_rows(vn_ref, t, h), zero_rows(HEAD_DIM)], axis=0),
                 trit_ref[0:pad, 0:pad], key_idx < query_idx)
    for c in near_copies(b, slot):
        c.wait()
    near_block = (neg_logits(lambda h: head_rows(knear_ref.at[slot], tk, h)),
                  lambda h: head_rows(vnear_ref.at[slot], tk, h), trit_ref[...], None)
    carry, accs = walk([new_block, near_block], jnp.zeros((1, lanes), F32),
                       [jnp.zeros((t, HEAD_DIM), F32)] * NUM_GROUPS)

    any_live = lambda carry: jnp.max(carry) > EXP2_UNDERFLOW

    def older(state):
        j, _, carry, accs = state
        copies = cache_copies(b, j, kfar_ref, vfar_ref, far_sems.at[0], far_sems.at[1])
        for c in copies:
            c.start()
        for c in copies:
            c.wait()
        far_block = (neg_logits(lambda h: head_rows(kfar_ref, tk, h)),
                     lambda h: head_rows(vfar_ref, tk, h), trit_ref[...], None)
        carry, accs = walk([far_block], carry, list(accs))
        return j - 1, any_live(carry), carry, tuple(accs)

    _, _, _, accs = lax.while_loop(lambda s: jnp.logical_and(s[0] >= 0, s[1]), older,
                                   (jnp.int32(n_cache - 2), any_live(carry), carry, tuple(accs)))

    for h, acc in enumerate(accs):
        hs = slice(h * HEAD_DIM, (h + 1) * HEAD_DIM)
        o_ref[:, hs] = (acc * sg_ref[:, hs].astype(F32)).astype(BF16)


def _attend_sample(q, k_new, v_new, cache_k, cache_v, sg, tri_t, *, batch, t):
    width = q.shape[1]
    tk = tri_t.shape[0]
    block_rows = tk * NUM_GROUPS
    assert t & (t - 1) == 0 and NUM_GROUPS * t == tk and cache_k.shape[1] % block_rows == 0
    rowblock = pl.BlockSpec((t, width), lambda b: (b, 0))
    newblock = pl.BlockSpec((t * NUM_GROUPS, HEAD_DIM), lambda b: (b, 0))
    in_hbm = pl.BlockSpec(memory_space=pl.ANY)
    key_block = lambda *lead: pltpu.VMEM((*lead, block_rows, HEAD_DIM), F32)
    return pl.pallas_call(
        _attn_sample_kernel,
        grid=(batch,),
        in_specs=[rowblock, newblock, newblock, in_hbm, in_hbm, rowblock,
                  pl.BlockSpec(tri_t.shape, lambda b: (0, 0))],
        out_specs=rowblock,
        out_shape=jax.ShapeDtypeStruct(q.shape, BF16),
        scratch_shapes=[pltpu.VMEM((NUM_GROUPS // 2, NUM_GROUPS * t, 2 * HEAD_DIM), BF16),
                        key_block(2), key_block(2), key_block(), key_block(),
                        pltpu.SemaphoreType.DMA((2, 2)), pltpu.SemaphoreType.DMA((2,))],
        compiler_params=_compiler_params(("arbitrary",)),
        name="attn_sample",
    )(q, k_new, v_new, cache_k, cache_v, sg, tri_t)


def _out_kernel(x_ref, msb_ref, mcv_ref, wsb_ref, wcv_ref, g_ref, y_ref):
    y = (x_ref[...]
         + jnp.dot(msb_ref[...], wsb_ref[...], preferred_element_type=F32)
         + jnp.dot(mcv_ref[...], wcv_ref[...], preferred_element_type=F32))
    y_ref[...] = _rmsnorm_rows(y, g_ref[...])


def _merge_out(x2d, mix_sb, mix_cv, w_bf, final_g, *, tm):
    m, d = x2d.shape
    half = mix_sb.shape[1]
    rows = lambda width: pl.BlockSpec((tm, width), lambda i: (i, 0))
    w_rows = lambda r: pl.BlockSpec((half, d), lambda i: (r, 0))
    return pl.pallas_call(
        _out_kernel,
        grid=(m // tm,),
        in_specs=[rows(d), rows(half), rows(half), w_rows(0), w_rows(1),
                  pl.BlockSpec((1, d), lambda i: (0, 0))],
        out_specs=rows(d),
        out_shape=jax.ShapeDtypeStruct((m, d), F32),
        compiler_params=_compiler_params(("arbitrary",)),
        name="merge_out",
    )(x2d, mix_sb, mix_cv, w_bf, w_bf, final_g.reshape(1, d))


def _lower_tri(n):
    j = lax.broadcasted_iota(jnp.int32, (n, n), 0)
    s = lax.broadcasted_iota(jnp.int32, (n, n), 1)
    return (j >= s).astype(BF16)


def kernel(x_prompt, x_sample, cache_k, cache_v, state_conv, norm_g, w_in, conv_w, w_out, final_g):
    depth = w_in.shape[0]
    assert depth == 1, "single-layer step"
    bsz, seq, d = x_prompt.shape
    dbsz, dseq, _ = x_sample.shape
    past = cache_k.shape[2]
    width = NUM_GROUPS * HEAD_DIM

    w_in_bf = w_in[0].astype(BF16)
    w_out_bf = w_out[0].astype(BF16)
    tri = _lower_tri(KEY_BLOCK)

    xp = x_prompt.reshape(bsz * seq, d)
    zeros_left = jnp.zeros((bsz, CONV_W - 1, width), F32)
    qp, kp, vp, kbp, vbp, sgp, mcvp, tailp = _project(
        xp, norm_g[0], w_in_bf, conv_w[0], zeros_left, seq_rows=seq, tm=PROJ_ROWS)
    msbp = _attend_prompt(qp, kbp, vbp, sgp, tri, batch=bsz, seq=seq, tq=KEY_BLOCK,
                          heads_per_step=PROMPT_HEADS_PER_STEP)
    yp = _merge_out(xp, msbp, mcvp, w_out_bf, final_g, tm=OUT_ROWS)

    xs = x_sample.reshape(dbsz * dseq, d)
    qs, ks, vs, _, _, sgs, mcvs, tails = _project(
        xs, norm_g[0], w_in_bf, conv_w[0], state_conv[0], seq_rows=dseq, tm=PROJ_ROWS)
    msbs = _attend_sample(qs, ks, vs,
                          cache_k[0].reshape(dbsz, past * NUM_GROUPS, HEAD_DIM),
                          cache_v[0].reshape(dbsz, past * NUM_GROUPS, HEAD_DIM),
                          sgs, tri.T, batch=dbsz, t=dseq)
    ys = _merge_out(xs, msbs, mcvs, w_out_bf, final_g, tm=OUT_ROWS)

    heads = lambda a, b_, t_: a.reshape(1, b_, t_, NUM_GROUPS, HEAD_DIM)
    return (yp.reshape(bsz, seq, d), ys.reshape(dbsz, dseq, d),
            heads(kp, bsz, seq), heads(vp, bsz, seq), tailp[None],
            heads(ks, dbsz, dseq), heads(vs, dbsz, dseq), tails[None])
```

```python
import functools
import math

import jax
import jax.numpy as jnp
from jax import lax
from jax.experimental import pallas as pl
from jax.experimental.pallas import tpu as pltpu

F32 = jnp.float32
BF16 = jnp.bfloat16

HEAD_DIM = 128
NUM_GROUPS = 8
NUM_SEGMENTS = 8
CONV_W = 3
EPS = 1e-6
LOG2E = math.log2(math.e)
Q_PRESCALE = -(HEAD_DIM ** -0.5) * LOG2E
MXU_DEPTH = 256
KEY_BLOCK = MXU_DEPTH
PROJ_ROWS = 1024
OUT_ROWS = 512
OUT_ROW_CHUNKS = 2
PROMPT_HEADS_PER_STEP = 8
SAMPLE_STREAMS_PER_STEP = 2
EXP2_UNDERFLOW = -160.0

VMEM_LIMIT_BYTES = 56 * 1024 * 1024


def _compiler_params(semantics):
    return pltpu.CompilerParams(dimension_semantics=semantics,
                                vmem_limit_bytes=VMEM_LIMIT_BYTES)


def _rmsnorm_rows(x, g):
    r = lax.rsqrt(jnp.mean(x * x, axis=-1, keepdims=True) + EPS)
    return (x * r) * g


def _silu(x):
    return x * (1.0 / (1.0 + jnp.exp(-x)))


def _proj_kernel(x_ref, g_ref, wq_ref, wk_ref, wv_ref, wgs_ref, wb_ref, wc_ref, wu_ref, wgc_ref,
                 cw_ref, left_ref,
                 q_ref, k_ref, v_ref, kb_ref, vb_ref, sg_ref, mcv_ref, tail_ref,
                 hn_ref, carry_ref, *, seq_rows, tiles_per_seq):
    i = pl.program_id(0)
    h = pl.program_id(1)
    tm = x_ref.shape[0]

    @pl.when(h == 0)
    def _():
        hn_ref[...] = _rmsnorm_rows(x_ref[...], g_ref[...]).astype(BF16)

    @pl.when(jnp.logical_and(i == 0, h == 0))
    def _():
        carry_ref[...] = jnp.zeros(carry_ref.shape, F32)

    def project(*w_refs):
        w = jnp.concatenate([r[...] for r in w_refs], axis=1)
        acc = jnp.dot(hn_ref[...], w, preferred_element_type=F32)
        return [acc[:, s * HEAD_DIM:(s + 1) * HEAD_DIM] for s in range(len(w_refs))]

    c, u = project(wc_ref, wu_ref)
    b, g_cv = project(wb_ref, wgc_ref)
    q, k = project(wq_ref, wk_ref)
    v, g_sb = project(wv_ref, wgs_ref)

    q_ref[...] = (q * Q_PRESCALE).astype(BF16)
    k_ref[pl.ds(h, tm, stride=NUM_GROUPS), :] = k
    v_ref[pl.ds(h, tm, stride=NUM_GROUPS), :] = v
    kb_ref[...] = k.astype(BF16)
    vb_ref[...] = v.astype(BF16)
    sg_ref[...] = _silu(g_sb).astype(BF16)

    cu = c * u
    if tiles_per_seq > 1:
        left = jnp.where(i % tiles_per_seq == 0, left_ref[...], carry_ref[h][None, 0:2, :])
        carry_ref[h, 0:2, :] = cu[tm - 2:tm, :]
    else:
        left = left_ref[...]
    rows = seq_rows if tiles_per_seq == 1 else tm
    nseq = tm // rows
    expand = lambda a: jnp.broadcast_to(a, (nseq, rows, HEAD_DIM)).reshape(tm, HEAD_DIM)
    l0 = expand(left[:, 0:1, :])
    l1 = expand(left[:, 1:2, :])
    rs = lax.broadcasted_iota(jnp.int32, (tm, HEAD_DIM), 0) & (rows - 1)
    r1 = jnp.where(rs == 0, l1, pltpu.roll(cu, 1, axis=0))
    r2 = jnp.where(rs == 0, l0, jnp.where(rs == 1, l1, pltpu.roll(cu, 2, axis=0)))
    cw = cw_ref[...]
    conv = cw[0:1, :] * r2 + cw[1:2, :] * r1 + cw[2:3, :] * cu
    mcv_ref[...] = (b * conv * _silu(g_cv)).astype(BF16)
    tail_ref[...] = cu.reshape(nseq, rows, HEAD_DIM)[:, rows - 2:rows, :]


def _project(x2d, norm_g, w_bf, conv_w, left, *, seq_rows, tm):
    m, d = x2d.shape
    segment = lambda s: pl.BlockSpec((d, HEAD_DIM), lambda i, h: (0, s * NUM_GROUPS + h))
    width = NUM_GROUPS * HEAD_DIM
    if tm >= seq_rows:
        tiles_per_seq, ns = 1, tm // seq_rows
        left_idx = lambda i, h: (i, 0, h)
    else:
        tiles_per_seq, ns = seq_rows // tm, 1
        left_idx = lambda i, h: (i // tiles_per_seq, 0, h)
    n_tails = (m // tm) * ns
    tile = lambda dt: jax.ShapeDtypeStruct((m, width), dt)
    native = jax.ShapeDtypeStruct((m * NUM_GROUPS, HEAD_DIM), F32)
    col_block = pl.BlockSpec((tm, HEAD_DIM), lambda i, h: (i, h))
    native_block = pl.BlockSpec((tm * NUM_GROUPS, HEAD_DIM), lambda i, h: (i, 0))
    kern = functools.partial(_proj_kernel, seq_rows=seq_rows, tiles_per_seq=tiles_per_seq)
    q, k, v, kb, vb, sg, mcv, tails = pl.pallas_call(
        kern,
        grid=(m // tm, NUM_GROUPS),
        in_specs=[
            pl.BlockSpec((tm, d), lambda i, h: (i, 0)),
            pl.BlockSpec((1, d), lambda i, h: (0, 0)),
            *[segment(s) for s in range(NUM_SEGMENTS)],
            pl.BlockSpec((CONV_W, HEAD_DIM), lambda i, h: (0, h)),
            pl.BlockSpec((ns, 2, HEAD_DIM), left_idx),
        ],
        out_specs=[col_block, native_block, native_block, col_block, col_block, col_block,
                   col_block, pl.BlockSpec((ns, 2, HEAD_DIM), lambda i, h: (i, 0, h))],
        out_shape=[tile(BF16), native, native, tile(BF16), tile(BF16), tile(BF16), tile(BF16),
                   jax.ShapeDtypeStruct((n_tails, 2, width), F32)],
        scratch_shapes=[pltpu.VMEM((tm, d), BF16),
                        pltpu.VMEM((NUM_GROUPS, 8, HEAD_DIM), F32)],
        compiler_params=_compiler_params(("arbitrary", "arbitrary")),
        name="proj",
    )(x2d, norm_g.reshape(1, d), *([w_bf] * NUM_SEGMENTS), conv_w, left)
    tails = tails.reshape(-1, tiles_per_seq, 2, width)[:, tiles_per_seq - 1]
    return q, k, v, kb, vb, sg, mcv, tails


def _log2_one_minus_beta(zn, mask):
    softplus2 = jnp.log(1.0 + jnp.exp2(-jnp.abs(zn))) * LOG2E
    log_1m = jnp.minimum(zn, 0.0) - softplus2
    return log_1m if mask is None else jnp.where(mask, log_1m, 0.0)


def _split_bf16(x, axis):
    hi = x.astype(BF16)
    return jnp.concatenate([hi, (x - hi.astype(F32)).astype(BF16)], axis=axis)


def _block_weights(suffix, zn, mask):
    w = jnp.exp2(suffix - zn)
    return (w if mask is None else jnp.where(mask, w, 0.0)).astype(BF16)


def _attn_prompt_kernel(q_ref, k_ref, v_ref, sg_ref, tri_ref, o_ref, acc_ref):
    qi = pl.program_id(2)
    tq = q_ref.shape[0]
    tk = tri_ref.shape[0]
    heads = q_ref.shape[1] // HEAD_DIM
    tri2 = jnp.concatenate([tri_ref[...]] * 2, axis=0)
    hs = [slice(a * HEAD_DIM, (a + 1) * HEAD_DIM) for a in range(heads)]
    nt_dims = (((1,), (1,)), ((), ()))

    def rows_of(j):
        return pl.ds(j * tk if isinstance(j, int) else pl.multiple_of(j * tk, tk), tk)

    def logits(j):
        return tuple(lax.dot_general(q_ref[:, s], k_ref[rows_of(j), s], nt_dims,
                                     preferred_element_type=F32) for s in hs)

    def weights(zns, mask):
        pieces = [_split_bf16(_log2_one_minus_beta(zn, mask), axis=1) for zn in zns]
        suffixes = [jnp.dot(p, tri2, preferred_element_type=F32) for p in pieces]
        ws = tuple(_block_weights(sfx, zn, mask) for sfx, zn in zip(suffixes, zns))
        return ws, tuple(sfx[:, 0:1] for sfx in suffixes)

    def accumulate(ws, j, scales):
        for a, s in enumerate(hs):
            acc_ref[:, s] += scales[a] * jnp.dot(ws[a], v_ref[rows_of(j), s],
                                                 preferred_element_type=F32)

    def step(j, carries, mask):
        ws, totals = weights(logits(j), mask)
        accumulate(ws, j, [jnp.exp2(c) for c in carries])
        return tuple(c + tot for c, tot in zip(carries, totals))

    acc_ref[...] = jnp.zeros(acc_ref.shape, F32)
    rows = lax.broadcasted_iota(jnp.int32, (tq, tk), 0)
    cols = lax.broadcasted_iota(jnp.int32, (tq, tk), 1)
    carries = step(qi, (jnp.zeros((tq, 1), F32),) * heads, cols < rows)

    def any_live(carries):
        return jnp.max(functools.reduce(jnp.maximum, carries)) > EXP2_UNDERFLOW

    def body(state):
        t, _, carries = state
        carries = step(qi - 1 - t, carries, None)
        return t + 1, any_live(carries), carries

    lax.while_loop(lambda state: jnp.logical_and(state[0] < qi, state[1]), body,
                   (jnp.int32(0), any_live(carries), carries))
    o_ref[...] = (acc_ref[...] * sg_ref[...].astype(F32)).astype(BF16)


def _attend_prompt(q, kb, vb, sg, tri, *, batch, seq, tq, heads_per_step):
    nq = seq // tq
    assert tq == tri.shape[0], "query block and key block share the diagonal mask"
    cols = heads_per_step * HEAD_DIM
    qblock = pl.BlockSpec((tq, cols), lambda b, g, i: (b * nq + i, g))
    kvblock = pl.BlockSpec((seq, cols), lambda b, g, i: (b, g))
    return pl.pallas_call(
        _attn_prompt_kernel,
        grid=(batch, NUM_GROUPS // heads_per_step, nq),
        in_specs=[qblock, kvblock, kvblock, qblock,
                  pl.BlockSpec(tri.shape, lambda b, g, i: (0, 0))],
        out_specs=qblock,
        out_shape=jax.ShapeDtypeStruct(q.shape, BF16),
        scratch_shapes=[pltpu.VMEM((tq, cols), F32)],
        compiler_params=_compiler_params(("arbitrary", "arbitrary", "arbitrary")),
        name="attn_prompt",
    )(q, kb, vb, sg, tri)


def _attn_sample_kernel(q_ref, kn_ref, vn_ref, kc_hbm, vc_hbm, sg_ref, trit_ref, o_ref,
                        qnt_ref, knear_ref, vnear_ref, kfar_ref, vfar_ref, near_sems, far_sems,
                        *, t):
    step = pl.program_id(0)
    n = q_ref.shape[0] // t
    tk = trit_ref.shape[0]
    block_rows = tk * NUM_GROUPS
    n_cache = kc_hbm.shape[1] // block_rows
    lanes = NUM_GROUPS * t
    pairs = NUM_GROUPS // 2
    pad = HEAD_DIM
    nt_dims = (((1,), (1,)), ((), ()))
    slot = step % 2

    def cache_copies(stream, j, k_dst, v_dst, k_sem, v_sem):
        rows = pl.ds(j * block_rows, block_rows)
        return [pltpu.make_async_copy(kc_hbm.at[stream, rows, :], k_dst, k_sem),
                pltpu.make_async_copy(vc_hbm.at[stream, rows, :], v_dst, v_sem)]

    def near_copies(at_step, s):
        return [c for i in range(n) for c in cache_copies(
            at_step * n + i, n_cache - 1, knear_ref.at[s, i], vnear_ref.at[s, i],
            near_sems.at[0, s, i], near_sems.at[1, s, i])]

    @pl.when(step == 0)
    def _():
        for c in near_copies(0, 0):
            c.start()

    @pl.when(step + 1 < pl.num_programs(0))
    def _():
        for c in near_copies(step + 1, 1 - slot):
            c.start()

    qnt_ref[...] = jnp.zeros(qnt_ref.shape, BF16)
    for i in range(n):
        for h in range(NUM_GROUPS):
            p, half = divmod(h, 2)
            qnt_ref[i, p, h * t:(h + 1) * t, half * HEAD_DIM:(half + 1) * HEAD_DIM] = (
                q_ref[i * t:(i + 1) * t, h * HEAD_DIM:(h + 1) * HEAD_DIM])

    def neg_logits(i, load_k):
        zn = None
        for p in range(pairs):
            lhs = jnp.concatenate([load_k(2 * p), load_k(2 * p + 1)], axis=1).astype(BF16)
            d = lax.dot_general(lhs, qnt_ref[i, p], nt_dims, preferred_element_type=F32)
            zn = d if zn is None else zn + d
        return zn

    def walk(work):
        pieces = [[_split_bf16(_log2_one_minus_beta(zn, m), axis=0) for zn, _, _, m in blocks]
                  for blocks, _, _ in work]
        suffixes = [[jnp.dot(jnp.concatenate([tr, tr], axis=1), p, preferred_element_type=F32)
                     for (_, _, tr, _), p in zip(blocks, ps)]
                    for (blocks, _, _), ps in zip(work, pieces)]
        done = []
        for (blocks, carry, accs), sfxs in zip(work, suffixes):
            for (zn, load_v, _, m), sfx in zip(blocks, sfxs):
                w = jnp.exp2(sfx - zn + carry)
                if m is not None:
                    w = jnp.where(m, w, 0.0)
                w_t = w.T
                accs = [acc + jnp.dot(w_t[h * t:(h + 1) * t, :].astype(BF16),
                                      load_v(h).astype(BF16), preferred_element_type=F32)
                        for h, acc in enumerate(accs)]
                carry = carry + sfx[0:1, :]
            done.append((carry, tuple(accs)))
        return done

    head_rows = lambda ref, base, rows, h: ref[pl.ds(base + h, rows, stride=NUM_GROUPS), :]

    key_idx = lax.broadcasted_iota(jnp.int32, (pad, lanes), 0)
    query_idx = lax.broadcasted_iota(jnp.int32, (pad, lanes), 1) & (t - 1)
    zero_rows = lambda width: jnp.zeros((pad - t, width), F32)
    new_base = lambda i: i * t * NUM_GROUPS
    new_blocks = [
        (jnp.concatenate([neg_logits(i, lambda h, i=i: head_rows(kn_ref, new_base(i), t, h)),
                          zero_rows(lanes)], axis=0),
         lambda h, i=i: jnp.concatenate([head_rows(vn_ref, new_base(i), t, h),
                                         zero_rows(HEAD_DIM)], axis=0),
         trit_ref[0:pad, 0:pad], key_idx < query_idx) for i in range(n)]
    for c in near_copies(step, slot):
        c.wait()
    near_blocks = [
        (neg_logits(i, lambda h, i=i: head_rows(knear_ref.at[slot, i], 0, tk, h)),
         lambda h, i=i: head_rows(vnear_ref.at[slot, i], 0, tk, h), trit_ref[...], None)
        for i in range(n)]
    state = walk([([new_blocks[i], near_blocks[i]], jnp.zeros((1, lanes), F32),
                   [jnp.zeros((t, HEAD_DIM), F32)] * NUM_GROUPS) for i in range(n)])

    def any_live(state):
        return jnp.max(functools.reduce(jnp.maximum, [c for c, _ in state])) > EXP2_UNDERFLOW

    def older(loop_state):
        j, _, state = loop_state
        copies = [c for i in range(n) for c in cache_copies(
            step * n + i, j, kfar_ref.at[i], vfar_ref.at[i], far_sems.at[0, i], far_sems.at[1, i])]
        for c in copies:
            c.start()
        for c in copies:
            c.wait()
        far_blocks = [(neg_logits(i, lambda h, i=i: head_rows(kfar_ref.at[i], 0, tk, h)),
                       lambda h, i=i: head_rows(vfar_ref.at[i], 0, tk, h), trit_ref[...], None)
                      for i in range(n)]
        state = walk([([far_blocks[i]], state[i][0], list(state[i][1])) for i in range(n)])
        return j - 1, any_live(state), tuple(state)

    _, _, state = lax.while_loop(lambda s: jnp.logical_and(s[0] >= 0, s[1]), older,
                                 (jnp.int32(n_cache - 2), any_live(state), tuple(state)))

    for i, (_, accs) in enumerate(state):
        for h, acc in enumerate(accs):
            rows, hs = slice(i * t, (i + 1) * t), slice(h * HEAD_DIM, (h + 1) * HEAD_DIM)
            o_ref[rows, hs] = (acc * sg_ref[rows, hs].astype(F32)).astype(BF16)


def _attend_sample(q, k_new, v_new, cache_k, cache_v, sg, tri_t, *, batch, t):
    width = q.shape[1]
    tk = tri_t.shape[0]
    block_rows = tk * NUM_GROUPS
    assert t & (t - 1) == 0 and NUM_GROUPS * t == tk and cache_k.shape[1] % block_rows == 0
    n = SAMPLE_STREAMS_PER_STEP
    assert batch % n == 0
    rowblock = pl.BlockSpec((n * t, width), lambda b: (b, 0))
    newblock = pl.BlockSpec((n * t * NUM_GROUPS, HEAD_DIM), lambda b: (b, 0))
    in_hbm = pl.BlockSpec(memory_space=pl.ANY)
    key_blocks = lambda *lead: pltpu.VMEM((*lead, n, block_rows, HEAD_DIM), F32)
    return pl.pallas_call(
        functools.partial(_attn_sample_kernel, t=t),
        grid=(batch // n,),
        in_specs=[rowblock, newblock, newblock, in_hbm, in_hbm, rowblock,
                  pl.BlockSpec(tri_t.shape, lambda b: (0, 0))],
        out_specs=rowblock,
        out_shape=jax.ShapeDtypeStruct(q.shape, BF16),
        scratch_shapes=[pltpu.VMEM((n, NUM_GROUPS // 2, NUM_GROUPS * t, 2 * HEAD_DIM), BF16),
                        key_blocks(2), key_blocks(2), key_blocks(), key_blocks(),
                        pltpu.SemaphoreType.DMA((2, 2, n)), pltpu.SemaphoreType.DMA((2, n))],
        compiler_params=_compiler_params(("arbitrary",)),
        name="attn_sample",
    )(q, k_new, v_new, cache_k, cache_v, sg, tri_t)


def _out_kernel(x_ref, msb_ref, mcv_ref, w_ref, g_ref, y_ref):
    tm = x_ref.shape[0]
    w = w_ref[...]
    chunk = tm // OUT_ROW_CHUNKS
    halves = [slice(c * chunk, (c + 1) * chunk) for c in range(OUT_ROW_CHUNKS)]
    mixed = [jnp.dot(jnp.concatenate([msb_ref[r, :], mcv_ref[r, :]], axis=1), w,
                     preferred_element_type=F32) for r in halves]
    for r, m in zip(halves, mixed):
        y_ref[r, :] = _rmsnorm_rows(x_ref[r, :] + m, g_ref[...])


def _merge_out(x2d, mix_sb, mix_cv, w_bf, final_g, *, tm):
    m, d = x2d.shape
    half = mix_sb.shape[1]
    rows = lambda width: pl.BlockSpec((tm, width), lambda i: (i, 0))
    return pl.pallas_call(
        _out_kernel,
        grid=(m // tm,),
        in_specs=[rows(d), rows(half), rows(half), pl.BlockSpec(w_bf.shape, lambda i: (0, 0)),
                  pl.BlockSpec((1, d), lambda i: (0, 0))],
        out_specs=rows(d),
        out_shape=jax.ShapeDtypeStruct((m, d), F32),
        compiler_params=_compiler_params(("arbitrary",)),
        name="merge_out",
    )(x2d, mix_sb, mix_cv, w_bf, final_g.reshape(1, d))


def _lower_tri(n):
    j = lax.broadcasted_iota(jnp.int32, (n, n), 0)
    s = lax.broadcasted_iota(jnp.int32, (n, n), 1)
    return (j >= s).astype(BF16)


def kernel(x_prompt, x_sample, cache_k, cache_v, state_conv, norm_g, w_in, conv_w, w_out, final_g):
    depth = w_in.shape[0]
    assert depth == 1, "single-layer step"
    bsz, seq, d = x_prompt.shape
    dbsz, dseq, _ = x_sample.shape
    past = cache_k.shape[2]
    width = NUM_GROUPS * HEAD_DIM

    w_in_bf = w_in[0].astype(BF16)
    w_out_bf = w_out[0].astype(BF16)
    tri = _lower_tri(KEY_BLOCK)

    xp = x_prompt.reshape(bsz * seq, d)
    zeros_left = jnp.zeros((bsz, CONV_W - 1, width), F32)
    qp, kp, vp, kbp, vbp, sgp, mcvp, tailp = _project(
        xp, norm_g[0], w_in_bf, conv_w[0], zeros_left, seq_rows=seq, tm=PROJ_ROWS)
    msbp = _attend_prompt(qp, kbp, vbp, sgp, tri, batch=bsz, seq=seq, tq=KEY_BLOCK,
                          heads_per_step=PROMPT_HEADS_PER_STEP)
    yp = _merge_out(xp, msbp, mcvp, w_out_bf, final_g, tm=OUT_ROWS)

    xs = x_sample.reshape(dbsz * dseq, d)
    qs, ks, vs, _, _, sgs, mcvs, tails = _project(
        xs, norm_g[0], w_in_bf, conv_w[0], state_conv[0], seq_rows=dseq, tm=PROJ_ROWS)
    msbs = _attend_sample(qs, ks, vs,
                          cache_k[0].reshape(dbsz, past * NUM_GROUPS, HEAD_DIM),
                          cache_v[0].reshape(dbsz, past * NUM_GROUPS, HEAD_DIM),
                          sgs, tri.T, batch=dbsz, t=dseq)
    ys = _merge_out(xs, msbs, mcvs, w_out_bf, final_g, tm=OUT_ROWS)

    heads = lambda a, b_, t_: a.reshape(1, b_, t_, NUM_GROUPS, HEAD_DIM)
    return (yp.reshape(bsz, seq, d), ys.reshape(dbsz, dseq, d),
            heads(kp, bsz, seq), heads(vp, bsz, seq), tailp[None],
            heads(ks, dbsz, dseq), heads(vs, dbsz, dseq), tails[None])
```

```python
import functools
import math

import jax
import jax.numpy as jnp
from jax import lax
from jax.experimental import pallas as pl
from jax.experimental.pallas import tpu as pltpu

F32 = jnp.float32
BF16 = jnp.bfloat16

HEAD_DIM = 128
NUM_GROUPS = 8
NUM_SEGMENTS = 8
CONV_W = 3
EPS = 1e-6
LOG2E = math.log2(math.e)
Q_PRESCALE = -(HEAD_DIM ** -0.5) * LOG2E
MXU_DEPTH = 256
KEY_BLOCK = MXU_DEPTH
PROJ_ROWS = 1024
OUT_ROWS = 512
PROMPT_HEADS_PER_STEP = 8
SAMPLE_STREAMS_PER_STEP = 4
EXP2_UNDERFLOW = -160.0

VMEM_LIMIT_BYTES = 56 * 1024 * 1024


def _compiler_params(semantics):
    return pltpu.CompilerParams(dimension_semantics=semantics,
                                vmem_limit_bytes=VMEM_LIMIT_BYTES)


def _rmsnorm_rows(x, g):
    r = lax.rsqrt(jnp.mean(x * x, axis=-1, keepdims=True) + EPS)
    return (x * r) * g


def _silu(x):
    return x * (1.0 / (1.0 + jnp.exp(-x)))


def _proj_kernel(x_ref, g_ref, wq_ref, wk_ref, wv_ref, wgs_ref, wb_ref, wc_ref, wu_ref, wgc_ref,
                 cw_ref, left_ref,
                 q_ref, k_ref, v_ref, kb_ref, vb_ref, sg_ref, mcv_ref, tail_ref,
                 hn_ref, carry_ref, *, seq_rows, tiles_per_seq):
    i = pl.program_id(0)
    h = pl.program_id(1)
    tm = x_ref.shape[0]

    @pl.when(h == 0)
    def _():
        hn_ref[...] = _rmsnorm_rows(x_ref[...], g_ref[...]).astype(BF16)

    @pl.when(jnp.logical_and(i == 0, h == 0))
    def _():
        carry_ref[...] = jnp.zeros(carry_ref.shape, F32)

    def project(*w_refs):
        w = jnp.concatenate([r[...] for r in w_refs], axis=1)
        acc = jnp.dot(hn_ref[...], w, preferred_element_type=F32)
        return [acc[:, s * HEAD_DIM:(s + 1) * HEAD_DIM] for s in range(len(w_refs))]

    c, u = project(wc_ref, wu_ref)
    b, g_cv = project(wb_ref, wgc_ref)
    q, k = project(wq_ref, wk_ref)
    v, g_sb = project(wv_ref, wgs_ref)

    q_ref[...] = (q * Q_PRESCALE).astype(BF16)
    k_ref[pl.ds(h, tm, stride=NUM_GROUPS), :] = k
    v_ref[pl.ds(h, tm, stride=NUM_GROUPS), :] = v
    kb_ref[...] = k.astype(BF16)
    vb_ref[...] = v.astype(BF16)
    sg_ref[...] = _silu(g_sb).astype(BF16)

    cu = c * u
    if tiles_per_seq > 1:
        left = jnp.where(i % tiles_per_seq == 0, left_ref[...], carry_ref[h][None, 0:2, :])
        carry_ref[h, 0:2, :] = cu[tm - 2:tm, :]
    else:
        left = left_ref[...]
    rows = seq_rows if tiles_per_seq == 1 else tm
    nseq = tm // rows
    expand = lambda a: jnp.broadcast_to(a, (nseq, rows, HEAD_DIM)).reshape(tm, HEAD_DIM)
    l0 = expand(left[:, 0:1, :])
    l1 = expand(left[:, 1:2, :])
    rs = lax.broadcasted_iota(jnp.int32, (tm, HEAD_DIM), 0) & (rows - 1)
    r1 = jnp.where(rs == 0, l1, pltpu.roll(cu, 1, axis=0))
    r2 = jnp.where(rs == 0, l0, jnp.where(rs == 1, l1, pltpu.roll(cu, 2, axis=0)))
    cw = cw_ref[...]
    conv = cw[0:1, :] * r2 + cw[1:2, :] * r1 + cw[2:3, :] * cu
    mcv_ref[...] = (b * conv * _silu(g_cv)).astype(BF16)
    tail_ref[...] = cu.reshape(nseq, rows, HEAD_DIM)[:, rows - 2:rows, :]


def _project(x2d, norm_g, w_bf, conv_w, left, *, seq_rows, tm):
    m, d = x2d.shape
    segment = lambda s: pl.BlockSpec((d, HEAD_DIM), lambda i, h: (0, s * NUM_GROUPS + h))
    width = NUM_GROUPS * HEAD_DIM
    if tm >= seq_rows:
        tiles_per_seq, ns = 1, tm // seq_rows
        left_idx = lambda i, h: (i, 0, h)
    else:
        tiles_per_seq, ns = seq_rows // tm, 1
        left_idx = lambda i, h: (i // tiles_per_seq, 0, h)
    n_tails = (m // tm) * ns
    tile = lambda dt: jax.ShapeDtypeStruct((m, width), dt)
    native = jax.ShapeDtypeStruct((m * NUM_GROUPS, HEAD_DIM), F32)
    col_block = pl.BlockSpec((tm, HEAD_DIM), lambda i, h: (i, h))
    native_block = pl.BlockSpec((tm * NUM_GROUPS, HEAD_DIM), lambda i, h: (i, 0))
    kern = functools.partial(_proj_kernel, seq_rows=seq_rows, tiles_per_seq=tiles_per_seq)
    q, k, v, kb, vb, sg, mcv, tails = pl.pallas_call(
        kern,
        grid=(m // tm, NUM_GROUPS),
        in_specs=[
            pl.BlockSpec((tm, d), lambda i, h: (i, 0)),
            pl.BlockSpec((1, d), lambda i, h: (0, 0)),
            *[segment(s) for s in range(NUM_SEGMENTS)],
            pl.BlockSpec((CONV_W, HEAD_DIM), lambda i, h: (0, h)),
            pl.BlockSpec((ns, 2, HEAD_DIM), left_idx),
        ],
        out_specs=[col_block, native_block, native_block, col_block, col_block, col_block,
                   col_block, pl.BlockSpec((ns, 2, HEAD_DIM), lambda i, h: (i, 0, h))],
        out_shape=[tile(BF16), native, native, tile(BF16), tile(BF16), tile(BF16), tile(BF16),
                   jax.ShapeDtypeStruct((n_tails, 2, width), F32)],
        scratch_shapes=[pltpu.VMEM((tm, d), BF16),
                        pltpu.VMEM((NUM_GROUPS, 8, HEAD_DIM), F32)],
        compiler_params=_compiler_params(("arbitrary", "arbitrary")),
        name="proj",
    )(x2d, norm_g.reshape(1, d), *([w_bf] * NUM_SEGMENTS), conv_w, left)
    tails = tails.reshape(-1, tiles_per_seq, 2, width)[:, tiles_per_seq - 1]
    return q, k, v, kb, vb, sg, mcv, tails


def _log2_one_minus_beta(zn, mask):
    softplus2 = jnp.log(1.0 + jnp.exp2(-jnp.abs(zn))) * LOG2E
    log_1m = jnp.minimum(zn, 0.0) - softplus2
    return log_1m if mask is None else jnp.where(mask, log_1m, 0.0)


def _split_bf16(x, axis):
    hi = x.astype(BF16)
    return jnp.concatenate([hi, (x - hi.astype(F32)).astype(BF16)], axis=axis)


def _block_weights(suffix, zn, mask):
    w = jnp.exp2(suffix - zn)
    return (w if mask is None else jnp.where(mask, w, 0.0)).astype(BF16)


def _attn_prompt_kernel(q_ref, k_ref, v_ref, sg_ref, tri_ref, o_ref, acc_ref):
    qi = pl.program_id(2)
    tq = q_ref.shape[0]
    tk = tri_ref.shape[0]
    heads = q_ref.shape[1] // HEAD_DIM
    tri2 = jnp.concatenate([tri_ref[...]] * 2, axis=0)
    hs = [slice(a * HEAD_DIM, (a + 1) * HEAD_DIM) for a in range(heads)]
    nt_dims = (((1,), (1,)), ((), ()))

    def rows_of(j):
        return pl.ds(j * tk if isinstance(j, int) else pl.multiple_of(j * tk, tk), tk)

    def logits(j):
        return tuple(lax.dot_general(q_ref[:, s], k_ref[rows_of(j), s], nt_dims,
                                     preferred_element_type=F32) for s in hs)

    def weights(zns, mask):
        pieces = [_split_bf16(_log2_one_minus_beta(zn, mask), axis=1) for zn in zns]
        suffixes = [jnp.dot(p, tri2, preferred_element_type=F32) for p in pieces]
        ws = tuple(_block_weights(sfx, zn, mask) for sfx, zn in zip(suffixes, zns))
        return ws, tuple(sfx[:, 0:1] for sfx in suffixes)

    def accumulate(ws, j, scales):
        for a, s in enumerate(hs):
            acc_ref[:, s] += scales[a] * jnp.dot(ws[a], v_ref[rows_of(j), s],
                                                 preferred_element_type=F32)

    def step(j, carries, mask):
        ws, totals = weights(logits(j), mask)
        accumulate(ws, j, [jnp.exp2(c) for c in carries])
        return tuple(c + tot for c, tot in zip(carries, totals))

    acc_ref[...] = jnp.zeros(acc_ref.shape, F32)
    rows = lax.broadcasted_iota(jnp.int32, (tq, tk), 0)
    cols = lax.broadcasted_iota(jnp.int32, (tq, tk), 1)
    carries = step(qi, (jnp.zeros((tq, 1), F32),) * heads, cols < rows)

    def any_live(carries):
        return jnp.max(functools.reduce(jnp.maximum, carries)) > EXP2_UNDERFLOW

    def body(state):
        t, _, carries = state
        carries = step(qi - 1 - t, carries, None)
        return t + 1, any_live(carries), carries

    lax.while_loop(lambda state: jnp.logical_and(state[0] < qi, state[1]), body,
                   (jnp.int32(0), any_live(carries), carries))
    o_ref[...] = (acc_ref[...] * sg_ref[...].astype(F32)).astype(BF16)


def _attend_prompt(q, kb, vb, sg, tri, *, batch, seq, tq, heads_per_step):
    nq = seq // tq
    assert tq == tri.shape[0], "query block and key block share the diagonal mask"
    cols = heads_per_step * HEAD_DIM
    qblock = pl.BlockSpec((tq, cols), lambda b, g, i: (b * nq + i, g))
    kvblock = pl.BlockSpec((seq, cols), lambda b, g, i: (b, g))
    return pl.pallas_call(
        _attn_prompt_kernel,
        grid=(batch, NUM_GROUPS // heads_per_step, nq),
        in_specs=[qblock, kvblock, kvblock, qblock,
                  pl.BlockSpec(tri.shape, lambda b, g, i: (0, 0))],
        out_specs=qblock,
        out_shape=jax.ShapeDtypeStruct(q.shape, BF16),
        scratch_shapes=[pltpu.VMEM((tq, cols), F32)],
        compiler_params=_compiler_params(("arbitrary", "arbitrary", "arbitrary")),
        name="attn_prompt",
    )(q, kb, vb, sg, tri)


def _attn_sample_kernel(q_ref, kn_ref, vn_ref, kc_hbm, vc_hbm, sg_ref, trit_ref, o_ref,
                        qnt_ref, knear_ref, vnear_ref, kfar_ref, vfar_ref, near_sems, far_sems,
                        *, t):
    step = pl.program_id(0)
    n = q_ref.shape[0] // t
    tk = trit_ref.shape[0]
    block_rows = tk * NUM_GROUPS
    n_cache = kc_hbm.shape[1] // block_rows
    lanes = NUM_GROUPS * t
    pairs = NUM_GROUPS // 2
    pad = HEAD_DIM
    nt_dims = (((1,), (1,)), ((), ()))
    slot = step % 2

    def cache_copies(stream, j, k_dst, v_dst, k_sem, v_sem):
        rows = pl.ds(j * block_rows, block_rows)
        return [pltpu.make_async_copy(kc_hbm.at[stream, rows, :], k_dst, k_sem),
                pltpu.make_async_copy(vc_hbm.at[stream, rows, :], v_dst, v_sem)]

    def near_copies(at_step, s):
        return [c for i in range(n) for c in cache_copies(
            at_step * n + i, n_cache - 1, knear_ref.at[s, i], vnear_ref.at[s, i],
            near_sems.at[0, s, i], near_sems.at[1, s, i])]

    @pl.when(step == 0)
    def _():
        for c in near_copies(0, 0):
            c.start()

    @pl.when(step + 1 < pl.num_programs(0))
    def _():
        for c in near_copies(step + 1, 1 - slot):
            c.start()

    qnt_ref[...] = jnp.zeros(qnt_ref.shape, BF16)
    for i in range(n):
        for h in range(NUM_GROUPS):
            p, half = divmod(h, 2)
            qnt_ref[i, p, h * t:(h + 1) * t, half * HEAD_DIM:(half + 1) * HEAD_DIM] = (
                q_ref[i * t:(i + 1) * t, h * HEAD_DIM:(h + 1) * HEAD_DIM])

    def neg_logits(i, load_k):
        zn = None
        for p in range(pairs):
            lhs = jnp.concatenate([load_k(2 * p), load_k(2 * p + 1)], axis=1).astype(BF16)
            d = lax.dot_general(lhs, qnt_ref[i, p], nt_dims, preferred_element_type=F32)
            zn = d if zn is None else zn + d
        return zn

    def walk(work):
        pieces = [[_split_bf16(_log2_one_minus_beta(zn, m), axis=0) for zn, _, _, m in blocks]
                  for blocks, _, _ in work]
        suffixes = [[jnp.dot(jnp.concatenate([tr, tr], axis=1), p, preferred_element_type=F32)
                     for (_, _, tr, _), p in zip(blocks, ps)]
                    for (blocks, _, _), ps in zip(work, pieces)]
        done = []
        for (blocks, carry, accs), sfxs in zip(work, suffixes):
            for (zn, load_v, _, m), sfx in zip(blocks, sfxs):
                w = jnp.exp2(sfx - zn + carry)
                if m is not None:
                    w = jnp.where(m, w, 0.0)
                w_t = w.T
                accs = [acc + jnp.dot(w_t[h * t:(h + 1) * t, :].astype(BF16),
                                      load_v(h).astype(BF16), preferred_element_type=F32)
                        for h, acc in enumerate(accs)]
                carry = carry + sfx[0:1, :]
            done.append((carry, tuple(accs)))
        return done

    head_rows = lambda ref, base, rows, h: ref[pl.ds(base + h, rows, stride=NUM_GROUPS), :]

    key_idx = lax.broadcasted_iota(jnp.int32, (pad, lanes), 0)
    query_idx = lax.broadcasted_iota(jnp.int32, (pad, lanes), 1) & (t - 1)
    zero_rows = lambda width: jnp.zeros((pad - t, width), F32)
    new_base = lambda i: i * t * NUM_GROUPS
    new_blocks = [
        (jnp.concatenate([neg_logits(i, lambda h, i=i: head_rows(kn_ref, new_base(i), t, h)),
                          zero_rows(lanes)], axis=0),
         lambda h, i=i: jnp.concatenate([head_rows(vn_ref, new_base(i), t, h),
                                         zero_rows(HEAD_DIM)], axis=0),
         trit_ref[0:pad, 0:pad], key_idx < query_idx) for i in range(n)]
    for c in near_copies(step, slot):
        c.wait()
    near_blocks = [
        (neg_logits(i, lambda h, i=i: head_rows(knear_ref.at[slot, i], 0, tk, h)),
         lambda h, i=i: head_rows(vnear_ref.at[slot, i], 0, tk, h), trit_ref[...], None)
        for i in range(n)]
    state = walk([([new_blocks[i], near_blocks[i]], jnp.zeros((1, lanes), F32),
                   [jnp.zeros((t, HEAD_DIM), F32)] * NUM_GROUPS) for i in range(n)])

    def any_live(state):
        return jnp.max(functools.reduce(jnp.maximum, [c for c, _ in state])) > EXP2_UNDERFLOW

    def older(loop_state):
        j, _, state = loop_state
        copies = [c for i in range(n) for c in cache_copies(
            step * n + i, j, kfar_ref.at[i], vfar_ref.at[i], far_sems.at[0, i], far_sems.at[1, i])]
        for c in copies:
            c.start()
        for c in copies:
            c.wait()
        far_blocks = [(neg_logits(i, lambda h, i=i: head_rows(kfar_ref.at[i], 0, tk, h)),
                       lambda h, i=i: head_rows(vfar_ref.at[i], 0, tk, h), trit_ref[...], None)
                      for i in range(n)]
        state = walk([([far_blocks[i]], state[i][0], list(state[i][1])) for i in range(n)])
        return j - 1, any_live(state), tuple(state)

    _, _, state = lax.while_loop(lambda s: jnp.logical_and(s[0] >= 0, s[1]), older,
                                 (jnp.int32(n_cache - 2), any_live(state), tuple(state)))

    for i, (_, accs) in enumerate(state):
        for h, acc in enumerate(accs):
            rows, hs = slice(i * t, (i + 1) * t), slice(h * HEAD_DIM, (h + 1) * HEAD_DIM)
            o_ref[rows, hs] = (acc * sg_ref[rows, hs].astype(F32)).astype(BF16)


def _attend_sample(q, k_new, v_new, cache_k, cache_v, sg, tri_t, *, batch, t):
    width = q.shape[1]
    tk = tri_t.shape[0]
    block_rows = tk * NUM_GROUPS
    assert t & (t - 1) == 0 and NUM_GROUPS * t == tk and cache_k.shape[1] % block_rows == 0
    n = SAMPLE_STREAMS_PER_STEP
    assert batch % n == 0
    rowblock = pl.BlockSpec((n * t, width), lambda b: (b, 0))
    newblock = pl.BlockSpec((n * t * NUM_GROUPS, HEAD_DIM), lambda b: (b, 0))
    in_hbm = pl.BlockSpec(memory_space=pl.ANY)
    key_blocks = lambda *lead: pltpu.VMEM((*lead, n, block_rows, HEAD_DIM), F32)
    return pl.pallas_call(
        functools.partial(_attn_sample_kernel, t=t),
        grid=(batch // n,),
        in_specs=[rowblock, newblock, newblock, in_hbm, in_hbm, rowblock,
                  pl.BlockSpec(tri_t.shape, lambda b: (0, 0))],
        out_specs=rowblock,
        out_shape=jax.ShapeDtypeStruct(q.shape, BF16),
        scratch_shapes=[pltpu.VMEM((n, NUM_GROUPS // 2, NUM_GROUPS * t, 2 * HEAD_DIM), BF16),
                        key_blocks(2), key_blocks(2), key_blocks(), key_blocks(),
                        pltpu.SemaphoreType.DMA((2, 2, n)), pltpu.SemaphoreType.DMA((2, n))],
        compiler_params=_compiler_params(("arbitrary",)),
        name="attn_sample",
    )(q, k_new, v_new, cache_k, cache_v, sg, tri_t)


def _out_kernel(x_ref, msb_ref, mcv_ref, wsb_ref, wcv_ref, g_ref, y_ref):
    y = (x_ref[...]
         + jnp.dot(msb_ref[...], wsb_ref[...], preferred_element_type=F32)
         + jnp.dot(mcv_ref[...], wcv_ref[...], preferred_element_type=F32))
    y_ref[...] = _rmsnorm_rows(y, g_ref[...])


def _merge_out(x2d, mix_sb, mix_cv, w_bf, final_g, *, tm):
    m, d = x2d.shape
    half = mix_sb.shape[1]
    rows = lambda width: pl.BlockSpec((tm, width), lambda i: (i, 0))
    w_rows = lambda r: pl.BlockSpec((half, d), lambda i: (r, 0))
    return pl.pallas_call(
        _out_kernel,
        grid=(m // tm,),
        in_specs=[rows(d), rows(half), rows(half), w_rows(0), w_rows(1),
                  pl.BlockSpec((1, d), lambda i: (0, 0))],
        out_specs=rows(d),
        out_shape=jax.ShapeDtypeStruct((m, d), F32),
        compiler_params=_compiler_params(("arbitrary",)),
        name="merge_out",
    )(x2d, mix_sb, mix_cv, w_bf, w_bf, final_g.reshape(1, d))


def _lower_tri(n):
    j = lax.broadcasted_iota(jnp.int32, (n, n), 0)
    s = lax.broadcasted_iota(jnp.int32, (n, n), 1)
    return (j >= s).astype(BF16)


def kernel(x_prompt, x_sample, cache_k, cache_v, state_conv, norm_g, w_in, conv_w, w_out, final_g):
    depth = w_in.shape[0]
    assert depth == 1, "single-layer step"
    bsz, seq, d = x_prompt.shape
    dbsz, dseq, _ = x_sample.shape
    past = cache_k.shape[2]
    width = NUM_GROUPS * HEAD_DIM

    w_in_bf = w_in[0].astype(BF16)
    w_out_bf = w_out[0].astype(BF16)
    tri = _lower_tri(KEY_BLOCK)

    xp = x_prompt.reshape(bsz * seq, d)
    zeros_left = jnp.zeros((bsz, CONV_W - 1, width), F32)
    qp, kp, vp, kbp, vbp, sgp, mcvp, tailp = _project(
        xp, norm_g[0], w_in_bf, conv_w[0], zeros_left, seq_rows=seq, tm=PROJ_ROWS)
    msbp = _attend_prompt(qp, kbp, vbp, sgp, tri, batch=bsz, seq=seq, tq=KEY_BLOCK,
                          heads_per_step=PROMPT_HEADS_PER_STEP)
    yp = _merge_out(xp, msbp, mcvp, w_out_bf, final_g, tm=OUT_ROWS)

    xs = x_sample.reshape(dbsz * dseq, d)
    qs, ks, vs, _, _, sgs, mcvs, tails = _project(
        xs, norm_g[0], w_in_bf, conv_w[0], state_conv[0], seq_rows=dseq, tm=PROJ_ROWS)
    msbs = _attend_sample(qs, ks, vs,
                          cache_k[0].reshape(dbsz, past * NUM_GROUPS, HEAD_DIM),
                          cache_v[0].reshape(dbsz, past * NUM_GROUPS, HEAD_DIM),
                          sgs, tri.T, batch=dbsz, t=dseq)
    ys = _merge_out(xs, msbs, mcvs, w_out_bf, final_g, tm=OUT_ROWS)

    heads = lambda a, b_, t_: a.reshape(1, b_, t_, NUM_GROUPS, HEAD_DIM)
    return (yp.reshape(bsz, seq, d), ys.reshape(dbsz, dseq, d),
            heads(kp, bsz, seq), heads(vp, bsz, seq), tailp[None],
            heads(ks, dbsz, dseq), heads(vs, dbsz, dseq), tails[None])
```

```python
import functools
import math

import jax
import jax.numpy as jnp
from jax import lax
from jax.experimental import pallas as pl
from jax.experimental.pallas import tpu as pltpu

F32 = jnp.float32
BF16 = jnp.bfloat16

HEAD_DIM = 128
NUM_GROUPS = 8
NUM_SEGMENTS = 8
CONV_W = 3
EPS = 1e-6
LOG2E = math.log2(math.e)
Q_PRESCALE = -(HEAD_DIM ** -0.5) * LOG2E
MXU_DEPTH = 256
KEY_BLOCK = MXU_DEPTH
PROJ_ROWS = 1024
OUT_ROWS = 512
PROMPT_HEADS_PER_STEP = 8
SAMPLE_STREAMS_PER_STEP = 4
EXP2_UNDERFLOW = -1e30

VMEM_LIMIT_BYTES = 56 * 1024 * 1024


def _compiler_params(semantics):
    return pltpu.CompilerParams(dimension_semantics=semantics,
                                vmem_limit_bytes=VMEM_LIMIT_BYTES)


def _rmsnorm_rows(x, g):
    r = lax.rsqrt(jnp.mean(x * x, axis=-1, keepdims=True) + EPS)
    return (x * r) * g


def _silu(x):
    return x * (1.0 / (1.0 + jnp.exp(-x)))


def _proj_kernel(x_ref, g_ref, wq_ref, wk_ref, wv_ref, wgs_ref, wb_ref, wc_ref, wu_ref, wgc_ref,
                 cw_ref, left_ref,
                 q_ref, k_ref, v_ref, kb_ref, vb_ref, sg_ref, mcv_ref, tail_ref,
                 hn_ref, carry_ref, *, seq_rows, tiles_per_seq):
    i = pl.program_id(0)
    h = pl.program_id(1)
    tm = x_ref.shape[0]

    @pl.when(h == 0)
    def _():
        hn_ref[...] = _rmsnorm_rows(x_ref[...], g_ref[...]).astype(BF16)

    @pl.when(jnp.logical_and(i == 0, h == 0))
    def _():
        carry_ref[...] = jnp.zeros(carry_ref.shape, F32)

    def project(*w_refs):
        w = jnp.concatenate([r[...] for r in w_refs], axis=1)
        acc = jnp.dot(hn_ref[...], w, preferred_element_type=F32)
        return [acc[:, s * HEAD_DIM:(s + 1) * HEAD_DIM] for s in range(len(w_refs))]

    c, u = project(wc_ref, wu_ref)
    b, g_cv = project(wb_ref, wgc_ref)
    q, k = project(wq_ref, wk_ref)
    v, g_sb = project(wv_ref, wgs_ref)

    q_ref[...] = (q * Q_PRESCALE).astype(BF16)
    k_ref[pl.ds(h, tm, stride=NUM_GROUPS), :] = k
    v_ref[pl.ds(h, tm, stride=NUM_GROUPS), :] = v
    kb_ref[...] = k.astype(BF16)
    vb_ref[...] = v.astype(BF16)
    sg_ref[...] = _silu(g_sb).astype(BF16)

    cu = c * u
    if tiles_per_seq > 1:
        left = jnp.where(i % tiles_per_seq == 0, left_ref[...], carry_ref[h][None, 0:2, :])
        carry_ref[h, 0:2, :] = cu[tm - 2:tm, :]
    else:
        left = left_ref[...]
    rows = seq_rows if tiles_per_seq == 1 else tm
    nseq = tm // rows
    expand = lambda a: jnp.broadcast_to(a, (nseq, rows, HEAD_DIM)).reshape(tm, HEAD_DIM)
    l0 = expand(left[:, 0:1, :])
    l1 = expand(left[:, 1:2, :])
    rs = lax.broadcasted_iota(jnp.int32, (tm, HEAD_DIM), 0) & (rows - 1)
    r1 = jnp.where(rs == 0, l1, pltpu.roll(cu, 1, axis=0))
    r2 = jnp.where(rs == 0, l0, jnp.where(rs == 1, l1, pltpu.roll(cu, 2, axis=0)))
    cw = cw_ref[...]
    conv = cw[0:1, :] * r2 + cw[1:2, :] * r1 + cw[2:3, :] * cu
    mcv_ref[...] = (b * conv * _silu(g_cv)).astype(BF16)
    tail_ref[...] = cu.reshape(nseq, rows, HEAD_DIM)[:, rows - 2:rows, :]


def _project(x2d, norm_g, w_bf, conv_w, left, *, seq_rows, tm):
    m, d = x2d.shape
    segment = lambda s: pl.BlockSpec((d, HEAD_DIM), lambda i, h: (0, s * NUM_GROUPS + h))
    width = NUM_GROUPS * HEAD_DIM
    if tm >= seq_rows:
        tiles_per_seq, ns = 1, tm // seq_rows
        left_idx = lambda i, h: (i, 0, h)
    else:
        tiles_per_seq, ns = seq_rows // tm, 1
        left_idx = lambda i, h: (i // tiles_per_seq, 0, h)
    n_tails = (m // tm) * ns
    tile = lambda dt: jax.ShapeDtypeStruct((m, width), dt)
    native = jax.ShapeDtypeStruct((m * NUM_GROUPS, HEAD_DIM), F32)
    col_block = pl.BlockSpec((tm, HEAD_DIM), lambda i, h: (i, h))
    native_block = pl.BlockSpec((tm * NUM_GROUPS, HEAD_DIM), lambda i, h: (i, 0))
    kern = functools.partial(_proj_kernel, seq_rows=seq_rows, tiles_per_seq=tiles_per_seq)
    q, k, v, kb, vb, sg, mcv, tails = pl.pallas_call(
        kern,
        grid=(m // tm, NUM_GROUPS),
        in_specs=[
            pl.BlockSpec((tm, d), lambda i, h: (i, 0)),
            pl.BlockSpec((1, d), lambda i, h: (0, 0)),
            *[segment(s) for s in range(NUM_SEGMENTS)],
            pl.BlockSpec((CONV_W, HEAD_DIM), lambda i, h: (0, h)),
            pl.BlockSpec((ns, 2, HEAD_DIM), left_idx),
        ],
        out_specs=[col_block, native_block, native_block, col_block, col_block, col_block,
                   col_block, pl.BlockSpec((ns, 2, HEAD_DIM), lambda i, h: (i, 0, h))],
        out_shape=[tile(BF16), native, native, tile(BF16), tile(BF16), tile(BF16), tile(BF16),
                   jax.ShapeDtypeStruct((n_tails, 2, width), F32)],
        scratch_shapes=[pltpu.VMEM((tm, d), BF16),
                        pltpu.VMEM((NUM_GROUPS, 8, HEAD_DIM), F32)],
        compiler_params=_compiler_params(("arbitrary", "arbitrary")),
        name="proj",
    )(x2d, norm_g.reshape(1, d), *([w_bf] * NUM_SEGMENTS), conv_w, left)
    tails = tails.reshape(-1, tiles_per_seq, 2, width)[:, tiles_per_seq - 1]
    return q, k, v, kb, vb, sg, mcv, tails


def _log2_one_minus_beta(zn, mask):
    softplus2 = jnp.log(1.0 + jnp.exp2(-jnp.abs(zn))) * LOG2E
    log_1m = jnp.minimum(zn, 0.0) - softplus2
    return log_1m if mask is None else jnp.where(mask, log_1m, 0.0)


def _split_bf16(x, axis):
    hi = x.astype(BF16)
    return jnp.concatenate([hi, (x - hi.astype(F32)).astype(BF16)], axis=axis)


def _block_weights(suffix, zn, mask):
    w = jnp.exp2(suffix - zn)
    return (w if mask is None else jnp.where(mask, w, 0.0)).astype(BF16)


def _attn_prompt_kernel(q_ref, k_ref, v_ref, sg_ref, tri_ref, o_ref, acc_ref):
    qi = pl.program_id(2)
    tq = q_ref.shape[0]
    tk = tri_ref.shape[0]
    heads = q_ref.shape[1] // HEAD_DIM
    tri2 = jnp.concatenate([tri_ref[...]] * 2, axis=0)
    hs = [slice(a * HEAD_DIM, (a + 1) * HEAD_DIM) for a in range(heads)]
    nt_dims = (((1,), (1,)), ((), ()))

    def rows_of(j):
        return pl.ds(j * tk if isinstance(j, int) else pl.multiple_of(j * tk, tk), tk)

    def logits(j):
        return tuple(lax.dot_general(q_ref[:, s], k_ref[rows_of(j), s], nt_dims,
                                     preferred_element_type=F32) for s in hs)

    def weights(zns, mask):
        pieces = [_split_bf16(_log2_one_minus_beta(zn, mask), axis=1) for zn in zns]
        suffixes = [jnp.dot(p, tri2, preferred_element_type=F32) for p in pieces]
        ws = tuple(_block_weights(sfx, zn, mask) for sfx, zn in zip(suffixes, zns))
        return ws, tuple(sfx[:, 0:1] for sfx in suffixes)

    def accumulate(ws, j, scales):
        for a, s in enumerate(hs):
            acc_ref[:, s] += scales[a] * jnp.dot(ws[a], v_ref[rows_of(j), s],
                                                 preferred_element_type=F32)

    def step(j, carries, mask):
        ws, totals = weights(logits(j), mask)
        accumulate(ws, j, [jnp.exp2(c) for c in carries])
        return tuple(c + tot for c, tot in zip(carries, totals))

    acc_ref[...] = jnp.zeros(acc_ref.shape, F32)
    rows = lax.broadcasted_iota(jnp.int32, (tq, tk), 0)
    cols = lax.broadcasted_iota(jnp.int32, (tq, tk), 1)
    carries = step(qi, (jnp.zeros((tq, 1), F32),) * heads, cols < rows)

    def any_live(carries):
        return jnp.max(functools.reduce(jnp.maximum, carries)) > EXP2_UNDERFLOW

    def body(state):
        t, _, carries = state
        carries = step(qi - 1 - t, carries, None)
        return t + 1, any_live(carries), carries

    lax.while_loop(lambda state: jnp.logical_and(state[0] < qi, state[1]), body,
                   (jnp.int32(0), any_live(carries), carries))
    o_ref[...] = (acc_ref[...] * sg_ref[...].astype(F32)).astype(BF16)


def _attend_prompt(q, kb, vb, sg, tri, *, batch, seq, tq, heads_per_step):
    nq = seq // tq
    assert tq == tri.shape[0], "query block and key block share the diagonal mask"
    cols = heads_per_step * HEAD_DIM
    qblock = pl.BlockSpec((tq, cols), lambda b, g, i: (b * nq + i, g))
    kvblock = pl.BlockSpec((seq, cols), lambda b, g, i: (b, g))
    return pl.pallas_call(
        _attn_prompt_kernel,
        grid=(batch, NUM_GROUPS // heads_per_step, nq),
        in_specs=[qblock, kvblock, kvblock, qblock,
                  pl.BlockSpec(tri.shape, lambda b, g, i: (0, 0))],
        out_specs=qblock,
        out_shape=jax.ShapeDtypeStruct(q.shape, BF16),
        scratch_shapes=[pltpu.VMEM((tq, cols), F32)],
        compiler_params=_compiler_params(("arbitrary", "arbitrary", "arbitrary")),
        name="attn_prompt",
    )(q, kb, vb, sg, tri)


def _attn_sample_kernel(q_ref, kn_ref, vn_ref, kc_hbm, vc_hbm, sg_ref, trit_ref, o_ref,
                        qnt_ref, knear_ref, vnear_ref, kfar_ref, vfar_ref, near_sems, far_sems,
                        *, t):
    step = pl.program_id(0)
    n = q_ref.shape[0] // t
    tk = trit_ref.shape[0]
    block_rows = tk * NUM_GROUPS
    n_cache = kc_hbm.shape[1] // block_rows
    lanes = NUM_GROUPS * t
    pairs = NUM_GROUPS // 2
    pad = HEAD_DIM
    nt_dims = (((1,), (1,)), ((), ()))
    slot = step % 2

    def cache_copies(stream, j, k_dst, v_dst, k_sem, v_sem):
        rows = pl.ds(j * block_rows, block_rows)
        return [pltpu.make_async_copy(kc_hbm.at[stream, rows, :], k_dst, k_sem),
                pltpu.make_async_copy(vc_hbm.at[stream, rows, :], v_dst, v_sem)]

    def near_copies(at_step, s):
        return [c for i in range(n) for c in cache_copies(
            at_step * n + i, n_cache - 1, knear_ref.at[s, i], vnear_ref.at[s, i],
            near_sems.at[0, s, i], near_sems.at[1, s, i])]

    @pl.when(step == 0)
    def _():
        for c in near_copies(0, 0):
            c.start()

    @pl.when(step + 1 < pl.num_programs(0))
    def _():
        for c in near_copies(step + 1, 1 - slot):
            c.start()

    qnt_ref[...] = jnp.zeros(qnt_ref.shape, BF16)
    for i in range(n):
        for h in range(NUM_GROUPS):
            p, half = divmod(h, 2)
            qnt_ref[i, p, h * t:(h + 1) * t, half * HEAD_DIM:(half + 1) * HEAD_DIM] = (
                q_ref[i * t:(i + 1) * t, h * HEAD_DIM:(h + 1) * HEAD_DIM])

    def neg_logits(i, load_k):
        zn = None
        for p in range(pairs):
            lhs = jnp.concatenate([load_k(2 * p), load_k(2 * p + 1)], axis=1).astype(BF16)
            d = lax.dot_general(lhs, qnt_ref[i, p], nt_dims, preferred_element_type=F32)
            zn = d if zn is None else zn + d
        return zn

    def walk(work):
        pieces = [[_split_bf16(_log2_one_minus_beta(zn, m), axis=0) for zn, _, _, m in blocks]
                  for blocks, _, _ in work]
        suffixes = [[jnp.dot(jnp.concatenate([tr, tr], axis=1), p, preferred_element_type=F32)
                     for (_, _, tr, _), p in zip(blocks, ps)]
                    for (blocks, _, _), ps in zip(work, pieces)]
        done = []
        for (blocks, carry, accs), sfxs in zip(work, suffixes):
            for (zn, load_v, _, m), sfx in zip(blocks, sfxs):
                w = jnp.exp2(sfx - zn + carry)
                if m is not None:
                    w = jnp.where(m, w, 0.0)
                w_t = w.T
                accs = [acc + jnp.dot(w_t[h * t:(h + 1) * t, :].astype(BF16),
                                      load_v(h).astype(BF16), preferred_element_type=F32)
                        for h, acc in enumerate(accs)]
                carry = carry + sfx[0:1, :]
            done.append((carry, tuple(accs)))
        return done

    head_rows = lambda ref, base, rows, h: ref[pl.ds(base + h, rows, stride=NUM_GROUPS), :]

    key_idx = lax.broadcasted_iota(jnp.int32, (pad, lanes), 0)
    query_idx = lax.broadcasted_iota(jnp.int32, (pad, lanes), 1) & (t - 1)
    zero_rows = lambda width: jnp.zeros((pad - t, width), F32)
    new_base = lambda i: i * t * NUM_GROUPS
    new_blocks = [
        (jnp.concatenate([neg_logits(i, lambda h, i=i: head_rows(kn_ref, new_base(i), t, h)),
                          zero_rows(lanes)], axis=0),
         lambda h, i=i: jnp.concatenate([head_rows(vn_ref, new_base(i), t, h),
                                         zero_rows(HEAD_DIM)], axis=0),
         trit_ref[0:pad, 0:pad], key_idx < query_idx) for i in range(n)]
    for c in near_copies(step, slot):
        c.wait()
    near_blocks = [
        (neg_logits(i, lambda h, i=i: head_rows(knear_ref.at[slot, i], 0, tk, h)),
         lambda h, i=i: head_rows(vnear_ref.at[slot, i], 0, tk, h), trit_ref[...], None)
        for i in range(n)]
    state = walk([([new_blocks[i], near_blocks[i]], jnp.zeros((1, lanes), F32),
                   [jnp.zeros((t, HEAD_DIM), F32)] * NUM_GROUPS) for i in range(n)])

    def any_live(state):
        return jnp.max(functools.reduce(jnp.maximum, [c for c, _ in state])) > EXP2_UNDERFLOW

    def older(loop_state):
        j, _, state = loop_state
        copies = [c for i in range(n) for c in cache_copies(
            step * n + i, j, kfar_ref.at[i], vfar_ref.at[i], far_sems.at[0, i], far_sems.at[1, i])]
        for c in copies:
            c.start()
        for c in copies:
            c.wait()
        far_blocks = [(neg_logits(i, lambda h, i=i: head_rows(kfar_ref.at[i], 0, tk, h)),
                       lambda h, i=i: head_rows(vfar_ref.at[i], 0, tk, h), trit_ref[...], None)
                      for i in range(n)]
        state = walk([([far_blocks[i]], state[i][0], list(state[i][1])) for i in range(n)])
        return j - 1, any_live(state), tuple(state)

    _, _, state = lax.while_loop(lambda s: jnp.logical_and(s[0] >= 0, s[1]), older,
                                 (jnp.int32(n_cache - 2), any_live(state), tuple(state)))

    for i, (_, accs) in enumerate(state):
        for h, acc in enumerate(accs):
            rows, hs = slice(i * t, (i + 1) * t), slice(h * HEAD_DIM, (h + 1) * HEAD_DIM)
            o_ref[rows, hs] = (acc * sg_ref[rows, hs].astype(F32)).astype(BF16)


def _attend_sample(q, k_new, v_new, cache_k, cache_v, sg, tri_t, *, batch, t):
    width = q.shape[1]
    tk = tri_t.shape[0]
    block_rows = tk * NUM_GROUPS
    assert t & (t - 1) == 0 and NUM_GROUPS * t == tk and cache_k.shape[1] % block_rows == 0
    n = SAMPLE_STREAMS_PER_STEP
    assert batch % n == 0
    rowblock = pl.BlockSpec((n * t, width), lambda b: (b, 0))
    newblock = pl.BlockSpec((n * t * NUM_GROUPS, HEAD_DIM), lambda b: (b, 0))
    in_hbm = pl.BlockSpec(memory_space=pl.ANY)
    key_blocks = lambda *lead: pltpu.VMEM((*lead, n, block_rows, HEAD_DIM), F32)
    return pl.pallas_call(
        functools.partial(_attn_sample_kernel, t=t),
        grid=(batch // n,),
        in_specs=[rowblock, newblock, newblock, in_hbm, in_hbm, rowblock,
                  pl.BlockSpec(tri_t.shape, lambda b: (0, 0))],
        out_specs=rowblock,
        out_shape=jax.ShapeDtypeStruct(q.shape, BF16),
        scratch_shapes=[pltpu.VMEM((n, NUM_GROUPS // 2, NUM_GROUPS * t, 2 * HEAD_DIM), BF16),
                        key_blocks(2), key_blocks(2), key_blocks(), key_blocks(),
                        pltpu.SemaphoreType.DMA((2, 2, n)), pltpu.SemaphoreType.DMA((2, n))],
        compiler_params=_compiler_params(("arbitrary",)),
        name="attn_sample",
    )(q, k_new, v_new, cache_k, cache_v, sg, tri_t)


def _out_kernel(x_ref, msb_ref, mcv_ref, wsb_ref, wcv_ref, g_ref, y_ref):
    y = (x_ref[...]
         + jnp.dot(msb_ref[...], wsb_ref[...], preferred_element_type=F32)
         + jnp.dot(mcv_ref[...], wcv_ref[...], preferred_element_type=F32))
    y_ref[...] = _rmsnorm_rows(y, g_ref[...])


def _merge_out(x2d, mix_sb, mix_cv, w_bf, final_g, *, tm):
    m, d = x2d.shape
    half = mix_sb.shape[1]
    rows = lambda width: pl.BlockSpec((tm, width), lambda i: (i, 0))
    w_rows = lambda r: pl.BlockSpec((half, d), lambda i: (r, 0))
    return pl.pallas_call(
        _out_kernel,
        grid=(m // tm,),
        in_specs=[rows(d), rows(half), rows(half), w_rows(0), w_rows(1),
                  pl.BlockSpec((1, d), lambda i: (0, 0))],
        out_specs=rows(d),
        out_shape=jax.ShapeDtypeStruct((m, d), F32),
        compiler_params=_compiler_params(("arbitrary",)),
        name="merge_out",
    )(x2d, mix_sb, mix_cv, w_bf, w_bf, final_g.reshape(1, d))


def _lower_tri(n):
    j = lax.broadcasted_iota(jnp.int32, (n, n), 0)
    s = lax.broadcasted_iota(jnp.int32, (n, n), 1)
    return (j >= s).astype(BF16)


def kernel(x_prompt, x_sample, cache_k, cache_v, state_conv, norm_g, w_in, conv_w, w_out, final_g):
    depth = w_in.shape[0]
    assert depth == 1, "single-layer step"
    bsz, seq, d = x_prompt.shape
    dbsz, dseq, _ = x_sample.shape
    past = cache_k.shape[2]
    width = NUM_GROUPS * HEAD_DIM

    w_in_bf = w_in[0].astype(BF16)
    w_out_bf = w_out[0].astype(BF16)
    tri = _lower_tri(KEY_BLOCK)

    xp = x_prompt.reshape(bsz * seq, d)
    zeros_left = jnp.zeros((bsz, CONV_W - 1, width), F32)
    qp, kp, vp, kbp, vbp, sgp, mcvp, tailp = _project(
        xp, norm_g[0], w_in_bf, conv_w[0], zeros_left, seq_rows=seq, tm=PROJ_ROWS)
    msbp = _attend_prompt(qp, kbp, vbp, sgp, tri, batch=bsz, seq=seq, tq=KEY_BLOCK,
                          heads_per_step=PROMPT_HEADS_PER_STEP)
    yp = _merge_out(xp, msbp, mcvp, w_out_bf, final_g, tm=OUT_ROWS)

    xs = x_sample.reshape(dbsz * dseq, d)
    qs, ks, vs, _, _, sgs, mcvs, tails = _project(
        xs, norm_g[0], w_in_bf, conv_w[0], state_conv[0], seq_rows=dseq, tm=PROJ_ROWS)
    msbs = _attend_sample(qs, ks, vs,
                          cache_k[0].reshape(dbsz, past * NUM_GROUPS, HEAD_DIM),
                          cache_v[0].reshape(dbsz, past * NUM_GROUPS, HEAD_DIM),
                          sgs, tri.T, batch=dbsz, t=dseq)
    ys = _merge_out(xs, msbs, mcvs, w_out_bf, final_g, tm=OUT_ROWS)

    heads = lambda a, b_, t_: a.reshape(1, b_, t_, NUM_GROUPS, HEAD_DIM)
    return (yp.reshape(bsz, seq, d), ys.reshape(dbsz, dseq, d),
            heads(kp, bsz, seq), heads(vp, bsz, seq), tailp[None],
            heads(ks, dbsz, dseq), heads(vs, dbsz, dseq), tails[None])
```

```python
import functools
import math

import jax
import jax.numpy as jnp
from jax import lax
from jax.experimental import pallas as pl
from jax.experimental.pallas import tpu as pltpu

F32 = jnp.float32
BF16 = jnp.bfloat16

HEAD_DIM = 128
NUM_GROUPS = 8
NUM_SEGMENTS = 8
CONV_W = 3
EPS = 1e-6
LOG2E = math.log2(math.e)
Q_PRESCALE = -(HEAD_DIM ** -0.5) * LOG2E
MXU_DEPTH = 256
KEY_BLOCK = MXU_DEPTH
PROJ_ROWS = 1024
OUT_ROWS = 512
PROMPT_HEADS_PER_STEP = 8
SAMPLE_STREAMS_PER_STEP = 4
EXP2_UNDERFLOW = -160.0

VMEM_LIMIT_BYTES = 56 * 1024 * 1024


def _compiler_params(semantics):
    return pltpu.CompilerParams(dimension_semantics=semantics,
                                vmem_limit_bytes=VMEM_LIMIT_BYTES)


def _rmsnorm_rows(x, g):
    r = lax.rsqrt(jnp.mean(x * x, axis=-1, keepdims=True) + EPS)
    return (x * r) * g


def _silu(x):
    return x * (1.0 / (1.0 + jnp.exp(-x)))


def _proj_kernel(*refs, seq_rows, tiles_per_seq, copy_weights):
    x_ref, g_ref = refs[0:2]
    w_refs = refs[2:2 + NUM_SEGMENTS]
    cw_ref, left_ref = refs[10:12]
    q_ref, k_ref, v_ref, kb_ref, vb_ref, sg_ref, mcv_ref, tail_ref = refs[12:20]
    wcopy_refs = refs[20:20 + NUM_SEGMENTS] if copy_weights else ()
    hn_ref, carry_ref = refs[-2:]
    i = pl.program_id(0)
    h = pl.program_id(1)
    tm = x_ref.shape[0]

    @pl.when(h == 0)
    def _():
        hn_ref[...] = _rmsnorm_rows(x_ref[...], g_ref[...]).astype(BF16)

    @pl.when(jnp.logical_and(i == 0, h == 0))
    def _():
        carry_ref[...] = jnp.zeros(carry_ref.shape, F32)

    wq, wk, wv, wgs, wb, wc, wu, wgc = [r[...].astype(BF16) for r in w_refs]
    for copy_ref, w in zip(wcopy_refs, (wq, wk, wv, wgs, wb, wc, wu, wgc)):
        copy_ref[...] = w

    def project(*ws):
        acc = jnp.dot(hn_ref[...], jnp.concatenate(ws, axis=1), preferred_element_type=F32)
        return [acc[:, s * HEAD_DIM:(s + 1) * HEAD_DIM] for s in range(len(ws))]

    c, u = project(wc, wu)
    b, g_cv = project(wb, wgc)
    q, k = project(wq, wk)
    v, g_sb = project(wv, wgs)

    q_ref[...] = (q * Q_PRESCALE).astype(BF16)
    k_ref[pl.ds(h, tm, stride=NUM_GROUPS), :] = k
    v_ref[pl.ds(h, tm, stride=NUM_GROUPS), :] = v
    kb_ref[...] = k.astype(BF16)
    vb_ref[...] = v.astype(BF16)
    sg_ref[...] = _silu(g_sb).astype(BF16)

    cu = c * u
    if tiles_per_seq > 1:
        left = jnp.where(i % tiles_per_seq == 0, left_ref[...], carry_ref[h][None, 0:2, :])
        carry_ref[h, 0:2, :] = cu[tm - 2:tm, :]
    else:
        left = left_ref[...]
    rows = seq_rows if tiles_per_seq == 1 else tm
    nseq = tm // rows
    expand = lambda a: jnp.broadcast_to(a, (nseq, rows, HEAD_DIM)).reshape(tm, HEAD_DIM)
    l0 = expand(left[:, 0:1, :])
    l1 = expand(left[:, 1:2, :])
    rs = lax.broadcasted_iota(jnp.int32, (tm, HEAD_DIM), 0) & (rows - 1)
    r1 = jnp.where(rs == 0, l1, pltpu.roll(cu, 1, axis=0))
    r2 = jnp.where(rs == 0, l0, jnp.where(rs == 1, l1, pltpu.roll(cu, 2, axis=0)))
    cw = cw_ref[...]
    conv = cw[0:1, :] * r2 + cw[1:2, :] * r1 + cw[2:3, :] * cu
    mcv_ref[...] = (b * conv * _silu(g_cv)).astype(BF16)
    tail_ref[...] = cu.reshape(nseq, rows, HEAD_DIM)[:, rows - 2:rows, :]


def _project(x2d, norm_g, weights, conv_w, left, *, seq_rows, tm):
    m, d = x2d.shape
    width = NUM_GROUPS * HEAD_DIM
    copy_weights = not isinstance(weights, (list, tuple))
    if copy_weights:
        assert m == tm, "bf16 weight copies are written once, by a single row tile"
        w_args = [weights] * NUM_SEGMENTS
        segment = lambda s: pl.BlockSpec((d, HEAD_DIM), lambda i, h: (0, s * NUM_GROUPS + h))
    else:
        w_args = list(weights)
        segment = lambda s: pl.BlockSpec((d, HEAD_DIM), lambda i, h: (0, h))
    once = dict(pipeline_mode=pl.Buffered(1)) if m == tm else {}
    if tm >= seq_rows:
        tiles_per_seq, ns = 1, tm // seq_rows
        left_idx = lambda i, h: (i, 0, h)
    else:
        tiles_per_seq, ns = seq_rows // tm, 1
        left_idx = lambda i, h: (i // tiles_per_seq, 0, h)
    n_tails = (m // tm) * ns
    tile = lambda dt: jax.ShapeDtypeStruct((m, width), dt)
    native = jax.ShapeDtypeStruct((m * NUM_GROUPS, HEAD_DIM), F32)
    col_block = pl.BlockSpec((tm, HEAD_DIM), lambda i, h: (i, h))
    native_block = pl.BlockSpec((tm * NUM_GROUPS, HEAD_DIM), lambda i, h: (i, 0), **once)
    copy_specs, copy_shapes = [], []
    if copy_weights:
        copy_specs = [pl.BlockSpec((d, HEAD_DIM), lambda i, h: (0, h))] * NUM_SEGMENTS
        copy_shapes = [jax.ShapeDtypeStruct((d, width), BF16)] * NUM_SEGMENTS
    kern = functools.partial(_proj_kernel, seq_rows=seq_rows, tiles_per_seq=tiles_per_seq,
                             copy_weights=copy_weights)
    q, k, v, kb, vb, sg, mcv, tails, *w_copies = pl.pallas_call(
        kern,
        grid=(m // tm, NUM_GROUPS),
        in_specs=[
            pl.BlockSpec((tm, d), lambda i, h: (i, 0), **once),
            pl.BlockSpec((1, d), lambda i, h: (0, 0)),
            *[segment(s) for s in range(NUM_SEGMENTS)],
            pl.BlockSpec((CONV_W, HEAD_DIM), lambda i, h: (0, h)),
            pl.BlockSpec((ns, 2, HEAD_DIM), left_idx),
        ],
        out_specs=[col_block, native_block, native_block, col_block, col_block, col_block,
                   col_block, pl.BlockSpec((ns, 2, HEAD_DIM), lambda i, h: (i, 0, h)), *copy_specs],
        out_shape=[tile(BF16), native, native, tile(BF16), tile(BF16), tile(BF16), tile(BF16),
                   jax.ShapeDtypeStruct((n_tails, 2, width), F32), *copy_shapes],
        scratch_shapes=[pltpu.VMEM((tm, d), BF16),
                        pltpu.VMEM((NUM_GROUPS, 8, HEAD_DIM), F32)],
        compiler_params=_compiler_params(("arbitrary", "arbitrary")),
        name="proj",
    )(x2d, norm_g.reshape(1, d), *w_args, conv_w, left)
    tails = tails.reshape(-1, tiles_per_seq, 2, width)[:, tiles_per_seq - 1]
    return q, k, v, kb, vb, sg, mcv, tails, w_copies


def _log2_one_minus_beta(zn, mask):
    softplus2 = jnp.log(1.0 + jnp.exp2(-jnp.abs(zn))) * LOG2E
    log_1m = jnp.minimum(zn, 0.0) - softplus2
    return log_1m if mask is None else jnp.where(mask, log_1m, 0.0)


def _split_bf16(x, axis):
    hi = x.astype(BF16)
    return jnp.concatenate([hi, (x - hi.astype(F32)).astype(BF16)], axis=axis)


def _block_weights(suffix, zn, mask):
    w = jnp.exp2(suffix - zn)
    return (w if mask is None else jnp.where(mask, w, 0.0)).astype(BF16)


def _attn_prompt_kernel(q_ref, k_ref, v_ref, sg_ref, tri_ref, o_ref, acc_ref):
    qi = pl.program_id(2)
    tq = q_ref.shape[0]
    tk = tri_ref.shape[0]
    heads = q_ref.shape[1] // HEAD_DIM
    tri2 = jnp.concatenate([tri_ref[...]] * 2, axis=0)
    hs = [slice(a * HEAD_DIM, (a + 1) * HEAD_DIM) for a in range(heads)]
    nt_dims = (((1,), (1,)), ((), ()))

    def rows_of(j):
        return pl.ds(j * tk if isinstance(j, int) else pl.multiple_of(j * tk, tk), tk)

    def logits(j):
        return tuple(lax.dot_general(q_ref[:, s], k_ref[rows_of(j), s], nt_dims,
                                     preferred_element_type=F32) for s in hs)

    def weights(zns, mask):
        pieces = [_split_bf16(_log2_one_minus_beta(zn, mask), axis=1) for zn in zns]
        suffixes = [jnp.dot(p, tri2, preferred_element_type=F32) for p in pieces]
        ws = tuple(_block_weights(sfx, zn, mask) for sfx, zn in zip(suffixes, zns))
        return ws, tuple(sfx[:, 0:1] for sfx in suffixes)

    def accumulate(ws, j, scales):
        for a, s in enumerate(hs):
            acc_ref[:, s] += scales[a] * jnp.dot(ws[a], v_ref[rows_of(j), s],
                                                 preferred_element_type=F32)

    def step(j, carries, mask):
        ws, totals = weights(logits(j), mask)
        accumulate(ws, j, [jnp.exp2(c) for c in carries])
        return tuple(c + tot for c, tot in zip(carries, totals))

    acc_ref[...] = jnp.zeros(acc_ref.shape, F32)
    rows = lax.broadcasted_iota(jnp.int32, (tq, tk), 0)
    cols = lax.broadcasted_iota(jnp.int32, (tq, tk), 1)
    carries = step(qi, (jnp.zeros((tq, 1), F32),) * heads, cols < rows)

    def any_live(carries):
        return jnp.max(functools.reduce(jnp.maximum, carries)) > EXP2_UNDERFLOW

    def body(state):
        t, _, carries = state
        carries = step(qi - 1 - t, carries, None)
        return t + 1, any_live(carries), carries

    lax.while_loop(lambda state: jnp.logical_and(state[0] < qi, state[1]), body,
                   (jnp.int32(0), any_live(carries), carries))
    o_ref[...] = (acc_ref[...] * sg_ref[...].astype(F32)).astype(BF16)


def _attend_prompt(q, kb, vb, sg, tri, *, batch, seq, tq, heads_per_step):
    nq = seq // tq
    assert tq == tri.shape[0], "query block and key block share the diagonal mask"
    cols = heads_per_step * HEAD_DIM
    qblock = pl.BlockSpec((tq, cols), lambda b, g, i: (b * nq + i, g))
    kvblock = pl.BlockSpec((seq, cols), lambda b, g, i: (b, g))
    return pl.pallas_call(
        _attn_prompt_kernel,
        grid=(batch, NUM_GROUPS // heads_per_step, nq),
        in_specs=[qblock, kvblock, kvblock, qblock,
                  pl.BlockSpec(tri.shape, lambda b, g, i: (0, 0))],
        out_specs=qblock,
        out_shape=jax.ShapeDtypeStruct(q.shape, BF16),
        scratch_shapes=[pltpu.VMEM((tq, cols), F32)],
        compiler_params=_compiler_params(("arbitrary", "arbitrary", "arbitrary")),
        name="attn_prompt",
    )(q, kb, vb, sg, tri)


def _attn_sample_kernel(q_ref, kn_ref, vn_ref, kc_hbm, vc_hbm, sg_ref, trit_ref, o_ref,
                        qnt_ref, knear_ref, vnear_ref, kfar_ref, vfar_ref, near_sems, far_sems,
                        *, t):
    step = pl.program_id(0)
    n = q_ref.shape[0] // t
    tk = trit_ref.shape[0]
    block_rows = tk * NUM_GROUPS
    n_cache = kc_hbm.shape[1] // block_rows
    lanes = NUM_GROUPS * t
    pairs = NUM_GROUPS // 2
    pad = HEAD_DIM
    nt_dims = (((1,), (1,)), ((), ()))
    slot = step % 2

    def cache_copies(stream, j, k_dst, v_dst, k_sem, v_sem):
        rows = pl.ds(j * block_rows, block_rows)
        return [pltpu.make_async_copy(kc_hbm.at[stream, rows, :], k_dst, k_sem),
                pltpu.make_async_copy(vc_hbm.at[stream, rows, :], v_dst, v_sem)]

    def near_copies(at_step, s):
        return [c for i in range(n) for c in cache_copies(
            at_step * n + i, n_cache - 1, knear_ref.at[s, i], vnear_ref.at[s, i],
            near_sems.at[0, s, i], near_sems.at[1, s, i])]

    @pl.when(step == 0)
    def _():
        for c in near_copies(0, 0):
            c.start()

    @pl.when(step + 1 < pl.num_programs(0))
    def _():
        for c in near_copies(step + 1, 1 - slot):
            c.start()

    qnt_ref[...] = jnp.zeros(qnt_ref.shape, BF16)
    for i in range(n):
        for h in range(NUM_GROUPS):
            p, half = divmod(h, 2)
            qnt_ref[i, p, h * t:(h + 1) * t, half * HEAD_DIM:(half + 1) * HEAD_DIM] = (
                q_ref[i * t:(i + 1) * t, h * HEAD_DIM:(h + 1) * HEAD_DIM])

    def neg_logits(i, load_k):
        zn = None
        for p in range(pairs):
            lhs = jnp.concatenate([load_k(2 * p), load_k(2 * p + 1)], axis=1).astype(BF16)
            d = lax.dot_general(lhs, qnt_ref[i, p], nt_dims, preferred_element_type=F32)
            zn = d if zn is None else zn + d
        return zn

    def walk(work):
        pieces = [[_split_bf16(_log2_one_minus_beta(zn, m), axis=0) for zn, _, _, m in blocks]
                  for blocks, _, _ in work]
        suffixes = [[jnp.dot(jnp.concatenate([tr, tr], axis=1), p, preferred_element_type=F32)
                     for (_, _, tr, _), p in zip(blocks, ps)]
                    for (blocks, _, _), ps in zip(work, pieces)]
        done = []
        for (blocks, carry, accs), sfxs in zip(work, suffixes):
            for (zn, load_v, _, m), sfx in zip(blocks, sfxs):
                w = jnp.exp2(sfx - zn + carry)
                if m is not None:
                    w = jnp.where(m, w, 0.0)
                w_t = w.T
                accs = [acc + jnp.dot(w_t[h * t:(h + 1) * t, :].astype(BF16),
                                      load_v(h).astype(BF16), preferred_element_type=F32)
                        for h, acc in enumerate(accs)]
                carry = carry + sfx[0:1, :]
            done.append((carry, tuple(accs)))
        return done

    head_rows = lambda ref, base, rows, h: ref[pl.ds(base + h, rows, stride=NUM_GROUPS), :]

    key_idx = lax.broadcasted_iota(jnp.int32, (pad, lanes), 0)
    query_idx = lax.broadcasted_iota(jnp.int32, (pad, lanes), 1) & (t - 1)
    zero_rows = lambda width: jnp.zeros((pad - t, width), F32)
    new_base = lambda i: i * t * NUM_GROUPS
    new_blocks = [
        (jnp.concatenate([neg_logits(i, lambda h, i=i: head_rows(kn_ref, new_base(i), t, h)),
                          zero_rows(lanes)], axis=0),
         lambda h, i=i: jnp.concatenate([head_rows(vn_ref, new_base(i), t, h),
                                         zero_rows(HEAD_DIM)], axis=0),
         trit_ref[0:pad, 0:pad], key_idx < query_idx) for i in range(n)]
    for c in near_copies(step, slot):
        c.wait()
    near_blocks = [
        (neg_logits(i, lambda h, i=i: head_rows(knear_ref.at[slot, i], 0, tk, h)),
         lambda h, i=i: head_rows(vnear_ref.at[slot, i], 0, tk, h), trit_ref[...], None)
        for i in range(n)]
    state = walk([([new_blocks[i], near_blocks[i]], jnp.zeros((1, lanes), F32),
                   [jnp.zeros((t, HEAD_DIM), F32)] * NUM_GROUPS) for i in range(n)])

    def any_live(state):
        return jnp.max(functools.reduce(jnp.maximum, [c for c, _ in state])) > EXP2_UNDERFLOW

    def older(loop_state):
        j, _, state = loop_state
        copies = [c for i in range(n) for c in cache_copies(
            step * n + i, j, kfar_ref.at[i], vfar_ref.at[i], far_sems.at[0, i], far_sems.at[1, i])]
        for c in copies:
            c.start()
        for c in copies:
            c.wait()
        far_blocks = [(neg_logits(i, lambda h, i=i: head_rows(kfar_ref.at[i], 0, tk, h)),
                       lambda h, i=i: head_rows(vfar_ref.at[i], 0, tk, h), trit_ref[...], None)
                      for i in range(n)]
        state = walk([([far_blocks[i]], state[i][0], list(state[i][1])) for i in range(n)])
        return j - 1, any_live(state), tuple(state)

    _, _, state = lax.while_loop(lambda s: jnp.logical_and(s[0] >= 0, s[1]), older,
                                 (jnp.int32(n_cache - 2), any_live(state), tuple(state)))

    for i, (_, accs) in enumerate(state):
        for h, acc in enumerate(accs):
            rows, hs = slice(i * t, (i + 1) * t), slice(h * HEAD_DIM, (h + 1) * HEAD_DIM)
            o_ref[rows, hs] = (acc * sg_ref[rows, hs].astype(F32)).astype(BF16)


def _attend_sample(q, k_new, v_new, cache_k, cache_v, sg, tri_t, *, batch, t):
    width = q.shape[1]
    tk = tri_t.shape[0]
    block_rows = tk * NUM_GROUPS
    assert t & (t - 1) == 0 and NUM_GROUPS * t == tk and cache_k.shape[1] % block_rows == 0
    n = SAMPLE_STREAMS_PER_STEP
    assert batch % n == 0
    rowblock = pl.BlockSpec((n * t, width), lambda b: (b, 0))
    newblock = pl.BlockSpec((n * t * NUM_GROUPS, HEAD_DIM), lambda b: (b, 0))
    in_hbm = pl.BlockSpec(memory_space=pl.ANY)
    key_blocks = lambda *lead: pltpu.VMEM((*lead, n, block_rows, HEAD_DIM), F32)
    return pl.pallas_call(
        functools.partial(_attn_sample_kernel, t=t),
        grid=(batch // n,),
        in_specs=[rowblock, newblock, newblock, in_hbm, in_hbm, rowblock,
                  pl.BlockSpec(tri_t.shape, lambda b: (0, 0))],
        out_specs=rowblock,
        out_shape=jax.ShapeDtypeStruct(q.shape, BF16),
        scratch_shapes=[pltpu.VMEM((n, NUM_GROUPS // 2, NUM_GROUPS * t, 2 * HEAD_DIM), BF16),
                        key_blocks(2), key_blocks(2), key_blocks(), key_blocks(),
                        pltpu.SemaphoreType.DMA((2, 2, n)), pltpu.SemaphoreType.DMA((2, n))],
        compiler_params=_compiler_params(("arbitrary",)),
        name="attn_sample",
    )(q, k_new, v_new, cache_k, cache_v, sg, tri_t)


def _out_kernel(x_ref, msb_ref, mcv_ref, wsb_ref, wcv_ref, g_ref, y_ref):
    y = (x_ref[...]
         + jnp.dot(msb_ref[...], wsb_ref[...], preferred_element_type=F32)
         + jnp.dot(mcv_ref[...], wcv_ref[...], preferred_element_type=F32))
    y_ref[...] = _rmsnorm_rows(y, g_ref[...])


def _merge_out(x2d, mix_sb, mix_cv, w_bf, final_g, *, tm):
    m, d = x2d.shape
    half = mix_sb.shape[1]
    rows = lambda width: pl.BlockSpec((tm, width), lambda i: (i, 0))
    w_rows = lambda r: pl.BlockSpec((half, d), lambda i: (r, 0))
    return pl.pallas_call(
        _out_kernel,
        grid=(m // tm,),
        in_specs=[rows(d), rows(half), rows(half), w_rows(0), w_rows(1),
                  pl.BlockSpec((1, d), lambda i: (0, 0))],
        out_specs=rows(d),
        out_shape=jax.ShapeDtypeStruct((m, d), F32),
        compiler_params=_compiler_params(("arbitrary",)),
        name="merge_out",
    )(x2d, mix_sb, mix_cv, w_bf, w_bf, final_g.reshape(1, d))


def _lower_tri(n):
    j = lax.broadcasted_iota(jnp.int32, (n, n), 0)
    s = lax.broadcasted_iota(jnp.int32, (n, n), 1)
    return (j >= s).astype(BF16)


def kernel(x_prompt, x_sample, cache_k, cache_v, state_conv, norm_g, w_in, conv_w, w_out, final_g):
    depth = w_in.shape[0]
    assert depth == 1, "single-layer step"
    bsz, seq, d = x_prompt.shape
    dbsz, dseq, _ = x_sample.shape
    past = cache_k.shape[2]
    width = NUM_GROUPS * HEAD_DIM

    w_out_bf = w_out[0].astype(BF16)
    tri = _lower_tri(KEY_BLOCK)

    xs = x_sample.reshape(dbsz * dseq, d)
    qs, ks, vs, _, _, sgs, mcvs, tails, w_in_bf = _project(
        xs, norm_g[0], w_in[0], conv_w[0], state_conv[0], seq_rows=dseq,
        tm=min(PROJ_ROWS, xs.shape[0]))

    xp = x_prompt.reshape(bsz * seq, d)
    zeros_left = jnp.zeros((bsz, CONV_W - 1, width), F32)
    qp, kp, vp, kbp, vbp, sgp, mcvp, tailp, _ = _project(
        xp, norm_g[0], w_in_bf, conv_w[0], zeros_left, seq_rows=seq, tm=PROJ_ROWS)
    msbp = _attend_prompt(qp, kbp, vbp, sgp, tri, batch=bsz, seq=seq, tq=KEY_BLOCK,
                          heads_per_step=PROMPT_HEADS_PER_STEP)
    yp = _merge_out(xp, msbp, mcvp, w_out_bf, final_g, tm=OUT_ROWS)

    msbs = _attend_sample(qs, ks, vs,
                          cache_k[0].reshape(dbsz, past * NUM_GROUPS, HEAD_DIM),
                          cache_v[0].reshape(dbsz, past * NUM_GROUPS, HEAD_DIM),
                          sgs, tri.T, batch=dbsz, t=dseq)
    ys = _merge_out(xs, msbs, mcvs, w_out_bf, final_g, tm=OUT_ROWS)

    heads = lambda a, b_, t_: a.reshape(1, b_, t_, NUM_GROUPS, HEAD_DIM)
    return (yp.reshape(bsz, seq, d), ys.reshape(dbsz, dseq, d),
            heads(kp, bsz, seq), heads(vp, bsz, seq), tailp[None],
            heads(ks, dbsz, dseq), heads(vs, dbsz, dseq), tails[None])
```

```python
import functools
import math

import jax
import jax.numpy as jnp
from jax import lax
from jax.experimental import pallas as pl
from jax.experimental.pallas import tpu as pltpu

F32 = jnp.float32
BF16 = jnp.bfloat16

HEAD_DIM = 128
NUM_GROUPS = 8
NUM_SEGMENTS = 8
CONV_W = 3
EPS = 1e-6
LOG2E = math.log2(math.e)
Q_PRESCALE = -(HEAD_DIM ** -0.5) * LOG2E
MXU_DEPTH = 256
KEY_BLOCK = MXU_DEPTH
PROJ_ROWS = 1024
OUT_ROWS = 512
PROMPT_HEADS_PER_STEP = 8
SAMPLE_STREAMS_PER_STEP = 4
EXP2_UNDERFLOW = -160.0

VMEM_LIMIT_BYTES = 56 * 1024 * 1024


def _compiler_params(semantics):
    return pltpu.CompilerParams(dimension_semantics=semantics,
                                vmem_limit_bytes=VMEM_LIMIT_BYTES)


def _rmsnorm_rows(x, g):
    r = lax.rsqrt(jnp.mean(x * x, axis=-1, keepdims=True) + EPS)
    return (x * r) * g


def _silu(x):
    return x * (1.0 / (1.0 + jnp.exp(-x)))


def _proj_kernel(*refs, seq_rows, tiles_per_seq, copy_weights):
    x_ref, g_ref = refs[0:2]
    w_refs = refs[2:2 + NUM_SEGMENTS]
    cw_ref, left_ref = refs[10:12]
    q_ref, k_ref, v_ref, kb_ref, vb_ref, sg_ref, mcv_ref, tail_ref = refs[12:20]
    wcopy_refs = refs[20:20 + NUM_SEGMENTS] if copy_weights else ()
    hn_ref, carry_ref = refs[-2:]
    i = pl.program_id(0)
    h = pl.program_id(1)
    tm = x_ref.shape[0]

    @pl.when(h == 0)
    def _():
        hn_ref[...] = _rmsnorm_rows(x_ref[...], g_ref[...]).astype(BF16)

    @pl.when(jnp.logical_and(i == 0, h == 0))
    def _():
        carry_ref[...] = jnp.zeros(carry_ref.shape, F32)

    wq, wk, wv, wgs, wb, wc, wu, wgc = [r[...].astype(BF16) for r in w_refs]
    for copy_ref, w in zip(wcopy_refs, (wq, wk, wv, wgs, wb, wc, wu, wgc)):
        copy_ref[...] = w

    def project(*ws):
        acc = jnp.dot(hn_ref[...], jnp.concatenate(ws, axis=1), preferred_element_type=F32)
        return [acc[:, s * HEAD_DIM:(s + 1) * HEAD_DIM] for s in range(len(ws))]

    c, u = project(wc, wu)
    b, g_cv = project(wb, wgc)
    q, k = project(wq, wk)
    v, g_sb = project(wv, wgs)

    q_ref[...] = (q * Q_PRESCALE).astype(BF16)
    k_ref[pl.ds(h, tm, stride=NUM_GROUPS), :] = k
    v_ref[pl.ds(h, tm, stride=NUM_GROUPS), :] = v
    kb_ref[...] = k.astype(BF16)
    vb_ref[...] = v.astype(BF16)
    sg_ref[...] = _silu(g_sb).astype(BF16)

    cu = c * u
    if tiles_per_seq > 1:
        left = jnp.where(i % tiles_per_seq == 0, left_ref[...], carry_ref[h][None, 0:2, :])
        carry_ref[h, 0:2, :] = cu[tm - 2:tm, :]
    else:
        left = left_ref[...]
    rows = seq_rows if tiles_per_seq == 1 else tm
    nseq = tm // rows
    expand = lambda a: jnp.broadcast_to(a, (nseq, rows, HEAD_DIM)).reshape(tm, HEAD_DIM)
    l0 = expand(left[:, 0:1, :])
    l1 = expand(left[:, 1:2, :])
    rs = lax.broadcasted_iota(jnp.int32, (tm, HEAD_DIM), 0) & (rows - 1)
    r1 = jnp.where(rs == 0, l1, pltpu.roll(cu, 1, axis=0))
    r2 = jnp.where(rs == 0, l0, jnp.where(rs == 1, l1, pltpu.roll(cu, 2, axis=0)))
    cw = cw_ref[...]
    conv = cw[0:1, :] * r2 + cw[1:2, :] * r1 + cw[2:3, :] * cu
    mcv_ref[...] = (b * conv * _silu(g_cv)).astype(BF16)
    tail_ref[...] = cu.reshape(nseq, rows, HEAD_DIM)[:, rows - 2:rows, :]


def _project(x2d, norm_g, weights, conv_w, left, *, seq_rows, tm):
    m, d = x2d.shape
    width = NUM_GROUPS * HEAD_DIM
    copy_weights = not isinstance(weights, (list, tuple))
    if copy_weights:
        assert m == tm, "bf16 weight copies are written once, by a single row tile"
        w_args = [weights] * NUM_SEGMENTS
        segment = lambda s: pl.BlockSpec((d, HEAD_DIM), lambda i, h: (0, s * NUM_GROUPS + h))
    else:
        w_args = list(weights)
        segment = lambda s: pl.BlockSpec((d, HEAD_DIM), lambda i, h: (0, h))
    once = dict(pipeline_mode=pl.Buffered(1)) if m == tm else {}
    if tm >= seq_rows:
        tiles_per_seq, ns = 1, tm // seq_rows
        left_idx = lambda i, h: (i, 0, h)
    else:
        tiles_per_seq, ns = seq_rows // tm, 1
        left_idx = lambda i, h: (i // tiles_per_seq, 0, h)
    n_tails = (m // tm) * ns
    tile = lambda dt: jax.ShapeDtypeStruct((m, width), dt)
    native = jax.ShapeDtypeStruct((m * NUM_GROUPS, HEAD_DIM), F32)
    col_block = pl.BlockSpec((tm, HEAD_DIM), lambda i, h: (i, h))
    native_block = pl.BlockSpec((tm * NUM_GROUPS, HEAD_DIM), lambda i, h: (i, 0), **once)
    copy_specs, copy_shapes = [], []
    if copy_weights:
        copy_specs = [pl.BlockSpec((d, HEAD_DIM), lambda i, h: (0, h))] * NUM_SEGMENTS
        copy_shapes = [jax.ShapeDtypeStruct((d, width), BF16)] * NUM_SEGMENTS
    kern = functools.partial(_proj_kernel, seq_rows=seq_rows, tiles_per_seq=tiles_per_seq,
                             copy_weights=copy_weights)
    q, k, v, kb, vb, sg, mcv, tails, *w_copies = pl.pallas_call(
        kern,
        grid=(m // tm, NUM_GROUPS),
        in_specs=[
            pl.BlockSpec((tm, d), lambda i, h: (i, 0), **once),
            pl.BlockSpec((1, d), lambda i, h: (0, 0)),
            *[segment(s) for s in range(NUM_SEGMENTS)],
            pl.BlockSpec((CONV_W, HEAD_DIM), lambda i, h: (0, h)),
            pl.BlockSpec((ns, 2, HEAD_DIM), left_idx),
        ],
        out_specs=[col_block, native_block, native_block, col_block, col_block, col_block,
                   col_block, pl.BlockSpec((ns, 2, HEAD_DIM), lambda i, h: (i, 0, h)), *copy_specs],
        out_shape=[tile(BF16), native, native, tile(BF16), tile(BF16), tile(BF16), tile(BF16),
                   jax.ShapeDtypeStruct((n_tails, 2, width), F32), *copy_shapes],
        scratch_shapes=[pltpu.VMEM((tm, d), BF16),
                        pltpu.VMEM((NUM_GROUPS, 8, HEAD_DIM), F32)],
        compiler_params=_compiler_params(("arbitrary", "arbitrary")),
        name="proj",
    )(x2d, norm_g.reshape(1, d), *w_args, conv_w, left)
    tails = tails.reshape(-1, tiles_per_seq, 2, width)[:, tiles_per_seq - 1]
    return q, k, v, kb, vb, sg, mcv, tails, w_copies


def _log2_one_minus_beta(zn, mask):
    softplus2 = jnp.log(1.0 + jnp.exp2(-jnp.abs(zn))) * LOG2E
    log_1m = jnp.minimum(zn, 0.0) - softplus2
    return log_1m if mask is None else jnp.where(mask, log_1m, 0.0)


def _split_bf16(x, axis):
    hi = x.astype(BF16)
    return jnp.concatenate([hi, (x - hi.astype(F32)).astype(BF16)], axis=axis)


def _block_weights(suffix, zn, mask):
    w = jnp.exp2(suffix - zn)
    return (w if mask is None else jnp.where(mask, w, 0.0)).astype(BF16)


def _attn_prompt_kernel(q_ref, k_ref, v_ref, sg_ref, tri_ref, o_ref, acc_ref):
    qi = pl.program_id(2)
    tq = q_ref.shape[0]
    tk = tri_ref.shape[0]
    heads = q_ref.shape[1] // HEAD_DIM
    tri2 = jnp.concatenate([tri_ref[...]] * 2, axis=0)
    hs = [slice(a * HEAD_DIM, (a + 1) * HEAD_DIM) for a in range(heads)]
    nt_dims = (((1,), (1,)), ((), ()))

    def rows_of(j):
        return pl.ds(j * tk if isinstance(j, int) else pl.multiple_of(j * tk, tk), tk)

    def logits(j):
        return tuple(lax.dot_general(q_ref[:, s], k_ref[rows_of(j), s], nt_dims,
                                     preferred_element_type=F32) for s in hs)

    def weights(zns, mask):
        pieces = [_split_bf16(_log2_one_minus_beta(zn, mask), axis=1) for zn in zns]
        suffixes = [jnp.dot(p, tri2, preferred_element_type=F32) for p in pieces]
        ws = tuple(_block_weights(sfx, zn, mask) for sfx, zn in zip(suffixes, zns))
        return ws, tuple(sfx[:, 0:1] for sfx in suffixes)

    def accumulate(ws, j, scales):
        for a, s in enumerate(hs):
            acc_ref[:, s] += scales[a] * jnp.dot(ws[a], v_ref[rows_of(j), s],
                                                 preferred_element_type=F32)

    def step(j, carries, mask):
        ws, totals = weights(logits(j), mask)
        accumulate(ws, j, [jnp.exp2(c) for c in carries])
        return tuple(c + tot for c, tot in zip(carries, totals))

    acc_ref[...] = jnp.zeros(acc_ref.shape, F32)
    rows = lax.broadcasted_iota(jnp.int32, (tq, tk), 0)
    cols = lax.broadcasted_iota(jnp.int32, (tq, tk), 1)
    carries = step(qi, (jnp.zeros((tq, 1), F32),) * heads, cols < rows)

    def any_live(carries):
        return jnp.max(functools.reduce(jnp.maximum, carries)) > EXP2_UNDERFLOW

    def body(state):
        t, _, carries = state
        carries = step(qi - 1 - t, carries, None)
        return t + 1, any_live(carries), carries

    lax.while_loop(lambda state: jnp.logical_and(state[0] < qi, state[1]), body,
                   (jnp.int32(0), any_live(carries), carries))
    o_ref[...] = (acc_ref[...] * sg_ref[...].astype(F32)).astype(BF16)


def _attend_prompt(q, kb, vb, sg, tri, *, batch, seq, tq, heads_per_step):
    nq = seq // tq
    assert tq == tri.shape[0], "query block and key block share the diagonal mask"
    cols = heads_per_step * HEAD_DIM
    qblock = pl.BlockSpec((tq, cols), lambda b, g, i: (b * nq + i, g))
    kvblock = pl.BlockSpec((seq, cols), lambda b, g, i: (b, g))
    return pl.pallas_call(
        _attn_prompt_kernel,
        grid=(batch, NUM_GROUPS // heads_per_step, nq),
        in_specs=[qblock, kvblock, kvblock, qblock,
                  pl.BlockSpec(tri.shape, lambda b, g, i: (0, 0))],
        out_specs=qblock,
        out_shape=jax.ShapeDtypeStruct(q.shape, BF16),
        scratch_shapes=[pltpu.VMEM((tq, cols), F32)],
        compiler_params=_compiler_params(("arbitrary", "arbitrary", "arbitrary")),
        name="attn_prompt",
    )(q, kb, vb, sg, tri)


def _attn_sample_kernel(q_ref, kn_ref, vn_ref, kc_hbm, vc_hbm, sg_ref, trit_ref, o_ref,
                        qnt_ref, knear_ref, vnear_ref, kfar_ref, vfar_ref, near_sems, far_sems,
                        *, t):
    step = pl.program_id(0)
    n = q_ref.shape[0] // t
    tk = trit_ref.shape[0]
    block_rows = tk * NUM_GROUPS
    n_cache = kc_hbm.shape[1] // block_rows
    lanes = NUM_GROUPS * t
    pairs = NUM_GROUPS // 2
    pad = HEAD_DIM
    nt_dims = (((1,), (1,)), ((), ()))
    slot = step % 2

    def cache_copies(stream, j, k_dst, v_dst, k_sem, v_sem):
        rows = pl.ds(j * block_rows, block_rows)
        return [pltpu.make_async_copy(kc_hbm.at[stream, rows, :], k_dst, k_sem),
                pltpu.make_async_copy(vc_hbm.at[stream, rows, :], v_dst, v_sem)]

    def near_copies(at_step, s):
        return [c for i in range(n) for c in cache_copies(
            at_step * n + i, n_cache - 1, knear_ref.at[s, i], vnear_ref.at[s, i],
            near_sems.at[0, s, i], near_sems.at[1, s, i])]

    @pl.when(step == 0)
    def _():
        for c in near_copies(0, 0):
            c.start()

    @pl.when(step + 1 < pl.num_programs(0))
    def _():
        for c in near_copies(step + 1, 1 - slot):
            c.start()

    qnt_ref[...] = jnp.zeros(qnt_ref.shape, BF16)
    for i in range(n):
        for h in range(NUM_GROUPS):
            p, half = divmod(h, 2)
            qnt_ref[i, p, h * t:(h + 1) * t, half * HEAD_DIM:(half + 1) * HEAD_DIM] = (
                q_ref[i * t:(i + 1) * t, h * HEAD_DIM:(h + 1) * HEAD_DIM])

    def neg_logits(i, load_k):
        zn = None
        for p in range(pairs):
            lhs = jnp.concatenate([load_k(2 * p), load_k(2 * p + 1)], axis=1).astype(BF16)
            d = lax.dot_general(lhs, qnt_ref[i, p], nt_dims, preferred_element_type=F32)
            zn = d if zn is None else zn + d
        return zn

    def walk(work):
        pieces = [[_split_bf16(_log2_one_minus_beta(zn, m), axis=0) for zn, _, _, m in blocks]
                  for blocks, _, _ in work]
        suffixes = [[jnp.dot(jnp.concatenate([tr, tr], axis=1), p, preferred_element_type=F32)
                     for (_, _, tr, _), p in zip(blocks, ps)]
                    for (blocks, _, _), ps in zip(work, pieces)]
        done = []
        for (blocks, carry, accs), sfxs in zip(work, suffixes):
            for (zn, load_v, _, m), sfx in zip(blocks, sfxs):
                w = jnp.exp2(sfx - zn + carry)
                if m is not None:
                    w = jnp.where(m, w, 0.0)
                w_t = w.T
                accs = [acc + jnp.dot(w_t[h * t:(h + 1) * t, :].astype(BF16),
                                      load_v(h).astype(BF16), preferred_element_type=F32)
                        for h, acc in enumerate(accs)]
                carry = carry + sfx[0:1, :]
            done.append((carry, tuple(accs)))
        return done

    head_rows = lambda ref, base, rows, h: ref[pl.ds(base + h, rows, stride=NUM_GROUPS), :]

    key_idx = lax.broadcasted_iota(jnp.int32, (pad, lanes), 0)
    query_idx = lax.broadcasted_iota(jnp.int32, (pad, lanes), 1) & (t - 1)
    zero_rows = lambda width: jnp.zeros((pad - t, width), F32)
    new_base = lambda i: i * t * NUM_GROUPS
    new_blocks = [
        (jnp.concatenate([neg_logits(i, lambda h, i=i: head_rows(kn_ref, new_base(i), t, h)),
                          zero_rows(lanes)], axis=0),
         lambda h, i=i: jnp.concatenate([head_rows(vn_ref, new_base(i), t, h),
                                         zero_rows(HEAD_DIM)], axis=0),
         trit_ref[0:pad, 0:pad], key_idx < query_idx) for i in range(n)]
    for c in near_copies(step, slot):
        c.wait()
    near_blocks = [
        (neg_logits(i, lambda h, i=i: head_rows(knear_ref.at[slot, i], 0, tk, h)),
         lambda h, i=i: head_rows(vnear_ref.at[slot, i], 0, tk, h), trit_ref[...], None)
        for i in range(n)]
    state = walk([([new_blocks[i], near_blocks[i]], jnp.zeros((1, lanes), F32),
                   [jnp.zeros((t, HEAD_DIM), F32)] * NUM_GROUPS) for i in range(n)])

    def any_live(state):
        return jnp.max(functools.reduce(jnp.maximum, [c for c, _ in state])) > EXP2_UNDERFLOW

    def older(loop_state):
        j, _, state = loop_state
        copies = [c for i in range(n) for c in cache_copies(
            step * n + i, j, kfar_ref.at[i], vfar_ref.at[i], far_sems.at[0, i], far_sems.at[1, i])]
        for c in copies:
            c.start()
        for c in copies:
            c.wait()
        far_blocks = [(neg_logits(i, lambda h, i=i: head_rows(kfar_ref.at[i], 0, tk, h)),
                       lambda h, i=i: head_rows(vfar_ref.at[i], 0, tk, h), trit_ref[...], None)
                      for i in range(n)]
        state = walk([([far_blocks[i]], state[i][0], list(state[i][1])) for i in range(n)])
        return j - 1, any_live(state), tuple(state)

    _, _, state = lax.while_loop(lambda s: jnp.logical_and(s[0] >= 0, s[1]), older,
                                 (jnp.int32(n_cache - 2), any_live(state), tuple(state)))

    for i, (_, accs) in enumerate(state):
        for h, acc in enumerate(accs):
            rows, hs = slice(i * t, (i + 1) * t), slice(h * HEAD_DIM, (h + 1) * HEAD_DIM)
            o_ref[rows, hs] = (acc * sg_ref[rows, hs].astype(F32)).astype(BF16)


def _attend_sample(q, k_new, v_new, cache_k, cache_v, sg, tri_t, *, batch, t):
    width = q.shape[1]
    tk = tri_t.shape[0]
    block_rows = tk * NUM_GROUPS
    assert t & (t - 1) == 0 and NUM_GROUPS * t == tk and cache_k.shape[1] % block_rows == 0
    n = SAMPLE_STREAMS_PER_STEP
    assert batch % n == 0
    rowblock = pl.BlockSpec((n * t, width), lambda b: (b, 0))
    newblock = pl.BlockSpec((n * t * NUM_GROUPS, HEAD_DIM), lambda b: (b, 0))
    in_hbm = pl.BlockSpec(memory_space=pl.ANY)
    key_blocks = lambda *lead: pltpu.VMEM((*lead, n, block_rows, HEAD_DIM), F32)
    return pl.pallas_call(
        functools.partial(_attn_sample_kernel, t=t),
        grid=(batch // n,),
        in_specs=[rowblock, newblock, newblock, in_hbm, in_hbm, rowblock,
                  pl.BlockSpec(tri_t.shape, lambda b: (0, 0))],
        out_specs=rowblock,
        out_shape=jax.ShapeDtypeStruct(q.shape, BF16),
        scratch_shapes=[pltpu.VMEM((n, NUM_GROUPS // 2, NUM_GROUPS * t, 2 * HEAD_DIM), BF16),
                        key_blocks(2), key_blocks(2), key_blocks(), key_blocks(),
                        pltpu.SemaphoreType.DMA((2, 2, n)), pltpu.SemaphoreType.DMA((2, n))],
        compiler_params=_compiler_params(("arbitrary",)),
        name="attn_sample",
    )(q, k_new, v_new, cache_k, cache_v, sg, tri_t)


def _out_kernel(x_ref, msb_ref, mcv_ref, wsb_ref, wcv_ref, g_ref, y_ref, *wcopy_refs):
    w_sb, w_cv = wsb_ref[...].astype(BF16), wcv_ref[...].astype(BF16)
    for copy_ref, w in zip(wcopy_refs, (w_sb, w_cv)):
        copy_ref[...] = w
    y = (x_ref[...]
         + jnp.dot(msb_ref[...], w_sb, preferred_element_type=F32)
         + jnp.dot(mcv_ref[...], w_cv, preferred_element_type=F32))
    y_ref[...] = _rmsnorm_rows(y, g_ref[...])


def _merge_out(x2d, mix_sb, mix_cv, weights, final_g, *, tm):
    m, d = x2d.shape
    half = mix_sb.shape[1]
    copy_weights = not isinstance(weights, (list, tuple))
    rows = lambda width: pl.BlockSpec((tm, width), lambda i: (i, 0))
    if copy_weights:
        w_args = [weights, weights]
        w_specs = [pl.BlockSpec((half, d), lambda i, r=r: (r, 0), pipeline_mode=pl.Buffered(1))
                   for r in range(2)]
        copy_specs = [pl.BlockSpec((half, d), lambda i: (0, 0), pipeline_mode=pl.Buffered(1))] * 2
        copy_shapes = [jax.ShapeDtypeStruct((half, d), BF16)] * 2
    else:
        w_args = list(weights)
        w_specs = [pl.BlockSpec((half, d), lambda i: (0, 0))] * 2
        copy_specs, copy_shapes = [], []
    y, *w_copies = pl.pallas_call(
        _out_kernel,
        grid=(m // tm,),
        in_specs=[rows(d), rows(half), rows(half), *w_specs, pl.BlockSpec((1, d), lambda i: (0, 0))],
        out_specs=[rows(d), *copy_specs],
        out_shape=[jax.ShapeDtypeStruct((m, d), F32), *copy_shapes],
        compiler_params=_compiler_params(("arbitrary",)),
        name="merge_out",
    )(x2d, mix_sb, mix_cv, *w_args, final_g.reshape(1, d))
    return y, w_copies


def _lower_tri(n):
    j = lax.broadcasted_iota(jnp.int32, (n, n), 0)
    s = lax.broadcasted_iota(jnp.int32, (n, n), 1)
    return (j >= s).astype(BF16)


def kernel(x_prompt, x_sample, cache_k, cache_v, state_conv, norm_g, w_in, conv_w, w_out, final_g):
    depth = w_in.shape[0]
    assert depth == 1, "single-layer step"
    bsz, seq, d = x_prompt.shape
    dbsz, dseq, _ = x_sample.shape
    past = cache_k.shape[2]
    width = NUM_GROUPS * HEAD_DIM

    tri = _lower_tri(KEY_BLOCK)

    xs = x_sample.reshape(dbsz * dseq, d)
    qs, ks, vs, _, _, sgs, mcvs, tails, w_in_bf = _project(
        xs, norm_g[0], w_in[0], conv_w[0], state_conv[0], seq_rows=dseq,
        tm=min(PROJ_ROWS, xs.shape[0]))
    msbs = _attend_sample(qs, ks, vs,
                          cache_k[0].reshape(dbsz, past * NUM_GROUPS, HEAD_DIM),
                          cache_v[0].reshape(dbsz, past * NUM_GROUPS, HEAD_DIM),
                          sgs, tri.T, batch=dbsz, t=dseq)
    ys, w_out_bf = _merge_out(xs, msbs, mcvs, w_out[0], final_g, tm=OUT_ROWS)

    xp = x_prompt.reshape(bsz * seq, d)
    zeros_left = jnp.zeros((bsz, CONV_W - 1, width), F32)
    qp, kp, vp, kbp, vbp, sgp, mcvp, tailp, _ = _project(
        xp, norm_g[0], w_in_bf, conv_w[0], zeros_left, seq_rows=seq, tm=PROJ_ROWS)
    msbp = _attend_prompt(qp, kbp, vbp, sgp, tri, batch=bsz, seq=seq, tq=KEY_BLOCK,
                          heads_per_step=PROMPT_HEADS_PER_STEP)
    yp, _ = _merge_out(xp, msbp, mcvp, w_out_bf, final_g, tm=OUT_ROWS)

    heads = lambda a, b_, t_: a.reshape(1, b_, t_, NUM_GROUPS, HEAD_DIM)
    return (yp.reshape(bsz, seq, d), ys.reshape(dbsz, dseq, d),
            heads(kp, bsz, seq), heads(vp, bsz, seq), tailp[None],
            heads(ks, dbsz, dseq), heads(vs, dbsz, dseq), tails[None])
```

```python
import functools
import math

import jax
import jax.numpy as jnp
from jax import lax
from jax.experimental import pallas as pl
from jax.experimental.pallas import tpu as pltpu

F32 = jnp.float32
BF16 = jnp.bfloat16

HEAD_DIM = 128
NUM_GROUPS = 8
NUM_SEGMENTS = 8
CONV_W = 3
CONV_CTX = CONV_W - 1
assert CONV_W == 3, "the conv epilogue writes its three taps out explicitly"
SUBLANES = 8
EPS = 1e-6
LOG2E = math.log2(math.e)
Q_PRESCALE = -(HEAD_DIM ** -0.5) * LOG2E
MXU_DEPTH = 256
KEY_BLOCK = MXU_DEPTH
PROJ_ROWS = 1024
OUT_ROWS = 512
PROMPT_HEADS_PER_STEP = 8
SAMPLE_STREAMS_PER_STEP = 4
EXP2_UNDERFLOW = -1e30

VMEM_LIMIT_BYTES = 56 * 1024 * 1024


def _compiler_params(semantics):
    return pltpu.CompilerParams(dimension_semantics=semantics,
                                vmem_limit_bytes=VMEM_LIMIT_BYTES)


def _rmsnorm_rows(x, g):
    r = lax.rsqrt(jnp.mean(x * x, axis=-1, keepdims=True) + EPS)
    return (x * r) * g


def _silu(x):
    return x * (1.0 / (1.0 + jnp.exp(-x)))


def _proj_kernel(*refs, seq_rows, tiles_per_seq, copy_weights):
    refs = list(refs)
    take = lambda count: [refs.pop(0) for _ in range(count)]
    x_ref, g_ref = take(2)
    w_refs = take(NUM_SEGMENTS)
    cw_ref, left_ref = take(2)
    q_ref, k_ref, v_ref, kb_ref, vb_ref, sg_ref, mcv_ref, tail_ref = take(8)
    wcopy_refs = take(NUM_SEGMENTS) if copy_weights else ()
    hn_ref, carry_ref = take(2)
    i = pl.program_id(0)
    h = pl.program_id(1)
    tm = x_ref.shape[0]

    @pl.when(h == 0)
    def _():
        hn_ref[...] = _rmsnorm_rows(x_ref[...], g_ref[...]).astype(BF16)

    @pl.when(jnp.logical_and(i == 0, h == 0))
    def _():
        carry_ref[...] = jnp.zeros(carry_ref.shape, F32)

    wq, wk, wv, wgs, wb, wc, wu, wgc = [r[...].astype(BF16) for r in w_refs]
    for copy_ref, w in zip(wcopy_refs, (wq, wk, wv, wgs, wb, wc, wu, wgc)):
        copy_ref[...] = w

    def project(*ws):
        acc = jnp.dot(hn_ref[...], jnp.concatenate(ws, axis=1), preferred_element_type=F32)
        return [acc[:, s * HEAD_DIM:(s + 1) * HEAD_DIM] for s in range(len(ws))]

    c, u = project(wc, wu)
    b, g_cv = project(wb, wgc)
    k, v = project(wk, wv)
    q, g_sb = project(wq, wgs)

    q_ref[...] = (q * Q_PRESCALE).astype(BF16)
    k_ref[pl.ds(h, tm, stride=NUM_GROUPS), :] = k
    v_ref[pl.ds(h, tm, stride=NUM_GROUPS), :] = v
    kb_ref[...] = k.astype(BF16)
    vb_ref[...] = v.astype(BF16)
    sg_ref[...] = _silu(g_sb).astype(BF16)

    cu = c * u
    if tiles_per_seq > 1:
        left = jnp.where(i % tiles_per_seq == 0, left_ref[...], carry_ref[h][None, 0:CONV_CTX, :])
        carry_ref[h, 0:CONV_CTX, :] = cu[tm - CONV_CTX:tm, :]
    else:
        left = left_ref[...]
    rows = seq_rows if tiles_per_seq == 1 else tm
    nseq = tm // rows
    expand = lambda a: jnp.broadcast_to(a, (nseq, rows, HEAD_DIM)).reshape(tm, HEAD_DIM)
    l0 = expand(left[:, 0:1, :])
    l1 = expand(left[:, 1:2, :])
    rs = lax.broadcasted_iota(jnp.int32, (tm, HEAD_DIM), 0) & (rows - 1)
    r1 = jnp.where(rs == 0, l1, pltpu.roll(cu, 1, axis=0))
    r2 = jnp.where(rs == 0, l0, jnp.where(rs == 1, l1, pltpu.roll(cu, 2, axis=0)))
    cw = cw_ref[...]
    conv = cw[0:1, :] * r2 + cw[1:2, :] * r1 + cw[2:3, :] * cu
    mcv_ref[...] = (b * conv * _silu(g_cv)).astype(BF16)
    tail_ref[...] = cu.reshape(nseq, rows, HEAD_DIM)[:, rows - CONV_CTX:rows, :]


def _project(x2d, norm_g, weights, conv_w, left, *, seq_rows, tm):
    m, d = x2d.shape
    width = NUM_GROUPS * HEAD_DIM
    copy_weights = not isinstance(weights, (list, tuple))
    if copy_weights:
        assert m == tm, "bf16 weight copies are written once, by a single row tile"
        w_args = [weights] * NUM_SEGMENTS
        segment = lambda s: pl.BlockSpec((d, HEAD_DIM), lambda i, h: (0, s * NUM_GROUPS + h))
    else:
        w_args = list(weights)
        segment = lambda s: pl.BlockSpec((d, HEAD_DIM), lambda i, h: (0, h))
    once = dict(pipeline_mode=pl.Buffered(1)) if m == tm else {}
    if tm >= seq_rows:
        tiles_per_seq, ns = 1, tm // seq_rows
        left_idx = lambda i, h: (i, 0, h)
    else:
        tiles_per_seq, ns = seq_rows // tm, 1
        left_idx = lambda i, h: (i // tiles_per_seq, 0, h)
    n_tails = (m // tm) * ns
    tile = lambda dt: jax.ShapeDtypeStruct((m, width), dt)
    native = jax.ShapeDtypeStruct((m * NUM_GROUPS, HEAD_DIM), F32)
    col_block = pl.BlockSpec((tm, HEAD_DIM), lambda i, h: (i, h))
    native_block = pl.BlockSpec((tm * NUM_GROUPS, HEAD_DIM), lambda i, h: (i, 0), **once)
    copy_specs, copy_shapes = [], []
    if copy_weights:
        copy_specs = [pl.BlockSpec((d, HEAD_DIM), lambda i, h: (0, h))] * NUM_SEGMENTS
        copy_shapes = [jax.ShapeDtypeStruct((d, width), BF16)] * NUM_SEGMENTS
    kern = functools.partial(_proj_kernel, seq_rows=seq_rows, tiles_per_seq=tiles_per_seq,
                             copy_weights=copy_weights)
    q, k, v, kb, vb, sg, mcv, tails, *w_copies = pl.pallas_call(
        kern,
        grid=(m // tm, NUM_GROUPS),
        in_specs=[
            pl.BlockSpec((tm, d), lambda i, h: (i, 0), **once),
            pl.BlockSpec((1, d), lambda i, h: (0, 0)),
            *[segment(s) for s in range(NUM_SEGMENTS)],
            pl.BlockSpec((CONV_W, HEAD_DIM), lambda i, h: (0, h)),
            pl.BlockSpec((ns, CONV_CTX, HEAD_DIM), left_idx),
        ],
        out_specs=[col_block, native_block, native_block, col_block, col_block, col_block,
                   col_block, pl.BlockSpec((ns, CONV_CTX, HEAD_DIM), lambda i, h: (i, 0, h)),
                   *copy_specs],
        out_shape=[tile(BF16), native, native, tile(BF16), tile(BF16), tile(BF16), tile(BF16),
                   jax.ShapeDtypeStruct((n_tails, CONV_CTX, width), F32), *copy_shapes],
        scratch_shapes=[pltpu.VMEM((tm, d), BF16),
                        pltpu.VMEM((NUM_GROUPS, SUBLANES, HEAD_DIM), F32)],
        compiler_params=_compiler_params(("arbitrary", "arbitrary")),
        name="proj",
    )(x2d, norm_g.reshape(1, d), *w_args, conv_w, left)
    tails = tails.reshape(-1, tiles_per_seq, CONV_CTX, width)[:, tiles_per_seq - 1]
    return q, k, v, kb, vb, sg, mcv, tails, w_copies


def _log2_one_minus_beta(zn, mask):
    softplus2 = jnp.log(1.0 + jnp.exp2(-jnp.abs(zn))) * LOG2E
    log_1m = jnp.minimum(zn, 0.0) - softplus2
    return log_1m if mask is None else jnp.where(mask, log_1m, 0.0)


def _split_bf16(x, axis):
    hi = x.astype(BF16)
    return jnp.concatenate([hi, (x - hi.astype(F32)).astype(BF16)], axis=axis)


def _block_weights(suffix, zn, mask):
    w = jnp.exp2(suffix - zn)
    return (w if mask is None else jnp.where(mask, w, 0.0)).astype(BF16)


def _attn_prompt_kernel(q_ref, k_ref, v_ref, sg_ref, tri_ref, o_ref, acc_ref):
    qi = pl.program_id(2)
    tq = q_ref.shape[0]
    tk = tri_ref.shape[0]
    heads = q_ref.shape[1] // HEAD_DIM
    tri2 = jnp.concatenate([tri_ref[...]] * 2, axis=0)
    hs = [slice(a * HEAD_DIM, (a + 1) * HEAD_DIM) for a in range(heads)]
    nt_dims = (((1,), (1,)), ((), ()))

    def rows_of(j):
        return pl.ds(j * tk if isinstance(j, int) else pl.multiple_of(j * tk, tk), tk)

    def logits(j):
        return tuple(lax.dot_general(q_ref[:, s], k_ref[rows_of(j), s], nt_dims,
                                     preferred_element_type=F32) for s in hs)

    def weights(zns, mask):
        pieces = [_split_bf16(_log2_one_minus_beta(zn, mask), axis=1) for zn in zns]
        suffixes = [jnp.dot(p, tri2, preferred_element_type=F32) for p in pieces]
        ws = tuple(_block_weights(sfx, zn, mask) for sfx, zn in zip(suffixes, zns))
        return ws, tuple(sfx[:, 0:1] for sfx in suffixes)

    def accumulate(ws, j, scales):
        for a, s in enumerate(hs):
            acc_ref[:, s] += scales[a] * jnp.dot(ws[a], v_ref[rows_of(j), s],
                                                 preferred_element_type=F32)

    def step(j, carries, mask):
        ws, totals = weights(logits(j), mask)
        accumulate(ws, j, [jnp.exp2(c) for c in carries])
        return tuple(c + tot for c, tot in zip(carries, totals))

    acc_ref[...] = jnp.zeros(acc_ref.shape, F32)
    rows = lax.broadcasted_iota(jnp.int32, (tq, tk), 0)
    cols = lax.broadcasted_iota(jnp.int32, (tq, tk), 1)
    carries = step(qi, (jnp.zeros((tq, 1), F32),) * heads, cols < rows)

    def any_live(carries):
        return jnp.max(functools.reduce(jnp.maximum, carries)) > EXP2_UNDERFLOW

    def body(state):
        t, _, carries = state
        carries = step(qi - 1 - t, carries, None)
        return t + 1, any_live(carries), carries

    lax.while_loop(lambda state: jnp.logical_and(state[0] < qi, state[1]), body,
                   (jnp.int32(0), any_live(carries), carries))
    o_ref[...] = (acc_ref[...] * sg_ref[...].astype(F32)).astype(BF16)


def _attend_prompt(q, kb, vb, sg, tri, *, batch, seq, tq, heads_per_step):
    nq = seq // tq
    assert tq == tri.shape[0], "query block and key block share the diagonal mask"
    cols = heads_per_step * HEAD_DIM
    qblock = pl.BlockSpec((tq, cols), lambda b, g, i: (b * nq + i, g))
    kvblock = pl.BlockSpec((seq, cols), lambda b, g, i: (b, g))
    return pl.pallas_call(
        _attn_prompt_kernel,
        grid=(batch, NUM_GROUPS // heads_per_step, nq),
        in_specs=[qblock, kvblock, kvblock, qblock,
                  pl.BlockSpec(tri.shape, lambda b, g, i: (0, 0))],
        out_specs=qblock,
        out_shape=jax.ShapeDtypeStruct(q.shape, BF16),
        scratch_shapes=[pltpu.VMEM((tq, cols), F32)],
        compiler_params=_compiler_params(("arbitrary", "arbitrary", "arbitrary")),
        name="attn_prompt",
    )(q, kb, vb, sg, tri)


def _attn_sample_kernel(q_ref, kn_ref, vn_ref, kc_hbm, vc_hbm, sg_ref, trit_ref, o_ref,
                        qnt_ref, knear_ref, vnear_ref, kfar_ref, vfar_ref, near_sems, far_sems,
                        *, t):
    step = pl.program_id(0)
    n = q_ref.shape[0] // t
    tk = trit_ref.shape[0]
    block_rows = tk * NUM_GROUPS
    n_cache = kc_hbm.shape[1] // block_rows
    lanes = NUM_GROUPS * t
    pairs = NUM_GROUPS // 2
    pad = HEAD_DIM
    nt_dims = (((1,), (1,)), ((), ()))
    slot = step % 2

    def cache_copies(stream, j, k_dst, v_dst, k_sem, v_sem):
        rows = pl.ds(j * block_rows, block_rows)
        return [pltpu.make_async_copy(kc_hbm.at[stream, rows, :], k_dst, k_sem),
                pltpu.make_async_copy(vc_hbm.at[stream, rows, :], v_dst, v_sem)]

    def near_copies(at_step, s):
        return [c for i in range(n) for c in cache_copies(
            at_step * n + i, n_cache - 1, knear_ref.at[s, i], vnear_ref.at[s, i],
            near_sems.at[0, s, i], near_sems.at[1, s, i])]

    @pl.when(step == 0)
    def _():
        for c in near_copies(0, 0):
            c.start()

    @pl.when(step + 1 < pl.num_programs(0))
    def _():
        for c in near_copies(step + 1, 1 - slot):
            c.start()

    qnt_ref[...] = jnp.zeros(qnt_ref.shape, BF16)
    for i in range(n):
        for h in range(NUM_GROUPS):
            p, half = divmod(h, 2)
            qnt_ref[i, p, h * t:(h + 1) * t, half * HEAD_DIM:(half + 1) * HEAD_DIM] = (
                q_ref[i * t:(i + 1) * t, h * HEAD_DIM:(h + 1) * HEAD_DIM])

    def neg_logits(i, load_k):
        zn = None
        for p in range(pairs):
            lhs = jnp.concatenate([load_k(2 * p), load_k(2 * p + 1)], axis=1).astype(BF16)
            d = lax.dot_general(lhs, qnt_ref[i, p], nt_dims, preferred_element_type=F32)
            zn = d if zn is None else zn + d
        return zn

    def walk(work):
        pieces = [[_split_bf16(_log2_one_minus_beta(zn, m), axis=0) for zn, _, _, m in blocks]
                  for blocks, _, _ in work]
        suffixes = [[jnp.dot(jnp.concatenate([tr, tr], axis=1), p, preferred_element_type=F32)
                     for (_, _, tr, _), p in zip(blocks, ps)]
                    for (blocks, _, _), ps in zip(work, pieces)]
        done = []
        for (blocks, carry, accs), sfxs in zip(work, suffixes):
            for (zn, load_v, _, m), sfx in zip(blocks, sfxs):
                w = jnp.exp2(sfx - zn + carry)
                if m is not None:
                    w = jnp.where(m, w, 0.0)
                w_t = w.T
                accs = [acc + jnp.dot(w_t[h * t:(h + 1) * t, :].astype(BF16),
                                      load_v(h).astype(BF16), preferred_element_type=F32)
                        for h, acc in enumerate(accs)]
                carry = carry + sfx[0:1, :]
            done.append((carry, tuple(accs)))
        return done

    head_rows = lambda ref, base, rows, h: ref[pl.ds(base + h, rows, stride=NUM_GROUPS), :]

    key_idx = lax.broadcasted_iota(jnp.int32, (pad, lanes), 0)
    query_idx = lax.broadcasted_iota(jnp.int32, (pad, lanes), 1) & (t - 1)
    zero_rows = lambda width: jnp.zeros((pad - t, width), F32)
    new_base = lambda i: i * t * NUM_GROUPS
    new_blocks = [
        (jnp.concatenate([neg_logits(i, lambda h, i=i: head_rows(kn_ref, new_base(i), t, h)),
                          zero_rows(lanes)], axis=0),
         lambda h, i=i: jnp.concatenate([head_rows(vn_ref, new_base(i), t, h),
                                         zero_rows(HEAD_DIM)], axis=0),
         trit_ref[0:pad, 0:pad], key_idx < query_idx) for i in range(n)]
    for c in near_copies(step, slot):
        c.wait()
    near_blocks = [
        (neg_logits(i, lambda h, i=i: head_rows(knear_ref.at[slot, i], 0, tk, h)),
         lambda h, i=i: head_rows(vnear_ref.at[slot, i], 0, tk, h), trit_ref[...], None)
        for i in range(n)]
    state = walk([([new_blocks[i], near_blocks[i]], jnp.zeros((1, lanes), F32),
                   [jnp.zeros((t, HEAD_DIM), F32)] * NUM_GROUPS) for i in range(n)])

    def any_live(state):
        return jnp.max(functools.reduce(jnp.maximum, [c for c, _ in state])) > EXP2_UNDERFLOW

    first_far = n_cache - 2
    far_slot = lambda j: (first_far - j) % 2

    def far_copies(j):
        s = far_slot(j)
        return [c for i in range(n) for c in cache_copies(
            step * n + i, j, kfar_ref.at[s, i], vfar_ref.at[s, i],
            far_sems.at[0, s, i], far_sems.at[1, s, i])]

    def start_if(cond, j):
        @pl.when(cond)
        def _():
            for c in far_copies(j):
                c.start()

    def wait_if(cond, j):
        @pl.when(cond)
        def _():
            for c in far_copies(j):
                c.wait()

    def older(loop_state):
        j, _, state = loop_state
        s = far_slot(j)
        for c in far_copies(j):
            c.wait()
        start_if(j >= 1, j - 1)
        far_blocks = [(neg_logits(i, lambda h, i=i: head_rows(kfar_ref.at[s, i], 0, tk, h)),
                       lambda h, i=i: head_rows(vfar_ref.at[s, i], 0, tk, h), trit_ref[...], None)
                      for i in range(n)]
        state = walk([([far_blocks[i]], state[i][0], list(state[i][1])) for i in range(n)])
        return j - 1, any_live(state), tuple(state)

    live = any_live(state)
    if first_far >= 0:
        start_if(live, first_far)
    j_end, _, state = lax.while_loop(lambda s: jnp.logical_and(s[0] >= 0, s[1]), older,
                                     (jnp.int32(first_far), live, tuple(state)))
    if first_far >= 0:
        wait_if(jnp.logical_and(live, j_end >= 0), j_end)

    for i, (_, accs) in enumerate(state):
        for h, acc in enumerate(accs):
            rows, hs = slice(i * t, (i + 1) * t), slice(h * HEAD_DIM, (h + 1) * HEAD_DIM)
            o_ref[rows, hs] = (acc * sg_ref[rows, hs].astype(F32)).astype(BF16)


def _attend_sample(q, k_new, v_new, cache_k, cache_v, sg, tri_t, *, batch, t):
    width = q.shape[1]
    tk = tri_t.shape[0]
    block_rows = tk * NUM_GROUPS
    assert t & (t - 1) == 0 and NUM_GROUPS * t == tk and cache_k.shape[1] % block_rows == 0
    n = SAMPLE_STREAMS_PER_STEP
    assert batch % n == 0
    rowblock = pl.BlockSpec((n * t, width), lambda b: (b, 0))
    newblock = pl.BlockSpec((n * t * NUM_GROUPS, HEAD_DIM), lambda b: (b, 0))
    in_hbm = pl.BlockSpec(memory_space=pl.ANY)
    key_blocks = lambda *lead: pltpu.VMEM((*lead, n, block_rows, HEAD_DIM), F32)
    return pl.pallas_call(
        functools.partial(_attn_sample_kernel, t=t),
        grid=(batch // n,),
        in_specs=[rowblock, newblock, newblock, in_hbm, in_hbm, rowblock,
                  pl.BlockSpec(tri_t.shape, lambda b: (0, 0))],
        out_specs=rowblock,
        out_shape=jax.ShapeDtypeStruct(q.shape, BF16),
        scratch_shapes=[pltpu.VMEM((n, NUM_GROUPS // 2, NUM_GROUPS * t, 2 * HEAD_DIM), BF16),
                        key_blocks(2), key_blocks(2), key_blocks(2), key_blocks(2),
                        pltpu.SemaphoreType.DMA((2, 2, n)), pltpu.SemaphoreType.DMA((2, 2, n))],
        compiler_params=_compiler_params(("arbitrary",)),
        name="attn_sample",
    )(q, k_new, v_new, cache_k, cache_v, sg, tri_t)


def _out_kernel(x_ref, msb_ref, mcv_ref, wsb_ref, wcv_ref, g_ref, y_ref, *wcopy_refs):
    w_sb, w_cv = wsb_ref[...].astype(BF16), wcv_ref[...].astype(BF16)
    for copy_ref, w in zip(wcopy_refs, (w_sb, w_cv)):
        copy_ref[...] = w
    y = (x_ref[...]
         + jnp.dot(msb_ref[...], w_sb, preferred_element_type=F32)
         + jnp.dot(mcv_ref[...], w_cv, preferred_element_type=F32))
    y_ref[...] = _rmsnorm_rows(y, g_ref[...])


def _merge_out(x2d, mix_sb, mix_cv, weights, final_g, *, tm):
    m, d = x2d.shape
    half = mix_sb.shape[1]
    copy_weights = not isinstance(weights, (list, tuple))
    rows = lambda width: pl.BlockSpec((tm, width), lambda i: (i, 0))
    if copy_weights:
        w_args = [weights, weights]
        w_specs = [pl.BlockSpec((half, d), lambda i, r=r: (r, 0), pipeline_mode=pl.Buffered(1))
                   for r in range(2)]
        copy_specs = [pl.BlockSpec((half, d), lambda i: (0, 0), pipeline_mode=pl.Buffered(1))] * 2
        copy_shapes = [jax.ShapeDtypeStruct((half, d), BF16)] * 2
    else:
        w_args = list(weights)
        w_specs = [pl.BlockSpec((half, d), lambda i: (0, 0))] * 2
        copy_specs, copy_shapes = [], []
    y, *w_copies = pl.pallas_call(
        _out_kernel,
        grid=(m // tm,),
        in_specs=[rows(d), rows(half), rows(half), *w_specs, pl.BlockSpec((1, d), lambda i: (0, 0))],
        out_specs=[rows(d), *copy_specs],
        out_shape=[jax.ShapeDtypeStruct((m, d), F32), *copy_shapes],
        compiler_params=_compiler_params(("arbitrary",)),
        name="merge_out",
    )(x2d, mix_sb, mix_cv, *w_args, final_g.reshape(1, d))
    return y, w_copies


def _lower_tri(n):
    j = lax.broadcasted_iota(jnp.int32, (n, n), 0)
    s = lax.broadcasted_iota(jnp.int32, (n, n), 1)
    return (j >= s).astype(BF16)


def kernel(x_prompt, x_sample, cache_k, cache_v, state_conv, norm_g, w_in, conv_w, w_out, final_g):
    depth = w_in.shape[0]
    assert depth == 1, "single-layer step"
    bsz, seq, d = x_prompt.shape
    dbsz, dseq, _ = x_sample.shape
    past = cache_k.shape[2]
    width = NUM_GROUPS * HEAD_DIM

    tri = _lower_tri(KEY_BLOCK)

    xs = x_sample.reshape(dbsz * dseq, d)
    qs, ks, vs, _, _, sgs, mcvs, tails, w_in_bf = _project(
        xs, norm_g[0], w_in[0], conv_w[0], state_conv[0], seq_rows=dseq,
        tm=min(PROJ_ROWS, xs.shape[0]))
    msbs = _attend_sample(qs, ks, vs,
                          cache_k[0].reshape(dbsz, past * NUM_GROUPS, HEAD_DIM),
                          cache_v[0].reshape(dbsz, past * NUM_GROUPS, HEAD_DIM),
                          sgs, tri.T, batch=dbsz, t=dseq)
    ys, w_out_bf = _merge_out(xs, msbs, mcvs, w_out[0], final_g, tm=OUT_ROWS)

    xp = x_prompt.reshape(bsz * seq, d)
    zeros_left = jnp.zeros((bsz, CONV_W - 1, width), F32)
    qp, kp, vp, kbp, vbp, sgp, mcvp, tailp, _ = _project(
        xp, norm_g[0], w_in_bf, conv_w[0], zeros_left, seq_rows=seq, tm=PROJ_ROWS)
    msbp = _attend_prompt(qp, kbp, vbp, sgp, tri, batch=bsz, seq=seq, tq=KEY_BLOCK,
                          heads_per_step=PROMPT_HEADS_PER_STEP)
    yp, _ = _merge_out(xp, msbp, mcvp, w_out_bf, final_g, tm=OUT_ROWS)

    heads = lambda a, b_, t_: a.reshape(1, b_, t_, NUM_GROUPS, HEAD_DIM)
    return (yp.reshape(bsz, seq, d), ys.reshape(dbsz, dseq, d),
            heads(kp, bsz, seq), heads(vp, bsz, seq), tailp[None],
            heads(ks, dbsz, dseq), heads(vs, dbsz, dseq), tails[None])
```

```python
import functools
import math

import jax
import jax.numpy as jnp
from jax import lax
from jax.experimental import pallas as pl
from jax.experimental.pallas import tpu as pltpu

F32 = jnp.float32
BF16 = jnp.bfloat16

HEAD_DIM = 128
NUM_GROUPS = 8
NUM_SEGMENTS = 8
CONV_W = 3
CONV_CTX = CONV_W - 1
assert CONV_W == 3, "the conv epilogue writes its three taps out explicitly"
SUBLANES = 8
EPS = 1e-6
LOG2E = math.log2(math.e)
Q_PRESCALE = -(HEAD_DIM ** -0.5) * LOG2E
MXU_DEPTH = 256
KEY_BLOCK = MXU_DEPTH
PROJ_ROWS = 1024
OUT_ROWS = 512
PROMPT_HEADS_PER_STEP = 8
SAMPLE_STREAMS_PER_STEP = 4
EXP2_UNDERFLOW = -160.0

VMEM_LIMIT_BYTES = 56 * 1024 * 1024


def _compiler_params(semantics):
    return pltpu.CompilerParams(dimension_semantics=semantics,
                                vmem_limit_bytes=VMEM_LIMIT_BYTES)


def _rmsnorm_rows(x, g):
    r = lax.rsqrt(jnp.mean(x * x, axis=-1, keepdims=True) + EPS)
    return (x * r) * g


def _silu(x):
    return x * (1.0 / (1.0 + jnp.exp(-x)))


def _proj_kernel(*refs, seq_rows, tiles_per_seq, copy_weights):
    refs = list(refs)
    take = lambda count: [refs.pop(0) for _ in range(count)]
    x_ref, g_ref = take(2)
    w_refs = take(NUM_SEGMENTS)
    cw_ref, left_ref = take(2)
    q_ref, k_ref, v_ref, kb_ref, vb_ref, sg_ref, mcv_ref, tail_ref = take(8)
    wcopy_refs = take(NUM_SEGMENTS) if copy_weights else ()
    hn_ref, carry_ref = take(2)
    i = pl.program_id(0)
    h = pl.program_id(1)
    tm = x_ref.shape[0]

    @pl.when(h == 0)
    def _():
        hn_ref[...] = _rmsnorm_rows(x_ref[...], g_ref[...]).astype(BF16)

    @pl.when(jnp.logical_and(i == 0, h == 0))
    def _():
        carry_ref[...] = jnp.zeros(carry_ref.shape, F32)

    wq, wk, wv, wgs, wb, wc, wu, wgc = [r[...].astype(BF16) for r in w_refs]
    for copy_ref, w in zip(wcopy_refs, (wq, wk, wv, wgs, wb, wc, wu, wgc)):
        copy_ref[...] = w

    def project(*ws):
        acc = jnp.dot(hn_ref[...], jnp.concatenate(ws, axis=1), preferred_element_type=F32)
        return [acc[:, s * HEAD_DIM:(s + 1) * HEAD_DIM] for s in range(len(ws))]

    c, u = project(wc, wu)
    b, g_cv = project(wb, wgc)
    k, v = project(wk, wv)
    q, g_sb = project(wq, wgs)

    q_ref[...] = (q * Q_PRESCALE).astype(BF16)
    k_ref[pl.ds(h, tm, stride=NUM_GROUPS), :] = k
    v_ref[pl.ds(h, tm, stride=NUM_GROUPS), :] = v
    kb_ref[...] = k.astype(BF16)
    vb_ref[...] = v.astype(BF16)
    sg_ref[...] = _silu(g_sb).astype(BF16)

    cu = c * u
    if tiles_per_seq > 1:
        left = jnp.where(i % tiles_per_seq == 0, left_ref[...], carry_ref[h][None, 0:CONV_CTX, :])
        carry_ref[h, 0:CONV_CTX, :] = cu[tm - CONV_CTX:tm, :]
    else:
        left = left_ref[...]
    rows = seq_rows if tiles_per_seq == 1 else tm
    nseq = tm // rows
    expand = lambda a: jnp.broadcast_to(a, (nseq, rows, HEAD_DIM)).reshape(tm, HEAD_DIM)
    l0 = expand(left[:, 0:1, :])
    l1 = expand(left[:, 1:2, :])
    rs = lax.broadcasted_iota(jnp.int32, (tm, HEAD_DIM), 0) & (rows - 1)
    r1 = jnp.where(rs == 0, l1, pltpu.roll(cu, 1, axis=0))
    r2 = jnp.where(rs == 0, l0, jnp.where(rs == 1, l1, pltpu.roll(cu, 2, axis=0)))
    cw = cw_ref[...]
    conv = cw[0:1, :] * r2 + cw[1:2, :] * r1 + cw[2:3, :] * cu
    mcv_ref[...] = (b * conv * _silu(g_cv)).astype(BF16)
    tail_ref[...] = cu.reshape(nseq, rows, HEAD_DIM)[:, rows - CONV_CTX:rows, :]


def _project(x2d, norm_g, weights, conv_w, left, *, seq_rows, tm):
    m, d = x2d.shape
    width = NUM_GROUPS * HEAD_DIM
    copy_weights = not isinstance(weights, (list, tuple))
    if copy_weights:
        assert m == tm, "bf16 weight copies are written once, by a single row tile"
        w_args = [weights] * NUM_SEGMENTS
        segment = lambda s: pl.BlockSpec((d, HEAD_DIM), lambda i, h: (0, s * NUM_GROUPS + h))
    else:
        w_args = list(weights)
        segment = lambda s: pl.BlockSpec((d, HEAD_DIM), lambda i, h: (0, h))
    once = dict(pipeline_mode=pl.Buffered(1)) if m == tm else {}
    if tm >= seq_rows:
        tiles_per_seq, ns = 1, tm // seq_rows
        left_idx = lambda i, h: (i, 0, h)
    else:
        tiles_per_seq, ns = seq_rows // tm, 1
        left_idx = lambda i, h: (i // tiles_per_seq, 0, h)
    n_tails = (m // tm) * ns
    tile = lambda dt: jax.ShapeDtypeStruct((m, width), dt)
    native = jax.ShapeDtypeStruct((m * NUM_GROUPS, HEAD_DIM), F32)
    col_block = pl.BlockSpec((tm, HEAD_DIM), lambda i, h: (i, h))
    native_block = pl.BlockSpec((tm * NUM_GROUPS, HEAD_DIM), lambda i, h: (i, 0), **once)
    copy_specs, copy_shapes = [], []
    if copy_weights:
        copy_specs = [pl.BlockSpec((d, HEAD_DIM), lambda i, h: (0, h))] * NUM_SEGMENTS
        copy_shapes = [jax.ShapeDtypeStruct((d, width), BF16)] * NUM_SEGMENTS
    kern = functools.partial(_proj_kernel, seq_rows=seq_rows, tiles_per_seq=tiles_per_seq,
                             copy_weights=copy_weights)
    q, k, v, kb, vb, sg, mcv, tails, *w_copies = pl.pallas_call(
        kern,
        grid=(m // tm, NUM_GROUPS),
        in_specs=[
            pl.BlockSpec((tm, d), lambda i, h: (i, 0), **once),
            pl.BlockSpec((1, d), lambda i, h: (0, 0)),
            *[segment(s) for s in range(NUM_SEGMENTS)],
            pl.BlockSpec((CONV_W, HEAD_DIM), lambda i, h: (0, h)),
            pl.BlockSpec((ns, CONV_CTX, HEAD_DIM), left_idx),
        ],
        out_specs=[col_block, native_block, native_block, col_block, col_block, col_block,
                   col_block, pl.BlockSpec((ns, CONV_CTX, HEAD_DIM), lambda i, h: (i, 0, h)),
                   *copy_specs],
        out_shape=[tile(BF16), native, native, tile(BF16), tile(BF16), tile(BF16), tile(BF16),
                   jax.ShapeDtypeStruct((n_tails, CONV_CTX, width), F32), *copy_shapes],
        scratch_shapes=[pltpu.VMEM((tm, d), BF16),
                        pltpu.VMEM((NUM_GROUPS, SUBLANES, HEAD_DIM), F32)],
        compiler_params=_compiler_params(("arbitrary", "arbitrary")),
        name="proj",
    )(x2d, norm_g.reshape(1, d), *w_args, conv_w, left)
    tails = tails.reshape(-1, tiles_per_seq, CONV_CTX, width)[:, tiles_per_seq - 1]
    return q, k, v, kb, vb, sg, mcv, tails, w_copies


def _log2_one_minus_beta(zn, mask):
    softplus2 = jnp.log(1.0 + jnp.exp2(-jnp.abs(zn))) * LOG2E
    log_1m = jnp.minimum(zn, 0.0) - softplus2
    return log_1m if mask is None else jnp.where(mask, log_1m, 0.0)


def _split_bf16(x, axis):
    hi = x.astype(BF16)
    return jnp.concatenate([hi, (x - hi.astype(F32)).astype(BF16)], axis=axis)


def _block_weights(suffix, zn, mask):
    w = jnp.exp2(suffix - zn)
    return (w if mask is None else jnp.where(mask, w, 0.0)).astype(BF16)


def _attn_prompt_kernel(q_ref, k_ref, v_ref, sg_ref, tri_ref, o_ref, acc_ref):
    qi = pl.program_id(2)
    tq = q_ref.shape[0]
    tk = tri_ref.shape[0]
    heads = q_ref.shape[1] // HEAD_DIM
    tri2 = jnp.concatenate([tri_ref[...]] * 2, axis=0)
    hs = [slice(a * HEAD_DIM, (a + 1) * HEAD_DIM) for a in range(heads)]
    nt_dims = (((1,), (1,)), ((), ()))

    def rows_of(j):
        return pl.ds(j * tk if isinstance(j, int) else pl.multiple_of(j * tk, tk), tk)

    def logits(j):
        return tuple(lax.dot_general(q_ref[:, s], k_ref[rows_of(j), s], nt_dims,
                                     preferred_element_type=F32) for s in hs)

    def weights(zns, mask):
        pieces = [_split_bf16(_log2_one_minus_beta(zn, mask), axis=1) for zn in zns]
        suffixes = [jnp.dot(p, tri2, preferred_element_type=F32) for p in pieces]
        ws = tuple(_block_weights(sfx, zn, mask) for sfx, zn in zip(suffixes, zns))
        return ws, tuple(sfx[:, 0:1] for sfx in suffixes)

    def accumulate(ws, j, scales):
        for a, s in enumerate(hs):
            acc_ref[:, s] += scales[a] * jnp.dot(ws[a], v_ref[rows_of(j), s],
                                                 preferred_element_type=F32)

    def step(j, carries, mask):
        ws, totals = weights(logits(j), mask)
        accumulate(ws, j, [jnp.exp2(c) for c in carries])
        return tuple(c + tot for c, tot in zip(carries, totals))

    acc_ref[...] = jnp.zeros(acc_ref.shape, F32)
    rows = lax.broadcasted_iota(jnp.int32, (tq, tk), 0)
    cols = lax.broadcasted_iota(jnp.int32, (tq, tk), 1)
    carries = step(qi, (jnp.zeros((tq, 1), F32),) * heads, cols < rows)

    def any_live(carries):
        return jnp.max(functools.reduce(jnp.maximum, carries)) > EXP2_UNDERFLOW

    def body(state):
        t, _, carries = state
        carries = step(qi - 1 - t, carries, None)
        return t + 1, any_live(carries), carries

    lax.while_loop(lambda state: jnp.logical_and(state[0] < qi, state[1]), body,
                   (jnp.int32(0), any_live(carries), carries))
    o_ref[...] = (acc_ref[...] * sg_ref[...].astype(F32)).astype(BF16)


def _attend_prompt(q, kb, vb, sg, tri, *, batch, seq, tq, heads_per_step):
    nq = seq // tq
    assert tq == tri.shape[0], "query block and key block share the diagonal mask"
    cols = heads_per_step * HEAD_DIM
    qblock = pl.BlockSpec((tq, cols), lambda b, g, i: (b * nq + i, g))
    kvblock = pl.BlockSpec((seq, cols), lambda b, g, i: (b, g))
    return pl.pallas_call(
        _attn_prompt_kernel,
        grid=(batch, NUM_GROUPS // heads_per_step, nq),
        in_specs=[qblock, kvblock, kvblock, qblock,
                  pl.BlockSpec(tri.shape, lambda b, g, i: (0, 0))],
        out_specs=qblock,
        out_shape=jax.ShapeDtypeStruct(q.shape, BF16),
        scratch_shapes=[pltpu.VMEM((tq, cols), F32)],
        compiler_params=_compiler_params(("arbitrary", "arbitrary", "arbitrary")),
        name="attn_prompt",
    )(q, kb, vb, sg, tri)


def _attn_sample_kernel(q_ref, kn_ref, vn_ref, kc_hbm, vc_hbm, sg_ref, trit_ref, o_ref,
                        qnt_ref, knear_ref, vnear_ref, kfar_ref, vfar_ref, near_sems, far_sems,
                        *, t):
    step = pl.program_id(0)
    n = q_ref.shape[0] // t
    tk = trit_ref.shape[0]
    block_rows = tk * NUM_GROUPS
    n_cache = kc_hbm.shape[1] // block_rows
    lanes = NUM_GROUPS * t
    pairs = NUM_GROUPS // 2
    pad = HEAD_DIM
    nt_dims = (((1,), (1,)), ((), ()))
    slot = step % 2

    def cache_copies(stream, j, k_dst, v_dst, k_sem, v_sem):
        rows = pl.ds(j * block_rows, block_rows)
        return [pltpu.make_async_copy(kc_hbm.at[stream, rows, :], k_dst, k_sem),
                pltpu.make_async_copy(vc_hbm.at[stream, rows, :], v_dst, v_sem)]

    def near_copies(at_step, s):
        return [c for i in range(n) for c in cache_copies(
            at_step * n + i, n_cache - 1, knear_ref.at[s, i], vnear_ref.at[s, i],
            near_sems.at[0, s, i], near_sems.at[1, s, i])]

    @pl.when(step == 0)
    def _():
        for c in near_copies(0, 0):
            c.start()

    @pl.when(step + 1 < pl.num_programs(0))
    def _():
        for c in near_copies(step + 1, 1 - slot):
            c.start()

    qnt_ref[...] = jnp.zeros(qnt_ref.shape, BF16)
    for i in range(n):
        for h in range(NUM_GROUPS):
            p, half = divmod(h, 2)
            qnt_ref[i, p, h * t:(h + 1) * t, half * HEAD_DIM:(half + 1) * HEAD_DIM] = (
                q_ref[i * t:(i + 1) * t, h * HEAD_DIM:(h + 1) * HEAD_DIM])

    def neg_logits(i, load_k):
        zn = None
        for p in range(pairs):
            lhs = jnp.concatenate([load_k(2 * p), load_k(2 * p + 1)], axis=1).astype(BF16)
            d = lax.dot_general(lhs, qnt_ref[i, p], nt_dims, preferred_element_type=F32)
            zn = d if zn is None else zn + d
        return zn

    def walk(work):
        pieces = [[_split_bf16(_log2_one_minus_beta(zn, m), axis=0) for zn, _, _, m in blocks]
                  for blocks, _, _ in work]
        suffixes = [[jnp.dot(jnp.concatenate([tr, tr], axis=1), p, preferred_element_type=F32)
                     for (_, _, tr, _), p in zip(blocks, ps)]
                    for (blocks, _, _), ps in zip(work, pieces)]
        done = []
        for (blocks, carry, accs), sfxs in zip(work, suffixes):
            for (zn, load_v, _, m), sfx in zip(blocks, sfxs):
                w = jnp.exp2(sfx - zn + carry)
                if m is not None:
                    w = jnp.where(m, w, 0.0)
                w_t = w.T
                accs = [acc + jnp.dot(w_t[h * t:(h + 1) * t, :].astype(BF16),
                                      load_v(h).astype(BF16), preferred_element_type=F32)
                        for h, acc in enumerate(accs)]
                carry = carry + sfx[0:1, :]
            done.append((carry, tuple(accs)))
        return done

    head_rows = lambda ref, base, rows, h: ref[pl.ds(base + h, rows, stride=NUM_GROUPS), :]

    key_idx = lax.broadcasted_iota(jnp.int32, (pad, lanes), 0)
    query_idx = lax.broadcasted_iota(jnp.int32, (pad, lanes), 1) & (t - 1)
    zero_rows = lambda width: jnp.zeros((pad - t, width), F32)
    new_base = lambda i: i * t * NUM_GROUPS
    new_blocks = [
        (jnp.concatenate([neg_logits(i, lambda h, i=i: head_rows(kn_ref, new_base(i), t, h)),
                          zero_rows(lanes)], axis=0),
         lambda h, i=i: jnp.concatenate([head_rows(vn_ref, new_base(i), t, h),
                                         zero_rows(HEAD_DIM)], axis=0),
         trit_ref[0:pad, 0:pad], key_idx < query_idx) for i in range(n)]
    for c in near_copies(step, slot):
        c.wait()
    near_blocks = [
        (neg_logits(i, lambda h, i=i: head_rows(knear_ref.at[slot, i], 0, tk, h)),
         lambda h, i=i: head_rows(vnear_ref.at[slot, i], 0, tk, h), trit_ref[...], None)
        for i in range(n)]
    state = walk([([new_blocks[i], near_blocks[i]], jnp.zeros((1, lanes), F32),
                   [jnp.zeros((t, HEAD_DIM), F32)] * NUM_GROUPS) for i in range(n)])

    def any_live(state):
        return jnp.max(functools.reduce(jnp.maximum, [c for c, _ in state])) > EXP2_UNDERFLOW

    first_far = n_cache - 2
    far_slot = lambda j: (first_far - j) % 2

    def far_copies(j):
        s = far_slot(j)
        return [c for i in range(n) for c in cache_copies(
            step * n + i, j, kfar_ref.at[s, i], vfar_ref.at[s, i],
            far_sems.at[0, s, i], far_sems.at[1, s, i])]

    def start_if(cond, j):
        @pl.when(cond)
        def _():
            for c in far_copies(j):
                c.start()

    def wait_if(cond, j):
        @pl.when(cond)
        def _():
            for c in far_copies(j):
                c.wait()

    def older(loop_state):
        j, _, state = loop_state
        s = far_slot(j)
        for c in far_copies(j):
            c.wait()
        start_if(j >= 1, j - 1)
        far_blocks = [(neg_logits(i, lambda h, i=i: head_rows(kfar_ref.at[s, i], 0, tk, h)),
                       lambda h, i=i: head_rows(vfar_ref.at[s, i], 0, tk, h), trit_ref[...], None)
                      for i in range(n)]
        state = walk([([far_blocks[i]], state[i][0], list(state[i][1])) for i in range(n)])
        return j - 1, any_live(state), tuple(state)

    live = any_live(state)
    if first_far >= 0:
        start_if(live, first_far)
    j_end, _, state = lax.while_loop(lambda s: jnp.logical_and(s[0] >= 0, s[1]), older,
                                     (jnp.int32(first_far), live, tuple(state)))
    if first_far >= 0:
        wait_if(jnp.logical_and(live, j_end >= 0), j_end)

    for i, (_, accs) in enumerate(state):
        for h, acc in enumerate(accs):
            rows, hs = slice(i * t, (i + 1) * t), slice(h * HEAD_DIM, (h + 1) * HEAD_DIM)
            o_ref[rows, hs] = (acc * sg_ref[rows, hs].astype(F32)).astype(BF16)


def _attend_sample(q, k_new, v_new, cache_k, cache_v, sg, tri_t, *, batch, t):
    width = q.shape[1]
    tk = tri_t.shape[0]
    block_rows = tk * NUM_GROUPS
    assert t & (t - 1) == 0 and NUM_GROUPS * t == tk and cache_k.shape[1] % block_rows == 0
    n = SAMPLE_STREAMS_PER_STEP
    assert batch % n == 0
    rowblock = pl.BlockSpec((n * t, width), lambda b: (b, 0))
    newblock = pl.BlockSpec((n * t * NUM_GROUPS, HEAD_DIM), lambda b: (b, 0))
    in_hbm = pl.BlockSpec(memory_space=pl.ANY)
    key_blocks = lambda *lead: pltpu.VMEM((*lead, n, block_rows, HEAD_DIM), F32)
    return pl.pallas_call(
        functools.partial(_attn_sample_kernel, t=t),
        grid=(batch // n,),
        in_specs=[rowblock, newblock, newblock, in_hbm, in_hbm, rowblock,
                  pl.BlockSpec(tri_t.shape, lambda b: (0, 0))],
        out_specs=rowblock,
        out_shape=jax.ShapeDtypeStruct(q.shape, BF16),
        scratch_shapes=[pltpu.VMEM((n, NUM_GROUPS // 2, NUM_GROUPS * t, 2 * HEAD_DIM), BF16),
                        key_blocks(2), key_blocks(2), key_blocks(2), key_blocks(2),
                        pltpu.SemaphoreType.DMA((2, 2, n)), pltpu.SemaphoreType.DMA((2, 2, n))],
        compiler_params=_compiler_params(("arbitrary",)),
        name="attn_sample",
    )(q, k_new, v_new, cache_k, cache_v, sg, tri_t)


def _out_kernel(x_ref, msb_ref, mcv_ref, wsb_ref, wcv_ref, g_ref, y_ref, *wcopy_refs):
    w_sb, w_cv = wsb_ref[...].astype(BF16), wcv_ref[...].astype(BF16)
    for copy_ref, w in zip(wcopy_refs, (w_sb, w_cv)):
        copy_ref[...] = w
    y = (x_ref[...]
         + jnp.dot(msb_ref[...], w_sb, preferred_element_type=F32)
         + jnp.dot(mcv_ref[...], w_cv, preferred_element_type=F32))
    y_ref[...] = _rmsnorm_rows(y, g_ref[...])


def _merge_out(x2d, mix_sb, mix_cv, weights, final_g, *, tm):
    m, d = x2d.shape
    half = mix_sb.shape[1]
    copy_weights = not isinstance(weights, (list, tuple))
    rows = lambda width: pl.BlockSpec((tm, width), lambda i: (i, 0))
    if copy_weights:
        w_args = [weights, weights]
        w_specs = [pl.BlockSpec((half, d), lambda i, r=r: (r, 0), pipeline_mode=pl.Buffered(1))
                   for r in range(2)]
        copy_specs = [pl.BlockSpec((half, d), lambda i: (0, 0), pipeline_mode=pl.Buffered(1))] * 2
        copy_shapes = [jax.ShapeDtypeStruct((half, d), BF16)] * 2
    else:
        w_args = list(weights)
        w_specs = [pl.BlockSpec((half, d), lambda i: (0, 0))] * 2
        copy_specs, copy_shapes = [], []
    y, *w_copies = pl.pallas_call(
        _out_kernel,
        grid=(m // tm,),
        in_specs=[rows(d), rows(half), rows(half), *w_specs, pl.BlockSpec((1, d), lambda i: (0, 0))],
        out_specs=[rows(d), *copy_specs],
        out_shape=[jax.ShapeDtypeStruct((m, d), F32), *copy_shapes],
        compiler_params=_compiler_params(("arbitrary",)),
        name="merge_out",
    )(x2d, mix_sb, mix_cv, *w_args, final_g.reshape(1, d))
    return y, w_copies


def _lower_tri(n):
    j = lax.broadcasted_iota(jnp.int32, (n, n), 0)
    s = lax.broadcasted_iota(jnp.int32, (n, n), 1)
    return (j >= s).astype(BF16)


def kernel(x_prompt, x_sample, cache_k, cache_v, state_conv, norm_g, w_in, conv_w, w_out, final_g):
    depth = w_in.shape[0]
    assert depth == 1, "single-layer step"
    bsz, seq, d = x_prompt.shape
    dbsz, dseq, _ = x_sample.shape
    past = cache_k.shape[2]
    width = NUM_GROUPS * HEAD_DIM

    tri = _lower_tri(KEY_BLOCK)

    xs = x_sample.reshape(dbsz * dseq, d)
    qs, ks, vs, _, _, sgs, mcvs, tails, w_in_bf = _project(
        xs, norm_g[0], w_in[0], conv_w[0], state_conv[0], seq_rows=dseq,
        tm=min(PROJ_ROWS, xs.shape[0]))
    msbs = _attend_sample(qs, ks, vs,
                          cache_k[0].reshape(dbsz, past * NUM_GROUPS, HEAD_DIM),
                          cache_v[0].reshape(dbsz, past * NUM_GROUPS, HEAD_DIM),
                          sgs, tri.T, batch=dbsz, t=dseq)
    ys, w_out_bf = _merge_out(xs, msbs, mcvs, w_out[0], final_g, tm=OUT_ROWS)

    xp = x_prompt.reshape(bsz * seq, d)
    zeros_left = jnp.zeros((bsz, CONV_W - 1, width), F32)
    qp, kp, vp, kbp, vbp, sgp, mcvp, tailp, _ = _project(
        xp, norm_g[0], w_in_bf, conv_w[0], zeros_left, seq_rows=seq, tm=PROJ_ROWS)
    msbp = _attend_prompt(qp, kbp, vbp, sgp, tri, batch=bsz, seq=seq, tq=KEY_BLOCK,
                          heads_per_step=PROMPT_HEADS_PER_STEP)
    yp, _ = _merge_out(xp, msbp, mcvp, w_out_bf, final_g, tm=OUT_ROWS)

    heads = lambda a, b_, t_: a.reshape(1, b_, t_, NUM_GROUPS, HEAD_DIM)
    return (yp.reshape(bsz, seq, d), ys.reshape(dbsz, dseq, d),
            heads(kp, bsz, seq), heads(vp, bsz, seq), tailp[None],
            heads(ks, dbsz, dseq), heads(vs, dbsz, dseq), tails[None])
```

```python
import functools
import math

import jax
import jax.numpy as jnp
from jax import lax
from jax.experimental import pallas as pl
from jax.experimental.pallas import tpu as pltpu

F32 = jnp.float32
BF16 = jnp.bfloat16

HEAD_DIM = 128
NUM_GROUPS = 8
NUM_SEGMENTS = 8
CONV_W = 3
CONV_CTX = CONV_W - 1
assert CONV_W == 3, "the conv epilogue writes its three taps out explicitly"
SUBLANES = 8
EPS = 1e-6
LOG2E = math.log2(math.e)
Q_PRESCALE = -(HEAD_DIM ** -0.5) * LOG2E
MXU_DEPTH = 256
KEY_BLOCK = MXU_DEPTH
PROJ_ROWS = 1024
OUT_ROWS = 512
PROMPT_HEADS_PER_STEP = 8
SAMPLE_STREAMS_PER_STEP = 4
EXP2_UNDERFLOW = -160.0

VMEM_LIMIT_BYTES = 56 * 1024 * 1024


def _compiler_params(semantics):
    return pltpu.CompilerParams(dimension_semantics=semantics,
                                vmem_limit_bytes=VMEM_LIMIT_BYTES)


def _rmsnorm_rows(x, g):
    r = lax.rsqrt(jnp.mean(x * x, axis=-1, keepdims=True) + EPS)
    return (x * r) * g


def _silu(x):
    return x * (1.0 / (1.0 + jnp.exp(-x)))


def _proj_kernel(*refs, seq_rows, tiles_per_seq, copy_weights, lookahead):
    refs = list(refs)
    take = lambda count: [refs.pop(0) for _ in range(count)]
    x_ref, = take(1)
    xnext_ref, = take(1) if lookahead else (None,)
    g_ref, = take(1)
    w_refs = take(NUM_SEGMENTS)
    cw_ref, left_ref = take(2)
    q_ref, k_ref, v_ref, kb_ref, vb_ref, sg_ref, mcv_ref, tail_ref = take(8)
    wcopy_refs = take(NUM_SEGMENTS) if copy_weights else ()
    hn_even_ref, hn_odd_ref, carry_ref = take(3)
    i = pl.program_id(0)
    h = pl.program_id(1)
    tm = x_ref.shape[0]
    normalized = lambda ref: _rmsnorm_rows(ref[...], g_ref[...]).astype(BF16)

    @pl.when(jnp.logical_and(i == 0, h == 0))
    def _():
        hn_even_ref[...] = normalized(x_ref)
        carry_ref[...] = jnp.zeros(carry_ref.shape, F32)

    step = functools.partial(
        _proj_step, xnext_ref=xnext_ref, g_ref=g_ref, w_refs=w_refs, cw_ref=cw_ref,
        left_ref=left_ref, wcopy_refs=wcopy_refs, carry_ref=carry_ref, seq_rows=seq_rows,
        tiles_per_seq=tiles_per_seq,
        out_refs=(q_ref, k_ref, v_ref, kb_ref, vb_ref, sg_ref, mcv_ref, tail_ref))

    if not lookahead:
        step(hn_even_ref, None)
        return

    @pl.when(i % 2 == 0)
    def _():
        step(hn_even_ref, hn_odd_ref)

    @pl.when(i % 2 == 1)
    def _():
        step(hn_odd_ref, hn_even_ref)


def _proj_step(hn_ref, hn_next_ref, *, xnext_ref, g_ref, w_refs, cw_ref, left_ref, out_refs,
               wcopy_refs, carry_ref, seq_rows, tiles_per_seq):
    q_ref, k_ref, v_ref, kb_ref, vb_ref, sg_ref, mcv_ref, tail_ref = out_refs
    i = pl.program_id(0)
    h = pl.program_id(1)
    tm = hn_ref.shape[0]
    hn = hn_ref[...]

    if hn_next_ref is not None:
        chunk = xnext_ref.shape[0]
        hn_next_ref[pl.ds(pl.multiple_of(h * chunk, chunk), chunk), :] = (
            _rmsnorm_rows(xnext_ref[...], g_ref[...]).astype(BF16))

    wq, wk, wv, wgs, wb, wc, wu, wgc = [r[...].astype(BF16) for r in w_refs]
    for copy_ref, w in zip(wcopy_refs, (wq, wk, wv, wgs, wb, wc, wu, wgc)):
        copy_ref[...] = w

    def project(*ws):
        acc = jnp.dot(hn, jnp.concatenate(ws, axis=1), preferred_element_type=F32)
        return [acc[:, s * HEAD_DIM:(s + 1) * HEAD_DIM] for s in range(len(ws))]

    c, u = project(wc, wu)
    b, g_cv = project(wb, wgc)
    k, v = project(wk, wv)
    q, g_sb = project(wq, wgs)

    q_ref[...] = (q * Q_PRESCALE).astype(BF16)
    k_ref[pl.ds(h, tm, stride=NUM_GROUPS), :] = k
    v_ref[pl.ds(h, tm, stride=NUM_GROUPS), :] = v
    kb_ref[...] = k.astype(BF16)
    vb_ref[...] = v.astype(BF16)
    sg_ref[...] = _silu(g_sb).astype(BF16)

    cu = c * u
    if tiles_per_seq > 1:
        left = jnp.where(i % tiles_per_seq == 0, left_ref[...], carry_ref[h][None, 0:CONV_CTX, :])
        carry_ref[h, 0:CONV_CTX, :] = cu[tm - CONV_CTX:tm, :]
    else:
        left = left_ref[...]
    rows = seq_rows if tiles_per_seq == 1 else tm
    nseq = tm // rows
    expand = lambda a: jnp.broadcast_to(a, (nseq, rows, HEAD_DIM)).reshape(tm, HEAD_DIM)
    l0 = expand(left[:, 0:1, :])
    l1 = expand(left[:, 1:2, :])
    rs = lax.broadcasted_iota(jnp.int32, (tm, HEAD_DIM), 0) & (rows - 1)
    r1 = jnp.where(rs == 0, l1, pltpu.roll(cu, 1, axis=0))
    r2 = jnp.where(rs == 0, l0, jnp.where(rs == 1, l1, pltpu.roll(cu, 2, axis=0)))
    cw = cw_ref[...]
    conv = cw[0:1, :] * r2 + cw[1:2, :] * r1 + cw[2:3, :] * cu
    mcv_ref[...] = (b * conv * _silu(g_cv)).astype(BF16)
    tail_ref[...] = cu.reshape(nseq, rows, HEAD_DIM)[:, rows - CONV_CTX:rows, :]


def _project(x2d, norm_g, weights, conv_w, left, *, seq_rows, tm):
    m, d = x2d.shape
    width = NUM_GROUPS * HEAD_DIM
    copy_weights = not isinstance(weights, (list, tuple))
    if copy_weights:
        assert m == tm, "bf16 weight copies are written once, by a single row tile"
        w_args = [weights] * NUM_SEGMENTS
        segment = lambda s: pl.BlockSpec((d, HEAD_DIM), lambda i, h: (0, s * NUM_GROUPS + h))
    else:
        w_args = list(weights)
        segment = lambda s: pl.BlockSpec((d, HEAD_DIM), lambda i, h: (0, h))
    once = dict(pipeline_mode=pl.Buffered(1)) if m == tm else {}
    if tm >= seq_rows:
        tiles_per_seq, ns = 1, tm // seq_rows
        left_idx = lambda i, h: (i, 0, h)
    else:
        tiles_per_seq, ns = seq_rows // tm, 1
        left_idx = lambda i, h: (i // tiles_per_seq, 0, h)
    n_tails = (m // tm) * ns
    tile = lambda dt: jax.ShapeDtypeStruct((m, width), dt)
    native = jax.ShapeDtypeStruct((m * NUM_GROUPS, HEAD_DIM), F32)
    col_block = pl.BlockSpec((tm, HEAD_DIM), lambda i, h: (i, h))
    native_block = pl.BlockSpec((tm * NUM_GROUPS, HEAD_DIM), lambda i, h: (i, 0), **once)
    copy_specs, copy_shapes = [], []
    if copy_weights:
        copy_specs = [pl.BlockSpec((d, HEAD_DIM), lambda i, h: (0, h))] * NUM_SEGMENTS
        copy_shapes = [jax.ShapeDtypeStruct((d, width), BF16)] * NUM_SEGMENTS
    n_tiles = m // tm
    lookahead = n_tiles > 1
    x_args, x_specs = [x2d], [pl.BlockSpec((tm, d), lambda i, h: (0, 0),
                                           pipeline_mode=pl.Buffered(1))]
    if lookahead:
        x_args.append(x2d)
        x_specs.append(pl.BlockSpec(
            (tm // NUM_GROUPS, d),
            lambda i, h: (jnp.minimum(i + 1, n_tiles - 1) * NUM_GROUPS + h, 0)))
    kern = functools.partial(_proj_kernel, seq_rows=seq_rows, tiles_per_seq=tiles_per_seq,
                             copy_weights=copy_weights, lookahead=lookahead)
    q, k, v, kb, vb, sg, mcv, tails, *w_copies = pl.pallas_call(
        kern,
        grid=(n_tiles, NUM_GROUPS),
        in_specs=[
            *x_specs,
            pl.BlockSpec((1, d), lambda i, h: (0, 0)),
            *[segment(s) for s in range(NUM_SEGMENTS)],
            pl.BlockSpec((CONV_W, HEAD_DIM), lambda i, h: (0, h)),
            pl.BlockSpec((ns, CONV_CTX, HEAD_DIM), left_idx),
        ],
        out_specs=[col_block, native_block, native_block, col_block, col_block, col_block,
                   col_block, pl.BlockSpec((ns, CONV_CTX, HEAD_DIM), lambda i, h: (i, 0, h)),
                   *copy_specs],
        out_shape=[tile(BF16), native, native, tile(BF16), tile(BF16), tile(BF16), tile(BF16),
                   jax.ShapeDtypeStruct((n_tails, CONV_CTX, width), F32), *copy_shapes],
        scratch_shapes=[pltpu.VMEM((tm, d), BF16),
                        pltpu.VMEM((tm, d) if lookahead else (2 * SUBLANES, HEAD_DIM), BF16),
                        pltpu.VMEM((NUM_GROUPS, SUBLANES, HEAD_DIM), F32)],
        compiler_params=_compiler_params(("arbitrary", "arbitrary")),
        name="proj",
    )(*x_args, norm_g.reshape(1, d), *w_args, conv_w, left)
    tails = tails.reshape(-1, tiles_per_seq, CONV_CTX, width)[:, tiles_per_seq - 1]
    return q, k, v, kb, vb, sg, mcv, tails, w_copies


def _log2_one_minus_beta(zn, mask):
    softplus2 = jnp.log(1.0 + jnp.exp2(-jnp.abs(zn))) * LOG2E
    log_1m = jnp.minimum(zn, 0.0) - softplus2
    return log_1m if mask is None else jnp.where(mask, log_1m, 0.0)


def _split_bf16(x, axis):
    hi = x.astype(BF16)
    return jnp.concatenate([hi, (x - hi.astype(F32)).astype(BF16)], axis=axis)


def _block_weights(suffix, zn, mask):
    w = jnp.exp2(suffix - zn)
    return (w if mask is None else jnp.where(mask, w, 0.0)).astype(BF16)


def _attn_prompt_kernel(q_ref, k_ref, v_ref, sg_ref, tri_ref, o_ref, acc_ref):
    qi = pl.program_id(2)
    tq = q_ref.shape[0]
    tk = tri_ref.shape[0]
    heads = q_ref.shape[1] // HEAD_DIM
    tri2 = jnp.concatenate([tri_ref[...]] * 2, axis=0)
    hs = [slice(a * HEAD_DIM, (a + 1) * HEAD_DIM) for a in range(heads)]
    nt_dims = (((1,), (1,)), ((), ()))

    def rows_of(j):
        return pl.ds(j * tk if isinstance(j, int) else pl.multiple_of(j * tk, tk), tk)

    def logits(j):
        return tuple(lax.dot_general(q_ref[:, s], k_ref[rows_of(j), s], nt_dims,
                                     preferred_element_type=F32) for s in hs)

    def weights(zns, mask):
        pieces = [_split_bf16(_log2_one_minus_beta(zn, mask), axis=1) for zn in zns]
        suffixes = [jnp.dot(p, tri2, preferred_element_type=F32) for p in pieces]
        ws = tuple(_block_weights(sfx, zn, mask) for sfx, zn in zip(suffixes, zns))
        return ws, tuple(sfx[:, 0:1] for sfx in suffixes)

    def accumulate(ws, j, scales):
        for a, s in enumerate(hs):
            acc_ref[:, s] += scales[a] * jnp.dot(ws[a], v_ref[rows_of(j), s],
                                                 preferred_element_type=F32)

    def step(j, carries, mask):
        ws, totals = weights(logits(j), mask)
        accumulate(ws, j, [jnp.exp2(c) for c in carries])
        return tuple(c + tot for c, tot in zip(carries, totals))

    acc_ref[...] = jnp.zeros(acc_ref.shape, F32)
    rows = lax.broadcasted_iota(jnp.int32, (tq, tk), 0)
    cols = lax.broadcasted_iota(jnp.int32, (tq, tk), 1)
    carries = step(qi, (jnp.zeros((tq, 1), F32),) * heads, cols < rows)

    def any_live(carries):
        return jnp.max(functools.reduce(jnp.maximum, carries)) > EXP2_UNDERFLOW

    def body(state):
        t, _, carries = state
        carries = step(qi - 1 - t, carries, None)
        return t + 1, any_live(carries), carries

    lax.while_loop(lambda state: jnp.logical_and(state[0] < qi, state[1]), body,
                   (jnp.int32(0), any_live(carries), carries))
    o_ref[...] = (acc_ref[...] * sg_ref[...].astype(F32)).astype(BF16)


def _attend_prompt(q, kb, vb, sg, tri, *, batch, seq, tq, heads_per_step):
    nq = seq // tq
    assert tq == tri.shape[0], "query block and key block share the diagonal mask"
    cols = heads_per_step * HEAD_DIM
    qblock = pl.BlockSpec((tq, cols), lambda b, g, i: (b * nq + i, g))
    kvblock = pl.BlockSpec((seq, cols), lambda b, g, i: (b, g))
    return pl.pallas_call(
        _attn_prompt_kernel,
        grid=(batch, NUM_GROUPS // heads_per_step, nq),
        in_specs=[qblock, kvblock, kvblock, qblock,
                  pl.BlockSpec(tri.shape, lambda b, g, i: (0, 0))],
        out_specs=qblock,
        out_shape=jax.ShapeDtypeStruct(q.shape, BF16),
        scratch_shapes=[pltpu.VMEM((tq, cols), F32)],
        compiler_params=_compiler_params(("arbitrary", "arbitrary", "arbitrary")),
        name="attn_prompt",
    )(q, kb, vb, sg, tri)


def _attn_sample_kernel(q_ref, kn_ref, vn_ref, kc_hbm, vc_hbm, sg_ref, trit_ref, o_ref,
                        qnt_ref, knear_ref, vnear_ref, kfar_ref, vfar_ref, near_sems, far_sems,
                        *, t):
    step = pl.program_id(0)
    n = q_ref.shape[0] // t
    tk = trit_ref.shape[0]
    block_rows = tk * NUM_GROUPS
    n_cache = kc_hbm.shape[1] // block_rows
    lanes = NUM_GROUPS * t
    pairs = NUM_GROUPS // 2
    pad = HEAD_DIM
    nt_dims = (((1,), (1,)), ((), ()))
    slot = step % 2

    def cache_copies(stream, j, k_dst, v_dst, k_sem, v_sem):
        rows = pl.ds(j * block_rows, block_rows)
        return [pltpu.make_async_copy(kc_hbm.at[stream, rows, :], k_dst, k_sem),
                pltpu.make_async_copy(vc_hbm.at[stream, rows, :], v_dst, v_sem)]

    def near_copies(at_step, s):
        return [c for i in range(n) for c in cache_copies(
            at_step * n + i, n_cache - 1, knear_ref.at[s, i], vnear_ref.at[s, i],
            near_sems.at[0, s, i], near_sems.at[1, s, i])]

    @pl.when(step == 0)
    def _():
        for c in near_copies(0, 0):
            c.start()

    @pl.when(step + 1 < pl.num_programs(0))
    def _():
        for c in near_copies(step + 1, 1 - slot):
            c.start()

    qnt_ref[...] = jnp.zeros(qnt_ref.shape, BF16)
    for i in range(n):
        for h in range(NUM_GROUPS):
            p, half = divmod(h, 2)
            qnt_ref[i, p, h * t:(h + 1) * t, half * HEAD_DIM:(half + 1) * HEAD_DIM] = (
                q_ref[i * t:(i + 1) * t, h * HEAD_DIM:(h + 1) * HEAD_DIM])

    def neg_logits(i, load_k):
        zn = None
        for p in range(pairs):
            lhs = jnp.concatenate([load_k(2 * p), load_k(2 * p + 1)], axis=1).astype(BF16)
            d = lax.dot_general(lhs, qnt_ref[i, p], nt_dims, preferred_element_type=F32)
            zn = d if zn is None else zn + d
        return zn

    def walk(work):
        pieces = [[_split_bf16(_log2_one_minus_beta(zn, m), axis=0) for zn, _, _, m in blocks]
                  for blocks, _, _ in work]
        suffixes = [[jnp.dot(jnp.concatenate([tr, tr], axis=1), p, preferred_element_type=F32)
                     for (_, _, tr, _), p in zip(blocks, ps)]
                    for (blocks, _, _), ps in zip(work, pieces)]
        done = []
        for (blocks, carry, accs), sfxs in zip(work, suffixes):
            for (zn, load_v, _, m), sfx in zip(blocks, sfxs):
                w = jnp.exp2(sfx - zn + carry)
                if m is not None:
                    w = jnp.where(m, w, 0.0)
                w_t = w.T
                accs = [acc + jnp.dot(w_t[h * t:(h + 1) * t, :].astype(BF16),
                                      load_v(h).astype(BF16), preferred_element_type=F32)
                        for h, acc in enumerate(accs)]
                carry = carry + sfx[0:1, :]
            done.append((carry, tuple(accs)))
        return done

    head_rows = lambda ref, base, rows, h: ref[pl.ds(base + h, rows, stride=NUM_GROUPS), :]

    key_idx = lax.broadcasted_iota(jnp.int32, (pad, lanes), 0)
    query_idx = lax.broadcasted_iota(jnp.int32, (pad, lanes), 1) & (t - 1)
    zero_rows = lambda width: jnp.zeros((pad - t, width), F32)
    new_base = lambda i: i * t * NUM_GROUPS
    new_blocks = [
        (jnp.concatenate([neg_logits(i, lambda h, i=i: head_rows(kn_ref, new_base(i), t, h)),
                          zero_rows(lanes)], axis=0),
         lambda h, i=i: jnp.concatenate([head_rows(vn_ref, new_base(i), t, h),
                                         zero_rows(HEAD_DIM)], axis=0),
         trit_ref[0:pad, 0:pad], key_idx < query_idx) for i in range(n)]
    for c in near_copies(step, slot):
        c.wait()
    near_blocks = [
        (neg_logits(i, lambda h, i=i: head_rows(knear_ref.at[slot, i], 0, tk, h)),
         lambda h, i=i: head_rows(vnear_ref.at[slot, i], 0, tk, h), trit_ref[...], None)
        for i in range(n)]
    state = walk([([new_blocks[i], near_blocks[i]], jnp.zeros((1, lanes), F32),
                   [jnp.zeros((t, HEAD_DIM), F32)] * NUM_GROUPS) for i in range(n)])

    def any_live(state):
        return jnp.max(functools.reduce(jnp.maximum, [c for c, _ in state])) > EXP2_UNDERFLOW

    first_far = n_cache - 2
    far_slot = lambda j: (first_far - j) % 2

    def far_copies(j):
        s = far_slot(j)
        return [c for i in range(n) for c in cache_copies(
            step * n + i, j, kfar_ref.at[s, i], vfar_ref.at[s, i],
            far_sems.at[0, s, i], far_sems.at[1, s, i])]

    def start_if(cond, j):
        @pl.when(cond)
        def _():
            for c in far_copies(j):
                c.start()

    def wait_if(cond, j):
        @pl.when(cond)
        def _():
            for c in far_copies(j):
                c.wait()

    def older(loop_state):
        j, _, state = loop_state
        s = far_slot(j)
        for c in far_copies(j):
            c.wait()
        start_if(j >= 1, j - 1)
        far_blocks = [(neg_logits(i, lambda h, i=i: head_rows(kfar_ref.at[s, i], 0, tk, h)),
                       lambda h, i=i: head_rows(vfar_ref.at[s, i], 0, tk, h), trit_ref[...], None)
                      for i in range(n)]
        state = walk([([far_blocks[i]], state[i][0], list(state[i][1])) for i in range(n)])
        return j - 1, any_live(state), tuple(state)

    live = any_live(state)
    if first_far >= 0:
        start_if(live, first_far)
    j_end, _, state = lax.while_loop(lambda s: jnp.logical_and(s[0] >= 0, s[1]), older,
                                     (jnp.int32(first_far), live, tuple(state)))
    if first_far >= 0:
        wait_if(jnp.logical_and(live, j_end >= 0), j_end)

    for i, (_, accs) in enumerate(state):
        for h, acc in enumerate(accs):
            rows, hs = slice(i * t, (i + 1) * t), slice(h * HEAD_DIM, (h + 1) * HEAD_DIM)
            o_ref[rows, hs] = (acc * sg_ref[rows, hs].astype(F32)).astype(BF16)


def _attend_sample(q, k_new, v_new, cache_k, cache_v, sg, tri_t, *, batch, t):
    width = q.shape[1]
    tk = tri_t.shape[0]
    block_rows = tk * NUM_GROUPS
    assert t & (t - 1) == 0 and NUM_GROUPS * t == tk and cache_k.shape[1] % block_rows == 0
    n = SAMPLE_STREAMS_PER_STEP
    assert batch % n == 0
    rowblock = pl.BlockSpec((n * t, width), lambda b: (b, 0))
    newblock = pl.BlockSpec((n * t * NUM_GROUPS, HEAD_DIM), lambda b: (b, 0))
    in_hbm = pl.BlockSpec(memory_space=pl.ANY)
    key_blocks = lambda *lead: pltpu.VMEM((*lead, n, block_rows, HEAD_DIM), F32)
    return pl.pallas_call(
        functools.partial(_attn_sample_kernel, t=t),
        grid=(batch // n,),
        in_specs=[rowblock, newblock, newblock, in_hbm, in_hbm, rowblock,
                  pl.BlockSpec(tri_t.shape, lambda b: (0, 0))],
        out_specs=rowblock,
        out_shape=jax.ShapeDtypeStruct(q.shape, BF16),
        scratch_shapes=[pltpu.VMEM((n, NUM_GROUPS // 2, NUM_GROUPS * t, 2 * HEAD_DIM), BF16),
                        key_blocks(2), key_blocks(2), key_blocks(2), key_blocks(2),
                        pltpu.SemaphoreType.DMA((2, 2, n)), pltpu.SemaphoreType.DMA((2, 2, n))],
        compiler_params=_compiler_params(("arbitrary",)),
        name="attn_sample",
    )(q, k_new, v_new, cache_k, cache_v, sg, tri_t)


def _out_kernel(x_ref, msb_ref, mcv_ref, wsb_ref, wcv_ref, g_ref, y_ref, *wcopy_refs):
    w_sb, w_cv = wsb_ref[...].astype(BF16), wcv_ref[...].astype(BF16)
    for copy_ref, w in zip(wcopy_refs, (w_sb, w_cv)):
        copy_ref[...] = w
    y = (x_ref[...]
         + jnp.dot(msb_ref[...], w_sb, preferred_element_type=F32)
         + jnp.dot(mcv_ref[...], w_cv, preferred_element_type=F32))
    y_ref[...] = _rmsnorm_rows(y, g_ref[...])


def _merge_out(x2d, mix_sb, mix_cv, weights, final_g, *, tm):
    m, d = x2d.shape
    half = mix_sb.shape[1]
    copy_weights = not isinstance(weights, (list, tuple))
    rows = lambda width: pl.BlockSpec((tm, width), lambda i: (i, 0))
    if copy_weights:
        w_args = [weights, weights]
        w_specs = [pl.BlockSpec((half, d), lambda i, r=r: (r, 0), pipeline_mode=pl.Buffered(1))
                   for r in range(2)]
        copy_specs = [pl.BlockSpec((half, d), lambda i: (0, 0), pipeline_mode=pl.Buffered(1))] * 2
        copy_shapes = [jax.ShapeDtypeStruct((half, d), BF16)] * 2
    else:
        w_args = list(weights)
        w_specs = [pl.BlockSpec((half, d), lambda i: (0, 0))] * 2
        copy_specs, copy_shapes = [], []
    y, *w_copies = pl.pallas_call(
        _out_kernel,
        grid=(m // tm,),
        in_specs=[rows(d), rows(half), rows(half), *w_specs, pl.BlockSpec((1, d), lambda i: (0, 0))],
        out_specs=[rows(d), *copy_specs],
        out_shape=[jax.ShapeDtypeStruct((m, d), F32), *copy_shapes],
        compiler_params=_compiler_params(("arbitrary",)),
        name="merge_out",
    )(x2d, mix_sb, mix_cv, *w_args, final_g.reshape(1, d))
    return y, w_copies


def _lower_tri(n):
    j = lax.broadcasted_iota(jnp.int32, (n, n), 0)
    s = lax.broadcasted_iota(jnp.int32, (n, n), 1)
    return (j >= s).astype(BF16)


def kernel(x_prompt, x_sample, cache_k, cache_v, state_conv, norm_g, w_in, conv_w, w_out, final_g):
    depth = w_in.shape[0]
    assert depth == 1, "single-layer step"
    bsz, seq, d = x_prompt.shape
    dbsz, dseq, _ = x_sample.shape
    past = cache_k.shape[2]
    width = NUM_GROUPS * HEAD_DIM

    tri = _lower_tri(KEY_BLOCK)

    xs = x_sample.reshape(dbsz * dseq, d)
    qs, ks, vs, _, _, sgs, mcvs, tails, w_in_bf = _project(
        xs, norm_g[0], w_in[0], conv_w[0], state_conv[0], seq_rows=dseq,
        tm=min(PROJ_ROWS, xs.shape[0]))
    msbs = _attend_sample(qs, ks, vs,
                          cache_k[0].reshape(dbsz, past * NUM_GROUPS, HEAD_DIM),
                          cache_v[0].reshape(dbsz, past * NUM_GROUPS, HEAD_DIM),
                          sgs, tri.T, batch=dbsz, t=dseq)
    ys, w_out_bf = _merge_out(xs, msbs, mcvs, w_out[0], final_g, tm=OUT_ROWS)

    xp = x_prompt.reshape(bsz * seq, d)
    zeros_left = jnp.zeros((bsz, CONV_W - 1, width), F32)
    qp, kp, vp, kbp, vbp, sgp, mcvp, tailp, _ = _project(
        xp, norm_g[0], w_in_bf, conv_w[0], zeros_left, seq_rows=seq, tm=PROJ_ROWS)
    msbp = _attend_prompt(qp, kbp, vbp, sgp, tri, batch=bsz, seq=seq, tq=KEY_BLOCK,
                          heads_per_step=PROMPT_HEADS_PER_STEP)
    yp, _ = _merge_out(xp, msbp, mcvp, w_out_bf, final_g, tm=OUT_ROWS)

    heads = lambda a, b_, t_: a.reshape(1, b_, t_, NUM_GROUPS, HEAD_DIM)
    return (yp.reshape(bsz, seq, d), ys.reshape(dbsz, dseq, d),
            heads(kp, bsz, seq), heads(vp, bsz, seq), tailp[None],
            heads(ks, dbsz, dseq), heads(vs, dbsz, dseq), tails[None])
```

```python
import functools
import math

import jax
import jax.numpy as jnp
from jax import lax
from jax.experimental import pallas as pl
from jax.experimental.pallas import tpu as pltpu

F32 = jnp.float32
BF16 = jnp.bfloat16

HEAD_DIM = 128
NUM_GROUPS = 8
NUM_SEGMENTS = 8
CONV_W = 3
CONV_CTX = CONV_W - 1
assert CONV_W == 3, "the conv epilogue writes its three taps out explicitly"
SUBLANES = 8
EPS = 1e-6
LOG2E = math.log2(math.e)
Q_PRESCALE = -(HEAD_DIM ** -0.5) * LOG2E
MXU_DEPTH = 256
KEY_BLOCK = MXU_DEPTH
PROJ_ROWS = 1024
OUT_ROWS = 512
PROMPT_HEADS_PER_STEP = 8
SAMPLE_STREAMS_PER_STEP = 4
EXP2_UNDERFLOW = -160.0
MASKED_LOGIT = 1e30

VMEM_LIMIT_BYTES = 56 * 1024 * 1024


def _compiler_params(semantics):
    return pltpu.CompilerParams(dimension_semantics=semantics,
                                vmem_limit_bytes=VMEM_LIMIT_BYTES)


def _rmsnorm_rows(x, g):
    r = lax.rsqrt(jnp.mean(x * x, axis=-1, keepdims=True) + EPS)
    return (x * r) * g


def _silu(x):
    return x * (1.0 / (1.0 + jnp.exp(-x)))


def _proj_kernel(*refs, seq_rows, tiles_per_seq, copy_weights):
    refs = list(refs)
    take = lambda count: [refs.pop(0) for _ in range(count)]
    x_ref, g_ref = take(2)
    w_refs = take(NUM_SEGMENTS)
    cw_ref, left_ref = take(2)
    q_ref, k_ref, v_ref, kb_ref, vb_ref, sg_ref, mcv_ref, tail_ref = take(8)
    wcopy_refs = take(NUM_SEGMENTS) if copy_weights else ()
    hn_ref, carry_ref = take(2)
    i = pl.program_id(0)
    h = pl.program_id(1)
    tm = x_ref.shape[0]

    @pl.when(h == 0)
    def _():
        hn_ref[...] = _rmsnorm_rows(x_ref[...], g_ref[...]).astype(BF16)

    @pl.when(jnp.logical_and(i == 0, h == 0))
    def _():
        carry_ref[...] = jnp.zeros(carry_ref.shape, F32)

    wq, wk, wv, wgs, wb, wc, wu, wgc = [r[...].astype(BF16) for r in w_refs]
    for copy_ref, w in zip(wcopy_refs, (wq, wk, wv, wgs, wb, wc, wu, wgc)):
        copy_ref[...] = w

    def project(*ws):
        acc = jnp.dot(hn_ref[...], jnp.concatenate(ws, axis=1), preferred_element_type=F32)
        return [acc[:, s * HEAD_DIM:(s + 1) * HEAD_DIM] for s in range(len(ws))]

    c, u = project(wc, wu)
    b, g_cv = project(wb, wgc)
    k, v = project(wk, wv)
    q, g_sb = project(wq, wgs)

    q_ref[...] = (q * Q_PRESCALE).astype(BF16)
    k_ref[pl.ds(h, tm, stride=NUM_GROUPS), :] = k
    v_ref[pl.ds(h, tm, stride=NUM_GROUPS), :] = v
    kb_ref[...] = k.astype(BF16)
    vb_ref[...] = v.astype(BF16)
    sg_ref[...] = _silu(g_sb).astype(BF16)

    cu = c * u
    if tiles_per_seq > 1:
        left = jnp.where(i % tiles_per_seq == 0, left_ref[...], carry_ref[h][None, 0:CONV_CTX, :])
        carry_ref[h, 0:CONV_CTX, :] = cu[tm - CONV_CTX:tm, :]
    else:
        left = left_ref[...]
    rows = seq_rows if tiles_per_seq == 1 else tm
    nseq = tm // rows
    expand = lambda a: jnp.broadcast_to(a, (nseq, rows, HEAD_DIM)).reshape(tm, HEAD_DIM)
    l0 = expand(left[:, 0:1, :])
    l1 = expand(left[:, 1:2, :])
    rs = lax.broadcasted_iota(jnp.int32, (tm, HEAD_DIM), 0) & (rows - 1)
    r1 = jnp.where(rs == 0, l1, pltpu.roll(cu, 1, axis=0))
    r2 = jnp.where(rs == 0, l0, jnp.where(rs == 1, l1, pltpu.roll(cu, 2, axis=0)))
    cw = cw_ref[...]
    conv = cw[0:1, :] * r2 + cw[1:2, :] * r1 + cw[2:3, :] * cu
    mcv_ref[...] = (b * conv * _silu(g_cv)).astype(BF16)
    tail_ref[...] = cu.reshape(nseq, rows, HEAD_DIM)[:, rows - CONV_CTX:rows, :]


def _project(x2d, norm_g, weights, conv_w, left, *, seq_rows, tm):
    m, d = x2d.shape
    width = NUM_GROUPS * HEAD_DIM
    copy_weights = not isinstance(weights, (list, tuple))
    if copy_weights:
        assert m == tm, "bf16 weight copies are written once, by a single row tile"
        w_args = [weights] * NUM_SEGMENTS
        segment = lambda s: pl.BlockSpec((d, HEAD_DIM), lambda i, h: (0, s * NUM_GROUPS + h))
    else:
        w_args = list(weights)
        segment = lambda s: pl.BlockSpec((d, HEAD_DIM), lambda i, h: (0, h))
    once = dict(pipeline_mode=pl.Buffered(1)) if m == tm else {}
    if tm >= seq_rows:
        tiles_per_seq, ns = 1, tm // seq_rows
        left_idx = lambda i, h: (i, 0, h)
    else:
        tiles_per_seq, ns = seq_rows // tm, 1
        left_idx = lambda i, h: (i // tiles_per_seq, 0, h)
    n_tails = (m // tm) * ns
    tile = lambda dt: jax.ShapeDtypeStruct((m, width), dt)
    native = jax.ShapeDtypeStruct((m * NUM_GROUPS, HEAD_DIM), F32)
    col_block = pl.BlockSpec((tm, HEAD_DIM), lambda i, h: (i, h))
    native_block = pl.BlockSpec((tm * NUM_GROUPS, HEAD_DIM), lambda i, h: (i, 0), **once)
    copy_specs, copy_shapes = [], []
    if copy_weights:
        copy_specs = [pl.BlockSpec((d, HEAD_DIM), lambda i, h: (0, h))] * NUM_SEGMENTS
        copy_shapes = [jax.ShapeDtypeStruct((d, width), BF16)] * NUM_SEGMENTS
    kern = functools.partial(_proj_kernel, seq_rows=seq_rows, tiles_per_seq=tiles_per_seq,
                             copy_weights=copy_weights)
    q, k, v, kb, vb, sg, mcv, tails, *w_copies = pl.pallas_call(
        kern,
        grid=(m // tm, NUM_GROUPS),
        in_specs=[
            pl.BlockSpec((tm, d), lambda i, h: (i, 0), **once),
            pl.BlockSpec((1, d), lambda i, h: (0, 0)),
            *[segment(s) for s in range(NUM_SEGMENTS)],
            pl.BlockSpec((CONV_W, HEAD_DIM), lambda i, h: (0, h)),
            pl.BlockSpec((ns, CONV_CTX, HEAD_DIM), left_idx),
        ],
        out_specs=[col_block, native_block, native_block, col_block, col_block, col_block,
                   col_block, pl.BlockSpec((ns, CONV_CTX, HEAD_DIM), lambda i, h: (i, 0, h)),
                   *copy_specs],
        out_shape=[tile(BF16), native, native, tile(BF16), tile(BF16), tile(BF16), tile(BF16),
                   jax.ShapeDtypeStruct((n_tails, CONV_CTX, width), F32), *copy_shapes],
        scratch_shapes=[pltpu.VMEM((tm, d), BF16),
                        pltpu.VMEM((NUM_GROUPS, SUBLANES, HEAD_DIM), F32)],
        compiler_params=_compiler_params(("arbitrary", "arbitrary")),
        name="proj",
    )(x2d, norm_g.reshape(1, d), *w_args, conv_w, left)
    tails = tails.reshape(-1, tiles_per_seq, CONV_CTX, width)[:, tiles_per_seq - 1]
    return q, k, v, kb, vb, sg, mcv, tails, w_copies


def _mask_logits(zn, mask):
    return zn if mask is None else jnp.where(mask, zn, MASKED_LOGIT)


def _log2_one_minus_beta(zn):
    softplus2 = jnp.log(1.0 + jnp.exp2(-jnp.abs(zn))) * LOG2E
    return jnp.minimum(zn, 0.0) - softplus2


def _split_bf16(x, axis):
    hi = x.astype(BF16)
    return jnp.concatenate([hi, (x - hi.astype(F32)).astype(BF16)], axis=axis)


def _attn_prompt_kernel(q_ref, k_ref, v_ref, sg_ref, tri_ref, o_ref, acc_ref):
    qi = pl.program_id(2)
    tq = q_ref.shape[0]
    tk = tri_ref.shape[0]
    heads = q_ref.shape[1] // HEAD_DIM
    tri2 = jnp.concatenate([tri_ref[...]] * 2, axis=0)
    hs = [slice(a * HEAD_DIM, (a + 1) * HEAD_DIM) for a in range(heads)]
    nt_dims = (((1,), (1,)), ((), ()))

    def rows_of(j):
        return pl.ds(j * tk if isinstance(j, int) else pl.multiple_of(j * tk, tk), tk)

    def logits(j, mask):
        return tuple(_mask_logits(lax.dot_general(q_ref[:, s], k_ref[rows_of(j), s], nt_dims,
                                                  preferred_element_type=F32), mask) for s in hs)

    def weights(zns):
        pieces = [_split_bf16(_log2_one_minus_beta(zn), axis=1) for zn in zns]
        suffixes = [jnp.dot(p, tri2, preferred_element_type=F32) for p in pieces]
        ws = tuple(jnp.exp2(sfx - zn).astype(BF16) for sfx, zn in zip(suffixes, zns))
        return ws, tuple(sfx[:, 0:1] for sfx in suffixes)

    def accumulate(ws, j, scales):
        for a, s in enumerate(hs):
            acc_ref[:, s] += scales[a] * jnp.dot(ws[a], v_ref[rows_of(j), s],
                                                 preferred_element_type=F32)

    def step(j, carries, mask):
        ws, totals = weights(logits(j, mask))
        accumulate(ws, j, [jnp.exp2(c) for c in carries])
        return tuple(c + tot for c, tot in zip(carries, totals))

    acc_ref[...] = jnp.zeros(acc_ref.shape, F32)
    rows = lax.broadcasted_iota(jnp.int32, (tq, tk), 0)
    cols = lax.broadcasted_iota(jnp.int32, (tq, tk), 1)
    carries = step(qi, (jnp.zeros((tq, 1), F32),) * heads, cols < rows)

    def any_live(carries):
        return jnp.max(functools.reduce(jnp.maximum, carries)) > EXP2_UNDERFLOW

    def body(state):
        t, _, carries = state
        carries = step(qi - 1 - t, carries, None)
        return t + 1, any_live(carries), carries

    lax.while_loop(lambda state: jnp.logical_and(state[0] < qi, state[1]), body,
                   (jnp.int32(0), any_live(carries), carries))
    o_ref[...] = (acc_ref[...] * sg_ref[...].astype(F32)).astype(BF16)


def _attend_prompt(q, kb, vb, sg, tri, *, batch, seq, tq, heads_per_step):
    nq = seq // tq
    assert tq == tri.shape[0], "query block and key block share the diagonal mask"
    cols = heads_per_step * HEAD_DIM
    qblock = pl.BlockSpec((tq, cols), lambda b, g, i: (b * nq + i, g))
    kvblock = pl.BlockSpec((seq, cols), lambda b, g, i: (b, g))
    return pl.pallas_call(
        _attn_prompt_kernel,
        grid=(batch, NUM_GROUPS // heads_per_step, nq),
        in_specs=[qblock, kvblock, kvblock, qblock,
                  pl.BlockSpec(tri.shape, lambda b, g, i: (0, 0))],
        out_specs=qblock,
        out_shape=jax.ShapeDtypeStruct(q.shape, BF16),
        scratch_shapes=[pltpu.VMEM((tq, cols), F32)],
        compiler_params=_compiler_params(("arbitrary", "arbitrary", "arbitrary")),
        name="attn_prompt",
    )(q, kb, vb, sg, tri)


def _attn_sample_kernel(q_ref, kn_ref, vn_ref, kc_hbm, vc_hbm, sg_ref, trit_ref, o_ref,
                        qnt_ref, knear_ref, vnear_ref, kfar_ref, vfar_ref, near_sems, far_sems,
                        *, t):
    step = pl.program_id(0)
    n = q_ref.shape[0] // t
    tk = trit_ref.shape[0]
    block_rows = tk * NUM_GROUPS
    n_cache = kc_hbm.shape[1] // block_rows
    lanes = NUM_GROUPS * t
    pairs = NUM_GROUPS // 2
    pad = HEAD_DIM
    nt_dims = (((1,), (1,)), ((), ()))
    slot = step % 2

    def cache_copies(stream, j, k_dst, v_dst, k_sem, v_sem):
        rows = pl.ds(j * block_rows, block_rows)
        return [pltpu.make_async_copy(kc_hbm.at[stream, rows, :], k_dst, k_sem),
                pltpu.make_async_copy(vc_hbm.at[stream, rows, :], v_dst, v_sem)]

    def near_copies(at_step, s):
        return [c for i in range(n) for c in cache_copies(
            at_step * n + i, n_cache - 1, knear_ref.at[s, i], vnear_ref.at[s, i],
            near_sems.at[0, s, i], near_sems.at[1, s, i])]

    @pl.when(step == 0)
    def _():
        for c in near_copies(0, 0):
            c.start()

    @pl.when(step + 1 < pl.num_programs(0))
    def _():
        for c in near_copies(step + 1, 1 - slot):
            c.start()

    qnt_ref[...] = jnp.zeros(qnt_ref.shape, BF16)
    for i in range(n):
        for h in range(NUM_GROUPS):
            p, half = divmod(h, 2)
            qnt_ref[i, p, h * t:(h + 1) * t, half * HEAD_DIM:(half + 1) * HEAD_DIM] = (
                q_ref[i * t:(i + 1) * t, h * HEAD_DIM:(h + 1) * HEAD_DIM])

    def neg_logits(i, load_k):
        zn = None
        for p in range(pairs):
            lhs = jnp.concatenate([load_k(2 * p), load_k(2 * p + 1)], axis=1).astype(BF16)
            d = lax.dot_general(lhs, qnt_ref[i, p], nt_dims, preferred_element_type=F32)
            zn = d if zn is None else zn + d
        return zn

    def walk(work):
        pieces = [[_split_bf16(_log2_one_minus_beta(zn), axis=0) for zn, _, _ in blocks]
                  for blocks, _, _ in work]
        suffixes = [[jnp.dot(jnp.concatenate([tr, tr], axis=1), p, preferred_element_type=F32)
                     for (_, _, tr), p in zip(blocks, ps)]
                    for (blocks, _, _), ps in zip(work, pieces)]
        done = []
        for (blocks, carry, accs), sfxs in zip(work, suffixes):
            for (zn, load_v, _), sfx in zip(blocks, sfxs):
                w_t = jnp.exp2(sfx - zn + carry).T
                accs = [acc + jnp.dot(w_t[h * t:(h + 1) * t, :].astype(BF16),
                                      load_v(h).astype(BF16), preferred_element_type=F32)
                        for h, acc in enumerate(accs)]
                carry = carry + sfx[0:1, :]
            done.append((carry, tuple(accs)))
        return done

    head_rows = lambda ref, base, rows, h: ref[pl.ds(base + h, rows, stride=NUM_GROUPS), :]

    key_idx = lax.broadcasted_iota(jnp.int32, (pad, lanes), 0)
    query_idx = lax.broadcasted_iota(jnp.int32, (pad, lanes), 1) & (t - 1)
    zero_rows = lambda width: jnp.zeros((pad - t, width), F32)
    new_base = lambda i: i * t * NUM_GROUPS
    new_blocks = [
        (_mask_logits(jnp.concatenate(
            [neg_logits(i, lambda h, i=i: head_rows(kn_ref, new_base(i), t, h)), zero_rows(lanes)],
            axis=0), key_idx < query_idx),
         lambda h, i=i: jnp.concatenate([head_rows(vn_ref, new_base(i), t, h),
                                         zero_rows(HEAD_DIM)], axis=0),
         trit_ref[0:pad, 0:pad]) for i in range(n)]
    for c in near_copies(step, slot):
        c.wait()
    near_blocks = [
        (neg_logits(i, lambda h, i=i: head_rows(knear_ref.at[slot, i], 0, tk, h)),
         lambda h, i=i: head_rows(vnear_ref.at[slot, i], 0, tk, h), trit_ref[...])
        for i in range(n)]
    state = walk([([new_blocks[i], near_blocks[i]], jnp.zeros((1, lanes), F32),
                   [jnp.zeros((t, HEAD_DIM), F32)] * NUM_GROUPS) for i in range(n)])

    def any_live(state):
        return jnp.max(functools.reduce(jnp.maximum, [c for c, _ in state])) > EXP2_UNDERFLOW

    first_far = n_cache - 2
    far_slot = lambda j: (first_far - j) % 2

    def far_copies(j):
        s = far_slot(j)
        return [c for i in range(n) for c in cache_copies(
            step * n + i, j, kfar_ref.at[s, i], vfar_ref.at[s, i],
            far_sems.at[0, s, i], far_sems.at[1, s, i])]

    def start_if(cond, j):
        @pl.when(cond)
        def _():
            for c in far_copies(j):
                c.start()

    def wait_if(cond, j):
        @pl.when(cond)
        def _():
            for c in far_copies(j):
                c.wait()

    def older(loop_state):
        j, _, state = loop_state
        s = far_slot(j)
        for c in far_copies(j):
            c.wait()
        start_if(j >= 1, j - 1)
        far_blocks = [(neg_logits(i, lambda h, i=i: head_rows(kfar_ref.at[s, i], 0, tk, h)),
                       lambda h, i=i: head_rows(vfar_ref.at[s, i], 0, tk, h), trit_ref[...])
                      for i in range(n)]
        state = walk([([far_blocks[i]], state[i][0], list(state[i][1])) for i in range(n)])
        return j - 1, any_live(state), tuple(state)

    live = any_live(state)
    if first_far >= 0:
        start_if(live, first_far)
    j_end, _, state = lax.while_loop(lambda s: jnp.logical_and(s[0] >= 0, s[1]), older,
                                     (jnp.int32(first_far), live, tuple(state)))
    if first_far >= 0:
        wait_if(jnp.logical_and(live, j_end >= 0), j_end)

    for i, (_, accs) in enumerate(state):
        for h, acc in enumerate(accs):
            rows, hs = slice(i * t, (i + 1) * t), slice(h * HEAD_DIM, (h + 1) * HEAD_DIM)
            o_ref[rows, hs] = (acc * sg_ref[rows, hs].astype(F32)).astype(BF16)


def _attend_sample(q, k_new, v_new, cache_k, cache_v, sg, tri_t, *, batch, t):
    width = q.shape[1]
    tk = tri_t.shape[0]
    block_rows = tk * NUM_GROUPS
    assert t & (t - 1) == 0 and NUM_GROUPS * t == tk and cache_k.shape[1] % block_rows == 0
    n = SAMPLE_STREAMS_PER_STEP
    assert batch % n == 0
    rowblock = pl.BlockSpec((n * t, width), lambda b: (b, 0))
    newblock = pl.BlockSpec((n * t * NUM_GROUPS, HEAD_DIM), lambda b: (b, 0))
    in_hbm = pl.BlockSpec(memory_space=pl.ANY)
    key_blocks = lambda *lead: pltpu.VMEM((*lead, n, block_rows, HEAD_DIM), F32)
    return pl.pallas_call(
        functools.partial(_attn_sample_kernel, t=t),
        grid=(batch // n,),
        in_specs=[rowblock, newblock, newblock, in_hbm, in_hbm, rowblock,
                  pl.BlockSpec(tri_t.shape, lambda b: (0, 0))],
        out_specs=rowblock,
        out_shape=jax.ShapeDtypeStruct(q.shape, BF16),
        scratch_shapes=[pltpu.VMEM((n, NUM_GROUPS // 2, NUM_GROUPS * t, 2 * HEAD_DIM), BF16),
                        key_blocks(2), key_blocks(2), key_blocks(2), key_blocks(2),
                        pltpu.SemaphoreType.DMA((2, 2, n)), pltpu.SemaphoreType.DMA((2, 2, n))],
        compiler_params=_compiler_params(("arbitrary",)),
        name="attn_sample",
    )(q, k_new, v_new, cache_k, cache_v, sg, tri_t)


def _out_kernel(x_ref, msb_ref, mcv_ref, wsb_ref, wcv_ref, g_ref, y_ref, *wcopy_refs):
    w_sb, w_cv = wsb_ref[...].astype(BF16), wcv_ref[...].astype(BF16)
    for copy_ref, w in zip(wcopy_refs, (w_sb, w_cv)):
        copy_ref[...] = w
    y = (x_ref[...]
         + jnp.dot(msb_ref[...], w_sb, preferred_element_type=F32)
         + jnp.dot(mcv_ref[...], w_cv, preferred_element_type=F32))
    y_ref[...] = _rmsnorm_rows(y, g_ref[...])


def _merge_out(x2d, mix_sb, mix_cv, weights, final_g, *, tm):
    m, d = x2d.shape
    half = mix_sb.shape[1]
    copy_weights = not isinstance(weights, (list, tuple))
    rows = lambda width: pl.BlockSpec((tm, width), lambda i: (i, 0))
    if copy_weights:
        w_args = [weights, weights]
        w_specs = [pl.BlockSpec((half, d), lambda i, r=r: (r, 0), pipeline_mode=pl.Buffered(1))
                   for r in range(2)]
        copy_specs = [pl.BlockSpec((half, d), lambda i: (0, 0), pipeline_mode=pl.Buffered(1))] * 2
        copy_shapes = [jax.ShapeDtypeStruct((half, d), BF16)] * 2
    else:
        w_args = list(weights)
        w_specs = [pl.BlockSpec((half, d), lambda i: (0, 0))] * 2
        copy_specs, copy_shapes = [], []
    y, *w_copies = pl.pallas_call(
        _out_kernel,
        grid=(m // tm,),
        in_specs=[rows(d), rows(half), rows(half), *w_specs, pl.BlockSpec((1, d), lambda i: (0, 0))],
        out_specs=[rows(d), *copy_specs],
        out_shape=[jax.ShapeDtypeStruct((m, d), F32), *copy_shapes],
        compiler_params=_compiler_params(("arbitrary",)),
        name="merge_out",
    )(x2d, mix_sb, mix_cv, *w_args, final_g.reshape(1, d))
    return y, w_copies


def _lower_tri(n):
    j = lax.broadcasted_iota(jnp.int32, (n, n), 0)
    s = lax.broadcasted_iota(jnp.int32, (n, n), 1)
    return (j >= s).astype(BF16)


def kernel(x_prompt, x_sample, cache_k, cache_v, state_conv, norm_g, w_in, conv_w, w_out, final_g):
    depth = w_in.shape[0]
    assert depth == 1, "single-layer step"
    bsz, seq, d = x_prompt.shape
    dbsz, dseq, _ = x_sample.shape
    past = cache_k.shape[2]
    width = NUM_GROUPS * HEAD_DIM

    tri = _lower_tri(KEY_BLOCK)

    xs = x_sample.reshape(dbsz * dseq, d)
    qs, ks, vs, _, _, sgs, mcvs, tails, w_in_bf = _project(
        xs, norm_g[0], w_in[0], conv_w[0], state_conv[0], seq_rows=dseq,
        tm=min(PROJ_ROWS, xs.shape[0]))
    msbs = _attend_sample(qs, ks, vs,
                          cache_k[0].reshape(dbsz, past * NUM_GROUPS, HEAD_DIM),
                          cache_v[0].reshape(dbsz, past * NUM_GROUPS, HEAD_DIM),
                          sgs, tri.T, batch=dbsz, t=dseq)
    ys, w_out_bf = _merge_out(xs, msbs, mcvs, w_out[0], final_g, tm=OUT_ROWS)

    xp = x_prompt.reshape(bsz * seq, d)
    zeros_left = jnp.zeros((bsz, CONV_W - 1, width), F32)
    qp, kp, vp, kbp, vbp, sgp, mcvp, tailp, _ = _project(
        xp, norm_g[0], w_in_bf, conv_w[0], zeros_left, seq_rows=seq, tm=PROJ_ROWS)
    msbp = _attend_prompt(qp, kbp, vbp, sgp, tri, batch=bsz, seq=seq, tq=KEY_BLOCK,
                          heads_per_step=PROMPT_HEADS_PER_STEP)
    yp, _ = _merge_out(xp, msbp, mcvp, w_out_bf, final_g, tm=OUT_ROWS)

    heads = lambda a, b_, t_: a.reshape(1, b_, t_, NUM_GROUPS, HEAD_DIM)
    return (yp.reshape(bsz, seq, d), ys.reshape(dbsz, dseq, d),
            heads(kp, bsz, seq), heads(vp, bsz, seq), tailp[None],
            heads(ks, dbsz, dseq), heads(vs, dbsz, dseq), tails[None])
```

```python
import functools
import math

import jax
import jax.numpy as jnp
from jax import lax
from jax.experimental import pallas as pl
from jax.experimental.pallas import tpu as pltpu

F32 = jnp.float32
BF16 = jnp.bfloat16

HEAD_DIM = 128
NUM_GROUPS = 8
NUM_SEGMENTS = 8
CONV_W = 3
CONV_CTX = CONV_W - 1
assert CONV_W == 3, "the conv epilogue writes its three taps out explicitly"
SUBLANES = 8
EPS = 1e-6
LOG2E = math.log2(math.e)
Q_PRESCALE = -(HEAD_DIM ** -0.5) * LOG2E
MXU_DEPTH = 256
KEY_BLOCK = MXU_DEPTH
PROJ_ROWS = 1024
OUT_ROWS = 512
PROMPT_HEADS_PER_STEP = 8
SAMPLE_STREAMS_PER_STEP = 4
EXP2_UNDERFLOW = -160.0

VMEM_LIMIT_BYTES = 56 * 1024 * 1024


def _compiler_params(semantics):
    return pltpu.CompilerParams(dimension_semantics=semantics,
                                vmem_limit_bytes=VMEM_LIMIT_BYTES)


def _rmsnorm_rows(x, g):
    r = lax.rsqrt(jnp.mean(x * x, axis=-1, keepdims=True) + EPS)
    return (x * r) * g


def _silu(x):
    return x * (1.0 / (1.0 + jnp.exp(-x)))


def _proj_kernel(*refs, seq_rows, tiles_per_seq, copy_weights):
    refs = list(refs)
    take = lambda count: [refs.pop(0) for _ in range(count)]
    x_ref, g_ref = take(2)
    w_refs = take(NUM_SEGMENTS)
    cw_ref, left_ref = take(2)
    q_ref, k_ref, v_ref, kb_ref, vb_ref, sg_ref, mcv_ref, tail_ref = take(8)
    wcopy_refs = take(NUM_SEGMENTS) if copy_weights else ()
    hn_ref, carry_ref = take(2)
    i = pl.program_id(0)
    h = pl.program_id(1)
    tm = x_ref.shape[0]

    @pl.when(h == 0)
    def _():
        hn_ref[...] = _rmsnorm_rows(x_ref[...], g_ref[...]).astype(BF16)

    @pl.when(jnp.logical_and(i == 0, h == 0))
    def _():
        carry_ref[...] = jnp.zeros(carry_ref.shape, F32)

    wq, wk, wv, wgs, wb, wc, wu, wgc = [r[...].astype(BF16) for r in w_refs]
    for copy_ref, w in zip(wcopy_refs, (wq, wk, wv, wgs, wb, wc, wu, wgc)):
        copy_ref[...] = w

    def project(*ws):
        acc = jnp.dot(hn_ref[...], jnp.concatenate(ws, axis=1), preferred_element_type=F32)
        return [acc[:, s * HEAD_DIM:(s + 1) * HEAD_DIM] for s in range(len(ws))]

    c, u = project(wc, wu)
    b, g_cv = project(wb, wgc)
    k, v = project(wk, wv)
    q, g_sb = project(wq, wgs)

    q_ref[...] = (q * Q_PRESCALE).astype(BF16)
    k_ref[pl.ds(h, tm, stride=NUM_GROUPS), :] = k
    v_ref[pl.ds(h, tm, stride=NUM_GROUPS), :] = v
    kb_ref[...] = k.astype(BF16)
    vb_ref[...] = v.astype(BF16)
    sg_ref[...] = _silu(g_sb).astype(BF16)

    cu = c * u
    if tiles_per_seq > 1:
        left = jnp.where(i % tiles_per_seq == 0, left_ref[...], carry_ref[h][None, 0:CONV_CTX, :])
        carry_ref[h, 0:CONV_CTX, :] = cu[tm - CONV_CTX:tm, :]
    else:
        left = left_ref[...]
    rows = seq_rows if tiles_per_seq == 1 else tm
    nseq = tm // rows
    expand = lambda a: jnp.broadcast_to(a, (nseq, rows, HEAD_DIM)).reshape(tm, HEAD_DIM)
    l0 = expand(left[:, 0:1, :])
    l1 = expand(left[:, 1:2, :])
    rs = lax.broadcasted_iota(jnp.int32, (tm, HEAD_DIM), 0) & (rows - 1)
    r1 = jnp.where(rs == 0, l1, pltpu.roll(cu, 1, axis=0))
    r2 = jnp.where(rs == 0, l0, jnp.where(rs == 1, l1, pltpu.roll(cu, 2, axis=0)))
    cw = cw_ref[...]
    conv = cw[0:1, :] * r2 + cw[1:2, :] * r1 + cw[2:3, :] * cu
    mcv_ref[...] = (b * conv * _silu(g_cv)).astype(BF16)
    tail_ref[...] = cu.reshape(nseq, rows, HEAD_DIM)[:, rows - CONV_CTX:rows, :]


def _project(x2d, norm_g, weights, conv_w, left, *, seq_rows, tm):
    m, d = x2d.shape
    width = NUM_GROUPS * HEAD_DIM
    copy_weights = not isinstance(weights, (list, tuple))
    if copy_weights:
        assert m == tm, "bf16 weight copies are written once, by a single row tile"
        w_args = [weights] * NUM_SEGMENTS
        segment = lambda s: pl.BlockSpec((d, HEAD_DIM), lambda i, h: (0, s * NUM_GROUPS + h))
    else:
        w_args = list(weights)
        segment = lambda s: pl.BlockSpec((d, HEAD_DIM), lambda i, h: (0, h))
    once = dict(pipeline_mode=pl.Buffered(1)) if m == tm else {}
    if tm >= seq_rows:
        tiles_per_seq, ns = 1, tm // seq_rows
        left_idx = lambda i, h: (i, 0, h)
    else:
        tiles_per_seq, ns = seq_rows // tm, 1
        left_idx = lambda i, h: (i // tiles_per_seq, 0, h)
    n_tails = (m // tm) * ns
    tile = lambda dt: jax.ShapeDtypeStruct((m, width), dt)
    native = jax.ShapeDtypeStruct((m * NUM_GROUPS, HEAD_DIM), F32)
    col_block = pl.BlockSpec((tm, HEAD_DIM), lambda i, h: (i, h))
    native_block = pl.BlockSpec((tm * NUM_GROUPS, HEAD_DIM), lambda i, h: (i, 0), **once)
    copy_specs, copy_shapes = [], []
    if copy_weights:
        copy_specs = [pl.BlockSpec((d, HEAD_DIM), lambda i, h: (0, h))] * NUM_SEGMENTS
        copy_shapes = [jax.ShapeDtypeStruct((d, width), BF16)] * NUM_SEGMENTS
    kern = functools.partial(_proj_kernel, seq_rows=seq_rows, tiles_per_seq=tiles_per_seq,
                             copy_weights=copy_weights)
    q, k, v, kb, vb, sg, mcv, tails, *w_copies = pl.pallas_call(
        kern,
        grid=(m // tm, NUM_GROUPS),
        in_specs=[
            pl.BlockSpec((tm, d), lambda i, h: (i, 0), **once),
            pl.BlockSpec((1, d), lambda i, h: (0, 0)),
            *[segment(s) for s in range(NUM_SEGMENTS)],
            pl.BlockSpec((CONV_W, HEAD_DIM), lambda i, h: (0, h)),
            pl.BlockSpec((ns, CONV_CTX, HEAD_DIM), left_idx),
        ],
        out_specs=[col_block, native_block, native_block, col_block, col_block, col_block,
                   col_block, pl.BlockSpec((ns, CONV_CTX, HEAD_DIM), lambda i, h: (i, 0, h)),
                   *copy_specs],
        out_shape=[tile(BF16), native, native, tile(BF16), tile(BF16), tile(BF16), tile(BF16),
                   jax.ShapeDtypeStruct((n_tails, CONV_CTX, width), F32), *copy_shapes],
        scratch_shapes=[pltpu.VMEM((tm, d), BF16),
                        pltpu.VMEM((NUM_GROUPS, SUBLANES, HEAD_DIM), F32)],
        compiler_params=_compiler_params(("arbitrary", "arbitrary")),
        name="proj",
    )(x2d, norm_g.reshape(1, d), *w_args, conv_w, left)
    tails = tails.reshape(-1, tiles_per_seq, CONV_CTX, width)[:, tiles_per_seq - 1]
    return q, k, v, kb, vb, sg, mcv, tails, w_copies


def _log2_one_minus_beta(zn, mask):
    softplus2 = jnp.log(1.0 + jnp.exp2(-jnp.abs(zn))) * LOG2E
    log_1m = jnp.minimum(zn, 0.0) - softplus2
    return log_1m if mask is None else jnp.where(mask, log_1m, 0.0)


def _split_bf16(x, axis):
    hi = x.astype(BF16)
    return jnp.concatenate([hi, (x - hi.astype(F32)).astype(BF16)], axis=axis)


def _block_weights(suffix, zn, mask):
    w = jnp.exp2(suffix - zn)
    return (w if mask is None else jnp.where(mask, w, 0.0)).astype(BF16)


def _attn_prompt_kernel(q_ref, k_ref, v_ref, sg_ref, tri_ref, o_ref, acc_ref):
    qi = pl.program_id(2)
    tq = q_ref.shape[0]
    tk = tri_ref.shape[0]
    heads = q_ref.shape[1] // HEAD_DIM
    tri2 = jnp.concatenate([tri_ref[...]] * 2, axis=0)
    hs = [slice(a * HEAD_DIM, (a + 1) * HEAD_DIM) for a in range(heads)]
    nt_dims = (((1,), (1,)), ((), ()))

    def rows_of(j):
        return pl.ds(j * tk if isinstance(j, int) else pl.multiple_of(j * tk, tk), tk)

    def walk(blocks, carries):
        chains = [(j, m, a) for j, m in blocks for a in range(heads)]
        zns = [lax.dot_general(q_ref[:, hs[a]], k_ref[rows_of(j), hs[a]], nt_dims,
                               preferred_element_type=F32) for j, _, a in chains]
        pieces = [_split_bf16(_log2_one_minus_beta(zn, m), axis=1)
                  for zn, (_, m, _) in zip(zns, chains)]
        suffixes = [jnp.dot(p, tri2, preferred_element_type=F32) for p in pieces]
        pvs = [jnp.dot(_block_weights(sfx, zn, m), v_ref[rows_of(j), hs[a]],
                       preferred_element_type=F32)
               for sfx, zn, (j, m, a) in zip(suffixes, zns, chains)]
        first = carries is None
        carries = [None] * heads if first else list(carries)
        sums = [None] * heads
        for (_, _, a), pv, sfx in zip(chains, pvs, suffixes):
            term = pv if carries[a] is None else jnp.exp2(carries[a]) * pv
            sums[a] = term if sums[a] is None else sums[a] + term
            carries[a] = sfx[:, 0:1] if carries[a] is None else carries[a] + sfx[:, 0:1]
        for a, s in enumerate(hs):
            if first:
                acc_ref[:, s] = sums[a]
            else:
                acc_ref[:, s] += sums[a]
        return tuple(carries)

    rows = lax.broadcasted_iota(jnp.int32, (tq, tk), 0)
    cols = lax.broadcasted_iota(jnp.int32, (tq, tk), 1)
    causal = cols < rows
    carries = lax.cond(qi == 0,
                       lambda: walk([(0, causal)], None),
                       lambda: walk([(qi, causal), (qi - 1, None)], None))
    walked = jnp.minimum(qi, 1)

    def any_live(carries):
        return jnp.max(functools.reduce(jnp.maximum, carries)) > EXP2_UNDERFLOW

    def body(state):
        t, _, carries = state
        carries = walk([(qi - 1 - t, None)], carries)
        return t + 1, any_live(carries), carries

    lax.while_loop(lambda state: jnp.logical_and(state[0] < qi, state[1]), body,
                   (walked, any_live(carries), carries))
    o_ref[...] = (acc_ref[...] * sg_ref[...].astype(F32)).astype(BF16)


def _attend_prompt(q, kb, vb, sg, tri, *, batch, seq, tq, heads_per_step):
    nq = seq // tq
    assert tq == tri.shape[0], "query block and key block share the diagonal mask"
    cols = heads_per_step * HEAD_DIM
    qblock = pl.BlockSpec((tq, cols), lambda b, g, i: (b * nq + i, g))
    kvblock = pl.BlockSpec((seq, cols), lambda b, g, i: (b, g))
    return pl.pallas_call(
        _attn_prompt_kernel,
        grid=(batch, NUM_GROUPS // heads_per_step, nq),
        in_specs=[qblock, kvblock, kvblock, qblock,
                  pl.BlockSpec(tri.shape, lambda b, g, i: (0, 0))],
        out_specs=qblock,
        out_shape=jax.ShapeDtypeStruct(q.shape, BF16),
        scratch_shapes=[pltpu.VMEM((tq, cols), F32)],
        compiler_params=_compiler_params(("arbitrary", "arbitrary", "arbitrary")),
        name="attn_prompt",
    )(q, kb, vb, sg, tri)


def _attn_sample_kernel(q_ref, kn_ref, vn_ref, kc_hbm, vc_hbm, sg_ref, trit_ref, o_ref,
                        qnt_ref, knear_ref, vnear_ref, kfar_ref, vfar_ref, near_sems, far_sems,
                        *, t):
    step = pl.program_id(0)
    n = q_ref.shape[0] // t
    tk = trit_ref.shape[0]
    block_rows = tk * NUM_GROUPS
    n_cache = kc_hbm.shape[1] // block_rows
    lanes = NUM_GROUPS * t
    pairs = NUM_GROUPS // 2
    pad = HEAD_DIM
    nt_dims = (((1,), (1,)), ((), ()))
    slot = step % 2

    def cache_copies(stream, j, k_dst, v_dst, k_sem, v_sem):
        rows = pl.ds(j * block_rows, block_rows)
        return [pltpu.make_async_copy(kc_hbm.at[stream, rows, :], k_dst, k_sem),
                pltpu.make_async_copy(vc_hbm.at[stream, rows, :], v_dst, v_sem)]

    def near_copies(at_step, s):
        return [c for i in range(n) for c in cache_copies(
            at_step * n + i, n_cache - 1, knear_ref.at[s, i], vnear_ref.at[s, i],
            near_sems.at[0, s, i], near_sems.at[1, s, i])]

    @pl.when(step == 0)
    def _():
        for c in near_copies(0, 0):
            c.start()

    @pl.when(step + 1 < pl.num_programs(0))
    def _():
        for c in near_copies(step + 1, 1 - slot):
            c.start()

    qnt_ref[...] = jnp.zeros(qnt_ref.shape, BF16)
    for i in range(n):
        for h in range(NUM_GROUPS):
            p, half = divmod(h, 2)
            qnt_ref[i, p, h * t:(h + 1) * t, half * HEAD_DIM:(half + 1) * HEAD_DIM] = (
                q_ref[i * t:(i + 1) * t, h * HEAD_DIM:(h + 1) * HEAD_DIM])

    def neg_logits(i, load_k):
        zn = None
        for p in range(pairs):
            lhs = jnp.concatenate([load_k(2 * p), load_k(2 * p + 1)], axis=1).astype(BF16)
            d = lax.dot_general(lhs, qnt_ref[i, p], nt_dims, preferred_element_type=F32)
            zn = d if zn is None else zn + d
        return zn

    def walk(work):
        pieces = [[_split_bf16(_log2_one_minus_beta(zn, m), axis=0) for zn, _, _, m in blocks]
                  for blocks, _, _ in work]
        suffixes = [[jnp.dot(jnp.concatenate([tr, tr], axis=1), p, preferred_element_type=F32)
                     for (_, _, tr, _), p in zip(blocks, ps)]
                    for (blocks, _, _), ps in zip(work, pieces)]
        done = []
        for (blocks, carry, accs), sfxs in zip(work, suffixes):
            for (zn, load_v, _, m), sfx in zip(blocks, sfxs):
                w = jnp.exp2(sfx - zn + carry)
                if m is not None:
                    w = jnp.where(m, w, 0.0)
                w_t = w.T
                accs = [acc + jnp.dot(w_t[h * t:(h + 1) * t, :].astype(BF16),
                                      load_v(h).astype(BF16), preferred_element_type=F32)
                        for h, acc in enumerate(accs)]
                carry = carry + sfx[0:1, :]
            done.append((carry, tuple(accs)))
        return done

    head_rows = lambda ref, base, rows, h: ref[pl.ds(base + h, rows, stride=NUM_GROUPS), :]

    key_idx = lax.broadcasted_iota(jnp.int32, (pad, lanes), 0)
    query_idx = lax.broadcasted_iota(jnp.int32, (pad, lanes), 1) & (t - 1)
    zero_rows = lambda width: jnp.zeros((pad - t, width), F32)
    new_base = lambda i: i * t * NUM_GROUPS
    new_blocks = [
        (jnp.concatenate([neg_logits(i, lambda h, i=i: head_rows(kn_ref, new_base(i), t, h)),
                          zero_rows(lanes)], axis=0),
         lambda h, i=i: jnp.concatenate([head_rows(vn_ref, new_base(i), t, h),
                                         zero_rows(HEAD_DIM)], axis=0),
         trit_ref[0:pad, 0:pad], key_idx < query_idx) for i in range(n)]
    for c in near_copies(step, slot):
        c.wait()
    near_blocks = [
        (neg_logits(i, lambda h, i=i: head_rows(knear_ref.at[slot, i], 0, tk, h)),
         lambda h, i=i: head_rows(vnear_ref.at[slot, i], 0, tk, h), trit_ref[...], None)
        for i in range(n)]
    state = walk([([new_blocks[i], near_blocks[i]], jnp.zeros((1, lanes), F32),
                   [jnp.zeros((t, HEAD_DIM), F32)] * NUM_GROUPS) for i in range(n)])

    def any_live(state):
        return jnp.max(functools.reduce(jnp.maximum, [c for c, _ in state])) > EXP2_UNDERFLOW

    first_far = n_cache - 2
    far_slot = lambda j: (first_far - j) % 2

    def far_copies(j):
        s = far_slot(j)
        return [c for i in range(n) for c in cache_copies(
            step * n + i, j, kfar_ref.at[s, i], vfar_ref.at[s, i],
            far_sems.at[0, s, i], far_sems.at[1, s, i])]

    def start_if(cond, j):
        @pl.when(cond)
        def _():
            for c in far_copies(j):
                c.start()

    def wait_if(cond, j):
        @pl.when(cond)
        def _():
            for c in far_copies(j):
                c.wait()

    def older(loop_state):
        j, _, state = loop_state
        s = far_slot(j)
        for c in far_copies(j):
            c.wait()
        start_if(j >= 1, j - 1)
        far_blocks = [(neg_logits(i, lambda h, i=i: head_rows(kfar_ref.at[s, i], 0, tk, h)),
                       lambda h, i=i: head_rows(vfar_ref.at[s, i], 0, tk, h), trit_ref[...], None)
                      for i in range(n)]
        state = walk([([far_blocks[i]], state[i][0], list(state[i][1])) for i in range(n)])
        return j - 1, any_live(state), tuple(state)

    live = any_live(state)
    if first_far >= 0:
        start_if(live, first_far)
    j_end, _, state = lax.while_loop(lambda s: jnp.logical_and(s[0] >= 0, s[1]), older,
                                     (jnp.int32(first_far), live, tuple(state)))
    if first_far >= 0:
        wait_if(jnp.logical_and(live, j_end >= 0), j_end)

    for i, (_, accs) in enumerate(state):
        for h, acc in enumerate(accs):
            rows, hs = slice(i * t, (i + 1) * t), slice(h * HEAD_DIM, (h + 1) * HEAD_DIM)
            o_ref[rows, hs] = (acc * sg_ref[rows, hs].astype(F32)).astype(BF16)


def _attend_sample(q, k_new, v_new, cache_k, cache_v, sg, tri_t, *, batch, t):
    width = q.shape[1]
    tk = tri_t.shape[0]
    block_rows = tk * NUM_GROUPS
    assert t & (t - 1) == 0 and NUM_GROUPS * t == tk and cache_k.shape[1] % block_rows == 0
    n = SAMPLE_STREAMS_PER_STEP
    assert batch % n == 0
    rowblock = pl.BlockSpec((n * t, width), lambda b: (b, 0))
    newblock = pl.BlockSpec((n * t * NUM_GROUPS, HEAD_DIM), lambda b: (b, 0))
    in_hbm = pl.BlockSpec(memory_space=pl.ANY)
    key_blocks = lambda *lead: pltpu.VMEM((*lead, n, block_rows, HEAD_DIM), F32)
    return pl.pallas_call(
        functools.partial(_attn_sample_kernel, t=t),
        grid=(batch // n,),
        in_specs=[rowblock, newblock, newblock, in_hbm, in_hbm, rowblock,
                  pl.BlockSpec(tri_t.shape, lambda b: (0, 0))],
        out_specs=rowblock,
        out_shape=jax.ShapeDtypeStruct(q.shape, BF16),
        scratch_shapes=[pltpu.VMEM((n, NUM_GROUPS // 2, NUM_GROUPS * t, 2 * HEAD_DIM), BF16),
                        key_blocks(2), key_blocks(2), key_blocks(2), key_blocks(2),
                        pltpu.SemaphoreType.DMA((2, 2, n)), pltpu.SemaphoreType.DMA((2, 2, n))],
        compiler_params=_compiler_params(("arbitrary",)),
        name="attn_sample",
    )(q, k_new, v_new, cache_k, cache_v, sg, tri_t)


def _out_kernel(x_ref, msb_ref, mcv_ref, wsb_ref, wcv_ref, g_ref, y_ref, *wcopy_refs):
    w_sb, w_cv = wsb_ref[...].astype(BF16), wcv_ref[...].astype(BF16)
    for copy_ref, w in zip(wcopy_refs, (w_sb, w_cv)):
        copy_ref[...] = w
    y = (x_ref[...]
         + jnp.dot(msb_ref[...], w_sb, preferred_element_type=F32)
         + jnp.dot(mcv_ref[...], w_cv, preferred_element_type=F32))
    y_ref[...] = _rmsnorm_rows(y, g_ref[...])


def _merge_out(x2d, mix_sb, mix_cv, weights, final_g, *, tm):
    m, d = x2d.shape
    half = mix_sb.shape[1]
    copy_weights = not isinstance(weights, (list, tuple))
    rows = lambda width: pl.BlockSpec((tm, width), lambda i: (i, 0))
    if copy_weights:
        w_args = [weights, weights]
        w_specs = [pl.BlockSpec((half, d), lambda i, r=r: (r, 0), pipeline_mode=pl.Buffered(1))
                   for r in range(2)]
        copy_specs = [pl.BlockSpec((half, d), lambda i: (0, 0), pipeline_mode=pl.Buffered(1))] * 2
        copy_shapes = [jax.ShapeDtypeStruct((half, d), BF16)] * 2
    else:
        w_args = list(weights)
        w_specs = [pl.BlockSpec((half, d), lambda i: (0, 0))] * 2
        copy_specs, copy_shapes = [], []
    y, *w_copies = pl.pallas_call(
        _out_kernel,
        grid=(m // tm,),
        in_specs=[rows(d), rows(half), rows(half), *w_specs, pl.BlockSpec((1, d), lambda i: (0, 0))],
        out_specs=[rows(d), *copy_specs],
        out_shape=[jax.ShapeDtypeStruct((m, d), F32), *copy_shapes],
        compiler_params=_compiler_params(("arbitrary",)),
        name="merge_out",
    )(x2d, mix_sb, mix_cv, *w_args, final_g.reshape(1, d))
    return y, w_copies


def _lower_tri(n):
    j = lax.broadcasted_iota(jnp.int32, (n, n), 0)
    s = lax.broadcasted_iota(jnp.int32, (n, n), 1)
    return (j >= s).astype(BF16)


def kernel(x_prompt, x_sample, cache_k, cache_v, state_conv, norm_g, w_in, conv_w, w_out, final_g):
    depth = w_in.shape[0]
    assert depth == 1, "single-layer step"
    bsz, seq, d = x_prompt.shape
    dbsz, dseq, _ = x_sample.shape
    past = cache_k.shape[2]
    width = NUM_GROUPS * HEAD_DIM

    tri = _lower_tri(KEY_BLOCK)

    xs = x_sample.reshape(dbsz * dseq, d)
    qs, ks, vs, _, _, sgs, mcvs, tails, w_in_bf = _project(
        xs, norm_g[0], w_in[0], conv_w[0], state_conv[0], seq_rows=dseq,
        tm=min(PROJ_ROWS, xs.shape[0]))
    msbs = _attend_sample(qs, ks, vs,
                          cache_k[0].reshape(dbsz, past * NUM_GROUPS, HEAD_DIM),
                          cache_v[0].reshape(dbsz, past * NUM_GROUPS, HEAD_DIM),
                          sgs, tri.T, batch=dbsz, t=dseq)
    ys, w_out_bf = _merge_out(xs, msbs, mcvs, w_out[0], final_g, tm=OUT_ROWS)

    xp = x_prompt.reshape(bsz * seq, d)
    zeros_left = jnp.zeros((bsz, CONV_W - 1, width), F32)
    qp, kp, vp, kbp, vbp, sgp, mcvp, tailp, _ = _project(
        xp, norm_g[0], w_in_bf, conv_w[0], zeros_left, seq_rows=seq, tm=PROJ_ROWS)
    msbp = _attend_prompt(qp, kbp, vbp, sgp, tri, batch=bsz, seq=seq, tq=KEY_BLOCK,
                          heads_per_step=PROMPT_HEADS_PER_STEP)
    yp, _ = _merge_out(xp, msbp, mcvp, w_out_bf, final_g, tm=OUT_ROWS)

    heads = lambda a, b_, t_: a.reshape(1, b_, t_, NUM_GROUPS, HEAD_DIM)
    return (yp.reshape(bsz, seq, d), ys.reshape(dbsz, dseq, d),
            heads(kp, bsz, seq), heads(vp, bsz, seq), tailp[None],
            heads(ks, dbsz, dseq), heads(vs, dbsz, dseq), tails[None])
```

```python
import functools
import math

import jax
import jax.numpy as jnp
from jax import lax
from jax.experimental import pallas as pl
from jax.experimental.pallas import tpu as pltpu

F32 = jnp.float32
BF16 = jnp.bfloat16

HEAD_DIM = 128
NUM_GROUPS = 8
NUM_SEGMENTS = 8
CONV_W = 3
CONV_CTX = CONV_W - 1
assert CONV_W == 3, "the conv epilogue writes its three taps out explicitly"
SUBLANES = 8
EPS = 1e-6
LOG2E = math.log2(math.e)
Q_PRESCALE = -(HEAD_DIM ** -0.5) * LOG2E
MXU_DEPTH = 256
KEY_BLOCK = MXU_DEPTH
PROJ_ROWS = 1024
OUT_ROWS = 512
PROMPT_HEADS_PER_STEP = 8
SAMPLE_STREAMS_PER_STEP = 4
EXP2_UNDERFLOW = -160.0

VMEM_LIMIT_BYTES = 60 * 1024 * 1024


def _compiler_params(semantics):
    return pltpu.CompilerParams(dimension_semantics=semantics,
                                vmem_limit_bytes=VMEM_LIMIT_BYTES)


def _rmsnorm_rows(x, g):
    r = lax.rsqrt(jnp.mean(x * x, axis=-1, keepdims=True) + EPS)
    return (x * r) * g


def _silu(x):
    return x * (1.0 / (1.0 + jnp.exp(-x)))


def _proj_kernel(*refs, seq_rows, tiles_per_seq, copy_weights):
    refs = list(refs)
    take = lambda count: [refs.pop(0) for _ in range(count)]
    x_ref, g_ref = take(2)
    w_refs = take(NUM_SEGMENTS)
    cw_ref, left_ref = take(2)
    q_ref, k_ref, v_ref, kb_ref, vb_ref, sg_ref, mcv_ref, tail_ref = take(8)
    wcopy_refs = take(NUM_SEGMENTS) if copy_weights else ()
    hn_ref, carry_ref = take(2)
    i = pl.program_id(0)
    h = pl.program_id(1)
    tm = x_ref.shape[0]

    @pl.when(h == 0)
    def _():
        hn_ref[...] = _rmsnorm_rows(x_ref[...], g_ref[...]).astype(BF16)

    @pl.when(jnp.logical_and(i == 0, h == 0))
    def _():
        carry_ref[...] = jnp.zeros(carry_ref.shape, F32)

    wq, wk, wv, wgs, wb, wc, wu, wgc = [r[...].astype(BF16) for r in w_refs]
    for copy_ref, w in zip(wcopy_refs, (wq, wk, wv, wgs, wb, wc, wu, wgc)):
        copy_ref[...] = w

    def project(*ws):
        acc = jnp.dot(hn_ref[...], jnp.concatenate(ws, axis=1), preferred_element_type=F32)
        return [acc[:, s * HEAD_DIM:(s + 1) * HEAD_DIM] for s in range(len(ws))]

    c, u = project(wc, wu)
    b, g_cv = project(wb, wgc)
    k, v = project(wk, wv)
    q, g_sb = project(wq, wgs)

    q_ref[...] = (q * Q_PRESCALE).astype(BF16)
    k_ref[pl.ds(h, tm, stride=NUM_GROUPS), :] = k
    v_ref[pl.ds(h, tm, stride=NUM_GROUPS), :] = v
    kb_ref[...] = k.astype(BF16)
    vb_ref[...] = v.astype(BF16)
    sg_ref[...] = _silu(g_sb).astype(BF16)

    cu = c * u
    if tiles_per_seq > 1:
        left = jnp.where(i % tiles_per_seq == 0, left_ref[...], carry_ref[h][None, 0:CONV_CTX, :])
        carry_ref[h, 0:CONV_CTX, :] = cu[tm - CONV_CTX:tm, :]
    else:
        left = left_ref[...]
    rows = seq_rows if tiles_per_seq == 1 else tm
    nseq = tm // rows
    expand = lambda a: jnp.broadcast_to(a, (nseq, rows, HEAD_DIM)).reshape(tm, HEAD_DIM)
    l0 = expand(left[:, 0:1, :])
    l1 = expand(left[:, 1:2, :])
    rs = lax.broadcasted_iota(jnp.int32, (tm, HEAD_DIM), 0) & (rows - 1)
    r1 = jnp.where(rs == 0, l1, pltpu.roll(cu, 1, axis=0))
    r2 = jnp.where(rs == 0, l0, jnp.where(rs == 1, l1, pltpu.roll(cu, 2, axis=0)))
    cw = cw_ref[...]
    conv = cw[0:1, :] * r2 + cw[1:2, :] * r1 + cw[2:3, :] * cu
    mcv_ref[...] = (b * conv * _silu(g_cv)).astype(BF16)
    tail_ref[...] = cu.reshape(nseq, rows, HEAD_DIM)[:, rows - CONV_CTX:rows, :]


def _project(x2d, norm_g, weights, conv_w, left, *, seq_rows, tm):
    m, d = x2d.shape
    width = NUM_GROUPS * HEAD_DIM
    copy_weights = not isinstance(weights, (list, tuple))
    if copy_weights:
        assert m == tm, "bf16 weight copies are written once, by a single row tile"
        w_args = [weights] * NUM_SEGMENTS
        segment = lambda s: pl.BlockSpec((d, HEAD_DIM), lambda i, h: (0, s * NUM_GROUPS + h))
    else:
        w_args = list(weights)
        segment = lambda s: pl.BlockSpec((d, HEAD_DIM), lambda i, h: (0, h))
    once = dict(pipeline_mode=pl.Buffered(1)) if m == tm else {}
    if tm >= seq_rows:
        tiles_per_seq, ns = 1, tm // seq_rows
        left_idx = lambda i, h: (i, 0, h)
    else:
        tiles_per_seq, ns = seq_rows // tm, 1
        left_idx = lambda i, h: (i // tiles_per_seq, 0, h)
    n_tails = (m // tm) * ns
    tile = lambda dt: jax.ShapeDtypeStruct((m, width), dt)
    native = jax.ShapeDtypeStruct((m * NUM_GROUPS, HEAD_DIM), F32)
    col_block = pl.BlockSpec((tm, HEAD_DIM), lambda i, h: (i, h))
    native_block = pl.BlockSpec((tm * NUM_GROUPS, HEAD_DIM), lambda i, h: (i, 0), **once)
    copy_specs, copy_shapes = [], []
    if copy_weights:
        copy_specs = [pl.BlockSpec((d, HEAD_DIM), lambda i, h: (0, h))] * NUM_SEGMENTS
        copy_shapes = [jax.ShapeDtypeStruct((d, width), BF16)] * NUM_SEGMENTS
    kern = functools.partial(_proj_kernel, seq_rows=seq_rows, tiles_per_seq=tiles_per_seq,
                             copy_weights=copy_weights)
    q, k, v, kb, vb, sg, mcv, tails, *w_copies = pl.pallas_call(
        kern,
        grid=(m // tm, NUM_GROUPS),
        in_specs=[
            pl.BlockSpec((tm, d), lambda i, h: (i, 0), **once),
            pl.BlockSpec((1, d), lambda i, h: (0, 0)),
            *[segment(s) for s in range(NUM_SEGMENTS)],
            pl.BlockSpec((CONV_W, HEAD_DIM), lambda i, h: (0, h)),
            pl.BlockSpec((ns, CONV_CTX, HEAD_DIM), left_idx),
        ],
        out_specs=[col_block, native_block, native_block, col_block, col_block, col_block,
                   col_block, pl.BlockSpec((ns, CONV_CTX, HEAD_DIM), lambda i, h: (i, 0, h)),
                   *copy_specs],
        out_shape=[tile(BF16), native, native, tile(BF16), tile(BF16), tile(BF16), tile(BF16),
                   jax.ShapeDtypeStruct((n_tails, CONV_CTX, width), F32), *copy_shapes],
        scratch_shapes=[pltpu.VMEM((tm, d), BF16),
                        pltpu.VMEM((NUM_GROUPS, SUBLANES, HEAD_DIM), F32)],
        compiler_params=_compiler_params(("arbitrary", "arbitrary")),
        name="proj",
    )(x2d, norm_g.reshape(1, d), *w_args, conv_w, left)
    tails = tails.reshape(-1, tiles_per_seq, CONV_CTX, width)[:, tiles_per_seq - 1]
    return q, k, v, kb, vb, sg, mcv, tails, w_copies


def _log2_one_minus_beta(zn, mask):
    softplus2 = jnp.log(1.0 + jnp.exp2(-jnp.abs(zn))) * LOG2E
    log_1m = jnp.minimum(zn, 0.0) - softplus2
    return log_1m if mask is None else jnp.where(mask, log_1m, 0.0)


def _split_bf16(x, axis):
    hi = x.astype(BF16)
    return jnp.concatenate([hi, (x - hi.astype(F32)).astype(BF16)], axis=axis)


def _block_weights(suffix, zn, mask):
    w = jnp.exp2(suffix - zn)
    return (w if mask is None else jnp.where(mask, w, 0.0)).astype(BF16)


def _attn_prompt_kernel(q_ref, k_ref, v_ref, sg_ref, tri_ref, o_ref, acc_ref):
    qi = pl.program_id(2)
    tq = q_ref.shape[0]
    tk = tri_ref.shape[0]
    heads = q_ref.shape[1] // HEAD_DIM
    tri2 = jnp.concatenate([tri_ref[...]] * 2, axis=0)
    hs = [slice(a * HEAD_DIM, (a + 1) * HEAD_DIM) for a in range(heads)]
    nt_dims = (((1,), (1,)), ((), ()))

    def rows_of(j):
        return pl.ds(j * tk if isinstance(j, int) else pl.multiple_of(j * tk, tk), tk)

    def logits(j):
        return tuple(lax.dot_general(q_ref[:, s], k_ref[rows_of(j), s], nt_dims,
                                     preferred_element_type=F32) for s in hs)

    def weights(zns, mask):
        pieces = [_split_bf16(_log2_one_minus_beta(zn, mask), axis=1) for zn in zns]
        suffixes = [jnp.dot(p, tri2, preferred_element_type=F32) for p in pieces]
        ws = tuple(_block_weights(sfx, zn, mask) for sfx, zn in zip(suffixes, zns))
        return ws, tuple(sfx[:, 0:1] for sfx in suffixes)

    def accumulate(ws, j, scales):
        for a, s in enumerate(hs):
            acc_ref[:, s] += scales[a] * jnp.dot(ws[a], v_ref[rows_of(j), s],
                                                 preferred_element_type=F32)

    def step(j, carries, mask):
        ws, totals = weights(logits(j), mask)
        accumulate(ws, j, [jnp.exp2(c) for c in carries])
        return tuple(c + tot for c, tot in zip(carries, totals))

    acc_ref[...] = jnp.zeros(acc_ref.shape, F32)
    rows = lax.broadcasted_iota(jnp.int32, (tq, tk), 0)
    cols = lax.broadcasted_iota(jnp.int32, (tq, tk), 1)
    carries = step(qi, (jnp.zeros((tq, 1), F32),) * heads, cols < rows)

    def any_live(carries):
        return jnp.max(functools.reduce(jnp.maximum, carries)) > EXP2_UNDERFLOW

    def body(state):
        t, _, carries = state
        carries = step(qi - 1 - t, carries, None)
        return t + 1, any_live(carries), carries

    lax.while_loop(lambda state: jnp.logical_and(state[0] < qi, state[1]), body,
                   (jnp.int32(0), any_live(carries), carries))
    o_ref[...] = (acc_ref[...] * sg_ref[...].astype(F32)).astype(BF16)


def _attend_prompt(q, kb, vb, sg, tri, *, batch, seq, tq, heads_per_step):
    nq = seq // tq
    assert tq == tri.shape[0], "query block and key block share the diagonal mask"
    cols = heads_per_step * HEAD_DIM
    qblock = pl.BlockSpec((tq, cols), lambda b, g, i: (b * nq + i, g))
    kvblock = pl.BlockSpec((seq, cols), lambda b, g, i: (b, g))
    return pl.pallas_call(
        _attn_prompt_kernel,
        grid=(batch, NUM_GROUPS // heads_per_step, nq),
        in_specs=[qblock, kvblock, kvblock, qblock,
                  pl.BlockSpec(tri.shape, lambda b, g, i: (0, 0))],
        out_specs=qblock,
        out_shape=jax.ShapeDtypeStruct(q.shape, BF16),
        scratch_shapes=[pltpu.VMEM((tq, cols), F32)],
        compiler_params=_compiler_params(("arbitrary", "arbitrary", "arbitrary")),
        name="attn_prompt",
    )(q, kb, vb, sg, tri)


def _attn_sample_kernel(q_ref, kn_ref, vn_ref, kc_hbm, vc_hbm, sg_ref, trit_ref, o_ref,
                        qnt_ref, knear_ref, vnear_ref, kfar_ref, vfar_ref, near_sems, far_sems,
                        *, t):
    step = pl.program_id(0)
    n = q_ref.shape[0] // t
    tk = trit_ref.shape[0]
    block_rows = tk * NUM_GROUPS
    n_cache = kc_hbm.shape[1] // block_rows
    lanes = NUM_GROUPS * t
    pairs = NUM_GROUPS // 2
    pad = HEAD_DIM
    nt_dims = (((1,), (1,)), ((), ()))
    slot = step % 2

    def cache_copies(stream, j, k_dst, v_dst, k_sem, v_sem):
        rows = pl.ds(j * block_rows, block_rows)
        return [pltpu.make_async_copy(kc_hbm.at[stream, rows, :], k_dst, k_sem),
                pltpu.make_async_copy(vc_hbm.at[stream, rows, :], v_dst, v_sem)]

    def near_copies(at_step, s):
        return [c for i in range(n) for c in cache_copies(
            at_step * n + i, n_cache - 1, knear_ref.at[s, i], vnear_ref.at[s, i],
            near_sems.at[0, s, i], near_sems.at[1, s, i])]

    @pl.when(step == 0)
    def _():
        for c in near_copies(0, 0):
            c.start()

    @pl.when(step + 1 < pl.num_programs(0))
    def _():
        for c in near_copies(step + 1, 1 - slot):
            c.start()

    qnt_ref[...] = jnp.zeros(qnt_ref.shape, BF16)
    for i in range(n):
        for h in range(NUM_GROUPS):
            p, half = divmod(h, 2)
            qnt_ref[i, p, h * t:(h + 1) * t, half * HEAD_DIM:(half + 1) * HEAD_DIM] = (
                q_ref[i * t:(i + 1) * t, h * HEAD_DIM:(h + 1) * HEAD_DIM])

    def neg_logits(i, load_k):
        zn = None
        for p in range(pairs):
            lhs = jnp.concatenate([load_k(2 * p), load_k(2 * p + 1)], axis=1).astype(BF16)
            d = lax.dot_general(lhs, qnt_ref[i, p], nt_dims, preferred_element_type=F32)
            zn = d if zn is None else zn + d
        return zn

    def walk(work):
        pieces = [[_split_bf16(_log2_one_minus_beta(zn, m), axis=0) for zn, _, _, m in blocks]
                  for blocks, _, _ in work]
        suffixes = [[jnp.dot(jnp.concatenate([tr, tr], axis=1), p, preferred_element_type=F32)
                     for (_, _, tr, _), p in zip(blocks, ps)]
                    for (blocks, _, _), ps in zip(work, pieces)]
        done = []
        for (blocks, carry, accs), sfxs in zip(work, suffixes):
            for (zn, load_v, _, m), sfx in zip(blocks, sfxs):
                w = jnp.exp2(sfx - zn + carry)
                if m is not None:
                    w = jnp.where(m, w, 0.0)
                w_t = w.T
                accs = [acc + jnp.dot(w_t[h * t:(h + 1) * t, :].astype(BF16),
                                      load_v(h).astype(BF16), preferred_element_type=F32)
                        for h, acc in enumerate(accs)]
                carry = carry + sfx[0:1, :]
            done.append((carry, tuple(accs)))
        return done

    head_rows = lambda ref, base, rows, h: ref[pl.ds(base + h, rows, stride=NUM_GROUPS), :]

    key_idx = lax.broadcasted_iota(jnp.int32, (pad, lanes), 0)
    query_idx = lax.broadcasted_iota(jnp.int32, (pad, lanes), 1) & (t - 1)
    zero_rows = lambda width: jnp.zeros((pad - t, width), F32)
    new_base = lambda i: i * t * NUM_GROUPS
    new_blocks = [
        (jnp.concatenate([neg_logits(i, lambda h, i=i: head_rows(kn_ref, new_base(i), t, h)),
                          zero_rows(lanes)], axis=0),
         lambda h, i=i: jnp.concatenate([head_rows(vn_ref, new_base(i), t, h),
                                         zero_rows(HEAD_DIM)], axis=0),
         trit_ref[0:pad, 0:pad], key_idx < query_idx) for i in range(n)]
    for c in near_copies(step, slot):
        c.wait()
    near_blocks = [
        (neg_logits(i, lambda h, i=i: head_rows(knear_ref.at[slot, i], 0, tk, h)),
         lambda h, i=i: head_rows(vnear_ref.at[slot, i], 0, tk, h), trit_ref[...], None)
        for i in range(n)]
    state = walk([([new_blocks[i], near_blocks[i]], jnp.zeros((1, lanes), F32),
                   [jnp.zeros((t, HEAD_DIM), F32)] * NUM_GROUPS) for i in range(n)])

    def any_live(state):
        return jnp.max(functools.reduce(jnp.maximum, [c for c, _ in state])) > EXP2_UNDERFLOW

    first_far = n_cache - 2
    far_slot = lambda j: (first_far - j) % 2

    def far_copies(j):
        s = far_slot(j)
        return [c for i in range(n) for c in cache_copies(
            step * n + i, j, kfar_ref.at[s, i], vfar_ref.at[s, i],
            far_sems.at[0, s, i], far_sems.at[1, s, i])]

    def start_if(cond, j):
        @pl.when(cond)
        def _():
            for c in far_copies(j):
                c.start()

    def wait_if(cond, j):
        @pl.when(cond)
        def _():
            for c in far_copies(j):
                c.wait()

    def older(loop_state):
        j, _, state = loop_state
        s = far_slot(j)
        for c in far_copies(j):
            c.wait()
        start_if(j >= 1, j - 1)
        far_blocks = [(neg_logits(i, lambda h, i=i: head_rows(kfar_ref.at[s, i], 0, tk, h)),
                       lambda h, i=i: head_rows(vfar_ref.at[s, i], 0, tk, h), trit_ref[...], None)
                      for i in range(n)]
        state = walk([([far_blocks[i]], state[i][0], list(state[i][1])) for i in range(n)])
        return j - 1, any_live(state), tuple(state)

    live = any_live(state)
    if first_far >= 0:
        start_if(live, first_far)
    j_end, _, state = lax.while_loop(lambda s: jnp.logical_and(s[0] >= 0, s[1]), older,
                                     (jnp.int32(first_far), live, tuple(state)))
    if first_far >= 0:
        wait_if(jnp.logical_and(live, j_end >= 0), j_end)

    for i, (_, accs) in enumerate(state):
        for h, acc in enumerate(accs):
            rows, hs = slice(i * t, (i + 1) * t), slice(h * HEAD_DIM, (h + 1) * HEAD_DIM)
            o_ref[rows, hs] = (acc * sg_ref[rows, hs].astype(F32)).astype(BF16)


def _attend_sample(q, k_new, v_new, cache_k, cache_v, sg, tri_t, *, batch, t):
    width = q.shape[1]
    tk = tri_t.shape[0]
    block_rows = tk * NUM_GROUPS
    assert t & (t - 1) == 0 and NUM_GROUPS * t == tk and cache_k.shape[1] % block_rows == 0
    n = SAMPLE_STREAMS_PER_STEP
    assert batch % n == 0
    rowblock = pl.BlockSpec((n * t, width), lambda b: (b, 0))
    newblock = pl.BlockSpec((n * t * NUM_GROUPS, HEAD_DIM), lambda b: (b, 0))
    in_hbm = pl.BlockSpec(memory_space=pl.ANY)
    key_blocks = lambda *lead: pltpu.VMEM((*lead, n, block_rows, HEAD_DIM), F32)
    return pl.pallas_call(
        functools.partial(_attn_sample_kernel, t=t),
        grid=(batch // n,),
        in_specs=[rowblock, newblock, newblock, in_hbm, in_hbm, rowblock,
                  pl.BlockSpec(tri_t.shape, lambda b: (0, 0))],
        out_specs=rowblock,
        out_shape=jax.ShapeDtypeStruct(q.shape, BF16),
        scratch_shapes=[pltpu.VMEM((n, NUM_GROUPS // 2, NUM_GROUPS * t, 2 * HEAD_DIM), BF16),
                        key_blocks(2), key_blocks(2), key_blocks(2), key_blocks(2),
                        pltpu.SemaphoreType.DMA((2, 2, n)), pltpu.SemaphoreType.DMA((2, 2, n))],
        compiler_params=_compiler_params(("arbitrary",)),
        name="attn_sample",
    )(q, k_new, v_new, cache_k, cache_v, sg, tri_t)


def _out_kernel(x_ref, msb_ref, mcv_ref, wsb_ref, wcv_ref, g_ref, y_ref, *wcopy_refs):
    w_sb, w_cv = wsb_ref[...].astype(BF16), wcv_ref[...].astype(BF16)
    for copy_ref, w in zip(wcopy_refs, (w_sb, w_cv)):
        copy_ref[...] = w
    y = (x_ref[...]
         + jnp.dot(msb_ref[...], w_sb, preferred_element_type=F32)
         + jnp.dot(mcv_ref[...], w_cv, preferred_element_type=F32))
    y_ref[...] = _rmsnorm_rows(y, g_ref[...])


def _merge_out(x2d, mix_sb, mix_cv, weights, final_g, *, tm):
    m, d = x2d.shape
    half = mix_sb.shape[1]
    copy_weights = not isinstance(weights, (list, tuple))
    rows = lambda width: pl.BlockSpec((tm, width), lambda i: (i, 0))
    if copy_weights:
        w_args = [weights, weights]
        w_specs = [pl.BlockSpec((half, d), lambda i, r=r: (r, 0), pipeline_mode=pl.Buffered(1))
                   for r in range(2)]
        copy_specs = [pl.BlockSpec((half, d), lambda i: (0, 0), pipeline_mode=pl.Buffered(1))] * 2
        copy_shapes = [jax.ShapeDtypeStruct((half, d), BF16)] * 2
    else:
        w_args = list(weights)
        w_specs = [pl.BlockSpec((half, d), lambda i: (0, 0), pipeline_mode=pl.Buffered(1))] * 2
        copy_specs, copy_shapes = [], []
    y, *w_copies = pl.pallas_call(
        _out_kernel,
        grid=(m // tm,),
        in_specs=[rows(d), rows(half), rows(half), *w_specs, pl.BlockSpec((1, d), lambda i: (0, 0))],
        out_specs=[rows(d), *copy_specs],
        out_shape=[jax.ShapeDtypeStruct((m, d), F32), *copy_shapes],
        compiler_params=_compiler_params(("arbitrary",)),
        name="merge_out",
    )(x2d, mix_sb, mix_cv, *w_args, final_g.reshape(1, d))
    return y, w_copies


def _lower_tri(n):
    j = lax.broadcasted_iota(jnp.int32, (n, n), 0)
    s = lax.broadcasted_iota(jnp.int32, (n, n), 1)
    return (j >= s).astype(BF16)


def kernel(x_prompt, x_sample, cache_k, cache_v, state_conv, norm_g, w_in, conv_w, w_out, final_g):
    depth = w_in.shape[0]
    assert depth == 1, "single-layer step"
    bsz, seq, d = x_prompt.shape
    dbsz, dseq, _ = x_sample.shape
    past = cache_k.shape[2]
    width = NUM_GROUPS * HEAD_DIM

    tri = _lower_tri(KEY_BLOCK)

    xs = x_sample.reshape(dbsz * dseq, d)
    qs, ks, vs, _, _, sgs, mcvs, tails, w_in_bf = _project(
        xs, norm_g[0], w_in[0], conv_w[0], state_conv[0], seq_rows=dseq,
        tm=min(PROJ_ROWS, xs.shape[0]))
    msbs = _attend_sample(qs, ks, vs,
                          cache_k[0].reshape(dbsz, past * NUM_GROUPS, HEAD_DIM),
                          cache_v[0].reshape(dbsz, past * NUM_GROUPS, HEAD_DIM),
                          sgs, tri.T, batch=dbsz, t=dseq)
    ys, w_out_bf = _merge_out(xs, msbs, mcvs, w_out[0], final_g, tm=OUT_ROWS)

    xp = x_prompt.reshape(bsz * seq, d)
    zeros_left = jnp.zeros((bsz, CONV_W - 1, width), F32)
    qp, kp, vp, kbp, vbp, sgp, mcvp, tailp, _ = _project(
        xp, norm_g[0], w_in_bf, conv_w[0], zeros_left, seq_rows=seq, tm=PROJ_ROWS)
    msbp = _attend_prompt(qp, kbp, vbp, sgp, tri, batch=bsz, seq=seq, tq=KEY_BLOCK,
                          heads_per_step=PROMPT_HEADS_PER_STEP)
    yp, _ = _merge_out(xp, msbp, mcvp, w_out_bf, final_g, tm=2 * OUT_ROWS)

    heads = lambda a, b_, t_: a.reshape(1, b_, t_, NUM_GROUPS, HEAD_DIM)
    return (yp.reshape(bsz, seq, d), ys.reshape(dbsz, dseq, d),
            heads(kp, bsz, seq), heads(vp, bsz, seq), tailp[None],
            heads(ks, dbsz, dseq), heads(vs, dbsz, dseq), tails[None])
```

```python
import functools
import math

import jax
import jax.numpy as jnp
from jax import lax
from jax.experimental import pallas as pl
from jax.experimental.pallas import tpu as pltpu

F32 = jnp.float32
BF16 = jnp.bfloat16

HEAD_DIM = 128
NUM_GROUPS = 8
NUM_SEGMENTS = 8
CONV_W = 3
CONV_CTX = CONV_W - 1
assert CONV_W == 3, "the conv epilogue writes its three taps out explicitly"
SUBLANES = 8
EPS = 1e-6
LOG2E = math.log2(math.e)
Q_PRESCALE = -(HEAD_DIM ** -0.5) * LOG2E
MXU_DEPTH = 256
KEY_BLOCK = MXU_DEPTH
PROJ_ROWS = 1024
OUT_ROWS = 512
PROMPT_HEADS_PER_STEP = 8
SAMPLE_STREAMS_PER_STEP = 4
EXP2_UNDERFLOW = -160.0

VMEM_LIMIT_BYTES = 56 * 1024 * 1024


def _compiler_params(semantics):
    return pltpu.CompilerParams(dimension_semantics=semantics,
                                vmem_limit_bytes=VMEM_LIMIT_BYTES)


def _rmsnorm_rows(x, g):
    r = lax.rsqrt(jnp.mean(x * x, axis=-1, keepdims=True) + EPS)
    return (x * r) * g


def _silu(x):
    return x * (1.0 / (1.0 + jnp.exp(-x)))


def _proj_kernel(*refs, seq_rows, tiles_per_seq, copy_weights):
    refs = list(refs)
    take = lambda count: [refs.pop(0) for _ in range(count)]
    x_ref, g_ref = take(2)
    w_refs = take(NUM_SEGMENTS)
    cw_ref, left_ref = take(2)
    q_ref, k_ref, v_ref, kb_ref, vb_ref, sg_ref, mcv_ref, tail_ref = take(8)
    wcopy_refs = take(NUM_SEGMENTS) if copy_weights else ()
    hn_ref, carry_ref = take(2)
    i = pl.program_id(0)
    h = pl.program_id(1)
    tm = x_ref.shape[0]

    @pl.when(h == 0)
    def _():
        hn_ref[...] = _rmsnorm_rows(x_ref[...], g_ref[...]).astype(BF16)

    @pl.when(jnp.logical_and(i == 0, h == 0))
    def _():
        carry_ref[...] = jnp.zeros(carry_ref.shape, F32)

    wq, wk, wv, wgs, wb, wc, wu, wgc = [r[...].astype(BF16) for r in w_refs]
    for copy_ref, w in zip(wcopy_refs, (wq, wk, wv, wgs, wb, wc, wu, wgc)):
        copy_ref[...] = w

    def project(*ws):
        acc = jnp.dot(hn_ref[...], jnp.concatenate(ws, axis=1), preferred_element_type=F32)
        return [acc[:, s * HEAD_DIM:(s + 1) * HEAD_DIM] for s in range(len(ws))]

    c, u = project(wc, wu)
    b, g_cv = project(wb, wgc)
    k, v = project(wk, wv)
    q, g_sb = project(wq, wgs)

    q_ref[...] = (q * Q_PRESCALE).astype(BF16)
    k_ref[pl.ds(h, tm, stride=NUM_GROUPS), :] = k
    v_ref[pl.ds(h, tm, stride=NUM_GROUPS), :] = v
    kb_ref[...] = k.astype(BF16)
    vb_ref[...] = v.astype(BF16)
    sg_ref[...] = _silu(g_sb).astype(BF16)

    cu = c * u
    if tiles_per_seq > 1:
        left = jnp.where(i % tiles_per_seq == 0, left_ref[...], carry_ref[h][None, 0:CONV_CTX, :])
        carry_ref[h, 0:CONV_CTX, :] = cu[tm - CONV_CTX:tm, :]
    else:
        left = left_ref[...]
    rows = seq_rows if tiles_per_seq == 1 else tm
    nseq = tm // rows
    expand = lambda a: jnp.broadcast_to(a, (nseq, rows, HEAD_DIM)).reshape(tm, HEAD_DIM)
    l0 = expand(left[:, 0:1, :])
    l1 = expand(left[:, 1:2, :])
    rs = lax.broadcasted_iota(jnp.int32, (tm, HEAD_DIM), 0) & (rows - 1)
    r1 = jnp.where(rs == 0, l1, pltpu.roll(cu, 1, axis=0))
    r2 = jnp.where(rs == 0, l0, jnp.where(rs == 1, l1, pltpu.roll(cu, 2, axis=0)))
    cw = cw_ref[...]
    conv = cw[0:1, :] * r2 + cw[1:2, :] * r1 + cw[2:3, :] * cu
    mcv_ref[...] = (b * conv * _silu(g_cv)).astype(BF16)
    tail_ref[...] = cu.reshape(nseq, rows, HEAD_DIM)[:, rows - CONV_CTX:rows, :]


def _project(x2d, norm_g, weights, conv_w, left, *, seq_rows, tm):
    m, d = x2d.shape
    width = NUM_GROUPS * HEAD_DIM
    copy_weights = not isinstance(weights, (list, tuple))
    if copy_weights:
        assert m == tm, "bf16 weight copies are written once, by a single row tile"
        w_args = [weights] * NUM_SEGMENTS
        segment = lambda s: pl.BlockSpec((d, HEAD_DIM), lambda i, h: (0, s * NUM_GROUPS + h))
    else:
        w_args = list(weights)
        segment = lambda s: pl.BlockSpec((d, HEAD_DIM), lambda i, h: (0, h))
    once = dict(pipeline_mode=pl.Buffered(1)) if m == tm else {}
    if tm >= seq_rows:
        tiles_per_seq, ns = 1, tm // seq_rows
        left_idx = lambda i, h: (i, 0, h)
    else:
        tiles_per_seq, ns = seq_rows // tm, 1
        left_idx = lambda i, h: (i // tiles_per_seq, 0, h)
    n_tails = (m // tm) * ns
    tile = lambda dt: jax.ShapeDtypeStruct((m, width), dt)
    native = jax.ShapeDtypeStruct((m * NUM_GROUPS, HEAD_DIM), F32)
    col_block = pl.BlockSpec((tm, HEAD_DIM), lambda i, h: (i, h))
    native_block = pl.BlockSpec((tm * NUM_GROUPS, HEAD_DIM), lambda i, h: (i, 0), **once)
    copy_specs, copy_shapes = [], []
    if copy_weights:
        copy_specs = [pl.BlockSpec((d, HEAD_DIM), lambda i, h: (0, h))] * NUM_SEGMENTS
        copy_shapes = [jax.ShapeDtypeStruct((d, width), BF16)] * NUM_SEGMENTS
    kern = functools.partial(_proj_kernel, seq_rows=seq_rows, tiles_per_seq=tiles_per_seq,
                             copy_weights=copy_weights)
    q, k, v, kb, vb, sg, mcv, tails, *w_copies = pl.pallas_call(
        kern,
        grid=(m // tm, NUM_GROUPS),
        in_specs=[
            pl.BlockSpec((tm, d), lambda i, h: (i, 0), **once),
            pl.BlockSpec((1, d), lambda i, h: (0, 0)),
            *[segment(s) for s in range(NUM_SEGMENTS)],
            pl.BlockSpec((CONV_W, HEAD_DIM), lambda i, h: (0, h)),
            pl.BlockSpec((ns, CONV_CTX, HEAD_DIM), left_idx),
        ],
        out_specs=[col_block, native_block, native_block, col_block, col_block, col_block,
                   col_block, pl.BlockSpec((ns, CONV_CTX, HEAD_DIM), lambda i, h: (i, 0, h)),
                   *copy_specs],
        out_shape=[tile(BF16), native, native, tile(BF16), tile(BF16), tile(BF16), tile(BF16),
                   jax.ShapeDtypeStruct((n_tails, CONV_CTX, width), F32), *copy_shapes],
        scratch_shapes=[pltpu.VMEM((tm, d), BF16),
                        pltpu.VMEM((NUM_GROUPS, SUBLANES, HEAD_DIM), F32)],
        compiler_params=_compiler_params(("arbitrary", "arbitrary")),
        name="proj",
    )(x2d, norm_g.reshape(1, d), *w_args, conv_w, left)
    tails = tails.reshape(-1, tiles_per_seq, CONV_CTX, width)[:, tiles_per_seq - 1]
    return q, k, v, kb, vb, sg, mcv, tails, w_copies


def _log2_one_minus_beta(zn, mask):
    softplus2 = jnp.log(1.0 + jnp.exp2(-jnp.abs(zn))) * LOG2E
    log_1m = jnp.minimum(zn, 0.0) - softplus2
    return log_1m if mask is None else jnp.where(mask, log_1m, 0.0)


def _split_bf16(x, axis):
    hi = x.astype(BF16)
    return jnp.concatenate([hi, (x - hi.astype(F32)).astype(BF16)], axis=axis)


def _block_weights(suffix, zn, mask):
    w = jnp.exp2(suffix - zn)
    return (w if mask is None else jnp.where(mask, w, 0.0)).astype(BF16)


def _attn_out_prompt_kernel(q_ref, k_ref, v_ref, sg_ref, tri_ref, x_ref, mcv_ref, wsb_ref, wcv_ref,
                            g_ref, y_ref, acc_ref, mix_ref, *, blocks_per_seq):
    step = pl.program_id(0)
    qi = jnp.minimum(step, pl.num_programs(0) - 2) % blocks_per_seq

    @pl.when(step == 0)
    def _():
        mix_ref[...] = jnp.zeros(mix_ref.shape, BF16)

    tq = q_ref.shape[0]
    tk = tri_ref.shape[0]
    heads = q_ref.shape[1] // HEAD_DIM
    tri2 = jnp.concatenate([tri_ref[...]] * 2, axis=0)
    hs = [slice(a * HEAD_DIM, (a + 1) * HEAD_DIM) for a in range(heads)]
    nt_dims = (((1,), (1,)), ((), ()))

    def rows_of(j):
        return pl.ds(j * tk if isinstance(j, int) else pl.multiple_of(j * tk, tk), tk)

    def logits(j):
        return tuple(lax.dot_general(q_ref[:, s], k_ref[rows_of(j), s], nt_dims,
                                     preferred_element_type=F32) for s in hs)

    def weights(zns, mask):
        pieces = [_split_bf16(_log2_one_minus_beta(zn, mask), axis=1) for zn in zns]
        suffixes = [jnp.dot(p, tri2, preferred_element_type=F32) for p in pieces]
        ws = tuple(_block_weights(sfx, zn, mask) for sfx, zn in zip(suffixes, zns))
        return ws, tuple(sfx[:, 0:1] for sfx in suffixes)

    def accumulate(ws, j, scales):
        for a, s in enumerate(hs):
            acc_ref[:, s] += scales[a] * jnp.dot(ws[a], v_ref[rows_of(j), s],
                                                 preferred_element_type=F32)

    def step(j, carries, mask):
        ws, totals = weights(logits(j), mask)
        accumulate(ws, j, [jnp.exp2(c) for c in carries])
        return tuple(c + tot for c, tot in zip(carries, totals))

    rows = lax.broadcasted_iota(jnp.int32, (tq, tk), 0)
    cols = lax.broadcasted_iota(jnp.int32, (tq, tk), 1)
    causal = cols < rows
    zns = logits(qi)
    out_sb = jnp.dot(mix_ref[...], wsb_ref[...], preferred_element_type=F32)
    pieces = [_split_bf16(_log2_one_minus_beta(zn, causal), axis=1) for zn in zns]
    suffixes = [jnp.dot(p, tri2, preferred_element_type=F32) for p in pieces]
    out_cv = jnp.dot(mcv_ref[...], wcv_ref[...], preferred_element_type=F32)
    ws = [_block_weights(sfx, zn, causal) for sfx, zn in zip(suffixes, zns)]
    for w, s in zip(ws, hs):
        acc_ref[:, s] = jnp.dot(w, v_ref[rows_of(qi), s], preferred_element_type=F32)
    y_ref[...] = _rmsnorm_rows(x_ref[...] + out_sb + out_cv, g_ref[...])
    carries = tuple(sfx[:, 0:1] for sfx in suffixes)

    def any_live(carries):
        return jnp.max(functools.reduce(jnp.maximum, carries)) > EXP2_UNDERFLOW

    def body(state):
        t, _, carries = state
        carries = step(qi - 1 - t, carries, None)
        return t + 1, any_live(carries), carries

    lax.while_loop(lambda state: jnp.logical_and(state[0] < qi, state[1]), body,
                   (jnp.int32(0), any_live(carries), carries))
    mix_ref[...] = (acc_ref[...] * sg_ref[...].astype(F32)).astype(BF16)


def _attend_merge_prompt(q, kb, vb, sg, tri, x2d, mix_cv, w_halves, final_g, *, batch, seq, tq):
    nq = seq // tq
    n_blocks = batch * nq
    m, d = x2d.shape
    width = q.shape[1]
    assert tq == tri.shape[0], "query block and key block share the diagonal mask"
    attended = lambda s: jnp.minimum(s, n_blocks - 1)
    projected = lambda s: jnp.maximum(s - 1, 0)
    qblock = pl.BlockSpec((tq, width), lambda s: (attended(s), 0))
    kvblock = pl.BlockSpec((seq, width), lambda s: (attended(s) // nq, 0))
    const = lambda shape: pl.BlockSpec(shape, lambda s: (0, 0), pipeline_mode=pl.Buffered(1))
    return pl.pallas_call(
        functools.partial(_attn_out_prompt_kernel, blocks_per_seq=nq),
        grid=(n_blocks + 1,),
        in_specs=[qblock, kvblock, kvblock, qblock, const(tri.shape),
                  pl.BlockSpec((tq, d), lambda s: (projected(s), 0)),
                  pl.BlockSpec((tq, width), lambda s: (projected(s), 0)),
                  const(w_halves[0].shape), const(w_halves[1].shape), const((1, d))],
        out_specs=pl.BlockSpec((tq, d), lambda s: (projected(s), 0)),
        out_shape=jax.ShapeDtypeStruct((m, d), F32),
        scratch_shapes=[pltpu.VMEM((tq, width), F32), pltpu.VMEM((tq, width), BF16)],
        compiler_params=_compiler_params(("arbitrary",)),
        name="attn_out_prompt",
    )(q, kb, vb, sg, tri, x2d, mix_cv, w_halves[0], w_halves[1], final_g.reshape(1, d))


def _attn_sample_kernel(q_ref, kn_ref, vn_ref, kc_hbm, vc_hbm, sg_ref, trit_ref, o_ref,
                        qnt_ref, knear_ref, vnear_ref, kfar_ref, vfar_ref, near_sems, far_sems,
                        *, t):
    step = pl.program_id(0)
    n = q_ref.shape[0] // t
    tk = trit_ref.shape[0]
    block_rows = tk * NUM_GROUPS
    n_cache = kc_hbm.shape[1] // block_rows
    lanes = NUM_GROUPS * t
    pairs = NUM_GROUPS // 2
    pad = HEAD_DIM
    nt_dims = (((1,), (1,)), ((), ()))
    slot = step % 2

    def cache_copies(stream, j, k_dst, v_dst, k_sem, v_sem):
        rows = pl.ds(j * block_rows, block_rows)
        return [pltpu.make_async_copy(kc_hbm.at[stream, rows, :], k_dst, k_sem),
                pltpu.make_async_copy(vc_hbm.at[stream, rows, :], v_dst, v_sem)]

    def near_copies(at_step, s):
        return [c for i in range(n) for c in cache_copies(
            at_step * n + i, n_cache - 1, knear_ref.at[s, i], vnear_ref.at[s, i],
            near_sems.at[0, s, i], near_sems.at[1, s, i])]

    @pl.when(step == 0)
    def _():
        for c in near_copies(0, 0):
            c.start()

    @pl.when(step + 1 < pl.num_programs(0))
    def _():
        for c in near_copies(step + 1, 1 - slot):
            c.start()

    qnt_ref[...] = jnp.zeros(qnt_ref.shape, BF16)
    for i in range(n):
        for h in range(NUM_GROUPS):
            p, half = divmod(h, 2)
            qnt_ref[i, p, h * t:(h + 1) * t, half * HEAD_DIM:(half + 1) * HEAD_DIM] = (
                q_ref[i * t:(i + 1) * t, h * HEAD_DIM:(h + 1) * HEAD_DIM])

    def neg_logits(i, load_k):
        zn = None
        for p in range(pairs):
            lhs = jnp.concatenate([load_k(2 * p), load_k(2 * p + 1)], axis=1).astype(BF16)
            d = lax.dot_general(lhs, qnt_ref[i, p], nt_dims, preferred_element_type=F32)
            zn = d if zn is None else zn + d
        return zn

    def walk(work):
        pieces = [[_split_bf16(_log2_one_minus_beta(zn, m), axis=0) for zn, _, _, m in blocks]
                  for blocks, _, _ in work]
        suffixes = [[jnp.dot(jnp.concatenate([tr, tr], axis=1), p, preferred_element_type=F32)
                     for (_, _, tr, _), p in zip(blocks, ps)]
                    for (blocks, _, _), ps in zip(work, pieces)]
        done = []
        for (blocks, carry, accs), sfxs in zip(work, suffixes):
            for (zn, load_v, _, m), sfx in zip(blocks, sfxs):
                w = jnp.exp2(sfx - zn + carry)
                if m is not None:
                    w = jnp.where(m, w, 0.0)
                w_t = w.T
                accs = [acc + jnp.dot(w_t[h * t:(h + 1) * t, :].astype(BF16),
                                      load_v(h).astype(BF16), preferred_element_type=F32)
                        for h, acc in enumerate(accs)]
                carry = carry + sfx[0:1, :]
            done.append((carry, tuple(accs)))
        return done

    head_rows = lambda ref, base, rows, h: ref[pl.ds(base + h, rows, stride=NUM_GROUPS), :]

    key_idx = lax.broadcasted_iota(jnp.int32, (pad, lanes), 0)
    query_idx = lax.broadcasted_iota(jnp.int32, (pad, lanes), 1) & (t - 1)
    zero_rows = lambda width: jnp.zeros((pad - t, width), F32)
    new_base = lambda i: i * t * NUM_GROUPS
    new_blocks = [
        (jnp.concatenate([neg_logits(i, lambda h, i=i: head_rows(kn_ref, new_base(i), t, h)),
                          zero_rows(lanes)], axis=0),
         lambda h, i=i: jnp.concatenate([head_rows(vn_ref, new_base(i), t, h),
                                         zero_rows(HEAD_DIM)], axis=0),
         trit_ref[0:pad, 0:pad], key_idx < query_idx) for i in range(n)]
    for c in near_copies(step, slot):
        c.wait()
    near_blocks = [
        (neg_logits(i, lambda h, i=i: head_rows(knear_ref.at[slot, i], 0, tk, h)),
         lambda h, i=i: head_rows(vnear_ref.at[slot, i], 0, tk, h), trit_ref[...], None)
        for i in range(n)]
    state = walk([([new_blocks[i], near_blocks[i]], jnp.zeros((1, lanes), F32),
                   [jnp.zeros((t, HEAD_DIM), F32)] * NUM_GROUPS) for i in range(n)])

    def any_live(state):
        return jnp.max(functools.reduce(jnp.maximum, [c for c, _ in state])) > EXP2_UNDERFLOW

    first_far = n_cache - 2
    far_slot = lambda j: (first_far - j) % 2

    def far_copies(j):
        s = far_slot(j)
        return [c for i in range(n) for c in cache_copies(
            step * n + i, j, kfar_ref.at[s, i], vfar_ref.at[s, i],
            far_sems.at[0, s, i], far_sems.at[1, s, i])]

    def start_if(cond, j):
        @pl.when(cond)
        def _():
            for c in far_copies(j):
                c.start()

    def wait_if(cond, j):
        @pl.when(cond)
        def _():
            for c in far_copies(j):
                c.wait()

    def older(loop_state):
        j, _, state = loop_state
        s = far_slot(j)
        for c in far_copies(j):
            c.wait()
        start_if(j >= 1, j - 1)
        far_blocks = [(neg_logits(i, lambda h, i=i: head_rows(kfar_ref.at[s, i], 0, tk, h)),
                       lambda h, i=i: head_rows(vfar_ref.at[s, i], 0, tk, h), trit_ref[...], None)
                      for i in range(n)]
        state = walk([([far_blocks[i]], state[i][0], list(state[i][1])) for i in range(n)])
        return j - 1, any_live(state), tuple(state)

    live = any_live(state)
    if first_far >= 0:
        start_if(live, first_far)
    j_end, _, state = lax.while_loop(lambda s: jnp.logical_and(s[0] >= 0, s[1]), older,
                                     (jnp.int32(first_far), live, tuple(state)))
    if first_far >= 0:
        wait_if(jnp.logical_and(live, j_end >= 0), j_end)

    for i, (_, accs) in enumerate(state):
        for h, acc in enumerate(accs):
            rows, hs = slice(i * t, (i + 1) * t), slice(h * HEAD_DIM, (h + 1) * HEAD_DIM)
            o_ref[rows, hs] = (acc * sg_ref[rows, hs].astype(F32)).astype(BF16)


def _attend_sample(q, k_new, v_new, cache_k, cache_v, sg, tri_t, *, batch, t):
    width = q.shape[1]
    tk = tri_t.shape[0]
    block_rows = tk * NUM_GROUPS
    assert t & (t - 1) == 0 and NUM_GROUPS * t == tk and cache_k.shape[1] % block_rows == 0
    n = SAMPLE_STREAMS_PER_STEP
    assert batch % n == 0
    rowblock = pl.BlockSpec((n * t, width), lambda b: (b, 0))
    newblock = pl.BlockSpec((n * t * NUM_GROUPS, HEAD_DIM), lambda b: (b, 0))
    in_hbm = pl.BlockSpec(memory_space=pl.ANY)
    key_blocks = lambda *lead: pltpu.VMEM((*lead, n, block_rows, HEAD_DIM), F32)
    return pl.pallas_call(
        functools.partial(_attn_sample_kernel, t=t),
        grid=(batch // n,),
        in_specs=[rowblock, newblock, newblock, in_hbm, in_hbm, rowblock,
                  pl.BlockSpec(tri_t.shape, lambda b: (0, 0))],
        out_specs=rowblock,
        out_shape=jax.ShapeDtypeStruct(q.shape, BF16),
        scratch_shapes=[pltpu.VMEM((n, NUM_GROUPS // 2, NUM_GROUPS * t, 2 * HEAD_DIM), BF16),
                        key_blocks(2), key_blocks(2), key_blocks(2), key_blocks(2),
                        pltpu.SemaphoreType.DMA((2, 2, n)), pltpu.SemaphoreType.DMA((2, 2, n))],
        compiler_params=_compiler_params(("arbitrary",)),
        name="attn_sample",
    )(q, k_new, v_new, cache_k, cache_v, sg, tri_t)


def _out_kernel(x_ref, msb_ref, mcv_ref, wsb_ref, wcv_ref, g_ref, y_ref, *wcopy_refs):
    w_sb, w_cv = wsb_ref[...].astype(BF16), wcv_ref[...].astype(BF16)
    for copy_ref, w in zip(wcopy_refs, (w_sb, w_cv)):
        copy_ref[...] = w
    y = (x_ref[...]
         + jnp.dot(msb_ref[...], w_sb, preferred_element_type=F32)
         + jnp.dot(mcv_ref[...], w_cv, preferred_element_type=F32))
    y_ref[...] = _rmsnorm_rows(y, g_ref[...])


def _merge_out(x2d, mix_sb, mix_cv, weights, final_g, *, tm):
    m, d = x2d.shape
    half = mix_sb.shape[1]
    copy_weights = not isinstance(weights, (list, tuple))
    rows = lambda width: pl.BlockSpec((tm, width), lambda i: (i, 0))
    if copy_weights:
        w_args = [weights, weights]
        w_specs = [pl.BlockSpec((half, d), lambda i, r=r: (r, 0), pipeline_mode=pl.Buffered(1))
                   for r in range(2)]
        copy_specs = [pl.BlockSpec((half, d), lambda i: (0, 0), pipeline_mode=pl.Buffered(1))] * 2
        copy_shapes = [jax.ShapeDtypeStruct((half, d), BF16)] * 2
    else:
        w_args = list(weights)
        w_specs = [pl.BlockSpec((half, d), lambda i: (0, 0))] * 2
        copy_specs, copy_shapes = [], []
    y, *w_copies = pl.pallas_call(
        _out_kernel,
        grid=(m // tm,),
        in_specs=[rows(d), rows(half), rows(half), *w_specs, pl.BlockSpec((1, d), lambda i: (0, 0))],
        out_specs=[rows(d), *copy_specs],
        out_shape=[jax.ShapeDtypeStruct((m, d), F32), *copy_shapes],
        compiler_params=_compiler_params(("arbitrary",)),
        name="merge_out",
    )(x2d, mix_sb, mix_cv, *w_args, final_g.reshape(1, d))
    return y, w_copies


def _lower_tri(n):
    j = lax.broadcasted_iota(jnp.int32, (n, n), 0)
    s = lax.broadcasted_iota(jnp.int32, (n, n), 1)
    return (j >= s).astype(BF16)


def kernel(x_prompt, x_sample, cache_k, cache_v, state_conv, norm_g, w_in, conv_w, w_out, final_g):
    depth = w_in.shape[0]
    assert depth == 1, "single-layer step"
    bsz, seq, d = x_prompt.shape
    dbsz, dseq, _ = x_sample.shape
    past = cache_k.shape[2]
    width = NUM_GROUPS * HEAD_DIM

    tri = _lower_tri(KEY_BLOCK)

    xs = x_sample.reshape(dbsz * dseq, d)
    qs, ks, vs, _, _, sgs, mcvs, tails, w_in_bf = _project(
        xs, norm_g[0], w_in[0], conv_w[0], state_conv[0], seq_rows=dseq,
        tm=min(PROJ_ROWS, xs.shape[0]))
    msbs = _attend_sample(qs, ks, vs,
                          cache_k[0].reshape(dbsz, past * NUM_GROUPS, HEAD_DIM),
                          cache_v[0].reshape(dbsz, past * NUM_GROUPS, HEAD_DIM),
                          sgs, tri.T, batch=dbsz, t=dseq)
    ys, w_out_bf = _merge_out(xs, msbs, mcvs, w_out[0], final_g, tm=OUT_ROWS)

    xp = x_prompt.reshape(bsz * seq, d)
    zeros_left = jnp.zeros((bsz, CONV_W - 1, width), F32)
    qp, kp, vp, kbp, vbp, sgp, mcvp, tailp, _ = _project(
        xp, norm_g[0], w_in_bf, conv_w[0], zeros_left, seq_rows=seq, tm=PROJ_ROWS)
    yp = _attend_merge_prompt(qp, kbp, vbp, sgp, tri, xp, mcvp, w_out_bf, final_g,
                              batch=bsz, seq=seq, tq=KEY_BLOCK)

    heads = lambda a, b_, t_: a.reshape(1, b_, t_, NUM_GROUPS, HEAD_DIM)
    return (yp.reshape(bsz, seq, d), ys.reshape(dbsz, dseq, d),
            heads(kp, bsz, seq), heads(vp, bsz, seq), tailp[None],
            heads(ks, dbsz, dseq), heads(vs, dbsz, dseq), tails[None])
```

```python
import functools
import math

import jax
import jax.numpy as jnp
from jax import lax
from jax.experimental import pallas as pl
from jax.experimental.pallas import tpu as pltpu

F32 = jnp.float32
BF16 = jnp.bfloat16

HEAD_DIM = 128
NUM_GROUPS = 8
NUM_SEGMENTS = 8
CONV_W = 3
CONV_CTX = CONV_W - 1
assert CONV_W == 3, "the conv epilogue writes its three taps out explicitly"
SUBLANES = 8
EPS = 1e-6
LOG2E = math.log2(math.e)
Q_PRESCALE = -(HEAD_DIM ** -0.5) * LOG2E
MXU_DEPTH = 256
KEY_BLOCK = MXU_DEPTH
PROJ_ROWS = 1024
OUT_ROWS = 512
SAMPLE_STREAMS_PER_STEP = 4
EXP2_UNDERFLOW = -160.0

VMEM_LIMIT_BYTES = 56 * 1024 * 1024


def _compiler_params(semantics):
    return pltpu.CompilerParams(dimension_semantics=semantics,
                                vmem_limit_bytes=VMEM_LIMIT_BYTES)


def _rmsnorm_rows(x, g):
    r = lax.rsqrt(jnp.mean(x * x, axis=-1, keepdims=True) + EPS)
    return (x * r) * g


def _silu(x):
    return x * (1.0 / (1.0 + jnp.exp(-x)))


def _proj_kernel(*refs, seq_rows, tiles_per_seq, copy_weights):
    refs = list(refs)
    take = lambda count: [refs.pop(0) for _ in range(count)]
    x_ref, g_ref = take(2)
    w_refs = take(NUM_SEGMENTS)
    cw_ref, left_ref = take(2)
    q_ref, k_ref, v_ref, kb_ref, vb_ref, sg_ref, mcv_ref, tail_ref = take(8)
    wcopy_refs = take(NUM_SEGMENTS) if copy_weights else ()
    hn_ref, carry_ref = take(2)
    i = pl.program_id(0)
    h = pl.program_id(1)
    tm = x_ref.shape[0]

    @pl.when(h == 0)
    def _():
        hn_ref[...] = _rmsnorm_rows(x_ref[...], g_ref[...]).astype(BF16)

    @pl.when(jnp.logical_and(i == 0, h == 0))
    def _():
        carry_ref[...] = jnp.zeros(carry_ref.shape, F32)

    wq, wk, wv, wgs, wb, wc, wu, wgc = [r[...].astype(BF16) for r in w_refs]
    for copy_ref, w in zip(wcopy_refs, (wq, wk, wv, wgs, wb, wc, wu, wgc)):
        copy_ref[...] = w

    def project(*ws):
        acc = jnp.dot(hn_ref[...], jnp.concatenate(ws, axis=1), preferred_element_type=F32)
        return [acc[:, s * HEAD_DIM:(s + 1) * HEAD_DIM] for s in range(len(ws))]

    c, u = project(wc, wu)
    b, g_cv = project(wb, wgc)
    k, v = project(wk, wv)
    q, g_sb = project(wq, wgs)

    q_ref[...] = (q * Q_PRESCALE).astype(BF16)
    k_ref[pl.ds(h, tm, stride=NUM_GROUPS), :] = k
    v_ref[pl.ds(h, tm, stride=NUM_GROUPS), :] = v
    kb_ref[...] = k.astype(BF16)
    vb_ref[...] = v.astype(BF16)
    sg_ref[...] = _silu(g_sb).astype(BF16)

    cu = c * u
    if tiles_per_seq > 1:
        left = jnp.where(i % tiles_per_seq == 0, left_ref[...], carry_ref[h][None, 0:CONV_CTX, :])
        carry_ref[h, 0:CONV_CTX, :] = cu[tm - CONV_CTX:tm, :]
    else:
        left = left_ref[...]
    rows = seq_rows if tiles_per_seq == 1 else tm
    nseq = tm // rows
    expand = lambda a: jnp.broadcast_to(a, (nseq, rows, HEAD_DIM)).reshape(tm, HEAD_DIM)
    l0 = expand(left[:, 0:1, :])
    l1 = expand(left[:, 1:2, :])
    rs = lax.broadcasted_iota(jnp.int32, (tm, HEAD_DIM), 0) & (rows - 1)
    r1 = jnp.where(rs == 0, l1, pltpu.roll(cu, 1, axis=0))
    r2 = jnp.where(rs == 0, l0, jnp.where(rs == 1, l1, pltpu.roll(cu, 2, axis=0)))
    cw = cw_ref[...]
    conv = cw[0:1, :] * r2 + cw[1:2, :] * r1 + cw[2:3, :] * cu
    mcv_ref[...] = (b * conv * _silu(g_cv)).astype(BF16)
    tail_ref[...] = cu.reshape(nseq, rows, HEAD_DIM)[:, rows - CONV_CTX:rows, :]


def _project(x2d, norm_g, weights, conv_w, left, *, seq_rows, tm):
    m, d = x2d.shape
    width = NUM_GROUPS * HEAD_DIM
    copy_weights = not isinstance(weights, (list, tuple))
    if copy_weights:
        assert m == tm, "bf16 weight copies are written once, by a single row tile"
        w_args = [weights] * NUM_SEGMENTS
        segment = lambda s: pl.BlockSpec((d, HEAD_DIM), lambda i, h: (0, s * NUM_GROUPS + h))
    else:
        w_args = list(weights)
        segment = lambda s: pl.BlockSpec((d, HEAD_DIM), lambda i, h: (0, h))
    once = dict(pipeline_mode=pl.Buffered(1)) if m == tm else {}
    if tm >= seq_rows:
        tiles_per_seq, ns = 1, tm // seq_rows
        left_idx = lambda i, h: (i, 0, h)
    else:
        tiles_per_seq, ns = seq_rows // tm, 1
        left_idx = lambda i, h: (i // tiles_per_seq, 0, h)
    n_tails = (m // tm) * ns
    tile = lambda dt: jax.ShapeDtypeStruct((m, width), dt)
    native = jax.ShapeDtypeStruct((m * NUM_GROUPS, HEAD_DIM), F32)
    col_block = pl.BlockSpec((tm, HEAD_DIM), lambda i, h: (i, h))
    native_block = pl.BlockSpec((tm * NUM_GROUPS, HEAD_DIM), lambda i, h: (i, 0), **once)
    copy_specs, copy_shapes = [], []
    if copy_weights:
        copy_specs = [pl.BlockSpec((d, HEAD_DIM), lambda i, h: (0, h))] * NUM_SEGMENTS
        copy_shapes = [jax.ShapeDtypeStruct((d, width), BF16)] * NUM_SEGMENTS
    kern = functools.partial(_proj_kernel, seq_rows=seq_rows, tiles_per_seq=tiles_per_seq,
                             copy_weights=copy_weights)
    q, k, v, kb, vb, sg, mcv, tails, *w_copies = pl.pallas_call(
        kern,
        grid=(m // tm, NUM_GROUPS),
        in_specs=[
            pl.BlockSpec((tm, d), lambda i, h: (i, 0), **once),
            pl.BlockSpec((1, d), lambda i, h: (0, 0)),
            *[segment(s) for s in range(NUM_SEGMENTS)],
            pl.BlockSpec((CONV_W, HEAD_DIM), lambda i, h: (0, h)),
            pl.BlockSpec((ns, CONV_CTX, HEAD_DIM), left_idx),
        ],
        out_specs=[col_block, native_block, native_block, col_block, col_block, col_block,
                   col_block, pl.BlockSpec((ns, CONV_CTX, HEAD_DIM), lambda i, h: (i, 0, h)),
                   *copy_specs],
        out_shape=[tile(BF16), native, native, tile(BF16), tile(BF16), tile(BF16), tile(BF16),
                   jax.ShapeDtypeStruct((n_tails, CONV_CTX, width), F32), *copy_shapes],
        scratch_shapes=[pltpu.VMEM((tm, d), BF16),
                        pltpu.VMEM((NUM_GROUPS, SUBLANES, HEAD_DIM), F32)],
        compiler_params=_compiler_params(("arbitrary", "arbitrary")),
        name="proj",
    )(x2d, norm_g.reshape(1, d), *w_args, conv_w, left)
    tails = tails.reshape(-1, tiles_per_seq, CONV_CTX, width)[:, tiles_per_seq - 1]
    return q, k, v, kb, vb, sg, mcv, tails, w_copies


def _log2_one_minus_beta(zn, mask):
    softplus2 = jnp.log(1.0 + jnp.exp2(-jnp.abs(zn))) * LOG2E
    log_1m = jnp.minimum(zn, 0.0) - softplus2
    return log_1m if mask is None else jnp.where(mask, log_1m, 0.0)


def _split_bf16(x, axis):
    hi = x.astype(BF16)
    return jnp.concatenate([hi, (x - hi.astype(F32)).astype(BF16)], axis=axis)


def _block_weights(suffix, zn, mask):
    w = jnp.exp2(suffix - zn)
    return (w if mask is None else jnp.where(mask, w, 0.0)).astype(BF16)


def _attn_out_prompt_kernel(q_ref, k_ref, v_ref, sg_ref, tri_ref, x_ref, mcv_ref, wsb_ref, wcv_ref,
                            g_ref, y_ref, acc_ref, mix_ref, *, blocks_per_seq):
    step = pl.program_id(0)
    qi = jnp.minimum(step, pl.num_programs(0) - 2) % blocks_per_seq

    @pl.when(step == 0)
    def _():
        mix_ref[...] = jnp.zeros(mix_ref.shape, BF16)

    tq = q_ref.shape[0]
    tk = tri_ref.shape[0]
    heads = q_ref.shape[1] // HEAD_DIM
    tri2 = jnp.concatenate([tri_ref[...]] * 2, axis=0)
    hs = [slice(a * HEAD_DIM, (a + 1) * HEAD_DIM) for a in range(heads)]
    nt_dims = (((1,), (1,)), ((), ()))

    def rows_of(j):
        return pl.ds(j * tk if isinstance(j, int) else pl.multiple_of(j * tk, tk), tk)

    def logits(j):
        return tuple(lax.dot_general(q_ref[:, s], k_ref[rows_of(j), s], nt_dims,
                                     preferred_element_type=F32) for s in hs)

    def weights(zns, mask):
        pieces = [_split_bf16(_log2_one_minus_beta(zn, mask), axis=1) for zn in zns]
        suffixes = [jnp.dot(p, tri2, preferred_element_type=F32) for p in pieces]
        ws = tuple(_block_weights(sfx, zn, mask) for sfx, zn in zip(suffixes, zns))
        return ws, tuple(sfx[:, 0:1] for sfx in suffixes)

    def accumulate(ws, j, scales):
        for a, s in enumerate(hs):
            acc_ref[:, s] += scales[a] * jnp.dot(ws[a], v_ref[rows_of(j), s],
                                                 preferred_element_type=F32)

    def step(j, carries, mask):
        ws, totals = weights(logits(j), mask)
        accumulate(ws, j, [jnp.exp2(c) for c in carries])
        return tuple(c + tot for c, tot in zip(carries, totals))

    rows = lax.broadcasted_iota(jnp.int32, (tq, tk), 0)
    cols = lax.broadcasted_iota(jnp.int32, (tq, tk), 1)
    causal = cols < rows
    zns = logits(qi)
    out_sb = jnp.dot(mix_ref[...], wsb_ref[...], preferred_element_type=F32)
    pieces = [_split_bf16(_log2_one_minus_beta(zn, causal), axis=1) for zn in zns]
    suffixes = [jnp.dot(p, tri2, preferred_element_type=F32) for p in pieces]
    ws = [_block_weights(sfx, zn, causal) for sfx, zn in zip(suffixes, zns)]
    for w, s in zip(ws, hs):
        acc_ref[:, s] = jnp.dot(w, v_ref[rows_of(qi), s], preferred_element_type=F32)
    carries = tuple(sfx[:, 0:1] for sfx in suffixes)

    def project(out_cv):
        y_ref[...] = _rmsnorm_rows(x_ref[...] + out_sb + out_cv, g_ref[...])

    def with_previous_block():
        zns = logits(qi - 1)
        out_cv = jnp.dot(mcv_ref[...], wcv_ref[...], preferred_element_type=F32)
        ws, totals = weights(zns, None)
        accumulate(ws, qi - 1, [jnp.exp2(c) for c in carries])
        project(out_cv)
        return tuple(c + tot for c, tot in zip(carries, totals))

    def without_previous_block():
        project(jnp.dot(mcv_ref[...], wcv_ref[...], preferred_element_type=F32))
        return carries

    carries = lax.cond(qi >= 1, with_previous_block, without_previous_block)

    def any_live(carries):
        return jnp.max(functools.reduce(jnp.maximum, carries)) > EXP2_UNDERFLOW

    def body(state):
        t, _, carries = state
        carries = step(qi - 1 - t, carries, None)
        return t + 1, any_live(carries), carries

    lax.while_loop(lambda state: jnp.logical_and(state[0] < qi, state[1]), body,
                   (jnp.minimum(qi, 1), any_live(carries), carries))
    mix_ref[...] = (acc_ref[...] * sg_ref[...].astype(F32)).astype(BF16)


def _attend_merge_prompt(q, kb, vb, sg, tri, x2d, mix_cv, w_halves, final_g, *, batch, seq, tq):
    nq = seq // tq
    n_blocks = batch * nq
    m, d = x2d.shape
    width = q.shape[1]
    assert tq == tri.shape[0], "query block and key block share the diagonal mask"
    attended = lambda s: jnp.minimum(s, n_blocks - 1)
    projected = lambda s: jnp.maximum(s - 1, 0)
    qblock = pl.BlockSpec((tq, width), lambda s: (attended(s), 0))
    kvblock = pl.BlockSpec((seq, width), lambda s: (attended(s) // nq, 0))
    const = lambda shape: pl.BlockSpec(shape, lambda s: (0, 0), pipeline_mode=pl.Buffered(1))
    return pl.pallas_call(
        functools.partial(_attn_out_prompt_kernel, blocks_per_seq=nq),
        grid=(n_blocks + 1,),
        in_specs=[qblock, kvblock, kvblock, qblock, const(tri.shape),
                  pl.BlockSpec((tq, d), lambda s: (projected(s), 0)),
                  pl.BlockSpec((tq, width), lambda s: (projected(s), 0)),
                  const(w_halves[0].shape), const(w_halves[1].shape), const((1, d))],
        out_specs=pl.BlockSpec((tq, d), lambda s: (projected(s), 0)),
        out_shape=jax.ShapeDtypeStruct((m, d), F32),
        scratch_shapes=[pltpu.VMEM((tq, width), F32), pltpu.VMEM((tq, width), BF16)],
        compiler_params=_compiler_params(("arbitrary",)),
        name="attn_out_prompt",
    )(q, kb, vb, sg, tri, x2d, mix_cv, w_halves[0], w_halves[1], final_g.reshape(1, d))


def _attn_sample_kernel(q_ref, kn_ref, vn_ref, kc_hbm, vc_hbm, sg_ref, trit_ref, o_ref,
                        qnt_ref, knear_ref, vnear_ref, kfar_ref, vfar_ref, near_sems, far_sems,
                        *, t):
    step = pl.program_id(0)
    n = q_ref.shape[0] // t
    tk = trit_ref.shape[0]
    block_rows = tk * NUM_GROUPS
    n_cache = kc_hbm.shape[1] // block_rows
    lanes = NUM_GROUPS * t
    pairs = NUM_GROUPS // 2
    pad = HEAD_DIM
    nt_dims = (((1,), (1,)), ((), ()))
    slot = step % 2

    def cache_copies(stream, j, k_dst, v_dst, k_sem, v_sem):
        rows = pl.ds(j * block_rows, block_rows)
        return [pltpu.make_async_copy(kc_hbm.at[stream, rows, :], k_dst, k_sem),
                pltpu.make_async_copy(vc_hbm.at[stream, rows, :], v_dst, v_sem)]

    def near_copies(at_step, s):
        return [c for i in range(n) for c in cache_copies(
            at_step * n + i, n_cache - 1, knear_ref.at[s, i], vnear_ref.at[s, i],
            near_sems.at[0, s, i], near_sems.at[1, s, i])]

    @pl.when(step == 0)
    def _():
        for c in near_copies(0, 0):
            c.start()

    @pl.when(step + 1 < pl.num_programs(0))
    def _():
        for c in near_copies(step + 1, 1 - slot):
            c.start()

    qnt_ref[...] = jnp.zeros(qnt_ref.shape, BF16)
    for i in range(n):
        for h in range(NUM_GROUPS):
            p, half = divmod(h, 2)
            qnt_ref[i, p, h * t:(h + 1) * t, half * HEAD_DIM:(half + 1) * HEAD_DIM] = (
                q_ref[i * t:(i + 1) * t, h * HEAD_DIM:(h + 1) * HEAD_DIM])

    def neg_logits(i, load_k):
        zn = None
        for p in range(pairs):
            lhs = jnp.concatenate([load_k(2 * p), load_k(2 * p + 1)], axis=1).astype(BF16)
            d = lax.dot_general(lhs, qnt_ref[i, p], nt_dims, preferred_element_type=F32)
            zn = d if zn is None else zn + d
        return zn

    def walk(work):
        pieces = [[_split_bf16(_log2_one_minus_beta(zn, m), axis=0) for zn, _, _, m in blocks]
                  for blocks, _, _ in work]
        suffixes = [[jnp.dot(jnp.concatenate([tr, tr], axis=1), p, preferred_element_type=F32)
                     for (_, _, tr, _), p in zip(blocks, ps)]
                    for (blocks, _, _), ps in zip(work, pieces)]
        done = []
        for (blocks, carry, accs), sfxs in zip(work, suffixes):
            for (zn, load_v, _, m), sfx in zip(blocks, sfxs):
                w = jnp.exp2(sfx - zn + carry)
                if m is not None:
                    w = jnp.where(m, w, 0.0)
                w_t = w.T
                accs = [acc + jnp.dot(w_t[h * t:(h + 1) * t, :].astype(BF16),
                                      load_v(h).astype(BF16), preferred_element_type=F32)
                        for h, acc in enumerate(accs)]
                carry = carry + sfx[0:1, :]
            done.append((carry, tuple(accs)))
        return done

    head_rows = lambda ref, base, rows, h: ref[pl.ds(base + h, rows, stride=NUM_GROUPS), :]

    key_idx = lax.broadcasted_iota(jnp.int32, (pad, lanes), 0)
    query_idx = lax.broadcasted_iota(jnp.int32, (pad, lanes), 1) & (t - 1)
    zero_rows = lambda width: jnp.zeros((pad - t, width), F32)
    new_base = lambda i: i * t * NUM_GROUPS
    new_blocks = [
        (jnp.concatenate([neg_logits(i, lambda h, i=i: head_rows(kn_ref, new_base(i), t, h)),
                          zero_rows(lanes)], axis=0),
         lambda h, i=i: jnp.concatenate([head_rows(vn_ref, new_base(i), t, h),
                                         zero_rows(HEAD_DIM)], axis=0),
         trit_ref[0:pad, 0:pad], key_idx < query_idx) for i in range(n)]
    for c in near_copies(step, slot):
        c.wait()
    near_blocks = [
        (neg_logits(i, lambda h, i=i: head_rows(knear_ref.at[slot, i], 0, tk, h)),
         lambda h, i=i: head_rows(vnear_ref.at[slot, i], 0, tk, h), trit_ref[...], None)
        for i in range(n)]
    state = walk([([new_blocks[i], near_blocks[i]], jnp.zeros((1, lanes), F32),
                   [jnp.zeros((t, HEAD_DIM), F32)] * NUM_GROUPS) for i in range(n)])

    def any_live(state):
        return jnp.max(functools.reduce(jnp.maximum, [c for c, _ in state])) > EXP2_UNDERFLOW

    first_far = n_cache - 2
    far_slot = lambda j: (first_far - j) % 2

    def far_copies(j):
        s = far_slot(j)
        return [c for i in range(n) for c in cache_copies(
            step * n + i, j, kfar_ref.at[s, i], vfar_ref.at[s, i],
            far_sems.at[0, s, i], far_sems.at[1, s, i])]

    def start_if(cond, j):
        @pl.when(cond)
        def _():
            for c in far_copies(j):
                c.start()

    def wait_if(cond, j):
        @pl.when(cond)
        def _():
            for c in far_copies(j):
                c.wait()

    def older(loop_state):
        j, _, state = loop_state
        s = far_slot(j)
        for c in far_copies(j):
            c.wait()
        start_if(j >= 1, j - 1)
        far_blocks = [(neg_logits(i, lambda h, i=i: head_rows(kfar_ref.at[s, i], 0, tk, h)),
                       lambda h, i=i: head_rows(vfar_ref.at[s, i], 0, tk, h), trit_ref[...], None)
                      for i in range(n)]
        state = walk([([far_blocks[i]], state[i][0], list(state[i][1])) for i in range(n)])
        return j - 1, any_live(state), tuple(state)

    live = any_live(state)
    if first_far >= 0:
        start_if(live, first_far)
    j_end, _, state = lax.while_loop(lambda s: jnp.logical_and(s[0] >= 0, s[1]), older,
                                     (jnp.int32(first_far), live, tuple(state)))
    if first_far >= 0:
        wait_if(jnp.logical_and(live, j_end >= 0), j_end)

    for i, (_, accs) in enumerate(state):
        for h, acc in enumerate(accs):
            rows, hs = slice(i * t, (i + 1) * t), slice(h * HEAD_DIM, (h + 1) * HEAD_DIM)
            o_ref[rows, hs] = (acc * sg_ref[rows, hs].astype(F32)).astype(BF16)


def _attend_sample(q, k_new, v_new, cache_k, cache_v, sg, tri_t, *, batch, t):
    width = q.shape[1]
    tk = tri_t.shape[0]
    block_rows = tk * NUM_GROUPS
    assert t & (t - 1) == 0 and NUM_GROUPS * t == tk and cache_k.shape[1] % block_rows == 0
    n = SAMPLE_STREAMS_PER_STEP
    assert batch % n == 0
    rowblock = pl.BlockSpec((n * t, width), lambda b: (b, 0))
    newblock = pl.BlockSpec((n * t * NUM_GROUPS, HEAD_DIM), lambda b: (b, 0))
    in_hbm = pl.BlockSpec(memory_space=pl.ANY)
    key_blocks = lambda *lead: pltpu.VMEM((*lead, n, block_rows, HEAD_DIM), F32)
    return pl.pallas_call(
        functools.partial(_attn_sample_kernel, t=t),
        grid=(batch // n,),
        in_specs=[rowblock, newblock, newblock, in_hbm, in_hbm, rowblock,
                  pl.BlockSpec(tri_t.shape, lambda b: (0, 0))],
        out_specs=rowblock,
        out_shape=jax.ShapeDtypeStruct(q.shape, BF16),
        scratch_shapes=[pltpu.VMEM((n, NUM_GROUPS // 2, NUM_GROUPS * t, 2 * HEAD_DIM), BF16),
                        key_blocks(2), key_blocks(2), key_blocks(2), key_blocks(2),
                        pltpu.SemaphoreType.DMA((2, 2, n)), pltpu.SemaphoreType.DMA((2, 2, n))],
        compiler_params=_compiler_params(("arbitrary",)),
        name="attn_sample",
    )(q, k_new, v_new, cache_k, cache_v, sg, tri_t)


def _out_kernel(x_ref, msb_ref, mcv_ref, wsb_ref, wcv_ref, g_ref, y_ref, *wcopy_refs):
    w_sb, w_cv = wsb_ref[...].astype(BF16), wcv_ref[...].astype(BF16)
    for copy_ref, w in zip(wcopy_refs, (w_sb, w_cv)):
        copy_ref[...] = w
    y = (x_ref[...]
         + jnp.dot(msb_ref[...], w_sb, preferred_element_type=F32)
         + jnp.dot(mcv_ref[...], w_cv, preferred_element_type=F32))
    y_ref[...] = _rmsnorm_rows(y, g_ref[...])


def _merge_out(x2d, mix_sb, mix_cv, weights, final_g, *, tm):
    m, d = x2d.shape
    half = mix_sb.shape[1]
    copy_weights = not isinstance(weights, (list, tuple))
    rows = lambda width: pl.BlockSpec((tm, width), lambda i: (i, 0))
    if copy_weights:
        w_args = [weights, weights]
        w_specs = [pl.BlockSpec((half, d), lambda i, r=r: (r, 0), pipeline_mode=pl.Buffered(1))
                   for r in range(2)]
        copy_specs = [pl.BlockSpec((half, d), lambda i: (0, 0), pipeline_mode=pl.Buffered(1))] * 2
        copy_shapes = [jax.ShapeDtypeStruct((half, d), BF16)] * 2
    else:
        w_args = list(weights)
        w_specs = [pl.BlockSpec((half, d), lambda i: (0, 0))] * 2
        copy_specs, copy_shapes = [], []
    y, *w_copies = pl.pallas_call(
        _out_kernel,
        grid=(m // tm,),
        in_specs=[rows(d), rows(half), rows(half), *w_specs, pl.BlockSpec((1, d), lambda i: (0, 0))],
        out_specs=[rows(d), *copy_specs],
        out_shape=[jax.ShapeDtypeStruct((m, d), F32), *copy_shapes],
        compiler_params=_compiler_params(("arbitrary",)),
        name="merge_out",
    )(x2d, mix_sb, mix_cv, *w_args, final_g.reshape(1, d))
    return y, w_copies


def _lower_tri(n):
    j = lax.broadcasted_iota(jnp.int32, (n, n), 0)
    s = lax.broadcasted_iota(jnp.int32, (n, n), 1)
    return (j >= s).astype(BF16)


def kernel(x_prompt, x_sample, cache_k, cache_v, state_conv, norm_g, w_in, conv_w, w_out, final_g):
    depth = w_in.shape[0]
    assert depth == 1, "single-layer step"
    bsz, seq, d = x_prompt.shape
    dbsz, dseq, _ = x_sample.shape
    past = cache_k.shape[2]
    width = NUM_GROUPS * HEAD_DIM

    tri = _lower_tri(KEY_BLOCK)

    xs = x_sample.reshape(dbsz * dseq, d)
    qs, ks, vs, _, _, sgs, mcvs, tails, w_in_bf = _project(
        xs, norm_g[0], w_in[0], conv_w[0], state_conv[0], seq_rows=dseq,
        tm=min(PROJ_ROWS, xs.shape[0]))
    msbs = _attend_sample(qs, ks, vs,
                          cache_k[0].reshape(dbsz, past * NUM_GROUPS, HEAD_DIM),
                          cache_v[0].reshape(dbsz, past * NUM_GROUPS, HEAD_DIM),
                          sgs, tri.T, batch=dbsz, t=dseq)
    ys, w_out_bf = _merge_out(xs, msbs, mcvs, w_out[0], final_g, tm=OUT_ROWS)

    xp = x_prompt.reshape(bsz * seq, d)
    zeros_left = jnp.zeros((bsz, CONV_W - 1, width), F32)
    qp, kp, vp, kbp, vbp, sgp, mcvp, tailp, _ = _project(
        xp, norm_g[0], w_in_bf, conv_w[0], zeros_left, seq_rows=seq, tm=PROJ_ROWS)
    yp = _attend_merge_prompt(qp, kbp, vbp, sgp, tri, xp, mcvp, w_out_bf, final_g,
                              batch=bsz, seq=seq, tq=KEY_BLOCK)

    heads = lambda a, b_, t_: a.reshape(1, b_, t_, NUM_GROUPS, HEAD_DIM)
    return (yp.reshape(bsz, seq, d), ys.reshape(dbsz, dseq, d),
            heads(kp, bsz, seq), heads(vp, bsz, seq), tailp[None],
            heads(ks, dbsz, dseq), heads(vs, dbsz, dseq), tails[None])
```

```python
import functools
import math

import jax
import jax.numpy as jnp
from jax import lax
from jax.experimental import pallas as pl
from jax.experimental.pallas import tpu as pltpu

F32 = jnp.float32
BF16 = jnp.bfloat16

HEAD_DIM = 128
NUM_GROUPS = 8
NUM_SEGMENTS = 8
CONV_W = 3
CONV_CTX = CONV_W - 1
assert CONV_W == 3, "the conv epilogue writes its three taps out explicitly"
SUBLANES = 8
EPS = 1e-6
LOG2E = math.log2(math.e)
Q_PRESCALE = -(HEAD_DIM ** -0.5) * LOG2E
MXU_DEPTH = 256
KEY_BLOCK = MXU_DEPTH
PROJ_ROWS = 1024
OUT_ROWS = 512
PROMPT_PROJ_HEADS = 2
SAMPLE_STREAMS_PER_STEP = 4
EXP2_UNDERFLOW = -160.0

VMEM_LIMIT_BYTES = 56 * 1024 * 1024


def _compiler_params(semantics):
    return pltpu.CompilerParams(dimension_semantics=semantics,
                                vmem_limit_bytes=VMEM_LIMIT_BYTES)


def _rmsnorm_rows(x, g):
    r = lax.rsqrt(jnp.mean(x * x, axis=-1, keepdims=True) + EPS)
    return (x * r) * g


def _silu(x):
    return x * (1.0 / (1.0 + jnp.exp(-x)))


def _proj_kernel(*refs, seq_rows, tiles_per_seq, copy_weights):
    refs = list(refs)
    take = lambda count: [refs.pop(0) for _ in range(count)]
    x_hbm, g_ref = take(2)
    w_refs = take(NUM_SEGMENTS)
    cw_ref, left_ref = take(2)
    q_ref, k_ref, v_ref, kb_ref, vb_ref, sg_ref, mcv_ref, tail_ref = take(8)
    wcopy_refs = take(NUM_SEGMENTS) if copy_weights else ()
    x_ref, hn_ref, carry_ref, x_sem = take(4)
    i = pl.program_id(0)
    h = pl.program_id(1)
    tm = x_ref.shape[0]

    def x_copy(tile):
        return pltpu.make_async_copy(x_hbm.at[pl.ds(pl.multiple_of(tile * tm, tm), tm), :],
                                     x_ref, x_sem.at[0])

    @pl.when(jnp.logical_and(i == 0, h == 0))
    def _():
        x_copy(0).start()
        carry_ref[...] = jnp.zeros(carry_ref.shape, F32)

    @pl.when(h == 0)
    def _():
        x_copy(i).wait()
        hn_ref[...] = _rmsnorm_rows(x_ref[...], g_ref[...]).astype(BF16)

    @pl.when(jnp.logical_and(h == 0, i + 1 < pl.num_programs(0)))
    def _():
        x_copy(i + 1).start()

    w_blocks = [r[...].astype(BF16) for r in w_refs]
    for copy_ref, w in zip(wcopy_refs, w_blocks):
        copy_ref[...] = w

    def project(*ws):
        acc = jnp.dot(hn_ref[...], jnp.concatenate(ws, axis=1), preferred_element_type=F32)
        return [acc[:, s * HEAD_DIM:(s + 1) * HEAD_DIM] for s in range(len(ws))]

    rows = seq_rows if tiles_per_seq == 1 else tm
    nseq = tm // rows
    expand = lambda a: jnp.broadcast_to(a, (nseq, rows, HEAD_DIM)).reshape(tm, HEAD_DIM)
    rs = lax.broadcasted_iota(jnp.int32, (tm, HEAD_DIM), 0) & (rows - 1)

    for a in range(q_ref.shape[1] // HEAD_DIM):
        head = h * (q_ref.shape[1] // HEAD_DIM) + a
        cols = slice(a * HEAD_DIM, (a + 1) * HEAD_DIM)
        wq, wk, wv, wgs, wb, wc, wu, wgc = [w[:, cols] for w in w_blocks]

        c, u = project(wc, wu)
        b, g_cv = project(wb, wgc)
        k, v = project(wk, wv)
        q, g_sb = project(wq, wgs)

        q_ref[:, cols] = (q * Q_PRESCALE).astype(BF16)
        k_ref[pl.ds(head, tm, stride=NUM_GROUPS), :] = k
        v_ref[pl.ds(head, tm, stride=NUM_GROUPS), :] = v
        kb_ref[:, cols] = k.astype(BF16)
        vb_ref[:, cols] = v.astype(BF16)
        sg_ref[:, cols] = _silu(g_sb).astype(BF16)

        cu = c * u
        if tiles_per_seq > 1:
            left = jnp.where(i % tiles_per_seq == 0, left_ref[:, :, cols],
                             carry_ref[head][None, 0:CONV_CTX, :])
            carry_ref[head, 0:CONV_CTX, :] = cu[tm - CONV_CTX:tm, :]
        else:
            left = left_ref[:, :, cols]
        l0 = expand(left[:, 0:1, :])
        l1 = expand(left[:, 1:2, :])
        r1 = jnp.where(rs == 0, l1, pltpu.roll(cu, 1, axis=0))
        r2 = jnp.where(rs == 0, l0, jnp.where(rs == 1, l1, pltpu.roll(cu, 2, axis=0)))
        cw = cw_ref[:, cols]
        conv = cw[0:1, :] * r2 + cw[1:2, :] * r1 + cw[2:3, :] * cu
        mcv_ref[:, cols] = (b * conv * _silu(g_cv)).astype(BF16)
        tail_ref[:, :, cols] = cu.reshape(nseq, rows, HEAD_DIM)[:, rows - CONV_CTX:rows, :]


def _project(x2d, norm_g, weights, conv_w, left, *, seq_rows, tm, heads_per_step):
    m, d = x2d.shape
    width = NUM_GROUPS * HEAD_DIM
    cols = heads_per_step * HEAD_DIM
    steps = NUM_GROUPS // heads_per_step
    copy_weights = not isinstance(weights, (list, tuple))
    if copy_weights:
        assert m == tm, "bf16 weight copies are written once, by a single row tile"
        w_args = [weights] * NUM_SEGMENTS
        segment = lambda s: pl.BlockSpec((d, cols), lambda i, h: (0, s * steps + h))
    else:
        w_args = list(weights)
        segment = lambda s: pl.BlockSpec((d, cols), lambda i, h: (0, h))
    once = dict(pipeline_mode=pl.Buffered(1)) if m == tm else {}
    if tm >= seq_rows:
        tiles_per_seq, ns = 1, tm // seq_rows
        left_idx = lambda i, h: (i, 0, h)
    else:
        tiles_per_seq, ns = seq_rows // tm, 1
        left_idx = lambda i, h: (i // tiles_per_seq, 0, h)
    n_tails = (m // tm) * ns
    tile = lambda dt: jax.ShapeDtypeStruct((m, width), dt)
    native = jax.ShapeDtypeStruct((m * NUM_GROUPS, HEAD_DIM), F32)
    col_block = pl.BlockSpec((tm, cols), lambda i, h: (i, h))
    native_block = pl.BlockSpec((tm * NUM_GROUPS, HEAD_DIM), lambda i, h: (i, 0), **once)
    copy_specs, copy_shapes = [], []
    if copy_weights:
        copy_specs = [pl.BlockSpec((d, cols), lambda i, h: (0, h))] * NUM_SEGMENTS
        copy_shapes = [jax.ShapeDtypeStruct((d, width), BF16)] * NUM_SEGMENTS
    kern = functools.partial(_proj_kernel, seq_rows=seq_rows, tiles_per_seq=tiles_per_seq,
                             copy_weights=copy_weights)
    q, k, v, kb, vb, sg, mcv, tails, *w_copies = pl.pallas_call(
        kern,
        grid=(m // tm, steps),
        in_specs=[
            pl.BlockSpec(memory_space=pl.ANY),
            pl.BlockSpec((1, d), lambda i, h: (0, 0)),
            *[segment(s) for s in range(NUM_SEGMENTS)],
            pl.BlockSpec((CONV_W, cols), lambda i, h: (0, h)),
            pl.BlockSpec((ns, CONV_CTX, cols), left_idx),
        ],
        out_specs=[col_block, native_block, native_block, col_block, col_block, col_block,
                   col_block, pl.BlockSpec((ns, CONV_CTX, cols), lambda i, h: (i, 0, h)),
                   *copy_specs],
        out_shape=[tile(BF16), native, native, tile(BF16), tile(BF16), tile(BF16), tile(BF16),
                   jax.ShapeDtypeStruct((n_tails, CONV_CTX, width), F32), *copy_shapes],
        scratch_shapes=[pltpu.VMEM((tm, d), F32), pltpu.VMEM((tm, d), BF16),
                        pltpu.VMEM((NUM_GROUPS, SUBLANES, HEAD_DIM), F32),
                        pltpu.SemaphoreType.DMA((1,))],
        compiler_params=_compiler_params(("arbitrary", "arbitrary")),
        name="proj",
    )(x2d, norm_g.reshape(1, d), *w_args, conv_w, left)
    tails = tails.reshape(-1, tiles_per_seq, CONV_CTX, width)[:, tiles_per_seq - 1]
    return q, k, v, kb, vb, sg, mcv, tails, w_copies


def _log2_one_minus_beta(zn, mask):
    softplus2 = jnp.log(1.0 + jnp.exp2(-jnp.abs(zn))) * LOG2E
    log_1m = jnp.minimum(zn, 0.0) - softplus2
    return log_1m if mask is None else jnp.where(mask, log_1m, 0.0)


def _split_bf16(x, axis):
    hi = x.astype(BF16)
    return jnp.concatenate([hi, (x - hi.astype(F32)).astype(BF16)], axis=axis)


def _block_weights(suffix, zn, mask):
    w = jnp.exp2(suffix - zn)
    return (w if mask is None else jnp.where(mask, w, 0.0)).astype(BF16)


def _attn_out_prompt_kernel(q_ref, k_ref, v_ref, sg_ref, tri_ref, x_ref, mcv_ref, wsb_ref, wcv_ref,
                            g_ref, y_ref, acc_ref, mix_ref, *, blocks_per_seq):
    step = pl.program_id(0)
    qi = jnp.minimum(step, pl.num_programs(0) - 2) % blocks_per_seq

    @pl.when(step == 0)
    def _():
        mix_ref[...] = jnp.zeros(mix_ref.shape, BF16)

    tq = q_ref.shape[0]
    tk = tri_ref.shape[0]
    heads = q_ref.shape[1] // HEAD_DIM
    tri2 = jnp.concatenate([tri_ref[...]] * 2, axis=0)
    hs = [slice(a * HEAD_DIM, (a + 1) * HEAD_DIM) for a in range(heads)]
    nt_dims = (((1,), (1,)), ((), ()))

    def rows_of(j):
        return pl.ds(j * tk if isinstance(j, int) else pl.multiple_of(j * tk, tk), tk)

    def logits(j):
        return tuple(lax.dot_general(q_ref[:, s], k_ref[rows_of(j), s], nt_dims,
                                     preferred_element_type=F32) for s in hs)

    def weights(zns, mask):
        pieces = [_split_bf16(_log2_one_minus_beta(zn, mask), axis=1) for zn in zns]
        suffixes = [jnp.dot(p, tri2, preferred_element_type=F32) for p in pieces]
        ws = tuple(_block_weights(sfx, zn, mask) for sfx, zn in zip(suffixes, zns))
        return ws, tuple(sfx[:, 0:1] for sfx in suffixes)

    def accumulate(ws, j, scales):
        for a, s in enumerate(hs):
            acc_ref[:, s] += scales[a] * jnp.dot(ws[a], v_ref[rows_of(j), s],
                                                 preferred_element_type=F32)

    def step(j, carries, mask):
        ws, totals = weights(logits(j), mask)
        accumulate(ws, j, [jnp.exp2(c) for c in carries])
        return tuple(c + tot for c, tot in zip(carries, totals))

    rows = lax.broadcasted_iota(jnp.int32, (tq, tk), 0)
    cols = lax.broadcasted_iota(jnp.int32, (tq, tk), 1)
    causal = cols < rows
    zns = logits(qi)
    out_sb = jnp.dot(mix_ref[...], wsb_ref[...], preferred_element_type=F32)
    pieces = [_split_bf16(_log2_one_minus_beta(zn, causal), axis=1) for zn in zns]
    suffixes = [jnp.dot(p, tri2, preferred_element_type=F32) for p in pieces]
    out_cv = jnp.dot(mcv_ref[...], wcv_ref[...], preferred_element_type=F32)
    ws = [_block_weights(sfx, zn, causal) for sfx, zn in zip(suffixes, zns)]
    for w, s in zip(ws, hs):
        acc_ref[:, s] = jnp.dot(w, v_ref[rows_of(qi), s], preferred_element_type=F32)
    y_ref[...] = _rmsnorm_rows(x_ref[...] + out_sb + out_cv, g_ref[...])
    carries = tuple(sfx[:, 0:1] for sfx in suffixes)

    def any_live(carries):
        return jnp.max(functools.reduce(jnp.maximum, carries)) > EXP2_UNDERFLOW

    def body(state):
        t, _, carries = state
        carries = step(qi - 1 - t, carries, None)
        return t + 1, any_live(carries), carries

    lax.while_loop(lambda state: jnp.logical_and(state[0] < qi, state[1]), body,
                   (jnp.int32(0), any_live(carries), carries))
    mix_ref[...] = (acc_ref[...] * sg_ref[...].astype(F32)).astype(BF16)


def _attend_merge_prompt(q, kb, vb, sg, tri, x2d, mix_cv, w_halves, final_g, *, batch, seq, tq):
    nq = seq // tq
    n_blocks = batch * nq
    m, d = x2d.shape
    width = q.shape[1]
    assert tq == tri.shape[0], "query block and key block share the diagonal mask"
    attended = lambda s: jnp.minimum(s, n_blocks - 1)
    projected = lambda s: jnp.maximum(s - 1, 0)
    qblock = pl.BlockSpec((tq, width), lambda s: (attended(s), 0))
    kvblock = pl.BlockSpec((seq, width), lambda s: (attended(s) // nq, 0))
    const = lambda shape: pl.BlockSpec(shape, lambda s: (0, 0), pipeline_mode=pl.Buffered(1))
    return pl.pallas_call(
        functools.partial(_attn_out_prompt_kernel, blocks_per_seq=nq),
        grid=(n_blocks + 1,),
        in_specs=[qblock, kvblock, kvblock, qblock, const(tri.shape),
                  pl.BlockSpec((tq, d), lambda s: (projected(s), 0)),
                  pl.BlockSpec((tq, width), lambda s: (projected(s), 0)),
                  const(w_halves[0].shape), const(w_halves[1].shape), const((1, d))],
        out_specs=pl.BlockSpec((tq, d), lambda s: (projected(s), 0)),
        out_shape=jax.ShapeDtypeStruct((m, d), F32),
        scratch_shapes=[pltpu.VMEM((tq, width), F32), pltpu.VMEM((tq, width), BF16)],
        compiler_params=_compiler_params(("arbitrary",)),
        name="attn_out_prompt",
    )(q, kb, vb, sg, tri, x2d, mix_cv, w_halves[0], w_halves[1], final_g.reshape(1, d))


def _attn_sample_kernel(q_ref, kn_ref, vn_ref, kc_hbm, vc_hbm, sg_ref, trit_ref, o_ref,
                        qnt_ref, knear_ref, vnear_ref, kfar_ref, vfar_ref, near_sems, far_sems,
                        *, t):
    step = pl.program_id(0)
    n = q_ref.shape[0] // t
    tk = trit_ref.shape[0]
    block_rows = tk * NUM_GROUPS
    n_cache = kc_hbm.shape[1] // block_rows
    lanes = NUM_GROUPS * t
    pairs = NUM_GROUPS // 2
    pad = HEAD_DIM
    nt_dims = (((1,), (1,)), ((), ()))
    slot = step % 2

    def cache_copies(stream, j, k_dst, v_dst, k_sem, v_sem):
        rows = pl.ds(j * block_rows, block_rows)
        return [pltpu.make_async_copy(kc_hbm.at[stream, rows, :], k_dst, k_sem),
                pltpu.make_async_copy(vc_hbm.at[stream, rows, :], v_dst, v_sem)]

    def near_copies(at_step, s):
        return [c for i in range(n) for c in cache_copies(
            at_step * n + i, n_cache - 1, knear_ref.at[s, i], vnear_ref.at[s, i],
            near_sems.at[0, s, i], near_sems.at[1, s, i])]

    @pl.when(step == 0)
    def _():
        for c in near_copies(0, 0):
            c.start()

    @pl.when(step + 1 < pl.num_programs(0))
    def _():
        for c in near_copies(step + 1, 1 - slot):
            c.start()

    qnt_ref[...] = jnp.zeros(qnt_ref.shape, BF16)
    for i in range(n):
        for h in range(NUM_GROUPS):
            p, half = divmod(h, 2)
            qnt_ref[i, p, h * t:(h + 1) * t, half * HEAD_DIM:(half + 1) * HEAD_DIM] = (
                q_ref[i * t:(i + 1) * t, h * HEAD_DIM:(h + 1) * HEAD_DIM])

    def neg_logits(i, load_k):
        zn = None
        for p in range(pairs):
            lhs = jnp.concatenate([load_k(2 * p), load_k(2 * p + 1)], axis=1).astype(BF16)
            d = lax.dot_general(lhs, qnt_ref[i, p], nt_dims, preferred_element_type=F32)
            zn = d if zn is None else zn + d
        return zn

    def walk(work):
        pieces = [[_split_bf16(_log2_one_minus_beta(zn, m), axis=0) for zn, _, _, m in blocks]
                  for blocks, _, _ in work]
        suffixes = [[jnp.dot(jnp.concatenate([tr, tr], axis=1), p, preferred_element_type=F32)
                     for (_, _, tr, _), p in zip(blocks, ps)]
                    for (blocks, _, _), ps in zip(work, pieces)]
        done = []
        for (blocks, carry, accs), sfxs in zip(work, suffixes):
            for (zn, load_v, _, m), sfx in zip(blocks, sfxs):
                w = jnp.exp2(sfx - zn + carry)
                if m is not None:
                    w = jnp.where(m, w, 0.0)
                w_t = w.T
                accs = [acc + jnp.dot(w_t[h * t:(h + 1) * t, :].astype(BF16),
                                      load_v(h).astype(BF16), preferred_element_type=F32)
                        for h, acc in enumerate(accs)]
                carry = carry + sfx[0:1, :]
            done.append((carry, tuple(accs)))
        return done

    head_rows = lambda ref, base, rows, h: ref[pl.ds(base + h, rows, stride=NUM_GROUPS), :]

    key_idx = lax.broadcasted_iota(jnp.int32, (pad, lanes), 0)
    query_idx = lax.broadcasted_iota(jnp.int32, (pad, lanes), 1) & (t - 1)
    zero_rows = lambda width: jnp.zeros((pad - t, width), F32)
    new_base = lambda i: i * t * NUM_GROUPS
    new_blocks = [
        (jnp.concatenate([neg_logits(i, lambda h, i=i: head_rows(kn_ref, new_base(i), t, h)),
                          zero_rows(lanes)], axis=0),
         lambda h, i=i: jnp.concatenate([head_rows(vn_ref, new_base(i), t, h),
                                         zero_rows(HEAD_DIM)], axis=0),
         trit_ref[0:pad, 0:pad], key_idx < query_idx) for i in range(n)]
    for c in near_copies(step, slot):
        c.wait()
    near_blocks = [
        (neg_logits(i, lambda h, i=i: head_rows(knear_ref.at[slot, i], 0, tk, h)),
         lambda h, i=i: head_rows(vnear_ref.at[slot, i], 0, tk, h), trit_ref[...], None)
        for i in range(n)]
    state = walk([([new_blocks[i], near_blocks[i]], jnp.zeros((1, lanes), F32),
                   [jnp.zeros((t, HEAD_DIM), F32)] * NUM_GROUPS) for i in range(n)])

    def any_live(state):
        return jnp.max(functools.reduce(jnp.maximum, [c for c, _ in state])) > EXP2_UNDERFLOW

    first_far = n_cache - 2
    far_slot = lambda j: (first_far - j) % 2

    def far_copies(j):
        s = far_slot(j)
        return [c for i in range(n) for c in cache_copies(
            step * n + i, j, kfar_ref.at[s, i], vfar_ref.at[s, i],
            far_sems.at[0, s, i], far_sems.at[1, s, i])]

    def start_if(cond, j):
        @pl.when(cond)
        def _():
            for c in far_copies(j):
                c.start()

    def wait_if(cond, j):
        @pl.when(cond)
        def _():
            for c in far_copies(j):
                c.wait()

    def older(loop_state):
        j, _, state = loop_state
        s = far_slot(j)
        for c in far_copies(j):
            c.wait()
        start_if(j >= 1, j - 1)
        far_blocks = [(neg_logits(i, lambda h, i=i: head_rows(kfar_ref.at[s, i], 0, tk, h)),
                       lambda h, i=i: head_rows(vfar_ref.at[s, i], 0, tk, h), trit_ref[...], None)
                      for i in range(n)]
        state = walk([([far_blocks[i]], state[i][0], list(state[i][1])) for i in range(n)])
        return j - 1, any_live(state), tuple(state)

    live = any_live(state)
    if first_far >= 0:
        start_if(live, first_far)
    j_end, _, state = lax.while_loop(lambda s: jnp.logical_and(s[0] >= 0, s[1]), older,
                                     (jnp.int32(first_far), live, tuple(state)))
    if first_far >= 0:
        wait_if(jnp.logical_and(live, j_end >= 0), j_end)

    for i, (_, accs) in enumerate(state):
        for h, acc in enumerate(accs):
            rows, hs = slice(i * t, (i + 1) * t), slice(h * HEAD_DIM, (h + 1) * HEAD_DIM)
            o_ref[rows, hs] = (acc * sg_ref[rows, hs].astype(F32)).astype(BF16)


def _attend_sample(q, k_new, v_new, cache_k, cache_v, sg, tri_t, *, batch, t):
    width = q.shape[1]
    tk = tri_t.shape[0]
    block_rows = tk * NUM_GROUPS
    assert t & (t - 1) == 0 and NUM_GROUPS * t == tk and cache_k.shape[1] % block_rows == 0
    n = SAMPLE_STREAMS_PER_STEP
    assert batch % n == 0
    rowblock = pl.BlockSpec((n * t, width), lambda b: (b, 0))
    newblock = pl.BlockSpec((n * t * NUM_GROUPS, HEAD_DIM), lambda b: (b, 0))
    in_hbm = pl.BlockSpec(memory_space=pl.ANY)
    key_blocks = lambda *lead: pltpu.VMEM((*lead, n, block_rows, HEAD_DIM), F32)
    return pl.pallas_call(
        functools.partial(_attn_sample_kernel, t=t),
        grid=(batch // n,),
        in_specs=[rowblock, newblock, newblock, in_hbm, in_hbm, rowblock,
                  pl.BlockSpec(tri_t.shape, lambda b: (0, 0))],
        out_specs=rowblock,
        out_shape=jax.ShapeDtypeStruct(q.shape, BF16),
        scratch_shapes=[pltpu.VMEM((n, NUM_GROUPS // 2, NUM_GROUPS * t, 2 * HEAD_DIM), BF16),
                        key_blocks(2), key_blocks(2), key_blocks(2), key_blocks(2),
                        pltpu.SemaphoreType.DMA((2, 2, n)), pltpu.SemaphoreType.DMA((2, 2, n))],
        compiler_params=_compiler_params(("arbitrary",)),
        name="attn_sample",
    )(q, k_new, v_new, cache_k, cache_v, sg, tri_t)


def _out_kernel(x_ref, msb_ref, mcv_ref, wsb_ref, wcv_ref, g_ref, y_ref, *wcopy_refs):
    w_sb, w_cv = wsb_ref[...].astype(BF16), wcv_ref[...].astype(BF16)
    for copy_ref, w in zip(wcopy_refs, (w_sb, w_cv)):
        copy_ref[...] = w
    y = (x_ref[...]
         + jnp.dot(msb_ref[...], w_sb, preferred_element_type=F32)
         + jnp.dot(mcv_ref[...], w_cv, preferred_element_type=F32))
    y_ref[...] = _rmsnorm_rows(y, g_ref[...])


def _merge_out(x2d, mix_sb, mix_cv, weights, final_g, *, tm):
    m, d = x2d.shape
    half = mix_sb.shape[1]
    copy_weights = not isinstance(weights, (list, tuple))
    rows = lambda width: pl.BlockSpec((tm, width), lambda i: (i, 0))
    if copy_weights:
        w_args = [weights, weights]
        w_specs = [pl.BlockSpec((half, d), lambda i, r=r: (r, 0), pipeline_mode=pl.Buffered(1))
                   for r in range(2)]
        copy_specs = [pl.BlockSpec((half, d), lambda i: (0, 0), pipeline_mode=pl.Buffered(1))] * 2
        copy_shapes = [jax.ShapeDtypeStruct((half, d), BF16)] * 2
    else:
        w_args = list(weights)
        w_specs = [pl.BlockSpec((half, d), lambda i: (0, 0))] * 2
        copy_specs, copy_shapes = [], []
    y, *w_copies = pl.pallas_call(
        _out_kernel,
        grid=(m // tm,),
        in_specs=[rows(d), rows(half), rows(half), *w_specs, pl.BlockSpec((1, d), lambda i: (0, 0))],
        out_specs=[rows(d), *copy_specs],
        out_shape=[jax.ShapeDtypeStruct((m, d), F32), *copy_shapes],
        compiler_params=_compiler_params(("arbitrary",)),
        name="merge_out",
    )(x2d, mix_sb, mix_cv, *w_args, final_g.reshape(1, d))
    return y, w_copies


def _lower_tri(n):
    j = lax.broadcasted_iota(jnp.int32, (n, n), 0)
    s = lax.broadcasted_iota(jnp.int32, (n, n), 1)
    return (j >= s).astype(BF16)


def kernel(x_prompt, x_sample, cache_k, cache_v, state_conv, norm_g, w_in, conv_w, w_out, final_g):
    depth = w_in.shape[0]
    assert depth == 1, "single-layer step"
    bsz, seq, d = x_prompt.shape
    dbsz, dseq, _ = x_sample.shape
    past = cache_k.shape[2]
    width = NUM_GROUPS * HEAD_DIM

    tri = _lower_tri(KEY_BLOCK)

    xs = x_sample.reshape(dbsz * dseq, d)
    qs, ks, vs, _, _, sgs, mcvs, tails, w_in_bf = _project(
        xs, norm_g[0], w_in[0], conv_w[0], state_conv[0], seq_rows=dseq,
        tm=min(PROJ_ROWS, xs.shape[0]), heads_per_step=1)
    msbs = _attend_sample(qs, ks, vs,
                          cache_k[0].reshape(dbsz, past * NUM_GROUPS, HEAD_DIM),
                          cache_v[0].reshape(dbsz, past * NUM_GROUPS, HEAD_DIM),
                          sgs, tri.T, batch=dbsz, t=dseq)
    ys, w_out_bf = _merge_out(xs, msbs, mcvs, w_out[0], final_g, tm=OUT_ROWS)

    xp = x_prompt.reshape(bsz * seq, d)
    zeros_left = jnp.zeros((bsz, CONV_W - 1, width), F32)
    qp, kp, vp, kbp, vbp, sgp, mcvp, tailp, _ = _project(
        xp, norm_g[0], w_in_bf, conv_w[0], zeros_left, seq_rows=seq, tm=PROJ_ROWS,
        heads_per_step=PROMPT_PROJ_HEADS)
    yp = _attend_merge_prompt(qp, kbp, vbp, sgp, tri, xp, mcvp, w_out_bf, final_g,
                              batch=bsz, seq=seq, tq=KEY_BLOCK)

    heads = lambda a, b_, t_: a.reshape(1, b_, t_, NUM_GROUPS, HEAD_DIM)
    return (yp.reshape(bsz, seq, d), ys.reshape(dbsz, dseq, d),
            heads(kp, bsz, seq), heads(vp, bsz, seq), tailp[None],
            heads(ks, dbsz, dseq), heads(vs, dbsz, dseq), tails[None])
```

```python
import functools
import math

import jax
import jax.numpy as jnp
from jax import lax
from jax.experimental import pallas as pl
from jax.experimental.pallas import tpu as pltpu

F32 = jnp.float32
BF16 = jnp.bfloat16

HEAD_DIM = 128
NUM_GROUPS = 8
NUM_SEGMENTS = 8
CONV_W = 3
CONV_CTX = CONV_W - 1
assert CONV_W == 3, "the conv epilogue writes its three taps out explicitly"
SUBLANES = 8
EPS = 1e-6
LOG2E = math.log2(math.e)
Q_PRESCALE = -(HEAD_DIM ** -0.5) * LOG2E
MXU_DEPTH = 256
KEY_BLOCK = MXU_DEPTH
PROJ_ROWS = 1024
OUT_ROWS = 512
PROMPT_PROJ_HEADS = 2
SAMPLE_STREAMS_PER_STEP = 4
EXP2_UNDERFLOW = -1e30

VMEM_LIMIT_BYTES = 56 * 1024 * 1024


def _compiler_params(semantics):
    return pltpu.CompilerParams(dimension_semantics=semantics,
                                vmem_limit_bytes=VMEM_LIMIT_BYTES)


def _rmsnorm_rows(x, g):
    r = lax.rsqrt(jnp.mean(x * x, axis=-1, keepdims=True) + EPS)
    return (x * r) * g


def _silu(x):
    return x * (1.0 / (1.0 + jnp.exp(-x)))


def _proj_kernel(*refs, seq_rows, tiles_per_seq, copy_weights):
    refs = list(refs)
    take = lambda count: [refs.pop(0) for _ in range(count)]
    x_hbm, g_ref = take(2)
    w_refs = take(NUM_SEGMENTS)
    cw_ref, left_ref = take(2)
    q_ref, k_ref, v_ref, kb_ref, vb_ref, sg_ref, mcv_ref, tail_ref = take(8)
    wcopy_refs = take(NUM_SEGMENTS) if copy_weights else ()
    x_ref, hn_ref, carry_ref, x_sem = take(4)
    i = pl.program_id(0)
    h = pl.program_id(1)
    tm = x_ref.shape[0]

    def x_copy(tile):
        return pltpu.make_async_copy(x_hbm.at[pl.ds(pl.multiple_of(tile * tm, tm), tm), :],
                                     x_ref, x_sem.at[0])

    @pl.when(jnp.logical_and(i == 0, h == 0))
    def _():
        x_copy(0).start()
        carry_ref[...] = jnp.zeros(carry_ref.shape, F32)

    @pl.when(h == 0)
    def _():
        x_copy(i).wait()
        hn_ref[...] = _rmsnorm_rows(x_ref[...], g_ref[...]).astype(BF16)

    @pl.when(jnp.logical_and(h == 0, i + 1 < pl.num_programs(0)))
    def _():
        x_copy(i + 1).start()

    w_blocks = [r[...].astype(BF16) for r in w_refs]
    for copy_ref, w in zip(wcopy_refs, w_blocks):
        copy_ref[...] = w

    def project(*ws):
        acc = jnp.dot(hn_ref[...], jnp.concatenate(ws, axis=1), preferred_element_type=F32)
        return [acc[:, s * HEAD_DIM:(s + 1) * HEAD_DIM] for s in range(len(ws))]

    rows = seq_rows if tiles_per_seq == 1 else tm
    nseq = tm // rows
    expand = lambda a: jnp.broadcast_to(a, (nseq, rows, HEAD_DIM)).reshape(tm, HEAD_DIM)
    rs = lax.broadcasted_iota(jnp.int32, (tm, HEAD_DIM), 0) & (rows - 1)

    for a in range(q_ref.shape[1] // HEAD_DIM):
        head = h * (q_ref.shape[1] // HEAD_DIM) + a
        cols = slice(a * HEAD_DIM, (a + 1) * HEAD_DIM)
        wq, wk, wv, wgs, wb, wc, wu, wgc = [w[:, cols] for w in w_blocks]

        c, u = project(wc, wu)
        b, g_cv = project(wb, wgc)
        k, v = project(wk, wv)
        q, g_sb = project(wq, wgs)

        q_ref[:, cols] = (q * Q_PRESCALE).astype(BF16)
        k_ref[pl.ds(head, tm, stride=NUM_GROUPS), :] = k
        v_ref[pl.ds(head, tm, stride=NUM_GROUPS), :] = v
        kb_ref[:, cols] = k.astype(BF16)
        vb_ref[:, cols] = v.astype(BF16)
        sg_ref[:, cols] = _silu(g_sb).astype(BF16)

        cu = c * u
        if tiles_per_seq > 1:
            left = jnp.where(i % tiles_per_seq == 0, left_ref[:, :, cols],
                             carry_ref[head][None, 0:CONV_CTX, :])
            carry_ref[head, 0:CONV_CTX, :] = cu[tm - CONV_CTX:tm, :]
        else:
            left = left_ref[:, :, cols]
        l0 = expand(left[:, 0:1, :])
        l1 = expand(left[:, 1:2, :])
        r1 = jnp.where(rs == 0, l1, pltpu.roll(cu, 1, axis=0))
        r2 = jnp.where(rs == 0, l0, jnp.where(rs == 1, l1, pltpu.roll(cu, 2, axis=0)))
        cw = cw_ref[:, cols]
        conv = cw[0:1, :] * r2 + cw[1:2, :] * r1 + cw[2:3, :] * cu
        mcv_ref[:, cols] = (b * conv * _silu(g_cv)).astype(BF16)
        tail_ref[:, :, cols] = cu.reshape(nseq, rows, HEAD_DIM)[:, rows - CONV_CTX:rows, :]


def _project(x2d, norm_g, weights, conv_w, left, *, seq_rows, tm, heads_per_step):
    m, d = x2d.shape
    width = NUM_GROUPS * HEAD_DIM
    cols = heads_per_step * HEAD_DIM
    steps = NUM_GROUPS // heads_per_step
    copy_weights = not isinstance(weights, (list, tuple))
    if copy_weights:
        assert m == tm, "bf16 weight copies are written once, by a single row tile"
        w_args = [weights] * NUM_SEGMENTS
        segment = lambda s: pl.BlockSpec((d, cols), lambda i, h: (0, s * steps + h))
    else:
        w_args = list(weights)
        segment = lambda s: pl.BlockSpec((d, cols), lambda i, h: (0, h))
    once = dict(pipeline_mode=pl.Buffered(1)) if m == tm else {}
    if tm >= seq_rows:
        tiles_per_seq, ns = 1, tm // seq_rows
        left_idx = lambda i, h: (i, 0, h)
    else:
        tiles_per_seq, ns = seq_rows // tm, 1
        left_idx = lambda i, h: (i // tiles_per_seq, 0, h)
    n_tails = (m // tm) * ns
    tile = lambda dt: jax.ShapeDtypeStruct((m, width), dt)
    native = jax.ShapeDtypeStruct((m * NUM_GROUPS, HEAD_DIM), F32)
    col_block = pl.BlockSpec((tm, cols), lambda i, h: (i, h))
    native_block = pl.BlockSpec((tm * NUM_GROUPS, HEAD_DIM), lambda i, h: (i, 0), **once)
    copy_specs, copy_shapes = [], []
    if copy_weights:
        copy_specs = [pl.BlockSpec((d, cols), lambda i, h: (0, h))] * NUM_SEGMENTS
        copy_shapes = [jax.ShapeDtypeStruct((d, width), BF16)] * NUM_SEGMENTS
    kern = functools.partial(_proj_kernel, seq_rows=seq_rows, tiles_per_seq=tiles_per_seq,
                             copy_weights=copy_weights)
    q, k, v, kb, vb, sg, mcv, tails, *w_copies = pl.pallas_call(
        kern,
        grid=(m // tm, steps),
        in_specs=[
            pl.BlockSpec(memory_space=pl.ANY),
            pl.BlockSpec((1, d), lambda i, h: (0, 0)),
            *[segment(s) for s in range(NUM_SEGMENTS)],
            pl.BlockSpec((CONV_W, cols), lambda i, h: (0, h)),
            pl.BlockSpec((ns, CONV_CTX, cols), left_idx),
        ],
        out_specs=[col_block, native_block, native_block, col_block, col_block, col_block,
                   col_block, pl.BlockSpec((ns, CONV_CTX, cols), lambda i, h: (i, 0, h)),
                   *copy_specs],
        out_shape=[tile(BF16), native, native, tile(BF16), tile(BF16), tile(BF16), tile(BF16),
                   jax.ShapeDtypeStruct((n_tails, CONV_CTX, width), F32), *copy_shapes],
        scratch_shapes=[pltpu.VMEM((tm, d), F32), pltpu.VMEM((tm, d), BF16),
                        pltpu.VMEM((NUM_GROUPS, SUBLANES, HEAD_DIM), F32),
                        pltpu.SemaphoreType.DMA((1,))],
        compiler_params=_compiler_params(("arbitrary", "arbitrary")),
        name="proj",
    )(x2d, norm_g.reshape(1, d), *w_args, conv_w, left)
    tails = tails.reshape(-1, tiles_per_seq, CONV_CTX, width)[:, tiles_per_seq - 1]
    return q, k, v, kb, vb, sg, mcv, tails, w_copies


def _log2_one_minus_beta(zn, mask):
    softplus2 = jnp.log(1.0 + jnp.exp2(-jnp.abs(zn))) * LOG2E
    log_1m = jnp.minimum(zn, 0.0) - softplus2
    return log_1m if mask is None else jnp.where(mask, log_1m, 0.0)


def _split_bf16(x, axis):
    hi = x.astype(BF16)
    return jnp.concatenate([hi, (x - hi.astype(F32)).astype(BF16)], axis=axis)


def _block_weights(suffix, zn, mask):
    w = jnp.exp2(suffix - zn)
    return (w if mask is None else jnp.where(mask, w, 0.0)).astype(BF16)


def _attn_out_prompt_kernel(q_ref, k_ref, v_ref, sg_ref, tri_ref, x_ref, mcv_ref, wsb_ref, wcv_ref,
                            g_ref, y_ref, acc_ref, mix_ref, *, blocks_per_seq):
    step = pl.program_id(0)
    qi = jnp.minimum(step, pl.num_programs(0) - 2) % blocks_per_seq

    @pl.when(step == 0)
    def _():
        mix_ref[...] = jnp.zeros(mix_ref.shape, BF16)

    tq = q_ref.shape[0]
    tk = tri_ref.shape[0]
    heads = q_ref.shape[1] // HEAD_DIM
    tri2 = jnp.concatenate([tri_ref[...]] * 2, axis=0)
    hs = [slice(a * HEAD_DIM, (a + 1) * HEAD_DIM) for a in range(heads)]
    nt_dims = (((1,), (1,)), ((), ()))

    def rows_of(j):
        return pl.ds(j * tk if isinstance(j, int) else pl.multiple_of(j * tk, tk), tk)

    def logits(j):
        return tuple(lax.dot_general(q_ref[:, s], k_ref[rows_of(j), s], nt_dims,
                                     preferred_element_type=F32) for s in hs)

    def weights(zns, mask):
        pieces = [_split_bf16(_log2_one_minus_beta(zn, mask), axis=1) for zn in zns]
        suffixes = [jnp.dot(p, tri2, preferred_element_type=F32) for p in pieces]
        ws = tuple(_block_weights(sfx, zn, mask) for sfx, zn in zip(suffixes, zns))
        return ws, tuple(sfx[:, 0:1] for sfx in suffixes)

    def accumulate(ws, j, scales):
        for a, s in enumerate(hs):
            acc_ref[:, s] += scales[a] * jnp.dot(ws[a], v_ref[rows_of(j), s],
                                                 preferred_element_type=F32)

    def step(j, carries, mask):
        ws, totals = weights(logits(j), mask)
        accumulate(ws, j, [jnp.exp2(c) for c in carries])
        return tuple(c + tot for c, tot in zip(carries, totals))

    rows = lax.broadcasted_iota(jnp.int32, (tq, tk), 0)
    cols = lax.broadcasted_iota(jnp.int32, (tq, tk), 1)
    causal = cols < rows
    zns = logits(qi)
    out_sb = jnp.dot(mix_ref[...], wsb_ref[...], preferred_element_type=F32)
    pieces = [_split_bf16(_log2_one_minus_beta(zn, causal), axis=1) for zn in zns]
    suffixes = [jnp.dot(p, tri2, preferred_element_type=F32) for p in pieces]
    out_cv = jnp.dot(mcv_ref[...], wcv_ref[...], preferred_element_type=F32)
    ws = [_block_weights(sfx, zn, causal) for sfx, zn in zip(suffixes, zns)]
    for w, s in zip(ws, hs):
        acc_ref[:, s] = jnp.dot(w, v_ref[rows_of(qi), s], preferred_element_type=F32)
    y_ref[...] = _rmsnorm_rows(x_ref[...] + out_sb + out_cv, g_ref[...])
    carries = tuple(sfx[:, 0:1] for sfx in suffixes)

    def any_live(carries):
        return jnp.max(functools.reduce(jnp.maximum, carries)) > EXP2_UNDERFLOW

    def body(state):
        t, _, carries = state
        carries = step(qi - 1 - t, carries, None)
        return t + 1, any_live(carries), carries

    lax.while_loop(lambda state: jnp.logical_and(state[0] < qi, state[1]), body,
                   (jnp.int32(0), any_live(carries), carries))
    mix_ref[...] = (acc_ref[...] * sg_ref[...].astype(F32)).astype(BF16)


def _attend_merge_prompt(q, kb, vb, sg, tri, x2d, mix_cv, w_halves, final_g, *, batch, seq, tq):
    nq = seq // tq
    n_blocks = batch * nq
    m, d = x2d.shape
    width = q.shape[1]
    assert tq == tri.shape[0], "query block and key block share the diagonal mask"
    attended = lambda s: jnp.minimum(s, n_blocks - 1)
    projected = lambda s: jnp.maximum(s - 1, 0)
    qblock = pl.BlockSpec((tq, width), lambda s: (attended(s), 0))
    kvblock = pl.BlockSpec((seq, width), lambda s: (attended(s) // nq, 0))
    const = lambda shape: pl.BlockSpec(shape, lambda s: (0, 0), pipeline_mode=pl.Buffered(1))
    return pl.pallas_call(
        functools.partial(_attn_out_prompt_kernel, blocks_per_seq=nq),
        grid=(n_blocks + 1,),
        in_specs=[qblock, kvblock, kvblock, qblock, const(tri.shape),
                  pl.BlockSpec((tq, d), lambda s: (projected(s), 0)),
                  pl.BlockSpec((tq, width), lambda s: (projected(s), 0)),
                  const(w_halves[0].shape), const(w_halves[1].shape), const((1, d))],
        out_specs=pl.BlockSpec((tq, d), lambda s: (projected(s), 0)),
        out_shape=jax.ShapeDtypeStruct((m, d), F32),
        scratch_shapes=[pltpu.VMEM((tq, width), F32), pltpu.VMEM((tq, width), BF16)],
        compiler_params=_compiler_params(("arbitrary",)),
        name="attn_out_prompt",
    )(q, kb, vb, sg, tri, x2d, mix_cv, w_halves[0], w_halves[1], final_g.reshape(1, d))


def _attn_sample_kernel(q_ref, kn_ref, vn_ref, kc_hbm, vc_hbm, sg_ref, trit_ref, o_ref,
                        qnt_ref, knear_ref, vnear_ref, kfar_ref, vfar_ref, near_sems, far_sems,
                        *, t):
    step = pl.program_id(0)
    n = q_ref.shape[0] // t
    tk = trit_ref.shape[0]
    block_rows = tk * NUM_GROUPS
    n_cache = kc_hbm.shape[1] // block_rows
    lanes = NUM_GROUPS * t
    pairs = NUM_GROUPS // 2
    pad = HEAD_DIM
    nt_dims = (((1,), (1,)), ((), ()))
    slot = step % 2

    def cache_copies(stream, j, k_dst, v_dst, k_sem, v_sem):
        rows = pl.ds(j * block_rows, block_rows)
        return [pltpu.make_async_copy(kc_hbm.at[stream, rows, :], k_dst, k_sem),
                pltpu.make_async_copy(vc_hbm.at[stream, rows, :], v_dst, v_sem)]

    def near_copies(at_step, s):
        return [c for i in range(n) for c in cache_copies(
            at_step * n + i, n_cache - 1, knear_ref.at[s, i], vnear_ref.at[s, i],
            near_sems.at[0, s, i], near_sems.at[1, s, i])]

    @pl.when(step == 0)
    def _():
        for c in near_copies(0, 0):
            c.start()

    @pl.when(step + 1 < pl.num_programs(0))
    def _():
        for c in near_copies(step + 1, 1 - slot):
            c.start()

    qnt_ref[...] = jnp.zeros(qnt_ref.shape, BF16)
    for i in range(n):
        for h in range(NUM_GROUPS):
            p, half = divmod(h, 2)
            qnt_ref[i, p, h * t:(h + 1) * t, half * HEAD_DIM:(half + 1) * HEAD_DIM] = (
                q_ref[i * t:(i + 1) * t, h * HEAD_DIM:(h + 1) * HEAD_DIM])

    def neg_logits(i, load_k):
        zn = None
        for p in range(pairs):
            lhs = jnp.concatenate([load_k(2 * p), load_k(2 * p + 1)], axis=1).astype(BF16)
            d = lax.dot_general(lhs, qnt_ref[i, p], nt_dims, preferred_element_type=F32)
            zn = d if zn is None else zn + d
        return zn

    def walk(work):
        pieces = [[_split_bf16(_log2_one_minus_beta(zn, m), axis=0) for zn, _, _, m in blocks]
                  for blocks, _, _ in work]
        suffixes = [[jnp.dot(jnp.concatenate([tr, tr], axis=1), p, preferred_element_type=F32)
                     for (_, _, tr, _), p in zip(blocks, ps)]
                    for (blocks, _, _), ps in zip(work, pieces)]
        done = []
        for (blocks, carry, accs), sfxs in zip(work, suffixes):
            for (zn, load_v, _, m), sfx in zip(blocks, sfxs):
                w = jnp.exp2(sfx - zn + carry)
                if m is not None:
                    w = jnp.where(m, w, 0.0)
                w_t = w.T
                accs = [acc + jnp.dot(w_t[h * t:(h + 1) * t, :].astype(BF16),
                                      load_v(h).astype(BF16), preferred_element_type=F32)
                        for h, acc in enumerate(accs)]
                carry = carry + sfx[0:1, :]
            done.append((carry, tuple(accs)))
        return done

    head_rows = lambda ref, base, rows, h: ref[pl.ds(base + h, rows, stride=NUM_GROUPS), :]

    key_idx = lax.broadcasted_iota(jnp.int32, (pad, lanes), 0)
    query_idx = lax.broadcasted_iota(jnp.int32, (pad, lanes), 1) & (t - 1)
    zero_rows = lambda width: jnp.zeros((pad - t, width), F32)
    new_base = lambda i: i * t * NUM_GROUPS
    new_blocks = [
        (jnp.concatenate([neg_logits(i, lambda h, i=i: head_rows(kn_ref, new_base(i), t, h)),
                          zero_rows(lanes)], axis=0),
         lambda h, i=i: jnp.concatenate([head_rows(vn_ref, new_base(i), t, h),
                                         zero_rows(HEAD_DIM)], axis=0),
         trit_ref[0:pad, 0:pad], key_idx < query_idx) for i in range(n)]
    for c in near_copies(step, slot):
        c.wait()
    near_blocks = [
        (neg_logits(i, lambda h, i=i: head_rows(knear_ref.at[slot, i], 0, tk, h)),
         lambda h, i=i: head_rows(vnear_ref.at[slot, i], 0, tk, h), trit_ref[...], None)
        for i in range(n)]
    state = walk([([new_blocks[i], near_blocks[i]], jnp.zeros((1, lanes), F32),
                   [jnp.zeros((t, HEAD_DIM), F32)] * NUM_GROUPS) for i in range(n)])

    def any_live(state):
        return jnp.max(functools.reduce(jnp.maximum, [c for c, _ in state])) > EXP2_UNDERFLOW

    first_far = n_cache - 2
    far_slot = lambda j: (first_far - j) % 2

    def far_copies(j):
        s = far_slot(j)
        return [c for i in range(n) for c in cache_copies(
            step * n + i, j, kfar_ref.at[s, i], vfar_ref.at[s, i],
            far_sems.at[0, s, i], far_sems.at[1, s, i])]

    def start_if(cond, j):
        @pl.when(cond)
        def _():
            for c in far_copies(j):
                c.start()

    def wait_if(cond, j):
        @pl.when(cond)
        def _():
            for c in far_copies(j):
                c.wait()

    def older(loop_state):
        j, _, state = loop_state
        s = far_slot(j)
        for c in far_copies(j):
            c.wait()
        start_if(j >= 1, j - 1)
        far_blocks = [(neg_logits(i, lambda h, i=i: head_rows(kfar_ref.at[s, i], 0, tk, h)),
                       lambda h, i=i: head_rows(vfar_ref.at[s, i], 0, tk, h), trit_ref[...], None)
                      for i in range(n)]
        state = walk([([far_blocks[i]], state[i][0], list(state[i][1])) for i in range(n)])
        return j - 1, any_live(state), tuple(state)

    live = any_live(state)
    if first_far >= 0:
        start_if(live, first_far)
    j_end, _, state = lax.while_loop(lambda s: jnp.logical_and(s[0] >= 0, s[1]), older,
                                     (jnp.int32(first_far), live, tuple(state)))
    if first_far >= 0:
        wait_if(jnp.logical_and(live, j_end >= 0), j_end)

    for i, (_, accs) in enumerate(state):
        for h, acc in enumerate(accs):
            rows, hs = slice(i * t, (i + 1) * t), slice(h * HEAD_DIM, (h + 1) * HEAD_DIM)
            o_ref[rows, hs] = (acc * sg_ref[rows, hs].astype(F32)).astype(BF16)


def _attend_sample(q, k_new, v_new, cache_k, cache_v, sg, tri_t, *, batch, t):
    width = q.shape[1]
    tk = tri_t.shape[0]
    block_rows = tk * NUM_GROUPS
    assert t & (t - 1) == 0 and NUM_GROUPS * t == tk and cache_k.shape[1] % block_rows == 0
    n = SAMPLE_STREAMS_PER_STEP
    assert batch % n == 0
    rowblock = pl.BlockSpec((n * t, width), lambda b: (b, 0))
    newblock = pl.BlockSpec((n * t * NUM_GROUPS, HEAD_DIM), lambda b: (b, 0))
    in_hbm = pl.BlockSpec(memory_space=pl.ANY)
    key_blocks = lambda *lead: pltpu.VMEM((*lead, n, block_rows, HEAD_DIM), F32)
    return pl.pallas_call(
        functools.partial(_attn_sample_kernel, t=t),
        grid=(batch // n,),
        in_specs=[rowblock, newblock, newblock, in_hbm, in_hbm, rowblock,
                  pl.BlockSpec(tri_t.shape, lambda b: (0, 0))],
        out_specs=rowblock,
        out_shape=jax.ShapeDtypeStruct(q.shape, BF16),
        scratch_shapes=[pltpu.VMEM((n, NUM_GROUPS // 2, NUM_GROUPS * t, 2 * HEAD_DIM), BF16),
                        key_blocks(2), key_blocks(2), key_blocks(2), key_blocks(2),
                        pltpu.SemaphoreType.DMA((2, 2, n)), pltpu.SemaphoreType.DMA((2, 2, n))],
        compiler_params=_compiler_params(("arbitrary",)),
        name="attn_sample",
    )(q, k_new, v_new, cache_k, cache_v, sg, tri_t)


def _out_kernel(x_ref, msb_ref, mcv_ref, wsb_ref, wcv_ref, g_ref, y_ref, *wcopy_refs):
    w_sb, w_cv = wsb_ref[...].astype(BF16), wcv_ref[...].astype(BF16)
    for copy_ref, w in zip(wcopy_refs, (w_sb, w_cv)):
        copy_ref[...] = w
    y = (x_ref[...]
         + jnp.dot(msb_ref[...], w_sb, preferred_element_type=F32)
         + jnp.dot(mcv_ref[...], w_cv, preferred_element_type=F32))
    y_ref[...] = _rmsnorm_rows(y, g_ref[...])


def _merge_out(x2d, mix_sb, mix_cv, weights, final_g, *, tm):
    m, d = x2d.shape
    half = mix_sb.shape[1]
    copy_weights = not isinstance(weights, (list, tuple))
    rows = lambda width: pl.BlockSpec((tm, width), lambda i: (i, 0))
    if copy_weights:
        w_args = [weights, weights]
        w_specs = [pl.BlockSpec((half, d), lambda i, r=r: (r, 0), pipeline_mode=pl.Buffered(1))
                   for r in range(2)]
        copy_specs = [pl.BlockSpec((half, d), lambda i: (0, 0), pipeline_mode=pl.Buffered(1))] * 2
        copy_shapes = [jax.ShapeDtypeStruct((half, d), BF16)] * 2
    else:
        w_args = list(weights)
        w_specs = [pl.BlockSpec((half, d), lambda i: (0, 0))] * 2
        copy_specs, copy_shapes = [], []
    y, *w_copies = pl.pallas_call(
        _out_kernel,
        grid=(m // tm,),
        in_specs=[rows(d), rows(half), rows(half), *w_specs, pl.BlockSpec((1, d), lambda i: (0, 0))],
        out_specs=[rows(d), *copy_specs],
        out_shape=[jax.ShapeDtypeStruct((m, d), F32), *copy_shapes],
        compiler_params=_compiler_params(("arbitrary",)),
        name="merge_out",
    )(x2d, mix_sb, mix_cv, *w_args, final_g.reshape(1, d))
    return y, w_copies


def _lower_tri(n):
    j = lax.broadcasted_iota(jnp.int32, (n, n), 0)
    s = lax.broadcasted_iota(jnp.int32, (n, n), 1)
    return (j >= s).astype(BF16)


def kernel(x_prompt, x_sample, cache_k, cache_v, state_conv, norm_g, w_in, conv_w, w_out, final_g):
    depth = w_in.shape[0]
    assert depth == 1, "single-layer step"
    bsz, seq, d = x_prompt.shape
    dbsz, dseq, _ = x_sample.shape
    past = cache_k.shape[2]
    width = NUM_GROUPS * HEAD_DIM

    tri = _lower_tri(KEY_BLOCK)

    xs = x_sample.reshape(dbsz * dseq, d)
    qs, ks, vs, _, _, sgs, mcvs, tails, w_in_bf = _project(
        xs, norm_g[0], w_in[0], conv_w[0], state_conv[0], seq_rows=dseq,
        tm=min(PROJ_ROWS, xs.shape[0]), heads_per_step=1)
    msbs = _attend_sample(qs, ks, vs,
                          cache_k[0].reshape(dbsz, past * NUM_GROUPS, HEAD_DIM),
                          cache_v[0].reshape(dbsz, past * NUM_GROUPS, HEAD_DIM),
                          sgs, tri.T, batch=dbsz, t=dseq)
    ys, w_out_bf = _merge_out(xs, msbs, mcvs, w_out[0], final_g, tm=OUT_ROWS)

    xp = x_prompt.reshape(bsz * seq, d)
    zeros_left = jnp.zeros((bsz, CONV_W - 1, width), F32)
    qp, kp, vp, kbp, vbp, sgp, mcvp, tailp, _ = _project(
        xp, norm_g[0], w_in_bf, conv_w[0], zeros_left, seq_rows=seq, tm=PROJ_ROWS,
        heads_per_step=PROMPT_PROJ_HEADS)
    yp = _attend_merge_prompt(qp, kbp, vbp, sgp, tri, xp, mcvp, w_out_bf, final_g,
                              batch=bsz, seq=seq, tq=KEY_BLOCK)

    heads = lambda a, b_, t_: a.reshape(1, b_, t_, NUM_GROUPS, HEAD_DIM)
    return (yp.reshape(bsz, seq, d), ys.reshape(dbsz, dseq, d),
            heads(kp, bsz, seq), heads(vp, bsz, seq), tailp[None],
            heads(ks, dbsz, dseq), heads(vs, dbsz, dseq), tails[None])
```

```python
import functools
import math

import jax
import jax.numpy as jnp
from jax import lax
from jax.experimental import pallas as pl
from jax.experimental.pallas import tpu as pltpu

F32 = jnp.float32
BF16 = jnp.bfloat16

HEAD_DIM = 128
NUM_GROUPS = 8
NUM_SEGMENTS = 8
CONV_W = 3
CONV_CTX = CONV_W - 1
assert CONV_W == 3, "the conv epilogue writes its three taps out explicitly"
SUBLANES = 8
EPS = 1e-6
LOG2E = math.log2(math.e)
Q_PRESCALE = -(HEAD_DIM ** -0.5) * LOG2E
MXU_DEPTH = 256
KEY_BLOCK = MXU_DEPTH
PROJ_ROWS = 1024
OUT_ROWS = 512
PROMPT_PROJ_HEADS = 2
SAMPLE_STREAMS_PER_STEP = 4
EXP2_UNDERFLOW = -160.0

VMEM_LIMIT_BYTES = 56 * 1024 * 1024


def _compiler_params(semantics):
    return pltpu.CompilerParams(dimension_semantics=semantics,
                                vmem_limit_bytes=VMEM_LIMIT_BYTES)


def _rmsnorm_rows(x, g):
    r = lax.rsqrt(jnp.mean(x * x, axis=-1, keepdims=True) + EPS)
    return (x * r) * g


def _silu(x):
    return x * (1.0 / (1.0 + jnp.exp(-x)))


def _proj_kernel(*refs, seq_rows, tiles_per_seq, copy_weights):
    refs = list(refs)
    take = lambda count: [refs.pop(0) for _ in range(count)]
    x_hbm, g_ref = take(2)
    w_refs = take(NUM_SEGMENTS)
    cw_ref, left_ref = take(2)
    q_ref, k_ref, v_ref, kb_ref, vb_ref, sg_ref, mcv_ref, tail_ref = take(8)
    wcopy_refs = take(NUM_SEGMENTS) if copy_weights else ()
    x_ref, hn_ref, carry_ref, x_sem = take(4)
    i = pl.program_id(0)
    h = pl.program_id(1)
    tm = x_ref.shape[0]

    def x_copy(tile):
        return pltpu.make_async_copy(x_hbm.at[pl.ds(pl.multiple_of(tile * tm, tm), tm), :],
                                     x_ref, x_sem.at[0])

    @pl.when(jnp.logical_and(i == 0, h == 0))
    def _():
        x_copy(0).start()
        carry_ref[...] = jnp.zeros(carry_ref.shape, F32)

    @pl.when(h == 0)
    def _():
        x_copy(i).wait()
        hn_ref[...] = _rmsnorm_rows(x_ref[...], g_ref[...]).astype(BF16)

    @pl.when(jnp.logical_and(h == 0, i + 1 < pl.num_programs(0)))
    def _():
        x_copy(i + 1).start()

    w_blocks = [r[...].astype(BF16) for r in w_refs]
    for copy_ref, w in zip(wcopy_refs, w_blocks):
        copy_ref[...] = w

    def project(*ws):
        acc = jnp.dot(hn_ref[...], jnp.concatenate(ws, axis=1), preferred_element_type=F32)
        return [acc[:, s * HEAD_DIM:(s + 1) * HEAD_DIM] for s in range(len(ws))]

    rows = seq_rows if tiles_per_seq == 1 else tm
    nseq = tm // rows
    expand = lambda a: jnp.broadcast_to(a, (nseq, rows, HEAD_DIM)).reshape(tm, HEAD_DIM)
    rs = lax.broadcasted_iota(jnp.int32, (tm, HEAD_DIM), 0) & (rows - 1)

    for a in range(q_ref.shape[1] // HEAD_DIM):
        head = h * (q_ref.shape[1] // HEAD_DIM) + a
        cols = slice(a * HEAD_DIM, (a + 1) * HEAD_DIM)
        wq, wk, wv, wgs, wb, wc, wu, wgc = [w[:, cols] for w in w_blocks]

        c, u = project(wc, wu)
        b, g_cv = project(wb, wgc)
        k, v = project(wk, wv)
        q, g_sb = project(wq, wgs)

        q_ref[:, cols] = (q * Q_PRESCALE).astype(BF16)
        k_ref[pl.ds(head, tm, stride=NUM_GROUPS), :] = k
        v_ref[pl.ds(head, tm, stride=NUM_GROUPS), :] = v
        kb_ref[:, cols] = k.astype(BF16)
        vb_ref[:, cols] = v.astype(BF16)
        sg_ref[:, cols] = _silu(g_sb).astype(BF16)

        cu = c * u
        if tiles_per_seq > 1:
            left = jnp.where(i % tiles_per_seq == 0, left_ref[:, :, cols],
                             carry_ref[head][None, 0:CONV_CTX, :])
            carry_ref[head, 0:CONV_CTX, :] = cu[tm - CONV_CTX:tm, :]
        else:
            left = left_ref[:, :, cols]
        l0 = expand(left[:, 0:1, :])
        l1 = expand(left[:, 1:2, :])
        r1 = jnp.where(rs == 0, l1, pltpu.roll(cu, 1, axis=0))
        r2 = jnp.where(rs == 0, l0, jnp.where(rs == 1, l1, pltpu.roll(cu, 2, axis=0)))
        cw = cw_ref[:, cols]
        conv = cw[0:1, :] * r2 + cw[1:2, :] * r1 + cw[2:3, :] * cu
        mcv_ref[:, cols] = (b * conv * _silu(g_cv)).astype(BF16)
        tail_ref[:, :, cols] = cu.reshape(nseq, rows, HEAD_DIM)[:, rows - CONV_CTX:rows, :]


def _project(x2d, norm_g, weights, conv_w, left, *, seq_rows, tm, heads_per_step):
    m, d = x2d.shape
    width = NUM_GROUPS * HEAD_DIM
    cols = heads_per_step * HEAD_DIM
    steps = NUM_GROUPS // heads_per_step
    copy_weights = not isinstance(weights, (list, tuple))
    if copy_weights:
        assert m == tm, "bf16 weight copies are written once, by a single row tile"
        w_args = [weights] * NUM_SEGMENTS
        segment = lambda s: pl.BlockSpec((d, cols), lambda i, h: (0, s * steps + h))
    else:
        w_args = list(weights)
        segment = lambda s: pl.BlockSpec((d, cols), lambda i, h: (0, h))
    once = dict(pipeline_mode=pl.Buffered(1)) if m == tm else {}
    if tm >= seq_rows:
        tiles_per_seq, ns = 1, tm // seq_rows
        left_idx = lambda i, h: (i, 0, h)
    else:
        tiles_per_seq, ns = seq_rows // tm, 1
        left_idx = lambda i, h: (i // tiles_per_seq, 0, h)
    n_tails = (m // tm) * ns
    tile = lambda dt: jax.ShapeDtypeStruct((m, width), dt)
    native = jax.ShapeDtypeStruct((m * NUM_GROUPS, HEAD_DIM), F32)
    col_block = pl.BlockSpec((tm, cols), lambda i, h: (i, h))
    native_block = pl.BlockSpec((tm * NUM_GROUPS, HEAD_DIM), lambda i, h: (i, 0), **once)
    copy_specs, copy_shapes = [], []
    if copy_weights:
        copy_specs = [pl.BlockSpec((d, cols), lambda i, h: (0, h))] * NUM_SEGMENTS
        copy_shapes = [jax.ShapeDtypeStruct((d, width), BF16)] * NUM_SEGMENTS
    kern = functools.partial(_proj_kernel, seq_rows=seq_rows, tiles_per_seq=tiles_per_seq,
                             copy_weights=copy_weights)
    q, k, v, kb, vb, sg, mcv, tails, *w_copies = pl.pallas_call(
        kern,
        grid=(m // tm, steps),
        in_specs=[
            pl.BlockSpec(memory_space=pl.ANY),
            pl.BlockSpec((1, d), lambda i, h: (0, 0)),
            *[segment(s) for s in range(NUM_SEGMENTS)],
            pl.BlockSpec((CONV_W, cols), lambda i, h: (0, h)),
            pl.BlockSpec((ns, CONV_CTX, cols), left_idx),
        ],
        out_specs=[col_block, native_block, native_block, col_block, col_block, col_block,
                   col_block, pl.BlockSpec((ns, CONV_CTX, cols), lambda i, h: (i, 0, h)),
                   *copy_specs],
        out_shape=[tile(BF16), native, native, tile(BF16), tile(BF16), tile(BF16), tile(BF16),
                   jax.ShapeDtypeStruct((n_tails, CONV_CTX, width), F32), *copy_shapes],
        scratch_shapes=[pltpu.VMEM((tm, d), F32), pltpu.VMEM((tm, d), BF16),
                        pltpu.VMEM((NUM_GROUPS, SUBLANES, HEAD_DIM), F32),
                        pltpu.SemaphoreType.DMA((1,))],
        compiler_params=_compiler_params(("arbitrary", "arbitrary")),
        name="proj",
    )(x2d, norm_g.reshape(1, d), *w_args, conv_w, left)
    tails = tails.reshape(-1, tiles_per_seq, CONV_CTX, width)[:, tiles_per_seq - 1]
    return q, k, v, kb, vb, sg, mcv, tails, w_copies


def _log2_one_minus_beta(zn, mask):
    softplus2 = jnp.log(1.0 + jnp.exp2(-jnp.abs(zn))) * LOG2E
    log_1m = jnp.minimum(zn, 0.0) - softplus2
    return log_1m if mask is None else jnp.where(mask, log_1m, 0.0)


def _split_bf16(x, axis):
    hi = x.astype(BF16)
    return jnp.concatenate([hi, (x - hi.astype(F32)).astype(BF16)], axis=axis)


def _block_weights(suffix, zn, mask):
    w = jnp.exp2(suffix - zn)
    return (w if mask is None else jnp.where(mask, w, 0.0)).astype(BF16)


def _attn_out_prompt_kernel(q_ref, k_ref, v_ref, sg_ref, tri_ref, x_ref, mcv_ref, wsb_ref, wcv_ref,
                            g_ref, y_ref, acc_ref, mix_ref, *, blocks_per_seq):
    grid_step = pl.program_id(0)
    last_step = pl.num_programs(0) - 1
    qi = grid_step % blocks_per_seq

    tq = q_ref.shape[0]
    tk = tri_ref.shape[0]
    heads = q_ref.shape[1] // HEAD_DIM
    tri2 = jnp.concatenate([tri_ref[...]] * 2, axis=0)
    hs = [slice(a * HEAD_DIM, (a + 1) * HEAD_DIM) for a in range(heads)]
    nt_dims = (((1,), (1,)), ((), ()))

    def rows_of(j):
        return pl.ds(j * tk if isinstance(j, int) else pl.multiple_of(j * tk, tk), tk)

    def logits(j):
        return tuple(lax.dot_general(q_ref[:, s], k_ref[rows_of(j), s], nt_dims,
                                     preferred_element_type=F32) for s in hs)

    def weights(zns, mask):
        pieces = [_split_bf16(_log2_one_minus_beta(zn, mask), axis=1) for zn in zns]
        suffixes = [jnp.dot(p, tri2, preferred_element_type=F32) for p in pieces]
        ws = tuple(_block_weights(sfx, zn, mask) for sfx, zn in zip(suffixes, zns))
        return ws, tuple(sfx[:, 0:1] for sfx in suffixes)

    def accumulate(ws, j, scales):
        for a, s in enumerate(hs):
            acc_ref[:, s] += scales[a] * jnp.dot(ws[a], v_ref[rows_of(j), s],
                                                 preferred_element_type=F32)

    def step(j, carries, mask):
        ws, totals = weights(logits(j), mask)
        accumulate(ws, j, [jnp.exp2(c) for c in carries])
        return tuple(c + tot for c, tot in zip(carries, totals))

    project_sb = lambda: jnp.dot(mix_ref[...], wsb_ref[...], preferred_element_type=F32)
    project_cv = lambda: jnp.dot(mcv_ref[...], wcv_ref[...], preferred_element_type=F32)

    def finish_projection(out_sb, out_cv):
        y_ref[...] = _rmsnorm_rows(x_ref[...] + out_sb + out_cv, g_ref[...])

    def any_live(carries):
        return jnp.max(functools.reduce(jnp.maximum, carries)) > EXP2_UNDERFLOW

    def attend(with_projection):
        rows = lax.broadcasted_iota(jnp.int32, (tq, tk), 0)
        cols = lax.broadcasted_iota(jnp.int32, (tq, tk), 1)
        causal = cols < rows
        zns = logits(qi)
        out_sb = project_sb() if with_projection else None
        pieces = [_split_bf16(_log2_one_minus_beta(zn, causal), axis=1) for zn in zns]
        suffixes = [jnp.dot(p, tri2, preferred_element_type=F32) for p in pieces]
        out_cv = project_cv() if with_projection else None
        ws = [_block_weights(sfx, zn, causal) for sfx, zn in zip(suffixes, zns)]
        for w, s in zip(ws, hs):
            acc_ref[:, s] = jnp.dot(w, v_ref[rows_of(qi), s], preferred_element_type=F32)
        if with_projection:
            finish_projection(out_sb, out_cv)
        carries = tuple(sfx[:, 0:1] for sfx in suffixes)

        def body(state):
            t, _, carries = state
            carries = step(qi - 1 - t, carries, None)
            return t + 1, any_live(carries), carries

        lax.while_loop(lambda state: jnp.logical_and(state[0] < qi, state[1]), body,
                       (jnp.int32(0), any_live(carries), carries))
        mix_ref[...] = (acc_ref[...] * sg_ref[...].astype(F32)).astype(BF16)

    pl.when(grid_step == 0)(lambda: attend(False))
    pl.when(jnp.logical_and(grid_step > 0, grid_step < last_step))(lambda: attend(True))
    pl.when(grid_step == last_step)(lambda: finish_projection(project_sb(), project_cv()))


def _attend_merge_prompt(q, kb, vb, sg, tri, x2d, mix_cv, w_halves, final_g, *, batch, seq, tq):
    nq = seq // tq
    n_blocks = batch * nq
    m, d = x2d.shape
    width = q.shape[1]
    assert tq == tri.shape[0], "query block and key block share the diagonal mask"
    attended = lambda s: jnp.minimum(s, n_blocks - 1)
    projected = lambda s: jnp.maximum(s - 1, 0)
    qblock = pl.BlockSpec((tq, width), lambda s: (attended(s), 0))
    kvblock = pl.BlockSpec((seq, width), lambda s: (attended(s) // nq, 0))
    const = lambda shape: pl.BlockSpec(shape, lambda s: (0, 0), pipeline_mode=pl.Buffered(1))
    return pl.pallas_call(
        functools.partial(_attn_out_prompt_kernel, blocks_per_seq=nq),
        grid=(n_blocks + 1,),
        in_specs=[qblock, kvblock, kvblock, qblock, const(tri.shape),
                  pl.BlockSpec((tq, d), lambda s: (projected(s), 0)),
                  pl.BlockSpec((tq, width), lambda s: (projected(s), 0)),
                  const(w_halves[0].shape), const(w_halves[1].shape), const((1, d))],
        out_specs=pl.BlockSpec((tq, d), lambda s: (projected(s), 0)),
        out_shape=jax.ShapeDtypeStruct((m, d), F32),
        scratch_shapes=[pltpu.VMEM((tq, width), F32), pltpu.VMEM((tq, width), BF16)],
        compiler_params=_compiler_params(("arbitrary",)),
        name="attn_out_prompt",
    )(q, kb, vb, sg, tri, x2d, mix_cv, w_halves[0], w_halves[1], final_g.reshape(1, d))


def _attn_sample_kernel(q_ref, kn_ref, vn_ref, kc_hbm, vc_hbm, sg_ref, trit_ref, o_ref,
                        qnt_ref, knear_ref, vnear_ref, kfar_ref, vfar_ref, near_sems, far_sems,
                        *, t):
    step = pl.program_id(0)
    n = q_ref.shape[0] // t
    tk = trit_ref.shape[0]
    block_rows = tk * NUM_GROUPS
    n_cache = kc_hbm.shape[1] // block_rows
    lanes = NUM_GROUPS * t
    pairs = NUM_GROUPS // 2
    pad = HEAD_DIM
    nt_dims = (((1,), (1,)), ((), ()))
    slot = step % 2

    def cache_copies(stream, j, k_dst, v_dst, k_sem, v_sem):
        rows = pl.ds(j * block_rows, block_rows)
        return [pltpu.make_async_copy(kc_hbm.at[stream, rows, :], k_dst, k_sem),
                pltpu.make_async_copy(vc_hbm.at[stream, rows, :], v_dst, v_sem)]

    def near_copies(at_step, s):
        return [c for i in range(n) for c in cache_copies(
            at_step * n + i, n_cache - 1, knear_ref.at[s, i], vnear_ref.at[s, i],
            near_sems.at[0, s, i], near_sems.at[1, s, i])]

    @pl.when(step == 0)
    def _():
        for c in near_copies(0, 0):
            c.start()

    @pl.when(step + 1 < pl.num_programs(0))
    def _():
        for c in near_copies(step + 1, 1 - slot):
            c.start()

    qnt_ref[...] = jnp.zeros(qnt_ref.shape, BF16)
    for i in range(n):
        for h in range(NUM_GROUPS):
            p, half = divmod(h, 2)
            qnt_ref[i, p, h * t:(h + 1) * t, half * HEAD_DIM:(half + 1) * HEAD_DIM] = (
                q_ref[i * t:(i + 1) * t, h * HEAD_DIM:(h + 1) * HEAD_DIM])

    def neg_logits(i, load_k):
        zn = None
        for p in range(pairs):
            lhs = jnp.concatenate([load_k(2 * p), load_k(2 * p + 1)], axis=1).astype(BF16)
            d = lax.dot_general(lhs, qnt_ref[i, p], nt_dims, preferred_element_type=F32)
            zn = d if zn is None else zn + d
        return zn

    def walk(work):
        pieces = [[_split_bf16(_log2_one_minus_beta(zn, m), axis=0) for zn, _, _, m in blocks]
                  for blocks, _, _ in work]
        suffixes = [[jnp.dot(jnp.concatenate([tr, tr], axis=1), p, preferred_element_type=F32)
                     for (_, _, tr, _), p in zip(blocks, ps)]
                    for (blocks, _, _), ps in zip(work, pieces)]
        done = []
        for (blocks, carry, accs), sfxs in zip(work, suffixes):
            for (zn, load_v, _, m), sfx in zip(blocks, sfxs):
                w = jnp.exp2(sfx - zn + carry)
                if m is not None:
                    w = jnp.where(m, w, 0.0)
                w_t = w.T
                accs = [acc + jnp.dot(w_t[h * t:(h + 1) * t, :].astype(BF16),
                                      load_v(h).astype(BF16), preferred_element_type=F32)
                        for h, acc in enumerate(accs)]
                carry = carry + sfx[0:1, :]
            done.append((carry, tuple(accs)))
        return done

    head_rows = lambda ref, base, rows, h: ref[pl.ds(base + h, rows, stride=NUM_GROUPS), :]

    key_idx = lax.broadcasted_iota(jnp.int32, (pad, lanes), 0)
    query_idx = lax.broadcasted_iota(jnp.int32, (pad, lanes), 1) & (t - 1)
    zero_rows = lambda width: jnp.zeros((pad - t, width), F32)
    new_base = lambda i: i * t * NUM_GROUPS
    new_blocks = [
        (jnp.concatenate([neg_logits(i, lambda h, i=i: head_rows(kn_ref, new_base(i), t, h)),
                          zero_rows(lanes)], axis=0),
         lambda h, i=i: jnp.concatenate([head_rows(vn_ref, new_base(i), t, h),
                                         zero_rows(HEAD_DIM)], axis=0),
         trit_ref[0:pad, 0:pad], key_idx < query_idx) for i in range(n)]
    for c in near_copies(step, slot):
        c.wait()
    near_blocks = [
        (neg_logits(i, lambda h, i=i: head_rows(knear_ref.at[slot, i], 0, tk, h)),
         lambda h, i=i: head_rows(vnear_ref.at[slot, i], 0, tk, h), trit_ref[...], None)
        for i in range(n)]
    state = walk([([new_blocks[i], near_blocks[i]], jnp.zeros((1, lanes), F32),
                   [jnp.zeros((t, HEAD_DIM), F32)] * NUM_GROUPS) for i in range(n)])

    def any_live(state):
        return jnp.max(functools.reduce(jnp.maximum, [c for c, _ in state])) > EXP2_UNDERFLOW

    first_far = n_cache - 2
    far_slot = lambda j: (first_far - j) % 2

    def far_copies(j):
        s = far_slot(j)
        return [c for i in range(n) for c in cache_copies(
            step * n + i, j, kfar_ref.at[s, i], vfar_ref.at[s, i],
            far_sems.at[0, s, i], far_sems.at[1, s, i])]

    def start_if(cond, j):
        @pl.when(cond)
        def _():
            for c in far_copies(j):
                c.start()

    def wait_if(cond, j):
        @pl.when(cond)
        def _():
            for c in far_copies(j):
                c.wait()

    def older(loop_state):
        j, _, state = loop_state
        s = far_slot(j)
        for c in far_copies(j):
            c.wait()
        start_if(j >= 1, j - 1)
        far_blocks = [(neg_logits(i, lambda h, i=i: head_rows(kfar_ref.at[s, i], 0, tk, h)),
                       lambda h, i=i: head_rows(vfar_ref.at[s, i], 0, tk, h), trit_ref[...], None)
                      for i in range(n)]
        state = walk([([far_blocks[i]], state[i][0], list(state[i][1])) for i in range(n)])
        return j - 1, any_live(state), tuple(state)

    live = any_live(state)
    if first_far >= 0:
        start_if(live, first_far)
    j_end, _, state = lax.while_loop(lambda s: jnp.logical_and(s[0] >= 0, s[1]), older,
                                     (jnp.int32(first_far), live, tuple(state)))
    if first_far >= 0:
        wait_if(jnp.logical_and(live, j_end >= 0), j_end)

    for i, (_, accs) in enumerate(state):
        for h, acc in enumerate(accs):
            rows, hs = slice(i * t, (i + 1) * t), slice(h * HEAD_DIM, (h + 1) * HEAD_DIM)
            o_ref[rows, hs] = (acc * sg_ref[rows, hs].astype(F32)).astype(BF16)


def _attend_sample(q, k_new, v_new, cache_k, cache_v, sg, tri_t, *, batch, t):
    width = q.shape[1]
    tk = tri_t.shape[0]
    block_rows = tk * NUM_GROUPS
    assert t & (t - 1) == 0 and NUM_GROUPS * t == tk and cache_k.shape[1] % block_rows == 0
    n = SAMPLE_STREAMS_PER_STEP
    assert batch % n == 0
    rowblock = pl.BlockSpec((n * t, width), lambda b: (b, 0))
    newblock = pl.BlockSpec((n * t * NUM_GROUPS, HEAD_DIM), lambda b: (b, 0))
    in_hbm = pl.BlockSpec(memory_space=pl.ANY)
    key_blocks = lambda *lead: pltpu.VMEM((*lead, n, block_rows, HEAD_DIM), F32)
    return pl.pallas_call(
        functools.partial(_attn_sample_kernel, t=t),
        grid=(batch // n,),
        in_specs=[rowblock, newblock, newblock, in_hbm, in_hbm, rowblock,
                  pl.BlockSpec(tri_t.shape, lambda b: (0, 0))],
        out_specs=rowblock,
        out_shape=jax.ShapeDtypeStruct(q.shape, BF16),
        scratch_shapes=[pltpu.VMEM((n, NUM_GROUPS // 2, NUM_GROUPS * t, 2 * HEAD_DIM), BF16),
                        key_blocks(2), key_blocks(2), key_blocks(2), key_blocks(2),
                        pltpu.SemaphoreType.DMA((2, 2, n)), pltpu.SemaphoreType.DMA((2, 2, n))],
        compiler_params=_compiler_params(("arbitrary",)),
        name="attn_sample",
    )(q, k_new, v_new, cache_k, cache_v, sg, tri_t)


def _out_kernel(x_ref, msb_ref, mcv_ref, wsb_ref, wcv_ref, g_ref, y_ref, *wcopy_refs):
    w_sb, w_cv = wsb_ref[...].astype(BF16), wcv_ref[...].astype(BF16)
    for copy_ref, w in zip(wcopy_refs, (w_sb, w_cv)):
        copy_ref[...] = w
    y = (x_ref[...]
         + jnp.dot(msb_ref[...], w_sb, preferred_element_type=F32)
         + jnp.dot(mcv_ref[...], w_cv, preferred_element_type=F32))
    y_ref[...] = _rmsnorm_rows(y, g_ref[...])


def _merge_out(x2d, mix_sb, mix_cv, weights, final_g, *, tm):
    m, d = x2d.shape
    half = mix_sb.shape[1]
    copy_weights = not isinstance(weights, (list, tuple))
    rows = lambda width: pl.BlockSpec((tm, width), lambda i: (i, 0))
    if copy_weights:
        w_args = [weights, weights]
        w_specs = [pl.BlockSpec((half, d), lambda i, r=r: (r, 0), pipeline_mode=pl.Buffered(1))
                   for r in range(2)]
        copy_specs = [pl.BlockSpec((half, d), lambda i: (0, 0), pipeline_mode=pl.Buffered(1))] * 2
        copy_shapes = [jax.ShapeDtypeStruct((half, d), BF16)] * 2
    else:
        w_args = list(weights)
        w_specs = [pl.BlockSpec((half, d), lambda i: (0, 0))] * 2
        copy_specs, copy_shapes = [], []
    y, *w_copies = pl.pallas_call(
        _out_kernel,
        grid=(m // tm,),
        in_specs=[rows(d), rows(half), rows(half), *w_specs, pl.BlockSpec((1, d), lambda i: (0, 0))],
        out_specs=[rows(d), *copy_specs],
        out_shape=[jax.ShapeDtypeStruct((m, d), F32), *copy_shapes],
        compiler_params=_compiler_params(("arbitrary",)),
        name="merge_out",
    )(x2d, mix_sb, mix_cv, *w_args, final_g.reshape(1, d))
    return y, w_copies


def _lower_tri(n):
    j = lax.broadcasted_iota(jnp.int32, (n, n), 0)
    s = lax.broadcasted_iota(jnp.int32, (n, n), 1)
    return (j >= s).astype(BF16)


def kernel(x_prompt, x_sample, cache_k, cache_v, state_conv, norm_g, w_in, conv_w, w_out, final_g):
    depth = w_in.shape[0]
    assert depth == 1, "single-layer step"
    bsz, seq, d = x_prompt.shape
    dbsz, dseq, _ = x_sample.shape
    past = cache_k.shape[2]
    width = NUM_GROUPS * HEAD_DIM

    tri = _lower_tri(KEY_BLOCK)

    xs = x_sample.reshape(dbsz * dseq, d)
    qs, ks, vs, _, _, sgs, mcvs, tails, w_in_bf = _project(
        xs, norm_g[0], w_in[0], conv_w[0], state_conv[0], seq_rows=dseq,
        tm=min(PROJ_ROWS, xs.shape[0]), heads_per_step=1)
    msbs = _attend_sample(qs, ks, vs,
                          cache_k[0].reshape(dbsz, past * NUM_GROUPS, HEAD_DIM),
                          cache_v[0].reshape(dbsz, past * NUM_GROUPS, HEAD_DIM),
                          sgs, tri.T, batch=dbsz, t=dseq)
    ys, w_out_bf = _merge_out(xs, msbs, mcvs, w_out[0], final_g, tm=OUT_ROWS)

    xp = x_prompt.reshape(bsz * seq, d)
    zeros_left = jnp.zeros((bsz, CONV_W - 1, width), F32)
    qp, kp, vp, kbp, vbp, sgp, mcvp, tailp, _ = _project(
        xp, norm_g[0], w_in_bf, conv_w[0], zeros_left, seq_rows=seq, tm=PROJ_ROWS,
        heads_per_step=PROMPT_PROJ_HEADS)
    yp = _attend_merge_prompt(qp, kbp, vbp, sgp, tri, xp, mcvp, w_out_bf, final_g,
                              batch=bsz, seq=seq, tq=KEY_BLOCK)

    heads = lambda a, b_, t_: a.reshape(1, b_, t_, NUM_GROUPS, HEAD_DIM)
    return (yp.reshape(bsz, seq, d), ys.reshape(dbsz, dseq, d),
            heads(kp, bsz, seq), heads(vp, bsz, seq), tailp[None],
            heads(ks, dbsz, dseq), heads(vs, dbsz, dseq), tails[None])
```

```python
import functools
import math

import jax
import jax.numpy as jnp
from jax import lax
from jax.experimental import pallas as pl
from jax.experimental.pallas import tpu as pltpu

F32 = jnp.float32
BF16 = jnp.bfloat16

HEAD_DIM = 128
NUM_GROUPS = 8
NUM_SEGMENTS = 8
CONV_W = 3
CONV_CTX = CONV_W - 1
assert CONV_W == 3, "the conv epilogue writes its three taps out explicitly"
SUBLANES = 8
EPS = 1e-6
LOG2E = math.log2(math.e)
Q_PRESCALE = -(HEAD_DIM ** -0.5) * LOG2E
MXU_DEPTH = 256
KEY_BLOCK = MXU_DEPTH
PROJ_ROWS = 1024
OUT_ROWS = 512
PROMPT_PROJ_HEADS = 2
SAMPLE_STREAMS_PER_STEP = 4
EXP2_UNDERFLOW = -160.0

VMEM_LIMIT_BYTES = 56 * 1024 * 1024


def _compiler_params(semantics):
    return pltpu.CompilerParams(dimension_semantics=semantics,
                                vmem_limit_bytes=VMEM_LIMIT_BYTES)


def _rmsnorm_rows(x, g):
    r = lax.rsqrt(jnp.mean(x * x, axis=-1, keepdims=True) + EPS)
    return (x * r) * g


def _silu(x):
    return x * (1.0 / (1.0 + jnp.exp(-x)))


def _proj_kernel(*refs, seq_rows, tiles_per_seq, copy_weights):
    refs = list(refs)
    take = lambda count: [refs.pop(0) for _ in range(count)]
    x_hbm, g_ref = take(2)
    w_refs = take(NUM_SEGMENTS)
    cw_ref, left_ref = take(2)
    q_ref, k_ref, v_ref, kb_ref, vb_ref, sg_ref, mcv_ref, tail_ref = take(8)
    wcopy_refs = take(NUM_SEGMENTS) if copy_weights else ()
    x_ref, hn_ref, carry_ref, x_sem = take(4)
    i = pl.program_id(0)
    h = pl.program_id(1)
    tm = x_ref.shape[0]

    def x_copy(tile):
        return pltpu.make_async_copy(x_hbm.at[pl.ds(pl.multiple_of(tile * tm, tm), tm), :],
                                     x_ref, x_sem.at[0])

    @pl.when(jnp.logical_and(i == 0, h == 0))
    def _():
        x_copy(0).start()
        carry_ref[...] = jnp.zeros(carry_ref.shape, F32)

    @pl.when(h == 0)
    def _():
        x_copy(i).wait()
        hn_ref[...] = _rmsnorm_rows(x_ref[...], g_ref[...]).astype(BF16)

    @pl.when(jnp.logical_and(h == 0, i + 1 < pl.num_programs(0)))
    def _():
        x_copy(i + 1).start()

    w_blocks = [r[...].astype(BF16) for r in w_refs]
    for copy_ref, w in zip(wcopy_refs, w_blocks):
        copy_ref[...] = w

    def project(*ws):
        acc = jnp.dot(hn_ref[...], jnp.concatenate(ws, axis=1), preferred_element_type=F32)
        return [acc[:, s * HEAD_DIM:(s + 1) * HEAD_DIM] for s in range(len(ws))]

    rows = seq_rows if tiles_per_seq == 1 else tm
    nseq = tm // rows
    expand = lambda a: jnp.broadcast_to(a, (nseq, rows, HEAD_DIM)).reshape(tm, HEAD_DIM)
    rs = lax.broadcasted_iota(jnp.int32, (tm, HEAD_DIM), 0) & (rows - 1)

    for a in range(q_ref.shape[1] // HEAD_DIM):
        head = h * (q_ref.shape[1] // HEAD_DIM) + a
        cols = slice(a * HEAD_DIM, (a + 1) * HEAD_DIM)
        wq, wk, wv, wgs, wb, wc, wu, wgc = [w[:, cols] for w in w_blocks]

        c, u = project(wc, wu)
        b, g_cv = project(wb, wgc)
        k, v = project(wk, wv)
        q, g_sb = project(wq, wgs)

        q_ref[:, cols] = (q * Q_PRESCALE).astype(BF16)
        k_ref[pl.ds(head, tm, stride=NUM_GROUPS), :] = k
        v_ref[pl.ds(head, tm, stride=NUM_GROUPS), :] = v
        kb_ref[:, cols] = k.astype(BF16)
        vb_ref[:, cols] = v.astype(BF16)
        sg_ref[:, cols] = _silu(g_sb).astype(BF16)

        cu = c * u
        if tiles_per_seq > 1:
            left = jnp.where(i % tiles_per_seq == 0, left_ref[:, :, cols],
                             carry_ref[head][None, 0:CONV_CTX, :])
            carry_ref[head, 0:CONV_CTX, :] = cu[tm - CONV_CTX:tm, :]
        else:
            left = left_ref[:, :, cols]
        l0 = expand(left[:, 0:1, :])
        l1 = expand(left[:, 1:2, :])
        r1 = jnp.where(rs == 0, l1, pltpu.roll(cu, 1, axis=0))
        r2 = jnp.where(rs == 0, l0, jnp.where(rs == 1, l1, pltpu.roll(cu, 2, axis=0)))
        cw = cw_ref[:, cols]
        conv = cw[0:1, :] * r2 + cw[1:2, :] * r1 + cw[2:3, :] * cu
        mcv_ref[:, cols] = (b * conv * _silu(g_cv)).astype(BF16)
        tail_ref[:, :, cols] = cu.reshape(nseq, rows, HEAD_DIM)[:, rows - CONV_CTX:rows, :]


def _project(x2d, norm_g, weights, conv_w, left, *, seq_rows, tm, heads_per_step):
    m, d = x2d.shape
    width = NUM_GROUPS * HEAD_DIM
    cols = heads_per_step * HEAD_DIM
    steps = NUM_GROUPS // heads_per_step
    copy_weights = not isinstance(weights, (list, tuple))
    if copy_weights:
        assert m == tm, "bf16 weight copies are written once, by a single row tile"
        w_args = [weights] * NUM_SEGMENTS
        segment = lambda s: pl.BlockSpec((d, cols), lambda i, h: (0, s * steps + h))
    else:
        w_args = list(weights)
        segment = lambda s: pl.BlockSpec((d, cols), lambda i, h: (0, h))
    once = dict(pipeline_mode=pl.Buffered(1)) if m == tm else {}
    if tm >= seq_rows:
        tiles_per_seq, ns = 1, tm // seq_rows
        left_idx = lambda i, h: (i, 0, h)
    else:
        tiles_per_seq, ns = seq_rows // tm, 1
        left_idx = lambda i, h: (i // tiles_per_seq, 0, h)
    n_tails = (m // tm) * ns
    tile = lambda dt: jax.ShapeDtypeStruct((m, width), dt)
    native = jax.ShapeDtypeStruct((m * NUM_GROUPS, HEAD_DIM), F32)
    col_block = pl.BlockSpec((tm, cols), lambda i, h: (i, h))
    native_block = pl.BlockSpec((tm * NUM_GROUPS, HEAD_DIM), lambda i, h: (i, 0), **once)
    copy_specs, copy_shapes = [], []
    if copy_weights:
        copy_specs = [pl.BlockSpec((d, cols), lambda i, h: (0, h))] * NUM_SEGMENTS
        copy_shapes = [jax.ShapeDtypeStruct((d, width), BF16)] * NUM_SEGMENTS
    kern = functools.partial(_proj_kernel, seq_rows=seq_rows, tiles_per_seq=tiles_per_seq,
                             copy_weights=copy_weights)
    q, k, v, kb, vb, sg, mcv, tails, *w_copies = pl.pallas_call(
        kern,
        grid=(m // tm, steps),
        in_specs=[
            pl.BlockSpec(memory_space=pl.ANY),
            pl.BlockSpec((1, d), lambda i, h: (0, 0)),
            *[segment(s) for s in range(NUM_SEGMENTS)],
            pl.BlockSpec((CONV_W, cols), lambda i, h: (0, h)),
            pl.BlockSpec((ns, CONV_CTX, cols), left_idx),
        ],
        out_specs=[col_block, native_block, native_block, col_block, col_block, col_block,
                   col_block, pl.BlockSpec((ns, CONV_CTX, cols), lambda i, h: (i, 0, h)),
                   *copy_specs],
        out_shape=[tile(BF16), native, native, tile(BF16), tile(BF16), tile(BF16), tile(BF16),
                   jax.ShapeDtypeStruct((n_tails, CONV_CTX, width), F32), *copy_shapes],
        scratch_shapes=[pltpu.VMEM((tm, d), F32), pltpu.VMEM((tm, d), BF16),
                        pltpu.VMEM((NUM_GROUPS, SUBLANES, HEAD_DIM), F32),
                        pltpu.SemaphoreType.DMA((1,))],
        compiler_params=_compiler_params(("arbitrary", "arbitrary")),
        name="proj",
    )(x2d, norm_g.reshape(1, d), *w_args, conv_w, left)
    tails = tails.reshape(-1, tiles_per_seq, CONV_CTX, width)[:, tiles_per_seq - 1]
    return q, k, v, kb, vb, sg, mcv, tails, w_copies


def _log2_one_minus_beta(zn, mask):
    softplus2 = jnp.log(1.0 + jnp.exp2(-jnp.abs(zn))) * LOG2E
    log_1m = jnp.minimum(zn, 0.0) - softplus2
    return log_1m if mask is None else jnp.where(mask, log_1m, 0.0)


def _split_bf16(x, axis):
    hi = x.astype(BF16)
    return jnp.concatenate([hi, (x - hi.astype(F32)).astype(BF16)], axis=axis)


def _block_weights(suffix, zn, mask):
    w = jnp.exp2(suffix - zn)
    return (w if mask is None else jnp.where(mask, w, 0.0)).astype(BF16)


def _attn_out_prompt_kernel(q_ref, k_ref, v_ref, sg_ref, tri_ref, x_ref, mcv_ref, wsb_ref, wcv_ref,
                            g_ref, y_ref, acc_ref, mix_ref, *, blocks_per_seq):
    grid_step = pl.program_id(0)
    last_step = pl.num_programs(0) - 1
    qi = grid_step % blocks_per_seq

    tq = q_ref.shape[0]
    tk = tri_ref.shape[0]
    heads = q_ref.shape[1] // HEAD_DIM
    tri2 = jnp.concatenate([tri_ref[...]] * 2, axis=0)
    hs = [slice(a * HEAD_DIM, (a + 1) * HEAD_DIM) for a in range(heads)]
    nt_dims = (((1,), (1,)), ((), ()))

    def rows_of(j):
        return pl.ds(j * tk if isinstance(j, int) else pl.multiple_of(j * tk, tk), tk)

    def logits(j):
        return tuple(lax.dot_general(q_ref[:, s], k_ref[rows_of(j), s], nt_dims,
                                     preferred_element_type=F32) for s in hs)

    def weights(zns, mask):
        pieces = [_split_bf16(_log2_one_minus_beta(zn, mask), axis=1) for zn in zns]
        suffixes = [jnp.dot(p, tri2, preferred_element_type=F32) for p in pieces]
        ws = tuple(_block_weights(sfx, zn, mask) for sfx, zn in zip(suffixes, zns))
        return ws, tuple(sfx[:, 0:1] for sfx in suffixes)

    def accumulate(ws, j, scales):
        for a, s in enumerate(hs):
            acc_ref[:, s] += scales[a] * jnp.dot(ws[a], v_ref[rows_of(j), s],
                                                 preferred_element_type=F32)

    def step(j, carries, mask):
        ws, totals = weights(logits(j), mask)
        accumulate(ws, j, [jnp.exp2(c) for c in carries])
        return tuple(c + tot for c, tot in zip(carries, totals))

    project_sb = lambda: jnp.dot(mix_ref[...], wsb_ref[...], preferred_element_type=F32)
    project_cv = lambda: jnp.dot(mcv_ref[...], wcv_ref[...], preferred_element_type=F32)

    def finish_projection(out_sb, out_cv):
        y_ref[...] = _rmsnorm_rows(x_ref[...] + out_sb + out_cv, g_ref[...])

    def any_live(carries):
        return jnp.max(functools.reduce(jnp.maximum, carries)) > EXP2_UNDERFLOW

    def attend(with_projection):
        rows = lax.broadcasted_iota(jnp.int32, (tq, tk), 0)
        cols = lax.broadcasted_iota(jnp.int32, (tq, tk), 1)
        causal = cols < rows
        zns = logits(qi)
        out_sb = project_sb() if with_projection else None
        pieces = [_split_bf16(_log2_one_minus_beta(zn, causal), axis=1) for zn in zns]
        suffixes = [jnp.dot(p, tri2, preferred_element_type=F32) for p in pieces]
        out_cv = project_cv() if with_projection else None
        ws = [_block_weights(sfx, zn, causal) for sfx, zn in zip(suffixes, zns)]
        for w, s in zip(ws, hs):
            acc_ref[:, s] = jnp.dot(w, v_ref[rows_of(qi), s], preferred_element_type=F32)
        if with_projection:
            finish_projection(out_sb, out_cv)
        carries = tuple(sfx[:, 0:1] for sfx in suffixes)

        def body(state):
            t, _, carries = state
            carries = step(qi - 1 - t, carries, None)
            return t + 1, any_live(carries), carries

        lax.while_loop(lambda state: jnp.logical_and(state[0] < qi, state[1]), body,
                       (jnp.int32(0), any_live(carries), carries))
        mix_ref[...] = (acc_ref[...] * sg_ref[...].astype(F32)).astype(BF16)

    pl.when(grid_step == 0)(lambda: attend(False))
    pl.when(jnp.logical_and(grid_step > 0, grid_step < last_step))(lambda: attend(True))
    pl.when(grid_step == last_step)(lambda: finish_projection(project_sb(), project_cv()))


def _attend_merge_prompt(q, kb, vb, sg, tri, x2d, mix_cv, w_halves, final_g, *, batch, seq, tq):
    nq = seq // tq
    n_blocks = batch * nq
    m, d = x2d.shape
    width = q.shape[1]
    assert tq == tri.shape[0], "query block and key block share the diagonal mask"
    attended = lambda s: jnp.minimum(s, n_blocks - 1)
    projected = lambda s: jnp.maximum(s - 1, 0)
    qblock = pl.BlockSpec((tq, width), lambda s: (attended(s), 0))
    kvblock = pl.BlockSpec((seq, width), lambda s: (attended(s) // nq, 0))
    const = lambda shape: pl.BlockSpec(shape, lambda s: (0, 0), pipeline_mode=pl.Buffered(1))
    return pl.pallas_call(
        functools.partial(_attn_out_prompt_kernel, blocks_per_seq=nq),
        grid=(n_blocks + 1,),
        in_specs=[qblock, kvblock, kvblock, qblock, const(tri.shape),
                  pl.BlockSpec((tq, d), lambda s: (projected(s), 0)),
                  pl.BlockSpec((tq, width), lambda s: (projected(s), 0)),
                  const(w_halves[0].shape), const(w_halves[1].shape), const((1, d))],
        out_specs=pl.BlockSpec((tq, d), lambda s: (projected(s), 0)),
        out_shape=jax.ShapeDtypeStruct((m, d), F32),
        scratch_shapes=[pltpu.VMEM((tq, width), F32), pltpu.VMEM((tq, width), BF16)],
        compiler_params=_compiler_params(("arbitrary",)),
        name="attn_out_prompt",
    )(q, kb, vb, sg, tri, x2d, mix_cv, w_halves[0], w_halves[1], final_g.reshape(1, d))


def _attn_sample_kernel(q_ref, kn_ref, vn_ref, kc_hbm, vc_hbm, sg_ref, trit_ref, o_ref,
                        qnt_ref, knear_ref, vnear_ref, kfar_ref, vfar_ref, near_sems, far_sems,
                        *, t):
    step = pl.program_id(0)
    n = q_ref.shape[0] // t
    tk = trit_ref.shape[0]
    block_rows = tk * NUM_GROUPS
    n_cache = kc_hbm.shape[1] // block_rows
    lanes = NUM_GROUPS * t
    pairs = NUM_GROUPS // 2
    pad = HEAD_DIM
    nt_dims = (((1,), (1,)), ((), ()))
    slot = step % 2

    def cache_copies(stream, j, k_dst, v_dst, k_sem, v_sem):
        rows = pl.ds(j * block_rows, block_rows)
        return [pltpu.make_async_copy(kc_hbm.at[stream, rows, :], k_dst, k_sem),
                pltpu.make_async_copy(vc_hbm.at[stream, rows, :], v_dst, v_sem)]

    def near_copies(at_step, s):
        return [c for i in range(n) for c in cache_copies(
            at_step * n + i, n_cache - 1, knear_ref.at[s, i], vnear_ref.at[s, i],
            near_sems.at[0, s, i], near_sems.at[1, s, i])]

    @pl.when(step == 0)
    def _():
        for c in near_copies(0, 0):
            c.start()

    @pl.when(step + 1 < pl.num_programs(0))
    def _():
        for c in near_copies(step + 1, 1 - slot):
            c.start()

    qnt_ref[...] = jnp.zeros(qnt_ref.shape, BF16)
    for i in range(n):
        for h in range(NUM_GROUPS):
            p, half = divmod(h, 2)
            qnt_ref[i, p, h * t:(h + 1) * t, half * HEAD_DIM:(half + 1) * HEAD_DIM] = (
                q_ref[i * t:(i + 1) * t, h * HEAD_DIM:(h + 1) * HEAD_DIM])

    def neg_logits(i, load_k):
        zn = None
        for p in range(pairs):
            lhs = jnp.concatenate([load_k(2 * p), load_k(2 * p + 1)], axis=1).astype(BF16)
            d = lax.dot_general(lhs, qnt_ref[i, p], nt_dims, preferred_element_type=F32)
            zn = d if zn is None else zn + d
        return zn

    def walk(work):
        pieces = [[_split_bf16(_log2_one_minus_beta(zn, m), axis=0) for zn, _, _, m in blocks]
                  for blocks, _, _ in work]
        suffixes = [[jnp.dot(jnp.concatenate([tr, tr], axis=1), p, preferred_element_type=F32)
                     for (_, _, tr, _), p in zip(blocks, ps)]
                    for (blocks, _, _), ps in zip(work, pieces)]
        done = []
        for (blocks, carry, accs), sfxs in zip(work, suffixes):
            for (zn, load_v, _, m), sfx in zip(blocks, sfxs):
                w = jnp.exp2(sfx - zn + carry)
                if m is not None:
                    w = jnp.where(m, w, 0.0)
                w_t = w.T
                accs = [acc + jnp.dot(w_t[h * t:(h + 1) * t, :].astype(BF16),
                                      load_v(h).astype(BF16), preferred_element_type=F32)
                        for h, acc in enumerate(accs)]
                carry = carry + sfx[0:1, :]
            done.append((carry, tuple(accs)))
        return done

    head_rows = lambda ref, base, rows, h: ref[pl.ds(base + h, rows, stride=NUM_GROUPS), :]

    key_idx = lax.broadcasted_iota(jnp.int32, (pad, lanes), 0)
    query_idx = lax.broadcasted_iota(jnp.int32, (pad, lanes), 1) & (t - 1)
    zero_rows = lambda width: jnp.zeros((pad - t, width), F32)
    new_base = lambda i: i * t * NUM_GROUPS
    new_blocks = [
        (jnp.concatenate([neg_logits(i, lambda h, i=i: head_rows(kn_ref, new_base(i), t, h)),
                          zero_rows(lanes)], axis=0),
         lambda h, i=i: jnp.concatenate([head_rows(vn_ref, new_base(i), t, h),
                                         zero_rows(HEAD_DIM)], axis=0),
         trit_ref[0:pad, 0:pad], key_idx < query_idx) for i in range(n)]
    for c in near_copies(step, slot):
        c.wait()
    near_blocks = [
        (neg_logits(i, lambda h, i=i: head_rows(knear_ref.at[slot, i], 0, tk, h)),
         lambda h, i=i: head_rows(vnear_ref.at[slot, i], 0, tk, h), trit_ref[...], None)
        for i in range(n)]
    state = walk([([new_blocks[i], near_blocks[i]], jnp.zeros((1, lanes), F32),
                   [jnp.zeros((t, HEAD_DIM), F32)] * NUM_GROUPS) for i in range(n)])

    def any_live(state):
        return jnp.max(functools.reduce(jnp.maximum, [c for c, _ in state])) > EXP2_UNDERFLOW

    first_far = n_cache - 2
    far_slot = lambda j: (first_far - j) % 2

    def far_copies(j):
        s = far_slot(j)
        return [c for i in range(n) for c in cache_copies(
            step * n + i, j, kfar_ref.at[s, i], vfar_ref.at[s, i],
            far_sems.at[0, s, i], far_sems.at[1, s, i])]

    def start_if(cond, j):
        @pl.when(cond)
        def _():
            for c in far_copies(j):
                c.start()

    def wait_if(cond, j):
        @pl.when(cond)
        def _():
            for c in far_copies(j):
                c.wait()

    def older(loop_state):
        j, _, state = loop_state
        s = far_slot(j)
        for c in far_copies(j):
            c.wait()
        start_if(j >= 1, j - 1)
        far_blocks = [(neg_logits(i, lambda h, i=i: head_rows(kfar_ref.at[s, i], 0, tk, h)),
                       lambda h, i=i: head_rows(vfar_ref.at[s, i], 0, tk, h), trit_ref[...], None)
                      for i in range(n)]
        state = walk([([far_blocks[i]], state[i][0], list(state[i][1])) for i in range(n)])
        return j - 1, any_live(state), tuple(state)

    live = any_live(state)
    if first_far >= 0:
        start_if(live, first_far)
    j_end, _, state = lax.while_loop(lambda s: jnp.logical_and(s[0] >= 0, s[1]), older,
                                     (jnp.int32(first_far), live, tuple(state)))
    if first_far >= 0:
        wait_if(jnp.logical_and(live, j_end >= 0), j_end)

    for i, (_, accs) in enumerate(state):
        for h, acc in enumerate(accs):
            rows, hs = slice(i * t, (i + 1) * t), slice(h * HEAD_DIM, (h + 1) * HEAD_DIM)
            o_ref[rows, hs] = (acc * sg_ref[rows, hs].astype(F32)).astype(BF16)


def _attend_sample(q, k_new, v_new, cache_k, cache_v, sg, tri_t, *, batch, t):
    width = q.shape[1]
    tk = tri_t.shape[0]
    block_rows = tk * NUM_GROUPS
    assert t & (t - 1) == 0 and NUM_GROUPS * t == tk and cache_k.shape[1] % block_rows == 0
    n = SAMPLE_STREAMS_PER_STEP
    assert batch % n == 0
    rowblock = pl.BlockSpec((n * t, width), lambda b: (b, 0))
    newblock = pl.BlockSpec((n * t * NUM_GROUPS, HEAD_DIM), lambda b: (b, 0))
    in_hbm = pl.BlockSpec(memory_space=pl.ANY)
    key_blocks = lambda *lead: pltpu.VMEM((*lead, n, block_rows, HEAD_DIM), F32)
    return pl.pallas_call(
        functools.partial(_attn_sample_kernel, t=t),
        grid=(batch // n,),
        in_specs=[rowblock, newblock, newblock, in_hbm, in_hbm, rowblock,
                  pl.BlockSpec(tri_t.shape, lambda b: (0, 0))],
        out_specs=rowblock,
        out_shape=jax.ShapeDtypeStruct(q.shape, BF16),
        scratch_shapes=[pltpu.VMEM((n, NUM_GROUPS // 2, NUM_GROUPS * t, 2 * HEAD_DIM), BF16),
                        key_blocks(2), key_blocks(2), key_blocks(2), key_blocks(2),
                        pltpu.SemaphoreType.DMA((2, 2, n)), pltpu.SemaphoreType.DMA((2, 2, n))],
        compiler_params=_compiler_params(("arbitrary",)),
        name="attn_sample",
    )(q, k_new, v_new, cache_k, cache_v, sg, tri_t)


def _out_kernel(x_ref, msb_ref, mcv_ref, wsb_ref, wcv_ref, g_ref, y_ref, *wcopy_refs):
    w_sb, w_cv = wsb_ref[...].astype(BF16), wcv_ref[...].astype(BF16)
    for copy_ref, w in zip(wcopy_refs, (w_sb, w_cv)):
        copy_ref[...] = w
    y = (x_ref[...]
         + jnp.dot(msb_ref[...], w_sb, preferred_element_type=F32)
         + jnp.dot(mcv_ref[...], w_cv, preferred_element_type=F32))
    y_ref[...] = _rmsnorm_rows(y, g_ref[...])


def _merge_out(x2d, mix_sb, mix_cv, weights, final_g, *, tm):
    m, d = x2d.shape
    half = mix_sb.shape[1]
    rows = lambda width: pl.BlockSpec((tm, width), lambda i: (i, 0))
    w_args = [weights, weights]
    w_specs = [pl.BlockSpec((half, d), lambda i, r=r: (r, 0), pipeline_mode=pl.Buffered(1))
               for r in range(2)]
    copy_specs = [pl.BlockSpec((half, d), lambda i: (0, 0), pipeline_mode=pl.Buffered(1))] * 2
    copy_shapes = [jax.ShapeDtypeStruct((half, d), BF16)] * 2
    y, *w_copies = pl.pallas_call(
        _out_kernel,
        grid=(m // tm,),
        in_specs=[rows(d), rows(half), rows(half), *w_specs, pl.BlockSpec((1, d), lambda i: (0, 0))],
        out_specs=[rows(d), *copy_specs],
        out_shape=[jax.ShapeDtypeStruct((m, d), F32), *copy_shapes],
        compiler_params=_compiler_params(("arbitrary",)),
        name="merge_out",
    )(x2d, mix_sb, mix_cv, *w_args, final_g.reshape(1, d))
    return y, w_copies


def _lower_tri(n):
    j = lax.broadcasted_iota(jnp.int32, (n, n), 0)
    s = lax.broadcasted_iota(jnp.int32, (n, n), 1)
    return (j >= s).astype(BF16)


def kernel(x_prompt, x_sample, cache_k, cache_v, state_conv, norm_g, w_in, conv_w, w_out, final_g):
    depth = w_in.shape[0]
    assert depth == 1, "single-layer step"
    bsz, seq, d = x_prompt.shape
    dbsz, dseq, _ = x_sample.shape
    past = cache_k.shape[2]
    width = NUM_GROUPS * HEAD_DIM

    tri = _lower_tri(KEY_BLOCK)

    xs = x_sample.reshape(dbsz * dseq, d)
    qs, ks, vs, _, _, sgs, mcvs, tails, w_in_bf = _project(
        xs, norm_g[0], w_in[0], conv_w[0], state_conv[0], seq_rows=dseq,
        tm=min(PROJ_ROWS, xs.shape[0]), heads_per_step=1)
    msbs = _attend_sample(qs, ks, vs,
                          cache_k[0].reshape(dbsz, past * NUM_GROUPS, HEAD_DIM),
                          cache_v[0].reshape(dbsz, past * NUM_GROUPS, HEAD_DIM),
                          sgs, tri.T, batch=dbsz, t=dseq)
    ys, w_out_bf = _merge_out(xs, msbs, mcvs, w_out[0], final_g, tm=OUT_ROWS)

    xp = x_prompt.reshape(bsz * seq, d)
    zeros_left = jnp.zeros((bsz, CONV_W - 1, width), F32)
    qp, kp, vp, kbp, vbp, sgp, mcvp, tailp, _ = _project(
        xp, norm_g[0], w_in_bf, conv_w[0], zeros_left, seq_rows=seq, tm=PROJ_ROWS,
        heads_per_step=PROMPT_PROJ_HEADS)
    yp = _attend_merge_prompt(qp, kbp, vbp, sgp, tri, xp, mcvp, w_out_bf, final_g,
                              batch=bsz, seq=seq, tq=KEY_BLOCK)

    heads = lambda a, b_, t_: a.reshape(1, b_, t_, NUM_GROUPS, HEAD_DIM)
    return (yp.reshape(bsz, seq, d), ys.reshape(dbsz, dseq, d),
            heads(kp, bsz, seq), heads(vp, bsz, seq), tailp[None],
            heads(ks, dbsz, dseq), heads(vs, dbsz, dseq), tails[None])
```

```python
import functools
import math

import jax
import jax.numpy as jnp
from jax import lax
from jax.experimental import pallas as pl
from jax.experimental.pallas import tpu as pltpu

F32 = jnp.float32
BF16 = jnp.bfloat16

HEAD_DIM = 128
NUM_GROUPS = 8
NUM_SEGMENTS = 8
CONV_W = 3
CONV_CTX = CONV_W - 1
assert CONV_W == 3, "the conv epilogue writes its three taps out explicitly"
SUBLANES = 8
EPS = 1e-6
LOG2E = math.log2(math.e)
Q_PRESCALE = -(HEAD_DIM ** -0.5) * LOG2E
MXU_DEPTH = 256
KEY_BLOCK = MXU_DEPTH
PROJ_ROWS = 1024
OUT_ROWS = 512
PROMPT_PROJ_HEADS = 2
SAMPLE_STREAMS_PER_STEP = 4
EXP2_UNDERFLOW = -160.0

VMEM_LIMIT_BYTES = 56 * 1024 * 1024


def _compiler_params(semantics):
    return pltpu.CompilerParams(dimension_semantics=semantics,
                                vmem_limit_bytes=VMEM_LIMIT_BYTES)


def _rmsnorm_rows(x, g):
    r = lax.rsqrt(jnp.mean(x * x, axis=-1, keepdims=True) + EPS)
    return (x * r) * g


def _silu(x):
    return x * (1.0 / (1.0 + jnp.exp(-x)))


def _proj_kernel(*refs, seq_rows, tiles_per_seq, copy_weights):
    refs = list(refs)
    take = lambda count: [refs.pop(0) for _ in range(count)]
    x_hbm, g_ref = take(2)
    w_refs = take(NUM_SEGMENTS)
    cw_ref, left_ref = take(2)
    q_ref, k_ref, v_ref, kb_ref, vb_ref, sg_ref, mcv_ref, tail_ref = take(8)
    wcopy_refs = take(NUM_SEGMENTS) if copy_weights else ()
    x_ref, hn_ref, carry_ref, x_sem = take(4)
    i = pl.program_id(0)
    h = pl.program_id(1)
    tm = x_ref.shape[0]

    def x_copy(tile):
        return pltpu.make_async_copy(x_hbm.at[pl.ds(pl.multiple_of(tile * tm, tm), tm), :],
                                     x_ref, x_sem.at[0])

    @pl.when(jnp.logical_and(i == 0, h == 0))
    def _():
        x_copy(0).start()
        carry_ref[...] = jnp.zeros(carry_ref.shape, F32)

    @pl.when(h == 0)
    def _():
        x_copy(i).wait()
        hn_ref[...] = _rmsnorm_rows(x_ref[...], g_ref[...]).astype(BF16)

    @pl.when(jnp.logical_and(h == 0, i + 1 < pl.num_programs(0)))
    def _():
        x_copy(i + 1).start()

    w_blocks = [r[...].astype(BF16) for r in w_refs]
    for copy_ref, w in zip(wcopy_refs, w_blocks):
        copy_ref[...] = w

    def project(*ws):
        acc = jnp.dot(hn_ref[...], jnp.concatenate(ws, axis=1), preferred_element_type=F32)
        return [acc[:, s * HEAD_DIM:(s + 1) * HEAD_DIM] for s in range(len(ws))]

    rows = seq_rows if tiles_per_seq == 1 else tm
    nseq = tm // rows
    expand = lambda a: jnp.broadcast_to(a, (nseq, rows, HEAD_DIM)).reshape(tm, HEAD_DIM)
    rs = lax.broadcasted_iota(jnp.int32, (tm, HEAD_DIM), 0) & (rows - 1)

    for a in range(q_ref.shape[1] // HEAD_DIM):
        head = h * (q_ref.shape[1] // HEAD_DIM) + a
        cols = slice(a * HEAD_DIM, (a + 1) * HEAD_DIM)
        wq, wk, wv, wgs, wb, wc, wu, wgc = [w[:, cols] for w in w_blocks]

        c, u = project(wc, wu)
        b, g_cv = project(wb, wgc)
        k, v = project(wk, wv)
        q, g_sb = project(wq, wgs)

        q_ref[:, cols] = (q * Q_PRESCALE).astype(BF16)
        k_ref[pl.ds(head, tm, stride=NUM_GROUPS), :] = k
        v_ref[pl.ds(head, tm, stride=NUM_GROUPS), :] = v
        kb_ref[:, cols] = k.astype(BF16)
        vb_ref[:, cols] = v.astype(BF16)
        sg_ref[:, cols] = _silu(g_sb).astype(BF16)

        cu = c * u
        if tiles_per_seq > 1:
            left = jnp.where(i % tiles_per_seq == 0, left_ref[:, :, cols],
                             carry_ref[head][None, 0:CONV_CTX, :])
            carry_ref[head, 0:CONV_CTX, :] = cu[tm - CONV_CTX:tm, :]
        else:
            left = left_ref[:, :, cols]
        l0 = expand(left[:, 0:1, :])
        l1 = expand(left[:, 1:2, :])
        r1 = jnp.where(rs == 0, l1, pltpu.roll(cu, 1, axis=0))
        r2 = jnp.where(rs == 0, l0, jnp.where(rs == 1, l1, pltpu.roll(cu, 2, axis=0)))
        cw = cw_ref[:, cols]
        conv = cw[0:1, :] * r2 + cw[1:2, :] * r1 + cw[2:3, :] * cu
        mcv_ref[:, cols] = (b * conv * _silu(g_cv)).astype(BF16)
        tail_ref[:, :, cols] = cu.reshape(nseq, rows, HEAD_DIM)[:, rows - CONV_CTX:rows, :]


def _project(x2d, norm_g, weights, conv_w, left, *, seq_rows, tm, heads_per_step):
    m, d = x2d.shape
    width = NUM_GROUPS * HEAD_DIM
    cols = heads_per_step * HEAD_DIM
    steps = NUM_GROUPS // heads_per_step
    copy_weights = not isinstance(weights, (list, tuple))
    if copy_weights:
        assert m == tm, "bf16 weight copies are written once, by a single row tile"
        w_args = [weights] * NUM_SEGMENTS
        segment = lambda s: pl.BlockSpec((d, cols), lambda i, h: (0, s * steps + h))
    else:
        w_args = list(weights)
        segment = lambda s: pl.BlockSpec((d, cols), lambda i, h: (0, h))
    once = dict(pipeline_mode=pl.Buffered(1)) if m == tm else {}
    if tm >= seq_rows:
        tiles_per_seq, ns = 1, tm // seq_rows
        left_idx = lambda i, h: (i, 0, h)
    else:
        tiles_per_seq, ns = seq_rows // tm, 1
        left_idx = lambda i, h: (i // tiles_per_seq, 0, h)
    n_tails = (m // tm) * ns
    tile = lambda dt: jax.ShapeDtypeStruct((m, width), dt)
    native = jax.ShapeDtypeStruct((m * NUM_GROUPS, HEAD_DIM), F32)
    col_block = pl.BlockSpec((tm, cols), lambda i, h: (i, h))
    native_block = pl.BlockSpec((tm * NUM_GROUPS, HEAD_DIM), lambda i, h: (i, 0), **once)
    copy_specs, copy_shapes = [], []
    if copy_weights:
        copy_specs = [pl.BlockSpec((d, cols), lambda i, h: (0, h))] * NUM_SEGMENTS
        copy_shapes = [jax.ShapeDtypeStruct((d, width), BF16)] * NUM_SEGMENTS
    kern = functools.partial(_proj_kernel, seq_rows=seq_rows, tiles_per_seq=tiles_per_seq,
                             copy_weights=copy_weights)
    q, k, v, kb, vb, sg, mcv, tails, *w_copies = pl.pallas_call(
        kern,
        grid=(m // tm, steps),
        in_specs=[
            pl.BlockSpec(memory_space=pl.ANY),
            pl.BlockSpec((1, d), lambda i, h: (0, 0)),
            *[segment(s) for s in range(NUM_SEGMENTS)],
            pl.BlockSpec((CONV_W, cols), lambda i, h: (0, h)),
            pl.BlockSpec((ns, CONV_CTX, cols), left_idx),
        ],
        out_specs=[col_block, native_block, native_block, col_block, col_block, col_block,
                   col_block, pl.BlockSpec((ns, CONV_CTX, cols), lambda i, h: (i, 0, h)),
                   *copy_specs],
        out_shape=[tile(BF16), native, native, tile(BF16), tile(BF16), tile(BF16), tile(BF16),
                   jax.ShapeDtypeStruct((n_tails, CONV_CTX, width), F32), *copy_shapes],
        scratch_shapes=[pltpu.VMEM((tm, d), F32), pltpu.VMEM((tm, d), BF16),
                        pltpu.VMEM((NUM_GROUPS, SUBLANES, HEAD_DIM), F32),
                        pltpu.SemaphoreType.DMA((1,))],
        compiler_params=_compiler_params(("arbitrary", "arbitrary")),
        name="proj",
    )(x2d, norm_g.reshape(1, d), *w_args, conv_w, left)
    tails = tails.reshape(-1, tiles_per_seq, CONV_CTX, width)[:, tiles_per_seq - 1]
    return q, k, v, kb, vb, sg, mcv, tails, w_copies


def _log2_one_minus_beta(zn, mask):
    softplus2 = jnp.log(1.0 + jnp.exp2(-jnp.abs(zn))) * LOG2E
    log_1m = jnp.minimum(zn, 0.0) - softplus2
    return log_1m if mask is None else jnp.where(mask, log_1m, 0.0)


def _split_bf16(x, axis):
    hi = x.astype(BF16)
    return jnp.concatenate([hi, (x - hi.astype(F32)).astype(BF16)], axis=axis)


def _block_weights(suffix, zn, mask):
    w = jnp.exp2(suffix - zn)
    return (w if mask is None else jnp.where(mask, w, 0.0)).astype(BF16)


def _attn_out_prompt_kernel(q_ref, k_ref, v_ref, sg_ref, tri_ref, x_ref, mcv_ref, wsb_ref, wcv_ref,
                            g_ref, y_ref, acc_ref, mix_ref, *, blocks_per_seq):
    grid_step = pl.program_id(0)
    last_step = pl.num_programs(0) - 1
    qi = grid_step % blocks_per_seq

    tq = q_ref.shape[0]
    tk = tri_ref.shape[0]
    heads = q_ref.shape[1] // HEAD_DIM
    tri2 = jnp.concatenate([tri_ref[...]] * 2, axis=0)
    hs = [slice(a * HEAD_DIM, (a + 1) * HEAD_DIM) for a in range(heads)]
    nt_dims = (((1,), (1,)), ((), ()))

    def rows_of(j):
        return pl.ds(j * tk if isinstance(j, int) else pl.multiple_of(j * tk, tk), tk)

    def logits(j):
        return tuple(lax.dot_general(q_ref[:, s], k_ref[rows_of(j), s], nt_dims,
                                     preferred_element_type=F32) for s in hs)

    def weights(zns, mask):
        pieces = [_split_bf16(_log2_one_minus_beta(zn, mask), axis=1) for zn in zns]
        suffixes = [jnp.dot(p, tri2, preferred_element_type=F32) for p in pieces]
        ws = tuple(_block_weights(sfx, zn, mask) for sfx, zn in zip(suffixes, zns))
        return ws, tuple(sfx[:, 0:1] for sfx in suffixes)

    def accumulate(ws, j, scales):
        for a, s in enumerate(hs):
            acc_ref[:, s] += scales[a] * jnp.dot(ws[a], v_ref[rows_of(j), s],
                                                 preferred_element_type=F32)

    def step(j, carries, mask):
        ws, totals = weights(logits(j), mask)
        accumulate(ws, j, [jnp.exp2(c) for c in carries])
        return tuple(c + tot for c, tot in zip(carries, totals))

    def project():
        return jnp.dot(jnp.concatenate([mix_ref[...], mcv_ref[...]], axis=1),
                       jnp.concatenate([wsb_ref[...], wcv_ref[...]], axis=0),
                       preferred_element_type=F32)

    def finish_projection(out):
        y_ref[...] = _rmsnorm_rows(x_ref[...] + out, g_ref[...])

    def any_live(carries):
        return jnp.max(functools.reduce(jnp.maximum, carries)) > EXP2_UNDERFLOW

    def attend(with_projection):
        rows = lax.broadcasted_iota(jnp.int32, (tq, tk), 0)
        cols = lax.broadcasted_iota(jnp.int32, (tq, tk), 1)
        causal = cols < rows
        zns = logits(qi)
        out = project() if with_projection else None
        pieces = [_split_bf16(_log2_one_minus_beta(zn, causal), axis=1) for zn in zns]
        suffixes = [jnp.dot(p, tri2, preferred_element_type=F32) for p in pieces]
        ws = [_block_weights(sfx, zn, causal) for sfx, zn in zip(suffixes, zns)]
        for w, s in zip(ws, hs):
            acc_ref[:, s] = jnp.dot(w, v_ref[rows_of(qi), s], preferred_element_type=F32)
        if with_projection:
            finish_projection(out)
        carries = tuple(sfx[:, 0:1] for sfx in suffixes)

        def body(state):
            t, _, carries = state
            carries = step(qi - 1 - t, carries, None)
            return t + 1, any_live(carries), carries

        lax.while_loop(lambda state: jnp.logical_and(state[0] < qi, state[1]), body,
                       (jnp.int32(0), any_live(carries), carries))
        mix_ref[...] = (acc_ref[...] * sg_ref[...].astype(F32)).astype(BF16)

    pl.when(grid_step == 0)(lambda: attend(False))
    pl.when(jnp.logical_and(grid_step > 0, grid_step < last_step))(lambda: attend(True))
    pl.when(grid_step == last_step)(lambda: finish_projection(project()))


def _attend_merge_prompt(q, kb, vb, sg, tri, x2d, mix_cv, w_halves, final_g, *, batch, seq, tq):
    nq = seq // tq
    n_blocks = batch * nq
    m, d = x2d.shape
    width = q.shape[1]
    assert tq == tri.shape[0], "query block and key block share the diagonal mask"
    attended = lambda s: jnp.minimum(s, n_blocks - 1)
    projected = lambda s: jnp.maximum(s - 1, 0)
    qblock = pl.BlockSpec((tq, width), lambda s: (attended(s), 0))
    kvblock = pl.BlockSpec((seq, width), lambda s: (attended(s) // nq, 0))
    const = lambda shape: pl.BlockSpec(shape, lambda s: (0, 0), pipeline_mode=pl.Buffered(1))
    return pl.pallas_call(
        functools.partial(_attn_out_prompt_kernel, blocks_per_seq=nq),
        grid=(n_blocks + 1,),
        in_specs=[qblock, kvblock, kvblock, qblock, const(tri.shape),
                  pl.BlockSpec((tq, d), lambda s: (projected(s), 0)),
                  pl.BlockSpec((tq, width), lambda s: (projected(s), 0)),
                  const(w_halves[0].shape), const(w_halves[1].shape), const((1, d))],
        out_specs=pl.BlockSpec((tq, d), lambda s: (projected(s), 0)),
        out_shape=jax.ShapeDtypeStruct((m, d), F32),
        scratch_shapes=[pltpu.VMEM((tq, width), F32), pltpu.VMEM((tq, width), BF16)],
        compiler_params=_compiler_params(("arbitrary",)),
        name="attn_out_prompt",
    )(q, kb, vb, sg, tri, x2d, mix_cv, w_halves[0], w_halves[1], final_g.reshape(1, d))


def _attn_sample_kernel(q_ref, kn_ref, vn_ref, kc_hbm, vc_hbm, sg_ref, trit_ref, o_ref,
                        qnt_ref, knear_ref, vnear_ref, kfar_ref, vfar_ref, near_sems, far_sems,
                        *, t):
    step = pl.program_id(0)
    n = q_ref.shape[0] // t
    tk = trit_ref.shape[0]
    block_rows = tk * NUM_GROUPS
    n_cache = kc_hbm.shape[1] // block_rows
    lanes = NUM_GROUPS * t
    pairs = NUM_GROUPS // 2
    pad = HEAD_DIM
    nt_dims = (((1,), (1,)), ((), ()))
    slot = step % 2

    def cache_copies(stream, j, k_dst, v_dst, k_sem, v_sem):
        rows = pl.ds(j * block_rows, block_rows)
        return [pltpu.make_async_copy(kc_hbm.at[stream, rows, :], k_dst, k_sem),
                pltpu.make_async_copy(vc_hbm.at[stream, rows, :], v_dst, v_sem)]

    def near_copies(at_step, s):
        return [c for i in range(n) for c in cache_copies(
            at_step * n + i, n_cache - 1, knear_ref.at[s, i], vnear_ref.at[s, i],
            near_sems.at[0, s, i], near_sems.at[1, s, i])]

    @pl.when(step == 0)
    def _():
        for c in near_copies(0, 0):
            c.start()

    @pl.when(step + 1 < pl.num_programs(0))
    def _():
        for c in near_copies(step + 1, 1 - slot):
            c.start()

    qnt_ref[...] = jnp.zeros(qnt_ref.shape, BF16)
    for i in range(n):
        for h in range(NUM_GROUPS):
            p, half = divmod(h, 2)
            qnt_ref[i, p, h * t:(h + 1) * t, half * HEAD_DIM:(half + 1) * HEAD_DIM] = (
                q_ref[i * t:(i + 1) * t, h * HEAD_DIM:(h + 1) * HEAD_DIM])

    def neg_logits(i, load_k):
        zn = None
        for p in range(pairs):
            lhs = jnp.concatenate([load_k(2 * p), load_k(2 * p + 1)], axis=1).astype(BF16)
            d = lax.dot_general(lhs, qnt_ref[i, p], nt_dims, preferred_element_type=F32)
            zn = d if zn is None else zn + d
        return zn

    def walk(work):
        pieces = [[_split_bf16(_log2_one_minus_beta(zn, m), axis=0) for zn, _, _, m in blocks]
                  for blocks, _, _ in work]
        suffixes = [[jnp.dot(jnp.concatenate([tr, tr], axis=1), p, preferred_element_type=F32)
                     for (_, _, tr, _), p in zip(blocks, ps)]
                    for (blocks, _, _), ps in zip(work, pieces)]
        done = []
        for (blocks, carry, accs), sfxs in zip(work, suffixes):
            for (zn, load_v, _, m), sfx in zip(blocks, sfxs):
                w = jnp.exp2(sfx - zn + carry)
                if m is not None:
                    w = jnp.where(m, w, 0.0)
                w_t = w.T
                accs = [acc + jnp.dot(w_t[h * t:(h + 1) * t, :].astype(BF16),
                                      load_v(h).astype(BF16), preferred_element_type=F32)
                        for h, acc in enumerate(accs)]
                carry = carry + sfx[0:1, :]
            done.append((carry, tuple(accs)))
        return done

    head_rows = lambda ref, base, rows, h: ref[pl.ds(base + h, rows, stride=NUM_GROUPS), :]

    key_idx = lax.broadcasted_iota(jnp.int32, (pad, lanes), 0)
    query_idx = lax.broadcasted_iota(jnp.int32, (pad, lanes), 1) & (t - 1)
    zero_rows = lambda width: jnp.zeros((pad - t, width), F32)
    new_base = lambda i: i * t * NUM_GROUPS
    new_blocks = [
        (jnp.concatenate([neg_logits(i, lambda h, i=i: head_rows(kn_ref, new_base(i), t, h)),
                          zero_rows(lanes)], axis=0),
         lambda h, i=i: jnp.concatenate([head_rows(vn_ref, new_base(i), t, h),
                                         zero_rows(HEAD_DIM)], axis=0),
         trit_ref[0:pad, 0:pad], key_idx < query_idx) for i in range(n)]
    for c in near_copies(step, slot):
        c.wait()
    near_blocks = [
        (neg_logits(i, lambda h, i=i: head_rows(knear_ref.at[slot, i], 0, tk, h)),
         lambda h, i=i: head_rows(vnear_ref.at[slot, i], 0, tk, h), trit_ref[...], None)
        for i in range(n)]
    state = walk([([new_blocks[i], near_blocks[i]], jnp.zeros((1, lanes), F32),
                   [jnp.zeros((t, HEAD_DIM), F32)] * NUM_GROUPS) for i in range(n)])

    def any_live(state):
        return jnp.max(functools.reduce(jnp.maximum, [c for c, _ in state])) > EXP2_UNDERFLOW

    first_far = n_cache - 2
    far_slot = lambda j: (first_far - j) % 2

    def far_copies(j):
        s = far_slot(j)
        return [c for i in range(n) for c in cache_copies(
            step * n + i, j, kfar_ref.at[s, i], vfar_ref.at[s, i],
            far_sems.at[0, s, i], far_sems.at[1, s, i])]

    def start_if(cond, j):
        @pl.when(cond)
        def _():
            for c in far_copies(j):
                c.start()

    def wait_if(cond, j):
        @pl.when(cond)
        def _():
            for c in far_copies(j):
                c.wait()

    def older(loop_state):
        j, _, state = loop_state
        s = far_slot(j)
        for c in far_copies(j):
            c.wait()
        start_if(j >= 1, j - 1)
        far_blocks = [(neg_logits(i, lambda h, i=i: head_rows(kfar_ref.at[s, i], 0, tk, h)),
                       lambda h, i=i: head_rows(vfar_ref.at[s, i], 0, tk, h), trit_ref[...], None)
                      for i in range(n)]
        state = walk([([far_blocks[i]], state[i][0], list(state[i][1])) for i in range(n)])
        return j - 1, any_live(state), tuple(state)

    live = any_live(state)
    if first_far >= 0:
        start_if(live, first_far)
    j_end, _, state = lax.while_loop(lambda s: jnp.logical_and(s[0] >= 0, s[1]), older,
                                     (jnp.int32(first_far), live, tuple(state)))
    if first_far >= 0:
        wait_if(jnp.logical_and(live, j_end >= 0), j_end)

    for i, (_, accs) in enumerate(state):
        for h, acc in enumerate(accs):
            rows, hs = slice(i * t, (i + 1) * t), slice(h * HEAD_DIM, (h + 1) * HEAD_DIM)
            o_ref[rows, hs] = (acc * sg_ref[rows, hs].astype(F32)).astype(BF16)


def _attend_sample(q, k_new, v_new, cache_k, cache_v, sg, tri_t, *, batch, t):
    width = q.shape[1]
    tk = tri_t.shape[0]
    block_rows = tk * NUM_GROUPS
    assert t & (t - 1) == 0 and NUM_GROUPS * t == tk and cache_k.shape[1] % block_rows == 0
    n = SAMPLE_STREAMS_PER_STEP
    assert batch % n == 0
    rowblock = pl.BlockSpec((n * t, width), lambda b: (b, 0))
    newblock = pl.BlockSpec((n * t * NUM_GROUPS, HEAD_DIM), lambda b: (b, 0))
    in_hbm = pl.BlockSpec(memory_space=pl.ANY)
    key_blocks = lambda *lead: pltpu.VMEM((*lead, n, block_rows, HEAD_DIM), F32)
    return pl.pallas_call(
        functools.partial(_attn_sample_kernel, t=t),
        grid=(batch // n,),
        in_specs=[rowblock, newblock, newblock, in_hbm, in_hbm, rowblock,
                  pl.BlockSpec(tri_t.shape, lambda b: (0, 0))],
        out_specs=rowblock,
        out_shape=jax.ShapeDtypeStruct(q.shape, BF16),
        scratch_shapes=[pltpu.VMEM((n, NUM_GROUPS // 2, NUM_GROUPS * t, 2 * HEAD_DIM), BF16),
                        key_blocks(2), key_blocks(2), key_blocks(2), key_blocks(2),
                        pltpu.SemaphoreType.DMA((2, 2, n)), pltpu.SemaphoreType.DMA((2, 2, n))],
        compiler_params=_compiler_params(("arbitrary",)),
        name="attn_sample",
    )(q, k_new, v_new, cache_k, cache_v, sg, tri_t)


def _out_kernel(x_ref, msb_ref, mcv_ref, wsb_ref, wcv_ref, g_ref, y_ref, *wcopy_refs):
    w_sb, w_cv = wsb_ref[...].astype(BF16), wcv_ref[...].astype(BF16)
    for copy_ref, w in zip(wcopy_refs, (w_sb, w_cv)):
        copy_ref[...] = w
    y = (x_ref[...]
         + jnp.dot(msb_ref[...], w_sb, preferred_element_type=F32)
         + jnp.dot(mcv_ref[...], w_cv, preferred_element_type=F32))
    y_ref[...] = _rmsnorm_rows(y, g_ref[...])


def _merge_out(x2d, mix_sb, mix_cv, weights, final_g, *, tm):
    m, d = x2d.shape
    half = mix_sb.shape[1]
    rows = lambda width: pl.BlockSpec((tm, width), lambda i: (i, 0))
    w_args = [weights, weights]
    w_specs = [pl.BlockSpec((half, d), lambda i, r=r: (r, 0), pipeline_mode=pl.Buffered(1))
               for r in range(2)]
    copy_specs = [pl.BlockSpec((half, d), lambda i: (0, 0), pipeline_mode=pl.Buffered(1))] * 2
    copy_shapes = [jax.ShapeDtypeStruct((half, d), BF16)] * 2
    y, *w_copies = pl.pallas_call(
        _out_kernel,
        grid=(m // tm,),
        in_specs=[rows(d), rows(half), rows(half), *w_specs, pl.BlockSpec((1, d), lambda i: (0, 0))],
        out_specs=[rows(d), *copy_specs],
        out_shape=[jax.ShapeDtypeStruct((m, d), F32), *copy_shapes],
        compiler_params=_compiler_params(("arbitrary",)),
        name="merge_out",
    )(x2d, mix_sb, mix_cv, *w_args, final_g.reshape(1, d))
    return y, w_copies


def _lower_tri(n):
    j = lax.broadcasted_iota(jnp.int32, (n, n), 0)
    s = lax.broadcasted_iota(jnp.int32, (n, n), 1)
    return (j >= s).astype(BF16)


def kernel(x_prompt, x_sample, cache_k, cache_v, state_conv, norm_g, w_in, conv_w, w_out, final_g):
    depth = w_in.shape[0]
    assert depth == 1, "single-layer step"
    bsz, seq, d = x_prompt.shape
    dbsz, dseq, _ = x_sample.shape
    past = cache_k.shape[2]
    width = NUM_GROUPS * HEAD_DIM

    tri = _lower_tri(KEY_BLOCK)

    xs = x_sample.reshape(dbsz * dseq, d)
    qs, ks, vs, _, _, sgs, mcvs, tails, w_in_bf = _project(
        xs, norm_g[0], w_in[0], conv_w[0], state_conv[0], seq_rows=dseq,
        tm=min(PROJ_ROWS, xs.shape[0]), heads_per_step=1)
    msbs = _attend_sample(qs, ks, vs,
                          cache_k[0].reshape(dbsz, past * NUM_GROUPS, HEAD_DIM),
                          cache_v[0].reshape(dbsz, past * NUM_GROUPS, HEAD_DIM),
                          sgs, tri.T, batch=dbsz, t=dseq)
    ys, w_out_bf = _merge_out(xs, msbs, mcvs, w_out[0], final_g, tm=OUT_ROWS)

    xp = x_prompt.reshape(bsz * seq, d)
    zeros_left = jnp.zeros((bsz, CONV_W - 1, width), F32)
    qp, kp, vp, kbp, vbp, sgp, mcvp, tailp, _ = _project(
        xp, norm_g[0], w_in_bf, conv_w[0], zeros_left, seq_rows=seq, tm=PROJ_ROWS,
        heads_per_step=PROMPT_PROJ_HEADS)
    yp = _attend_merge_prompt(qp, kbp, vbp, sgp, tri, xp, mcvp, w_out_bf, final_g,
                              batch=bsz, seq=seq, tq=KEY_BLOCK)

    heads = lambda a, b_, t_: a.reshape(1, b_, t_, NUM_GROUPS, HEAD_DIM)
    return (yp.reshape(bsz, seq, d), ys.reshape(dbsz, dseq, d),
            heads(kp, bsz, seq), heads(vp, bsz, seq), tailp[None],
            heads(ks, dbsz, dseq), heads(vs, dbsz, dseq), tails[None])
```

```python
import functools
import math

import jax
import jax.numpy as jnp
from jax import lax
from jax.experimental import pallas as pl
from jax.experimental.pallas import tpu as pltpu

F32 = jnp.float32
BF16 = jnp.bfloat16

HEAD_DIM = 128
NUM_GROUPS = 8
NUM_SEGMENTS = 8
CONV_W = 3
CONV_CTX = CONV_W - 1
assert CONV_W == 3, "the conv epilogue writes its three taps out explicitly"
SUBLANES = 8
EPS = 1e-6
LOG2E = math.log2(math.e)
Q_PRESCALE = -(HEAD_DIM ** -0.5) * LOG2E
MXU_DEPTH = 256
KEY_BLOCK = MXU_DEPTH
PROJ_ROWS = 1024
OUT_ROWS = 512
PROMPT_PROJ_HEADS = 2
SAMPLE_STREAMS_PER_STEP = 4
EXP2_UNDERFLOW = -160.0

VMEM_LIMIT_BYTES = 56 * 1024 * 1024


def _compiler_params(semantics):
    return pltpu.CompilerParams(dimension_semantics=semantics,
                                vmem_limit_bytes=VMEM_LIMIT_BYTES)


def _rmsnorm_rows(x, g):
    r = lax.rsqrt(jnp.mean(x * x, axis=-1, keepdims=True) + EPS)
    return (x * r) * g


def _silu(x):
    return x * (1.0 / (1.0 + jnp.exp(-x)))


def _proj_kernel(*refs, seq_rows, tiles_per_seq, copy_weights):
    refs = list(refs)
    take = lambda count: [refs.pop(0) for _ in range(count)]
    x_hbm, g_ref = take(2)
    w_refs = take(NUM_SEGMENTS)
    cw_ref, left_ref = take(2)
    q_ref, k_ref, v_ref, kb_ref, vb_ref, sg_ref, mcv_ref, tail_ref = take(8)
    wcopy_refs = take(NUM_SEGMENTS) if copy_weights else ()
    x_ref, hn_ref, carry_ref, x_sem = take(4)
    i = pl.program_id(0)
    h = pl.program_id(1)
    tm = x_ref.shape[0]

    def x_copy(tile):
        return pltpu.make_async_copy(x_hbm.at[pl.ds(pl.multiple_of(tile * tm, tm), tm), :],
                                     x_ref, x_sem.at[0])

    @pl.when(jnp.logical_and(i == 0, h == 0))
    def _():
        x_copy(0).start()
        carry_ref[...] = jnp.zeros(carry_ref.shape, F32)

    @pl.when(h == 0)
    def _():
        x_copy(i).wait()
        hn_ref[...] = _rmsnorm_rows(x_ref[...], g_ref[...]).astype(BF16)

    @pl.when(jnp.logical_and(h == 0, i + 1 < pl.num_programs(0)))
    def _():
        x_copy(i + 1).start()

    w_blocks = [r[...].astype(BF16) for r in w_refs]
    for copy_ref, w in zip(wcopy_refs, w_blocks):
        copy_ref[...] = w

    def project(*ws):
        acc = jnp.dot(hn_ref[...], jnp.concatenate(ws, axis=1), preferred_element_type=F32)
        return [acc[:, s * HEAD_DIM:(s + 1) * HEAD_DIM] for s in range(len(ws))]

    rows = seq_rows if tiles_per_seq == 1 else tm
    nseq = tm // rows
    expand = lambda a: jnp.broadcast_to(a, (nseq, rows, HEAD_DIM)).reshape(tm, HEAD_DIM)
    rs = lax.broadcasted_iota(jnp.int32, (tm, HEAD_DIM), 0) & (rows - 1)

    for a in range(q_ref.shape[1] // HEAD_DIM):
        head = h * (q_ref.shape[1] // HEAD_DIM) + a
        cols = slice(a * HEAD_DIM, (a + 1) * HEAD_DIM)
        wq, wk, wv, wgs, wb, wc, wu, wgc = [w[:, cols] for w in w_blocks]

        c, u = project(wc, wu)
        b, g_cv = project(wb, wgc)
        k, v = project(wk, wv)
        q, g_sb = project(wq, wgs)

        q_ref[:, cols] = (q * Q_PRESCALE).astype(BF16)
        k_ref[pl.ds(head, tm, stride=NUM_GROUPS), :] = k
        v_ref[pl.ds(head, tm, stride=NUM_GROUPS), :] = v
        kb_ref[:, cols] = k.astype(BF16)
        vb_ref[:, cols] = v.astype(BF16)
        sg_ref[:, cols] = _silu(g_sb).astype(BF16)

        cu = c * u
        if tiles_per_seq > 1:
            left = jnp.where(i % tiles_per_seq == 0, left_ref[:, :, cols],
                             carry_ref[head][None, 0:CONV_CTX, :])
            carry_ref[head, 0:CONV_CTX, :] = cu[tm - CONV_CTX:tm, :]
        else:
            left = left_ref[:, :, cols]
        l0 = expand(left[:, 0:1, :])
        l1 = expand(left[:, 1:2, :])
        r1 = jnp.where(rs == 0, l1, pltpu.roll(cu, 1, axis=0))
        r2 = jnp.where(rs == 0, l0, jnp.where(rs == 1, l1, pltpu.roll(cu, 2, axis=0)))
        cw = cw_ref[:, cols]
        conv = cw[0:1, :] * r2 + cw[1:2, :] * r1 + cw[2:3, :] * cu
        mcv_ref[:, cols] = (b * conv * _silu(g_cv)).astype(BF16)
        tail_ref[:, :, cols] = cu.reshape(nseq, rows, HEAD_DIM)[:, rows - CONV_CTX:rows, :]


def _project(x2d, norm_g, weights, conv_w, left, *, seq_rows, tm, heads_per_step):
    m, d = x2d.shape
    width = NUM_GROUPS * HEAD_DIM
    cols = heads_per_step * HEAD_DIM
    steps = NUM_GROUPS // heads_per_step
    copy_weights = not isinstance(weights, (list, tuple))
    if copy_weights:
        assert m == tm, "bf16 weight copies are written once, by a single row tile"
        w_args = [weights] * NUM_SEGMENTS
        segment = lambda s: pl.BlockSpec((d, cols), lambda i, h: (0, s * steps + h))
    else:
        w_args = list(weights)
        segment = lambda s: pl.BlockSpec((d, cols), lambda i, h: (0, h))
    once = dict(pipeline_mode=pl.Buffered(1)) if m == tm else {}
    if tm >= seq_rows:
        tiles_per_seq, ns = 1, tm // seq_rows
        left_idx = lambda i, h: (i, 0, h)
    else:
        tiles_per_seq, ns = seq_rows // tm, 1
        left_idx = lambda i, h: (i // tiles_per_seq, 0, h)
    n_tails = (m // tm) * ns
    tile = lambda dt: jax.ShapeDtypeStruct((m, width), dt)
    native = jax.ShapeDtypeStruct((m * NUM_GROUPS, HEAD_DIM), F32)
    col_block = pl.BlockSpec((tm, cols), lambda i, h: (i, h))
    native_block = pl.BlockSpec((tm * NUM_GROUPS, HEAD_DIM), lambda i, h: (i, 0), **once)
    copy_specs, copy_shapes = [], []
    if copy_weights:
        copy_specs = [pl.BlockSpec((d, cols), lambda i, h: (0, h))] * NUM_SEGMENTS
        copy_shapes = [jax.ShapeDtypeStruct((d, width), BF16)] * NUM_SEGMENTS
    kern = functools.partial(_proj_kernel, seq_rows=seq_rows, tiles_per_seq=tiles_per_seq,
                             copy_weights=copy_weights)
    q, k, v, kb, vb, sg, mcv, tails, *w_copies = pl.pallas_call(
        kern,
        grid=(m // tm, steps),
        in_specs=[
            pl.BlockSpec(memory_space=pl.ANY),
            pl.BlockSpec((1, d), lambda i, h: (0, 0)),
            *[segment(s) for s in range(NUM_SEGMENTS)],
            pl.BlockSpec((CONV_W, cols), lambda i, h: (0, h)),
            pl.BlockSpec((ns, CONV_CTX, cols), left_idx),
        ],
        out_specs=[col_block, native_block, native_block, col_block, col_block, col_block,
                   col_block, pl.BlockSpec((ns, CONV_CTX, cols), lambda i, h: (i, 0, h)),
                   *copy_specs],
        out_shape=[tile(BF16), native, native, tile(BF16), tile(BF16), tile(BF16), tile(BF16),
                   jax.ShapeDtypeStruct((n_tails, CONV_CTX, width), F32), *copy_shapes],
        scratch_shapes=[pltpu.VMEM((tm, d), F32), pltpu.VMEM((tm, d), BF16),
                        pltpu.VMEM((NUM_GROUPS, SUBLANES, HEAD_DIM), F32),
                        pltpu.SemaphoreType.DMA((1,))],
        compiler_params=_compiler_params(("arbitrary", "arbitrary")),
        name="proj",
    )(x2d, norm_g.reshape(1, d), *w_args, conv_w, left)
    tails = tails.reshape(-1, tiles_per_seq, CONV_CTX, width)[:, tiles_per_seq - 1]
    return q, k, v, kb, vb, sg, mcv, tails, w_copies


def _log2_one_minus_beta(zn, mask):
    softplus2 = jnp.log(1.0 + jnp.exp2(-jnp.abs(zn))) * LOG2E
    log_1m = jnp.minimum(zn, 0.0) - softplus2
    return log_1m if mask is None else jnp.where(mask, log_1m, 0.0)


def _split_bf16(x, axis):
    hi = x.astype(BF16)
    return jnp.concatenate([hi, (x - hi.astype(F32)).astype(BF16)], axis=axis)


def _block_weights(suffix, zn, mask):
    w = jnp.exp2(suffix - zn)
    return (w if mask is None else jnp.where(mask, w, 0.0)).astype(BF16)


def _attn_out_prompt_kernel(q_ref, k_ref, v_ref, sg_ref, tri_ref, x_ref, mcv_ref, w_ref,
                            g_ref, y_ref, acc_ref, mix_ref, *, blocks_per_seq):
    grid_step = pl.program_id(0)
    last_step = pl.num_programs(0) - 1
    qi = grid_step % blocks_per_seq

    tq = q_ref.shape[0]
    tk = tri_ref.shape[0]
    heads = q_ref.shape[1] // HEAD_DIM
    tri2 = jnp.concatenate([tri_ref[...]] * 2, axis=0)
    hs = [slice(a * HEAD_DIM, (a + 1) * HEAD_DIM) for a in range(heads)]
    nt_dims = (((1,), (1,)), ((), ()))

    def rows_of(j):
        return pl.ds(j * tk if isinstance(j, int) else pl.multiple_of(j * tk, tk), tk)

    def logits(j):
        return tuple(lax.dot_general(q_ref[:, s], k_ref[rows_of(j), s], nt_dims,
                                     preferred_element_type=F32) for s in hs)

    def weights(zns, mask):
        pieces = [_split_bf16(_log2_one_minus_beta(zn, mask), axis=1) for zn in zns]
        suffixes = [jnp.dot(p, tri2, preferred_element_type=F32) for p in pieces]
        ws = tuple(_block_weights(sfx, zn, mask) for sfx, zn in zip(suffixes, zns))
        return ws, tuple(sfx[:, 0:1] for sfx in suffixes)

    def accumulate(ws, j, scales):
        for a, s in enumerate(hs):
            acc_ref[:, s] += scales[a] * jnp.dot(ws[a], v_ref[rows_of(j), s],
                                                 preferred_element_type=F32)

    def step(j, carries, mask):
        ws, totals = weights(logits(j), mask)
        accumulate(ws, j, [jnp.exp2(c) for c in carries])
        return tuple(c + tot for c, tot in zip(carries, totals))

    def project():
        return jnp.dot(jnp.concatenate([mix_ref[...], mcv_ref[...]], axis=1), w_ref[...],
                       preferred_element_type=F32)

    def finish_projection(out):
        y_ref[...] = _rmsnorm_rows(x_ref[...] + out, g_ref[...])

    def any_live(carries):
        return jnp.max(functools.reduce(jnp.maximum, carries)) > EXP2_UNDERFLOW

    def attend(with_projection):
        rows = lax.broadcasted_iota(jnp.int32, (tq, tk), 0)
        cols = lax.broadcasted_iota(jnp.int32, (tq, tk), 1)
        causal = cols < rows
        zns = logits(qi)
        out = project() if with_projection else None
        pieces = [_split_bf16(_log2_one_minus_beta(zn, causal), axis=1) for zn in zns]
        suffixes = [jnp.dot(p, tri2, preferred_element_type=F32) for p in pieces]
        ws = [_block_weights(sfx, zn, causal) for sfx, zn in zip(suffixes, zns)]
        for w, s in zip(ws, hs):
            acc_ref[:, s] = jnp.dot(w, v_ref[rows_of(qi), s], preferred_element_type=F32)
        if with_projection:
            finish_projection(out)
        carries = tuple(sfx[:, 0:1] for sfx in suffixes)

        def body(state):
            t, _, carries = state
            carries = step(qi - 1 - t, carries, None)
            return t + 1, any_live(carries), carries

        lax.while_loop(lambda state: jnp.logical_and(state[0] < qi, state[1]), body,
                       (jnp.int32(0), any_live(carries), carries))
        mix_ref[...] = (acc_ref[...] * sg_ref[...].astype(F32)).astype(BF16)

    pl.when(grid_step == 0)(lambda: attend(False))
    pl.when(jnp.logical_and(grid_step > 0, grid_step < last_step))(lambda: attend(True))
    pl.when(grid_step == last_step)(lambda: finish_projection(project()))


def _attend_merge_prompt(q, kb, vb, sg, tri, x2d, mix_cv, w_out, final_g, *, batch, seq, tq):
    nq = seq // tq
    n_blocks = batch * nq
    m, d = x2d.shape
    width = q.shape[1]
    assert tq == tri.shape[0], "query block and key block share the diagonal mask"
    attended = lambda s: jnp.minimum(s, n_blocks - 1)
    projected = lambda s: jnp.maximum(s - 1, 0)
    qblock = pl.BlockSpec((tq, width), lambda s: (attended(s), 0))
    kvblock = pl.BlockSpec((seq, width), lambda s: (attended(s) // nq, 0))
    const = lambda shape: pl.BlockSpec(shape, lambda s: (0, 0), pipeline_mode=pl.Buffered(1))
    return pl.pallas_call(
        functools.partial(_attn_out_prompt_kernel, blocks_per_seq=nq),
        grid=(n_blocks + 1,),
        in_specs=[qblock, kvblock, kvblock, qblock, const(tri.shape),
                  pl.BlockSpec((tq, d), lambda s: (projected(s), 0)),
                  pl.BlockSpec((tq, width), lambda s: (projected(s), 0)),
                  const(w_out.shape), const((1, d))],
        out_specs=pl.BlockSpec((tq, d), lambda s: (projected(s), 0)),
        out_shape=jax.ShapeDtypeStruct((m, d), F32),
        scratch_shapes=[pltpu.VMEM((tq, width), F32), pltpu.VMEM((tq, width), BF16)],
        compiler_params=_compiler_params(("arbitrary",)),
        name="attn_out_prompt",
    )(q, kb, vb, sg, tri, x2d, mix_cv, w_out, final_g.reshape(1, d))


def _attn_sample_kernel(q_ref, kn_ref, vn_ref, kc_hbm, vc_hbm, sg_ref, trit_ref, o_ref,
                        qnt_ref, knear_ref, vnear_ref, kfar_ref, vfar_ref, near_sems, far_sems,
                        *, t):
    step = pl.program_id(0)
    n = q_ref.shape[0] // t
    tk = trit_ref.shape[0]
    block_rows = tk * NUM_GROUPS
    n_cache = kc_hbm.shape[1] // block_rows
    lanes = NUM_GROUPS * t
    pairs = NUM_GROUPS // 2
    pad = HEAD_DIM
    nt_dims = (((1,), (1,)), ((), ()))
    slot = step % 2

    def cache_copies(stream, j, k_dst, v_dst, k_sem, v_sem):
        rows = pl.ds(j * block_rows, block_rows)
        return [pltpu.make_async_copy(kc_hbm.at[stream, rows, :], k_dst, k_sem),
                pltpu.make_async_copy(vc_hbm.at[stream, rows, :], v_dst, v_sem)]

    def near_copies(at_step, s):
        return [c for i in range(n) for c in cache_copies(
            at_step * n + i, n_cache - 1, knear_ref.at[s, i], vnear_ref.at[s, i],
            near_sems.at[0, s, i], near_sems.at[1, s, i])]

    @pl.when(step == 0)
    def _():
        for c in near_copies(0, 0):
            c.start()

    @pl.when(step + 1 < pl.num_programs(0))
    def _():
        for c in near_copies(step + 1, 1 - slot):
            c.start()

    qnt_ref[...] = jnp.zeros(qnt_ref.shape, BF16)
    for i in range(n):
        for h in range(NUM_GROUPS):
            p, half = divmod(h, 2)
            qnt_ref[i, p, h * t:(h + 1) * t, half * HEAD_DIM:(half + 1) * HEAD_DIM] = (
                q_ref[i * t:(i + 1) * t, h * HEAD_DIM:(h + 1) * HEAD_DIM])

    def neg_logits(i, load_k):
        zn = None
        for p in range(pairs):
            lhs = jnp.concatenate([load_k(2 * p), load_k(2 * p + 1)], axis=1).astype(BF16)
            d = lax.dot_general(lhs, qnt_ref[i, p], nt_dims, preferred_element_type=F32)
            zn = d if zn is None else zn + d
        return zn

    def walk(work):
        pieces = [[_split_bf16(_log2_one_minus_beta(zn, m), axis=0) for zn, _, _, m in blocks]
                  for blocks, _, _ in work]
        suffixes = [[jnp.dot(jnp.concatenate([tr, tr], axis=1), p, preferred_element_type=F32)
                     for (_, _, tr, _), p in zip(blocks, ps)]
                    for (blocks, _, _), ps in zip(work, pieces)]
        done = []
        for (blocks, carry, accs), sfxs in zip(work, suffixes):
            for (zn, load_v, _, m), sfx in zip(blocks, sfxs):
                w = jnp.exp2(sfx - zn + carry)
                if m is not None:
                    w = jnp.where(m, w, 0.0)
                w_t = w.T
                accs = [acc + jnp.dot(w_t[h * t:(h + 1) * t, :].astype(BF16),
                                      load_v(h).astype(BF16), preferred_element_type=F32)
                        for h, acc in enumerate(accs)]
                carry = carry + sfx[0:1, :]
            done.append((carry, tuple(accs)))
        return done

    head_rows = lambda ref, base, rows, h: ref[pl.ds(base + h, rows, stride=NUM_GROUPS), :]

    key_idx = lax.broadcasted_iota(jnp.int32, (pad, lanes), 0)
    query_idx = lax.broadcasted_iota(jnp.int32, (pad, lanes), 1) & (t - 1)
    zero_rows = lambda width: jnp.zeros((pad - t, width), F32)
    new_base = lambda i: i * t * NUM_GROUPS
    new_blocks = [
        (jnp.concatenate([neg_logits(i, lambda h, i=i: head_rows(kn_ref, new_base(i), t, h)),
                          zero_rows(lanes)], axis=0),
         lambda h, i=i: jnp.concatenate([head_rows(vn_ref, new_base(i), t, h),
                                         zero_rows(HEAD_DIM)], axis=0),
         trit_ref[0:pad, 0:pad], key_idx < query_idx) for i in range(n)]
    for c in near_copies(step, slot):
        c.wait()
    near_blocks = [
        (neg_logits(i, lambda h, i=i: head_rows(knear_ref.at[slot, i], 0, tk, h)),
         lambda h, i=i: head_rows(vnear_ref.at[slot, i], 0, tk, h), trit_ref[...], None)
        for i in range(n)]
    state = walk([([new_blocks[i], near_blocks[i]], jnp.zeros((1, lanes), F32),
                   [jnp.zeros((t, HEAD_DIM), F32)] * NUM_GROUPS) for i in range(n)])

    def any_live(state):
        return jnp.max(functools.reduce(jnp.maximum, [c for c, _ in state])) > EXP2_UNDERFLOW

    first_far = n_cache - 2
    far_slot = lambda j: (first_far - j) % 2

    def far_copies(j):
        s = far_slot(j)
        return [c for i in range(n) for c in cache_copies(
            step * n + i, j, kfar_ref.at[s, i], vfar_ref.at[s, i],
            far_sems.at[0, s, i], far_sems.at[1, s, i])]

    def start_if(cond, j):
        @pl.when(cond)
        def _():
            for c in far_copies(j):
                c.start()

    def wait_if(cond, j):
        @pl.when(cond)
        def _():
            for c in far_copies(j):
                c.wait()

    def older(loop_state):
        j, _, state = loop_state
        s = far_slot(j)
        for c in far_copies(j):
            c.wait()
        start_if(j >= 1, j - 1)
        far_blocks = [(neg_logits(i, lambda h, i=i: head_rows(kfar_ref.at[s, i], 0, tk, h)),
                       lambda h, i=i: head_rows(vfar_ref.at[s, i], 0, tk, h), trit_ref[...], None)
                      for i in range(n)]
        state = walk([([far_blocks[i]], state[i][0], list(state[i][1])) for i in range(n)])
        return j - 1, any_live(state), tuple(state)

    live = any_live(state)
    if first_far >= 0:
        start_if(live, first_far)
    j_end, _, state = lax.while_loop(lambda s: jnp.logical_and(s[0] >= 0, s[1]), older,
                                     (jnp.int32(first_far), live, tuple(state)))
    if first_far >= 0:
        wait_if(jnp.logical_and(live, j_end >= 0), j_end)

    for i, (_, accs) in enumerate(state):
        for h, acc in enumerate(accs):
            rows, hs = slice(i * t, (i + 1) * t), slice(h * HEAD_DIM, (h + 1) * HEAD_DIM)
            o_ref[rows, hs] = (acc * sg_ref[rows, hs].astype(F32)).astype(BF16)


def _attend_sample(q, k_new, v_new, cache_k, cache_v, sg, tri_t, *, batch, t):
    width = q.shape[1]
    tk = tri_t.shape[0]
    block_rows = tk * NUM_GROUPS
    assert t & (t - 1) == 0 and NUM_GROUPS * t == tk and cache_k.shape[1] % block_rows == 0
    n = SAMPLE_STREAMS_PER_STEP
    assert batch % n == 0
    rowblock = pl.BlockSpec((n * t, width), lambda b: (b, 0))
    newblock = pl.BlockSpec((n * t * NUM_GROUPS, HEAD_DIM), lambda b: (b, 0))
    in_hbm = pl.BlockSpec(memory_space=pl.ANY)
    key_blocks = lambda *lead: pltpu.VMEM((*lead, n, block_rows, HEAD_DIM), F32)
    return pl.pallas_call(
        functools.partial(_attn_sample_kernel, t=t),
        grid=(batch // n,),
        in_specs=[rowblock, newblock, newblock, in_hbm, in_hbm, rowblock,
                  pl.BlockSpec(tri_t.shape, lambda b: (0, 0))],
        out_specs=rowblock,
        out_shape=jax.ShapeDtypeStruct(q.shape, BF16),
        scratch_shapes=[pltpu.VMEM((n, NUM_GROUPS // 2, NUM_GROUPS * t, 2 * HEAD_DIM), BF16),
                        key_blocks(2), key_blocks(2), key_blocks(2), key_blocks(2),
                        pltpu.SemaphoreType.DMA((2, 2, n)), pltpu.SemaphoreType.DMA((2, 2, n))],
        compiler_params=_compiler_params(("arbitrary",)),
        name="attn_sample",
    )(q, k_new, v_new, cache_k, cache_v, sg, tri_t)


def _out_kernel(x_ref, msb_ref, mcv_ref, w_ref, g_ref, y_ref, wcopy_ref):
    w = w_ref[...].astype(BF16)
    wcopy_ref[...] = w
    mix = jnp.concatenate([msb_ref[...], mcv_ref[...]], axis=1)
    y_ref[...] = _rmsnorm_rows(x_ref[...] + jnp.dot(mix, w, preferred_element_type=F32), g_ref[...])


def _merge_out(x2d, mix_sb, mix_cv, weights, final_g, *, tm):
    m, d = x2d.shape
    half = mix_sb.shape[1]
    rows = lambda width: pl.BlockSpec((tm, width), lambda i: (i, 0))
    whole = pl.BlockSpec(weights.shape, lambda i: (0, 0), pipeline_mode=pl.Buffered(1))
    return pl.pallas_call(
        _out_kernel,
        grid=(m // tm,),
        in_specs=[rows(d), rows(half), rows(half), whole, pl.BlockSpec((1, d), lambda i: (0, 0))],
        out_specs=[rows(d), whole],
        out_shape=[jax.ShapeDtypeStruct((m, d), F32), jax.ShapeDtypeStruct(weights.shape, BF16)],
        compiler_params=_compiler_params(("arbitrary",)),
        name="merge_out",
    )(x2d, mix_sb, mix_cv, weights, final_g.reshape(1, d))


def _lower_tri(n):
    j = lax.broadcasted_iota(jnp.int32, (n, n), 0)
    s = lax.broadcasted_iota(jnp.int32, (n, n), 1)
    return (j >= s).astype(BF16)


def kernel(x_prompt, x_sample, cache_k, cache_v, state_conv, norm_g, w_in, conv_w, w_out, final_g):
    depth = w_in.shape[0]
    assert depth == 1, "single-layer step"
    bsz, seq, d = x_prompt.shape
    dbsz, dseq, _ = x_sample.shape
    past = cache_k.shape[2]
    width = NUM_GROUPS * HEAD_DIM

    tri = _lower_tri(KEY_BLOCK)

    xs = x_sample.reshape(dbsz * dseq, d)
    qs, ks, vs, _, _, sgs, mcvs, tails, w_in_bf = _project(
        xs, norm_g[0], w_in[0], conv_w[0], state_conv[0], seq_rows=dseq,
        tm=min(PROJ_ROWS, xs.shape[0]), heads_per_step=1)
    msbs = _attend_sample(qs, ks, vs,
                          cache_k[0].reshape(dbsz, past * NUM_GROUPS, HEAD_DIM),
                          cache_v[0].reshape(dbsz, past * NUM_GROUPS, HEAD_DIM),
                          sgs, tri.T, batch=dbsz, t=dseq)
    ys, w_out_bf = _merge_out(xs, msbs, mcvs, w_out[0], final_g, tm=OUT_ROWS)

    xp = x_prompt.reshape(bsz * seq, d)
    zeros_left = jnp.zeros((bsz, CONV_W - 1, width), F32)
    qp, kp, vp, kbp, vbp, sgp, mcvp, tailp, _ = _project(
        xp, norm_g[0], w_in_bf, conv_w[0], zeros_left, seq_rows=seq, tm=PROJ_ROWS,
        heads_per_step=PROMPT_PROJ_HEADS)
    yp = _attend_merge_prompt(qp, kbp, vbp, sgp, tri, xp, mcvp, w_out_bf, final_g,
                              batch=bsz, seq=seq, tq=KEY_BLOCK)

    heads = lambda a, b_, t_: a.reshape(1, b_, t_, NUM_GROUPS, HEAD_DIM)
    return (yp.reshape(bsz, seq, d), ys.reshape(dbsz, dseq, d),
            heads(kp, bsz, seq), heads(vp, bsz, seq), tailp[None],
            heads(ks, dbsz, dseq), heads(vs, dbsz, dseq), tails[None])
```
